```python
import jax, jax.numpy as jnp
from jax import lax
import numpy as np

D_MODEL = 2048
BATCH = 2
SEQ = 4096
DEPTH = 4

MLA_HEADS = 8
QK_NOPE_DIM = 128
QK_ROPE_DIM = 64
V_HEAD_DIM = 128
Q_LORA_RANK = 768
KV_LORA_RANK = 512
ROPE_THETA = 10000.0
Q_BLOCK = 128
ATTN_WIDTH = MLA_HEADS * V_HEAD_DIM
SGU_GROUPS = 8
SGU_GROUP_DIM = 128
SGU_WIDTH = SGU_GROUPS * SGU_GROUP_DIM
CHUNK = 128
MIX_WIDTH = ATTN_WIDTH + SGU_WIDTH
SPLIT_AB = (Q_LORA_RANK,
            Q_LORA_RANK + KV_LORA_RANK,
            Q_LORA_RANK + KV_LORA_RANK + QK_ROPE_DIM,
            Q_LORA_RANK + KV_LORA_RANK + QK_ROPE_DIM + SGU_WIDTH)
IN_AB_WIDTH = Q_LORA_RANK + KV_LORA_RANK + QK_ROPE_DIM + 2 * SGU_WIDTH
POOL_WINDOWS = (2, 4, 8, 16)
POOL_GROUP_DIM = D_MODEL // 4
D_FF = 5632
N_EXPERTS = 8
TOP_K = 2
D_FF_EXPERT = 2816
N_EVEN = (DEPTH + 1) // 2
N_ODD = DEPTH // 2
DEEPNORM_ALPHA = (2 * DEPTH) ** 0.25
DEEPNORM_BETA = (8 * DEPTH) ** -0.25
LN_EPS = 1e-5
RMS_EPS = 1e-6

kernel_name = "hybrid_mla_sgu_pool_moe_deepnorm_adaln"


def layer_norm(x, g, b):
    xf = x.astype(jnp.float32)
    mu = jnp.mean(xf, axis=-1, keepdims=True)
    var = jnp.mean(jnp.square(xf - mu), axis=-1, keepdims=True)
    return ((xf - mu) * lax.rsqrt(var + LN_EPS) * g + b).astype(x.dtype)


def rms_norm(x, g):
    xf = x.astype(jnp.float32)
    return (xf * lax.rsqrt(jnp.mean(xf * xf, axis=-1, keepdims=True) + RMS_EPS) * g).astype(x.dtype)


def rope_tables(positions):
    inv = ROPE_THETA ** (-jnp.arange(0, QK_ROPE_DIM, 2, dtype=jnp.float32) / QK_ROPE_DIM)
    ang = positions.astype(jnp.float32)[..., None] * inv
    return jnp.cos(ang), jnp.sin(ang)


def apply_rope(x, cos, sin):
    half = QK_ROPE_DIM // 2
    xf = x.astype(jnp.float32)
    x1, x2 = xf[..., :half], xf[..., half:]
    return jnp.concatenate([x1 * cos - x2 * sin, x1 * sin + x2 * cos], axis=-1).astype(x.dtype)


def ada_modulation(c, w, b):
    mod = jax.nn.silu(c) @ w + b
    shift, scale, gate = jnp.split(mod[:, None, :], 3, axis=-1)
    return shift, scale, gate


def mla_attention(q_nope, q_rope, k_nope, k_rope, v):
    b_, s_, h_, dv = v.shape
    n_blocks = s_ // Q_BLOCK
    sm_scale = (QK_NOPE_DIM + QK_ROPE_DIM) ** -0.5
    k_pos = jnp.arange(s_)

    def block(i):
        start = i * Q_BLOCK
        qn = lax.dynamic_slice_in_dim(q_nope, start, Q_BLOCK, axis=1)
        qr = lax.dynamic_slice_in_dim(q_rope, start, Q_BLOCK, axis=1)
        s = (jnp.einsum('bqhd,bkhd->bhqk', qn, k_nope, preferred_element_type=jnp.float32)
             + jnp.einsum('bqhr,bkr->bhqk', qr, k_rope, preferred_element_type=jnp.float32)) * sm_scale
        causal = (start + jnp.arange(Q_BLOCK))[:, None] >= k_pos[None, :]
        s = jnp.where(causal, s, -jnp.inf)
        p = jax.nn.softmax(s, axis=-1).astype(v.dtype)
        return jnp.einsum('bhqk,bkhd->bqhd', p, v)

    out = lax.map(block, jnp.arange(n_blocks))
    return jnp.moveaxis(out, 0, 1).reshape(b_, s_, h_ * dv)


def spatial_gating(u, v, norm_g, norm_b, w_s, b_s):
    b_, s_, _ = v.shape
    n_chunks = s_ // CHUNK
    vg = v.reshape(b_, s_, SGU_GROUPS, SGU_GROUP_DIM)
    vg = layer_norm(vg, norm_g.reshape(SGU_GROUPS, SGU_GROUP_DIM), norm_b.reshape(SGU_GROUPS, SGU_GROUP_DIM))
    vg = vg.reshape(b_, n_chunks, CHUNK, SGU_GROUPS, SGU_GROUP_DIM)
    w_causal = w_s * jnp.tril(jnp.ones((CHUNK, CHUNK), dtype=w_s.dtype))
    s = jnp.einsum('gts,bnsgc->bntgc', w_causal, vg) + b_s.T[None, None, :, :, None]
    return u * s.reshape(b_, s_, SGU_WIDTH)


def mixer_ab(h, cos, sin, w_in, q_norm_g, w_q_up, kv_norm_g, w_kv_up,
             sgu_norm_g, sgu_norm_b, sgu_w, sgu_b, w_out):
    b_, s_, _ = h.shape
    z = h @ w_in
    cq, ckv, kr, zu, zv = jnp.split(z, SPLIT_AB, axis=-1)
    q = (rms_norm(cq, q_norm_g) @ w_q_up).reshape(b_, s_, MLA_HEADS, QK_NOPE_DIM + QK_ROPE_DIM)
    q_nope = q[..., :QK_NOPE_DIM]
    q_rope = apply_rope(q[..., QK_NOPE_DIM:], cos[:, :, None, :], sin[:, :, None, :])
    kv = (rms_norm(ckv, kv_norm_g) @ w_kv_up).reshape(b_, s_, MLA_HEADS, QK_NOPE_DIM + V_HEAD_DIM)
    k_nope, v = kv[..., :QK_NOPE_DIM], kv[..., QK_NOPE_DIM:]
    k_rope = apply_rope(kr, cos, sin)
    attn = mla_attention(q_nope, q_rope, k_nope, k_rope, v)
    sgu = spatial_gating(jax.nn.gelu(zu), jax.nn.gelu(zv), sgu_norm_g, sgu_norm_b, sgu_w, sgu_b)
    return jnp.concatenate([attn, sgu], axis=-1) @ w_out


def multiscale_pool(h, w_pool, pool_scale):
    s_ = h.shape[1]
    hf = h.astype(jnp.float32)
    cs0 = jnp.pad(jnp.cumsum(hf, axis=1), ((0, 0), (1, 0), (0, 0)))
    count_base = jnp.arange(1, s_ + 1, dtype=jnp.float32)[None, :, None]
    outs = []
    for gi, w in enumerate(POOL_WINDOWS):
        sl = slice(gi * POOL_GROUP_DIM, (gi + 1) * POOL_GROUP_DIM)
        cg = cs0[:, :, sl]
        lag = jnp.pad(cg, ((0, 0), (w, 0), (0, 0)))[:, 1:s_ + 1]
        mean = (cg[:, 1:] - lag) / jnp.minimum(count_base, float(w))
        d = (mean - hf[:, :, sl]).astype(h.dtype)
        outs.append(d @ w_pool[gi])
    return jnp.concatenate(outs, axis=-1) * pool_scale


def swiglu(h, w_gate, w_up, w_down):
    return (jax.nn.silu(h @ w_gate) * (h @ w_up)) @ w_down


def moe_swiglu(h, w_router, w_gate, w_up, w_down):
    b_, s_, d_ = h.shape
    t = h.reshape(-1, d_)
    logits = jnp.dot(t, w_router, preferred_element_type=jnp.float32)
    top_v, top_i = lax.top_k(logits, TOP_K)
    top_w = jax.nn.softmax(top_v, axis=-1)
    gates = jnp.sum(jax.nn.one_hot(top_i, N_EXPERTS, dtype=jnp.float32) * top_w[..., None], axis=1)
    gates = gates.astype(t.dtype)
    y = jnp.zeros_like(t)
    for e in range(N_EXPERTS):
        y = y + gates[:, e:e + 1] * swiglu(t, w_gate[e], w_up[e], w_down[e])
    return y.reshape(b_, s_, d_)


def setup_inputs(seed: int = 0) -> dict:
    key = jax.random.key(seed)
    ks = jax.random.split(key, 32)
    f32 = jnp.float32

    def nrm(k, shape, scale):
        return jax.random.normal(k, shape, f32) * scale

    D = D_MODEL
    offset = jax.random.randint(ks[2], (BATCH, 1), 0, 1024, dtype=jnp.int32)
    positions = (offset + jnp.arange(SEQ, dtype=jnp.int32)[None, :]).astype(jnp.int32)
    return {
        "x": nrm(ks[0], (BATCH, SEQ, D), 1.0),
        "c": nrm(ks[1], (BATCH, D), 1.0),
        "positions": positions,
        "ada_w": nrm(ks[3], (DEPTH, 2, D, 3 * D), 0.1 * D ** -0.5),
        "ada_b": nrm(ks[4], (DEPTH, 2, 3 * D), 0.02),
        "ln_g": 1.0 + nrm(ks[5], (DEPTH, 2, D), 0.02),
        "ln_b": nrm(ks[6], (DEPTH, 2, D), 0.02),
        "w_in_ab": nrm(ks[7], (N_EVEN, D, IN_AB_WIDTH), D ** -0.5),
        "q_norm_g": 1.0 + nrm(ks[8], (N_EVEN, Q_LORA_RANK), 0.02),
        "w_q_up": nrm(ks[9], (N_EVEN, Q_LORA_RANK, MLA_HEADS * (QK_NOPE_DIM + QK_ROPE_DIM)), Q_LORA_RANK ** -0.5),
        "kv_norm_g": 1.0 + nrm(ks[10], (N_EVEN, KV_LORA_RANK), 0.02),
        "w_kv_up": nrm(ks[11], (N_EVEN, KV_LORA_RANK, MLA_HEADS * (QK_NOPE_DIM + V_HEAD_DIM)), KV_LORA_RANK ** -0.5),
        "sgu_norm_g": 1.0 + nrm(ks[12], (N_EVEN, SGU_WIDTH), 0.02),
        "sgu_norm_b": nrm(ks[13], (N_EVEN, SGU_WIDTH), 0.02),
        "sgu_w": nrm(ks[14], (N_EVEN, SGU_GROUPS, CHUNK, CHUNK), CHUNK ** -0.5),
        "sgu_b": 1.0 + nrm(ks[15], (N_EVEN, SGU_GROUPS, CHUNK), 0.02),
        "w_out_ab": nrm(ks[16], (N_EVEN, MIX_WIDTH, D), DEEPNORM_BETA * MIX_WIDTH ** -0.5),
        "ffn_w_gate": nrm(ks[17], (N_EVEN, D, D_FF), D ** -0.5),
        "ffn_w_up": nrm(ks[18], (N_EVEN, D, D_FF), D ** -0.5),
        "ffn_w_down": nrm(ks[19], (N_EVEN, D_FF, D), DEEPNORM_BETA * D_FF ** -0.5),
        "pool_w": nrm(ks[20], (N_ODD, len(POOL_WINDOWS), POOL_GROUP_DIM, POOL_GROUP_DIM), POOL_GROUP_DIM ** -0.5),
        "pool_scale": 1.0 + nrm(ks[21], (N_ODD, D), 0.02),
        "w_out_c": nrm(ks[22], (N_ODD, D, D), DEEPNORM_BETA * D ** -0.5),
        "router_w": nrm(ks[23], (N_ODD, D, N_EXPERTS), D ** -0.5),
        "moe_w_gate": nrm(ks[24], (N_ODD, N_EXPERTS, D, D_FF_EXPERT), D ** -0.5),
        "moe_w_up": nrm(ks[25], (N_ODD, N_EXPERTS, D, D_FF_EXPERT), D ** -0.5),
        "moe_w_down": nrm(ks[26], (N_ODD, N_EXPERTS, D_FF_EXPERT, D), DEEPNORM_BETA * D_FF_EXPERT ** -0.5),
    }


def reference(x, c, positions, ada_w, ada_b, ln_g, ln_b, w_in_ab, q_norm_g, w_q_up,
              kv_norm_g, w_kv_up, sgu_norm_g, sgu_norm_b, sgu_w, sgu_b, w_out_ab,
              ffn_w_gate, ffn_w_up, ffn_w_down, pool_w, pool_scale, w_out_c, router_w,
              moe_w_gate, moe_w_up, moe_w_down):
    cos, sin = rope_tables(positions)
    for l in range(DEPTH):
        j = l // 2
        shift, scale, gate = ada_modulation(c, ada_w[l, 0], ada_b[l, 0])
        hm = x * (1 + scale) + shift
        if l % 2 == 0:
            y = mixer_ab(hm, cos, sin, w_in_ab[j], q_norm_g[j], w_q_up[j], kv_norm_g[j], w_kv_up[j],
                         sgu_norm_g[j], sgu_norm_b[j], sgu_w[j], sgu_b[j], w_out_ab[j])
        else:
            y = multiscale_pool(hm, pool_w[j], pool_scale[j]) @ w_out_c[j]
        x = layer_norm(DEEPNORM_ALPHA * x + (1 + gate) * y, ln_g[l, 0], ln_b[l, 0])
        shift, scale, gate = ada_modulation(c, ada_w[l, 1], ada_b[l, 1])
        hm = x * (1 + scale) + shift
        if l % 2 == 0:
            y = swiglu(hm, ffn_w_gate[j], ffn_w_up[j], ffn_w_down[j])
        else:
            y = moe_swiglu(hm, router_w[j], moe_w_gate[j], moe_w_up[j], moe_w_down[j])
        x = layer_norm(DEEPNORM_ALPHA * x + (1 + gate) * y, ln_g[l, 1], ln_b[l, 1])
    return x
```

```python
import functools

import jax
import jax.numpy as jnp
import numpy as np
from jax import lax
from jax.experimental import pallas as pl
from jax.experimental.pallas import tpu as pltpu

F32 = jnp.float32
BF16 = jnp.bfloat16

D_MODEL = 2048
BATCH = 2
SEQ = 4096
DEPTH = 4
MLA_HEADS = 8
QK_NOPE_DIM = 128
QK_ROPE_DIM = 64
V_HEAD_DIM = 128
Q_LORA_RANK = 768
KV_LORA_RANK = 512
ROPE_THETA = 10000.0
SGU_GROUPS = 8
SGU_GROUP_DIM = 128
SGU_WIDTH = SGU_GROUPS * SGU_GROUP_DIM
CHUNK = 128
ATTN_WIDTH = MLA_HEADS * V_HEAD_DIM
POOL_WINDOWS = (2, 4, 8, 16)
POOL_GROUP_DIM = D_MODEL // 4
D_FF = 5632
N_EXPERTS = 8
D_FF_EXPERT = 2816
DEEPNORM_ALPHA = (2 * DEPTH) ** 0.25
LN_EPS = 1e-5
RMS_EPS = 1e-6
SM_SCALE = (QK_NOPE_DIM + QK_ROPE_DIM) ** -0.5

LANES = 128
HEAD_LANES = 2 * LANES
QK_WIDTH = MLA_HEADS * HEAD_LANES
VMEM_LIMIT = 60 * 1024 * 1024
POOL_HALO = 128


def _params(semantics):
    return pltpu.CompilerParams(dimension_semantics=semantics, vmem_limit_bytes=VMEM_LIMIT)


def _const_spec(shape):
    nd = len(shape)
    return pl.BlockSpec(shape, lambda *_: (0,) * nd, pipeline_mode=pl.Buffered(1))


def _dot(a, b):
    return jnp.dot(a, b, preferred_element_type=F32)


def _modulate(x, shift, scale):
    return x * (1.0 + scale) + shift


def _deepnorm(x, y, gate, g, b):
    r = DEEPNORM_ALPHA * x + (1.0 + gate) * y
    mu = jnp.mean(r, axis=-1, keepdims=True)
    rc = r - mu
    var = jnp.mean(rc * rc, axis=-1, keepdims=True)
    return rc * lax.rsqrt(var + LN_EPS) * g + b


def _rms(x, g):
    return x * lax.rsqrt(jnp.mean(x * x, axis=-1, keepdims=True) + RMS_EPS) * g


def _rope(r, cos_t, sin_hi, sin_lo):
    return (r * cos_t + pltpu.roll(r, QK_ROPE_DIM // 2, 1) * sin_hi
            + pltpu.roll(r, LANES - QK_ROPE_DIM // 2, 1) * sin_lo)


def _mod_specs(k, n_grid):
    def spec(part):
        if n_grid == 2:
            return pl.BlockSpec((1, 1, D_MODEL), lambda b, i: (k * 8 + b, 0, part))
        if n_grid == 3:
            return pl.BlockSpec((1, 1, D_MODEL), lambda b, i, f: (k * 8 + b, 0, part))
        return pl.BlockSpec((1, 1, D_MODEL), lambda b, i, e, f: (k * 8 + b, 0, part))
    return spec(0), spec(1), spec(2)


def _ada_body(c_ref, w_ref, b_ref, o_ref):
    s = jax.nn.silu(c_ref[...]).astype(BF16)
    o_ref[0] = _dot(s, w_ref[0].astype(BF16)) + b_ref[0]


def _ada_modulation(c, ada_w, ada_b):
    n_mod = 2 * DEPTH
    tn = 1024
    c_pad = jnp.pad(c, ((0, 8 - BATCH), (0, 0)))
    out = pl.pallas_call(
        _ada_body,
        grid=(n_mod, 3 * D_MODEL // tn),
        in_specs=[
            pl.BlockSpec((8, D_MODEL), lambda k, n: (0, 0)),
            pl.BlockSpec((1, D_MODEL, tn), lambda k, n: (k, 0, n)),
            pl.BlockSpec((1, 1, tn), lambda k, n: (k, 0, n)),
        ],
        out_specs=pl.BlockSpec((1, 8, tn), lambda k, n: (k, 0, n)),
        out_shape=jax.ShapeDtypeStruct((n_mod, 8, 3 * D_MODEL), F32),
        compiler_params=_params(("parallel", "parallel")),
        name="ada_mod",
    )(c_pad, ada_w.reshape(n_mod, D_MODEL, 3 * D_MODEL), ada_b.reshape(n_mod, 1, 3 * D_MODEL))
    return out.reshape(n_mod * 8, 1, 3 * D_MODEL)


def _rope_table_body(pos_ref, inv_ref, cos_ref, hi_ref, lo_ref):
    ang = pos_ref[0].astype(F32) * inv_ref[...]
    lane = lax.broadcasted_iota(jnp.int32, ang.shape, 1)
    cos = jnp.cos(ang)
    sin = jnp.sin(ang)
    half = QK_ROPE_DIM // 2
    cos_ref[0] = jnp.where(lane < QK_ROPE_DIM, cos, 0.0)
    hi_ref[0] = jnp.where(lane < half, 0.0, jnp.where(lane < QK_ROPE_DIM, sin, 0.0))
    lo_ref[0] = jnp.where(lane < half, -sin, 0.0)


def _rope_tables(positions):
    tm = 512
    half = QK_ROPE_DIM // 2
    inv = ROPE_THETA ** (-jnp.arange(0, QK_ROPE_DIM, 2, dtype=F32) / QK_ROPE_DIM)
    inv_lanes = jnp.concatenate([inv, inv, jnp.zeros((LANES - 2 * half,), F32)]).reshape(1, LANES)
    tab = jax.ShapeDtypeStruct((BATCH, SEQ, LANES), F32)
    spec = pl.BlockSpec((1, tm, LANES), lambda b, i: (b, i, 0))
    return pl.pallas_call(
        _rope_table_body,
        grid=(BATCH, SEQ // tm),
        in_specs=[pl.BlockSpec((1, tm, 1), lambda b, i: (b, i, 0)),
                  pl.BlockSpec((1, LANES), lambda b, i: (0, 0))],
        out_specs=(spec, spec, spec),
        out_shape=(tab, tab, tab),
        compiler_params=_params(("parallel", "parallel")),
        name="rope_tables",
    )(positions.reshape(BATCH, SEQ, 1), inv_lanes)


def _inproj_body(x_ref, sh_ref, sc_ref, wq_ref, wkv_ref, wkr_ref, wu_ref, wv_ref,
                 cq_ref, ckv_ref, kr_ref, zu_ref, zv_ref):
    hm = _modulate(x_ref[0], sh_ref[0], sc_ref[0]).astype(BF16)
    cq_ref[0] = _dot(hm, wq_ref[...])
    ckv_ref[0] = _dot(hm, wkv_ref[...])
    kr_ref[0] = _dot(hm, wkr_ref[...])
    zu_ref[0] = _dot(hm, wu_ref[...])
    zv_ref[0] = _dot(hm, wv_ref[...])


def _inproj(x, mod, k, w_in):
    tm = 512
    a, b, c_, d = (Q_LORA_RANK, Q_LORA_RANK + KV_LORA_RANK,
                   Q_LORA_RANK + KV_LORA_RANK + QK_ROPE_DIM,
                   Q_LORA_RANK + KV_LORA_RANK + QK_ROPE_DIM + SGU_WIDTH)
    wq = w_in[:, :a].astype(BF16)
    wkv = w_in[:, a:b].astype(BF16)
    wkr = jnp.pad(w_in[:, b:c_], ((0, 0), (0, LANES - QK_ROPE_DIM))).astype(BF16)
    wu = w_in[:, c_:d].astype(BF16)
    wv = w_in[:, d:].astype(BF16)
    shift, scale, _ = _mod_specs(k, 2)
    widths = (Q_LORA_RANK, KV_LORA_RANK, LANES, SGU_WIDTH, SGU_WIDTH)
    return pl.pallas_call(
        _inproj_body,
        grid=(BATCH, SEQ // tm),
        in_specs=[pl.BlockSpec((1, tm, D_MODEL), lambda b, i: (b, i, 0)), shift, scale]
                 + [_const_spec((D_MODEL, w)) for w in widths],
        out_specs=tuple(pl.BlockSpec((1, tm, w), lambda b, i: (b, i, 0)) for w in widths),
        out_shape=tuple(jax.ShapeDtypeStruct((BATCH, SEQ, w), F32) for w in widths),
        compiler_params=_params(("parallel", "parallel")),
        name="inproj",
    )(x, mod, mod, wq, wkv, wkr, wu, wv)


def _qkv_body(cq_ref, ckv_ref, kr_ref, cos_ref, hi_ref, lo_ref, qg_ref, kvg_ref,
              wqn_ref, wqr_ref, wkn_ref, wvv_ref, q_ref, k_ref, v_ref):
    cos_t, sin_hi, sin_lo = cos_ref[0], hi_ref[0], lo_ref[0]
    cqn = _rms(cq_ref[0], qg_ref[...]).astype(BF16)
    qn = _dot(cqn, wqn_ref[...])
    qr = _dot(cqn, wqr_ref[...])
    ckvn = _rms(ckv_ref[0], kvg_ref[...]).astype(BF16)
    kn = _dot(ckvn, wkn_ref[...])
    v_ref[0] = _dot(ckvn, wvv_ref[...]).astype(BF16)
    k_rope = _rope(kr_ref[0], cos_t, sin_hi, sin_lo).astype(BF16)
    for h in range(MLA_HEADS):
        nope = slice(h * LANES, (h + 1) * LANES)
        lo = h * HEAD_LANES
        q_ref[0, :, lo:lo + LANES] = qn[:, nope].astype(BF16)
        q_ref[0, :, lo + LANES:lo + HEAD_LANES] = _rope(qr[:, nope], cos_t, sin_hi, sin_lo).astype(BF16)
        k_ref[0, :, lo:lo + LANES] = kn[:, nope].astype(BF16)
        k_ref[0, :, lo + LANES:lo + HEAD_LANES] = k_rope


def _qkv(cq, ckv, kr, tables, q_norm_g, w_q_up, kv_norm_g, w_kv_up):
    tm = 512
    wq = w_q_up.reshape(Q_LORA_RANK, MLA_HEADS, QK_NOPE_DIM + QK_ROPE_DIM)
    wqn = wq[:, :, :QK_NOPE_DIM].reshape(Q_LORA_RANK, MLA_HEADS * LANES).astype(BF16)
    wqr = jnp.pad(wq[:, :, QK_NOPE_DIM:], ((0, 0), (0, 0), (0, LANES - QK_ROPE_DIM)))
    wqr = wqr.reshape(Q_LORA_RANK, MLA_HEADS * LANES).astype(BF16)
    wkv = w_kv_up.reshape(KV_LORA_RANK, MLA_HEADS, QK_NOPE_DIM + V_HEAD_DIM)
    wkn = wkv[:, :, :QK_NOPE_DIM].reshape(KV_LORA_RANK, MLA_HEADS * LANES).astype(BF16)
    wvv = wkv[:, :, QK_NOPE_DIM:].reshape(KV_LORA_RANK, ATTN_WIDTH).astype(BF16)

    def rows(w):
        return pl.BlockSpec((1, tm, w), lambda b, i: (b, i, 0))

    return pl.pallas_call(
        _qkv_body,
        grid=(BATCH, SEQ // tm),
        in_specs=[rows(Q_LORA_RANK), rows(KV_LORA_RANK), rows(LANES),
                  rows(LANES), rows(LANES), rows(LANES),
                  _const_spec((1, Q_LORA_RANK)), _const_spec((1, KV_LORA_RANK)),
                  _const_spec((Q_LORA_RANK, MLA_HEADS * LANES)),
                  _const_spec((Q_LORA_RANK, MLA_HEADS * LANES)),
                  _const_spec((KV_LORA_RANK, MLA_HEADS * LANES)),
                  _const_spec((KV_LORA_RANK, ATTN_WIDTH))],
        out_specs=(rows(QK_WIDTH), rows(QK_WIDTH), rows(ATTN_WIDTH)),
        out_shape=(jax.ShapeDtypeStruct((BATCH, SEQ, QK_WIDTH), BF16),
                   jax.ShapeDtypeStruct((BATCH, SEQ, QK_WIDTH), BF16),
                   jax.ShapeDtypeStruct((BATCH, SEQ, ATTN_WIDTH), BF16)),
        compiler_params=_params(("parallel", "parallel")),
        name="qkv_up",
    )(cq, ckv, kr, *tables, q_norm_g.reshape(1, -1), kv_norm_g.reshape(1, -1), wqn, wqr, wkn, wvv)


def _attn_body(qi_ref, kj_ref, q_ref, k_ref, v_ref, o_ref, m_sc, l_sc, acc_sc, *, tq):
    p = pl.program_id(1)
    i = qi_ref[p]
    j = kj_ref[p]

    @pl.when(j == 0)
    def _():
        m_sc[...] = jnp.full(m_sc.shape, -jnp.inf, F32)
        l_sc[...] = jnp.zeros(l_sc.shape, F32)
        acc_sc[...] = jnp.zeros(acc_sc.shape, F32)

    row = lax.broadcasted_iota(jnp.int32, (tq, tq), 0)
    col = lax.broadcasted_iota(jnp.int32, (tq, tq), 1)
    visible = jnp.logical_or(j < i, col <= row)
    for h in range(MLA_HEADS):
        qk = slice(h * HEAD_LANES, (h + 1) * HEAD_LANES)
        hv = slice(h * V_HEAD_DIM, (h + 1) * V_HEAD_DIM)
        s = lax.dot_general(q_ref[0, :, qk], k_ref[0, :, qk], (((1,), (1,)), ((), ())),
                            preferred_element_type=F32) * SM_SCALE
        s = jnp.where(visible, s, -jnp.inf)
        m_prev = m_sc[h]
        m_new = jnp.maximum(m_prev, jnp.max(s, axis=-1, keepdims=True))
        alpha = jnp.exp(m_prev - m_new)
        pr = jnp.exp(s - m_new)
        l_sc[h] = alpha * l_sc[h] + jnp.sum(pr, axis=-1, keepdims=True)
        acc_sc[:, hv] = alpha * acc_sc[:, hv] + _dot(pr.astype(BF16), v_ref[0, :, hv])
        m_sc[h] = m_new

    @pl.when(j == i)
    def _():
        for h in range(MLA_HEADS):
            hv = slice(h * V_HEAD_DIM, (h + 1) * V_HEAD_DIM)
            o_ref[0, :, hv] = (acc_sc[:, hv] / l_sc[h]).astype(BF16)


def _attention(q, k, v):
    tq = 512
    nb = SEQ // tq
    pairs = [(i, j) for i in range(nb) for j in range(i + 1)]
    qi = jnp.asarray(np.array([p[0] for p in pairs], np.int32))
    kj = jnp.asarray(np.array([p[1] for p in pairs], np.int32))
    grid_spec = pltpu.PrefetchScalarGridSpec(
        num_scalar_prefetch=2,
        grid=(BATCH, len(pairs)),
        in_specs=[pl.BlockSpec((1, tq, QK_WIDTH), lambda b, p, qi, kj: (b, qi[p], 0)),
                  pl.BlockSpec((1, tq, QK_WIDTH), lambda b, p, qi, kj: (b, kj[p], 0)),
                  pl.BlockSpec((1, tq, ATTN_WIDTH), lambda b, p, qi, kj: (b, kj[p], 0))],
        out_specs=pl.BlockSpec((1, tq, ATTN_WIDTH), lambda b, p, qi, kj: (b, qi[p], 0)),
        scratch_shapes=[pltpu.VMEM((MLA_HEADS, tq, 1), F32),
                        pltpu.VMEM((MLA_HEADS, tq, 1), F32),
                        pltpu.VMEM((tq, ATTN_WIDTH), F32)],
    )
    return pl.pallas_call(
        functools.partial(_attn_body, tq=tq),
        grid_spec=grid_spec,
        out_shape=jax.ShapeDtypeStruct((BATCH, SEQ, ATTN_WIDTH), BF16),
        compiler_params=_params(("parallel", "arbitrary")),
        name="mla_attention",
    )(qi, kj, q, k, v)


def _sgu_body(zu_ref, zv_ref, g_ref, b_ref, w_ref, bias_ref, o_ref, *, tm):
    row = lax.broadcasted_iota(jnp.int32, (CHUNK, CHUNK), 0)
    col = lax.broadcasted_iota(jnp.int32, (CHUNK, CHUNK), 1)
    for g in range(SGU_GROUPS):
        gs = slice(g * SGU_GROUP_DIM, (g + 1) * SGU_GROUP_DIM)
        v = jax.nn.gelu(zv_ref[0, :, gs])
        mu = jnp.mean(v, axis=-1, keepdims=True)
        vc = v - mu
        var = jnp.mean(vc * vc, axis=-1, keepdims=True)
        vn = (vc * lax.rsqrt(var + LN_EPS) * g_ref[:, gs] + b_ref[:, gs]).astype(BF16)
        w = jnp.where(col <= row, w_ref[g], 0.0).astype(BF16)
        for n in range(tm // CHUNK):
            rs = slice(n * CHUNK, (n + 1) * CHUNK)
            s = _dot(w, vn[rs]) + bias_ref[:, gs]
            o_ref[0, rs, gs] = (jax.nn.gelu(zu_ref[0, rs, gs]) * s).astype(BF16)


def _sgu(zu, zv, norm_g, norm_b, sgu_w, sgu_b):
    tm = 512
    bias = jnp.repeat(sgu_b.T, SGU_GROUP_DIM, axis=1)
    rows = pl.BlockSpec((1, tm, SGU_WIDTH), lambda b, i: (b, i, 0))
    return pl.pallas_call(
        functools.partial(_sgu_body, tm=tm),
        grid=(BATCH, SEQ // tm),
        in_specs=[rows, rows, _const_spec((1, SGU_WIDTH)), _const_spec((1, SGU_WIDTH)),
                  _const_spec((SGU_GROUPS, CHUNK, CHUNK)), _const_spec((CHUNK, SGU_WIDTH))],
        out_specs=rows,
        out_shape=jax.ShapeDtypeStruct((BATCH, SEQ, SGU_WIDTH), BF16),
        compiler_params=_params(("parallel", "parallel")),
        name="sgu",
    )(zu, zv, norm_g.reshape(1, -1), norm_b.reshape(1, -1), sgu_w, bias)


def _outproj_body(a_ref, s_ref, x_ref, gate_ref, wa_ref, ws_ref, lg_ref, lb_ref, o_ref):
    y = _dot(a_ref[0], wa_ref[...]) + _dot(s_ref[0], ws_ref[...])
    o_ref[0] = _deepnorm(x_ref[0], y, gate_ref[0], lg_ref[...], lb_ref[...])


def _outproj(attn, sgu, x, mod, k, w_out, ln_g, ln_b):
    tm = 512
    wa = w_out[:ATTN_WIDTH].astype(BF16)
    ws = w_out[ATTN_WIDTH:].astype(BF16)
    _, _, gate = _mod_specs(k, 2)

    def rows(w):
        return pl.BlockSpec((1, tm, w), lambda b, i: (b, i, 0))

    return pl.pallas_call(
        _outproj_body,
        grid=(BATCH, SEQ // tm),
        in_specs=[rows(ATTN_WIDTH), rows(SGU_WIDTH), rows(D_MODEL), gate,
                  _const_spec((ATTN_WIDTH, D_MODEL)), _const_spec((SGU_WIDTH, D_MODEL)),
                  _const_spec((1, D_MODEL)), _const_spec((1, D_MODEL))],
        out_specs=rows(D_MODEL),
        out_shape=jax.ShapeDtypeStruct((BATCH, SEQ, D_MODEL), F32),
        compiler_params=_params(("parallel", "parallel")),
        name="outproj_ab",
    )(attn, sgu, x, mod, wa, ws, ln_g.reshape(1, -1), ln_b.reshape(1, -1))


def _ffn_body(x_ref, sh_ref, sc_ref, gate_ref, wg_ref, wu_ref, wd_ref, lg_ref, lb_ref,
              o_ref, hm_sc, *, n_f):
    f = pl.program_id(2)

    @pl.when(f == 0)
    def _():
        hm_sc[...] = _modulate(x_ref[0], sh_ref[0], sc_ref[0]).astype(BF16)

    hm = hm_sc[...]
    h = (jax.nn.silu(_dot(hm, wg_ref[...])) * _dot(hm, wu_ref[...])).astype(BF16)
    part = _dot(h, wd_ref[...])

    @pl.when(f == 0)
    def _():
        o_ref[0] = part

    @pl.when(f > 0)
    def _():
        o_ref[0] += part

    @pl.when(f == n_f - 1)
    def _():
        o_ref[0] = _deepnorm(x_ref[0], o_ref[0], gate_ref[0], lg_ref[...], lb_ref[...])


def _ffn(x, mod, k, w_gate, w_up, w_down, ln_g, ln_b):
    tm, tf = 1024, 512
    n_f = D_FF // tf
    shift, scale, gate = _mod_specs(k, 3)
    rows = pl.BlockSpec((1, tm, D_MODEL), lambda b, i, f: (b, i, 0))
    rows_in = pl.BlockSpec((1, tm, D_MODEL), lambda b, i, f: (b, i, 0), pipeline_mode=pl.Buffered(1))
    vec = pl.BlockSpec((1, D_MODEL), lambda b, i, f: (0, 0))
    return pl.pallas_call(
        functools.partial(_ffn_body, n_f=n_f),
        grid=(BATCH, SEQ // tm, n_f),
        in_specs=[rows_in, shift, scale, gate,
                  pl.BlockSpec((D_MODEL, tf), lambda b, i, f: (0, f)),
                  pl.BlockSpec((D_MODEL, tf), lambda b, i, f: (0, f)),
                  pl.BlockSpec((tf, D_MODEL), lambda b, i, f: (f, 0)),
                  vec, vec],
        out_specs=rows,
        out_shape=jax.ShapeDtypeStruct((BATCH, SEQ, D_MODEL), F32),
        scratch_shapes=[pltpu.VMEM((tm, D_MODEL), BF16)],
        compiler_params=_params(("parallel", "parallel", "arbitrary")),
        name="ffn_swiglu",
    )(x, mod, mod, mod, w_gate.astype(BF16), w_up.astype(BF16), w_down.astype(BF16),
      ln_g.reshape(1, -1), ln_b.reshape(1, -1))


def _pool_body(x_ref, halo_ref, sh_ref, sc_ref, gate_ref, pw_ref, ps_ref, wo_ref, lg_ref, lb_ref,
               o_ref, *, tm):
    i = pl.program_id(1)
    x = x_ref[0]
    hm = _modulate(x, sh_ref[0], sc_ref[0])
    halo = jnp.where(i > 0, _modulate(halo_ref[0], sh_ref[0], sc_ref[0]), 0.0)
    ext = jnp.concatenate([halo, hm], axis=0)
    ext_hi = ext.astype(BF16)
    ext_lo = (ext - ext_hi.astype(F32)).astype(BF16)
    t_loc = lax.broadcasted_iota(jnp.int32, (tm, tm + POOL_HALO), 0)
    s_loc = lax.broadcasted_iota(jnp.int32, (tm, tm + POOL_HALO), 1)
    lag = t_loc + POOL_HALO - s_loc
    t_glob = i * tm + lax.broadcasted_iota(jnp.int32, (tm, 1), 0)
    outs = []
    for gi, w in enumerate(POOL_WINDOWS):
        cs = slice(gi * POOL_GROUP_DIM, (gi + 1) * POOL_GROUP_DIM)
        band = jnp.where(lag >= 0, jnp.where(lag < w, 1.0, 0.0), 0.0).astype(BF16)
        wsum = _dot(band, ext_hi[:, cs]) + _dot(band, ext_lo[:, cs])
        count = jnp.minimum(t_glob + 1, w).astype(F32)
        d = (wsum / count - hm[:, cs]).astype(BF16)
        outs.append(_dot(d, pw_ref[gi]))
    pooled = (jnp.concatenate(outs, axis=-1) * ps_ref[...]).astype(BF16)
    y = _dot(pooled, wo_ref[...])
    o_ref[0] = _deepnorm(x, y, gate_ref[0], lg_ref[...], lb_ref[...])


def _pool_mixer(x, mod, k, pool_w, pool_scale, w_out_c, ln_g, ln_b):
    tm = 256
    per = tm // POOL_HALO
    shift, scale, gate = _mod_specs(k, 2)
    rows = pl.BlockSpec((1, tm, D_MODEL), lambda b, i: (b, i, 0))
    halo = pl.BlockSpec((1, POOL_HALO, D_MODEL), lambda b, i: (b, jnp.maximum(i * per - 1, 0), 0))
    return pl.pallas_call(
        functools.partial(_pool_body, tm=tm),
        grid=(BATCH, SEQ // tm),
        in_specs=[rows, halo, shift, scale, gate,
                  _const_spec((len(POOL_WINDOWS), POOL_GROUP_DIM, POOL_GROUP_DIM)),
                  _const_spec((1, D_MODEL)), _const_spec((D_MODEL, D_MODEL)),
                  _const_spec((1, D_MODEL)), _const_spec((1, D_MODEL))],
        out_specs=rows,
        out_shape=jax.ShapeDtypeStruct((BATCH, SEQ, D_MODEL), F32),
        compiler_params=_params(("parallel", "parallel")),
        name="pool_mixer",
    )(x, x, mod, mod, mod, pool_w.astype(BF16), pool_scale.reshape(1, -1),
      w_out_c.astype(BF16), ln_g.reshape(1, -1), ln_b.reshape(1, -1))


def _router_body(x_ref, sh_ref, sc_ref, wr_ref, g_ref):
    hm = _modulate(x_ref[0], sh_ref[0], sc_ref[0])
    hm_hi = hm.astype(BF16)
    hm_lo = (hm - hm_hi.astype(F32)).astype(BF16)
    w = wr_ref[...]
    w_hi = w.astype(BF16)
    w_lo = (w - w_hi.astype(F32)).astype(BF16)
    logits = _dot(hm_hi, w_hi) + (_dot(hm_lo, w_hi) + _dot(hm_hi, w_lo))
    lane = lax.broadcasted_iota(jnp.int32, logits.shape, 1).astype(F32)
    logits = jnp.where(lane < N_EXPERTS, logits, -jnp.inf)
    v1 = jnp.max(logits, axis=-1, keepdims=True)
    i1 = jnp.min(jnp.where(logits == v1, lane, float(LANES)), axis=-1, keepdims=True)
    rest = jnp.where(lane == i1, -jnp.inf, logits)
    v2 = jnp.max(rest, axis=-1, keepdims=True)
    i2 = jnp.min(jnp.where(rest == v2, lane, float(LANES)), axis=-1, keepdims=True)
    e2 = jnp.exp(v2 - v1)
    den = 1.0 + e2
    g_ref[0] = jnp.where(lane == i1, 1.0 / den, 0.0) + jnp.where(lane == i2, e2 / den, 0.0)


def _router(x, mod, k, w_router):
    tm = 512
    shift, scale, _ = _mod_specs(k, 2)
    wr = jnp.pad(w_router, ((0, 0), (0, LANES - N_EXPERTS)))
    return pl.pallas_call(
        _router_body,
        grid=(BATCH, SEQ // tm),
        in_specs=[pl.BlockSpec((1, tm, D_MODEL), lambda b, i: (b, i, 0)), shift, scale,
                  _const_spec((D_MODEL, LANES))],
        out_specs=pl.BlockSpec((1, tm, LANES), lambda b, i: (b, i, 0)),
        out_shape=jax.ShapeDtypeStruct((BATCH, SEQ, LANES), F32),
        compiler_params=_params(("parallel", "parallel")),
        name="router",
    )(x, mod, mod, wr)


def _moe_body(x_ref, g_ref, sh_ref, sc_ref, gate_ref, wg_ref, wu_ref, wd_ref, lg_ref, lb_ref,
              o_ref, hm_sc, acc_sc, *, n_f):
    e = pl.program_id(2)
    f = pl.program_id(3)

    @pl.when(jnp.logical_and(e == 0, f == 0))
    def _():
        hm_sc[...] = _modulate(x_ref[0], sh_ref[0], sc_ref[0]).astype(BF16)
        o_ref[0] = jnp.zeros(o_ref.shape[1:], F32)

    hm = hm_sc[...]
    h = (jax.nn.silu(_dot(hm, wg_ref[0])) * _dot(hm, wu_ref[0])).astype(BF16)
    part = _dot(h, wd_ref[0])

    @pl.when(f == 0)
    def _():
        acc_sc[...] = part

    @pl.when(f > 0)
    def _():
        acc_sc[...] += part

    @pl.when(f == n_f - 1)
    def _():
        lane = lax.broadcasted_iota(jnp.int32, g_ref.shape[1:], 1)
        g_e = jnp.sum(jnp.where(lane == e, g_ref[0], 0.0), axis=-1, keepdims=True)
        o_ref[0] += g_e * acc_sc[...]

    @pl.when(jnp.logical_and(e == N_EXPERTS - 1, f == n_f - 1))
    def _():
        o_ref[0] = _deepnorm(x_ref[0], o_ref[0], gate_ref[0], lg_ref[...], lb_ref[...])


def _moe(x, gates, mod, k, w_gate, w_up, w_down, ln_g, ln_b):
    tm, tf = 1024, 256
    n_f = D_FF_EXPERT // tf
    shift, scale, gate = _mod_specs(k, 4)
    rows = pl.BlockSpec((1, tm, D_MODEL), lambda b, i, e, f: (b, i, 0))
    rows_in = pl.BlockSpec((1, tm, D_MODEL), lambda b, i, e, f: (b, i, 0), pipeline_mode=pl.Buffered(1))
    vec = pl.BlockSpec((1, D_MODEL), lambda b, i, e, f: (0, 0))
    return pl.pallas_call(
        functools.partial(_moe_body, n_f=n_f),
        grid=(BATCH, SEQ // tm, N_EXPERTS, n_f),
        in_specs=[rows_in, pl.BlockSpec((1, tm, LANES), lambda b, i, e, f: (b, i, 0)),
                  shift, scale, gate,
                  pl.BlockSpec((1, D_MODEL, tf), lambda b, i, e, f: (e, 0, f)),
                  pl.BlockSpec((1, D_MODEL, tf), lambda b, i, e, f: (e, 0, f)),
                  pl.BlockSpec((1, tf, D_MODEL), lambda b, i, e, f: (e, f, 0)),
                  vec, vec],
        out_specs=rows,
        out_shape=jax.ShapeDtypeStruct((BATCH, SEQ, D_MODEL), F32),
        scratch_shapes=[pltpu.VMEM((tm, D_MODEL), BF16), pltpu.VMEM((tm, D_MODEL), F32)],
        compiler_params=_params(("parallel", "parallel", "arbitrary", "arbitrary")),
        name="moe_dense",
    )(x, gates, mod, mod, mod, w_gate.astype(BF16), w_up.astype(BF16), w_down.astype(BF16),
      ln_g.reshape(1, -1), ln_b.reshape(1, -1))


def kernel(x, c, positions, ada_w, ada_b, ln_g, ln_b, w_in_ab, q_norm_g, w_q_up, kv_norm_g, w_kv_up, sgu_norm_g, sgu_norm_b, sgu_w, sgu_b, w_out_ab, ffn_w_gate, ffn_w_up, ffn_w_down, pool_w, pool_scale, w_out_c, router_w, moe_w_gate, moe_w_up, moe_w_down):
    mod = _ada_modulation(c, ada_w, ada_b)
    tables = _rope_tables(positions)
    for l in range(DEPTH):
        j = l // 2
        k_tok, k_ch = 2 * l, 2 * l + 1
        if l % 2 == 0:
            cq, ckv, kr, zu, zv = _inproj(x, mod, k_tok, w_in_ab[j])
            q, k, v = _qkv(cq, ckv, kr, tables, q_norm_g[j], w_q_up[j], kv_norm_g[j], w_kv_up[j])
            attn = _attention(q, k, v)
            sgu = _sgu(zu, zv, sgu_norm_g[j], sgu_norm_b[j], sgu_w[j], sgu_b[j])
            x = _outproj(attn, sgu, x, mod, k_tok, w_out_ab[j], ln_g[l, 0], ln_b[l, 0])
            x = _ffn(x, mod, k_ch, ffn_w_gate[j], ffn_w_up[j], ffn_w_down[j], ln_g[l, 1], ln_b[l, 1])
        else:
            x = _pool_mixer(x, mod, k_tok, pool_w[j], pool_scale[j], w_out_c[j], ln_g[l, 0], ln_b[l, 0])
            gates = _router(x, mod, k_ch, router_w[j])
            x = _moe(x, gates, mod, k_ch, moe_w_gate[j], moe_w_up[j], moe_w_down[j],
                     ln_g[l, 1], ln_b[l, 1])
    return x
```

```python
import functools

import jax
import jax.numpy as jnp
import numpy as np
from jax import lax
from jax.experimental import pallas as pl
from jax.experimental.pallas import tpu as pltpu

F32 = jnp.float32
BF16 = jnp.bfloat16

D_MODEL = 2048
BATCH = 2
SEQ = 4096
DEPTH = 4
MLA_HEADS = 8
QK_NOPE_DIM = 128
QK_ROPE_DIM = 64
V_HEAD_DIM = 128
Q_LORA_RANK = 768
KV_LORA_RANK = 512
ROPE_THETA = 10000.0
SGU_GROUPS = 8
SGU_GROUP_DIM = 128
SGU_WIDTH = SGU_GROUPS * SGU_GROUP_DIM
CHUNK = 128
ATTN_WIDTH = MLA_HEADS * V_HEAD_DIM
POOL_WINDOWS = (2, 4, 8, 16)
POOL_GROUP_DIM = D_MODEL // 4
D_FF = 5632
N_EXPERTS = 8
D_FF_EXPERT = 2816
DEEPNORM_ALPHA = (2 * DEPTH) ** 0.25
LN_EPS = 1e-5
RMS_EPS = 1e-6
SM_SCALE = (QK_NOPE_DIM + QK_ROPE_DIM) ** -0.5

LANES = 128
HEAD_LANES = 2 * LANES
QK_WIDTH = MLA_HEADS * HEAD_LANES
VMEM_LIMIT = 60 * 1024 * 1024
POOL_HALO = 128


def _params(semantics):
    return pltpu.CompilerParams(dimension_semantics=semantics, vmem_limit_bytes=VMEM_LIMIT)


def _const_spec(shape):
    nd = len(shape)
    return pl.BlockSpec(shape, lambda *_: (0,) * nd, pipeline_mode=pl.Buffered(1))


def _dot(a, b):
    return jnp.dot(a, b, preferred_element_type=F32)


def _modulate(x, shift, scale):
    return x * (1.0 + scale) + shift


def _deepnorm(x, y, gate, g, b):
    r = DEEPNORM_ALPHA * x + (1.0 + gate) * y
    mu = jnp.mean(r, axis=-1, keepdims=True)
    rc = r - mu
    var = jnp.mean(rc * rc, axis=-1, keepdims=True)
    return rc * lax.rsqrt(var + LN_EPS) * g + b


def _rms(x, g):
    return x * lax.rsqrt(jnp.mean(x * x, axis=-1, keepdims=True) + RMS_EPS) * g


def _rope(r, cos_t, sin_hi, sin_lo):
    return (r * cos_t + pltpu.roll(r, QK_ROPE_DIM // 2, 1) * sin_hi
            + pltpu.roll(r, LANES - QK_ROPE_DIM // 2, 1) * sin_lo)


def _mod_specs(k, n_grid):
    def spec(part):
        if n_grid == 2:
            return pl.BlockSpec((1, 1, D_MODEL), lambda b, i: (k * 8 + b, 0, part))
        if n_grid == 3:
            return pl.BlockSpec((1, 1, D_MODEL), lambda b, i, f: (k * 8 + b, 0, part))
        return pl.BlockSpec((1, 1, D_MODEL), lambda b, i, e, f: (k * 8 + b, 0, part))
    return spec(0), spec(1), spec(2)


def _ada_body(c_ref, w_ref, b_ref, o_ref):
    s = jax.nn.silu(c_ref[...]).astype(BF16)
    o_ref[0] = _dot(s, w_ref[0].astype(BF16)) + b_ref[0]


def _ada_modulation(c, ada_w, ada_b):
    n_mod = 2 * DEPTH
    tn = 1024
    c_pad = jnp.pad(c, ((0, 8 - BATCH), (0, 0)))
    out = pl.pallas_call(
        _ada_body,
        grid=(n_mod, 3 * D_MODEL // tn),
        in_specs=[
            pl.BlockSpec((8, D_MODEL), lambda k, n: (0, 0)),
            pl.BlockSpec((1, D_MODEL, tn), lambda k, n: (k, 0, n)),
            pl.BlockSpec((1, 1, tn), lambda k, n: (k, 0, n)),
        ],
        out_specs=pl.BlockSpec((1, 8, tn), lambda k, n: (k, 0, n)),
        out_shape=jax.ShapeDtypeStruct((n_mod, 8, 3 * D_MODEL), F32),
        compiler_params=_params(("parallel", "parallel")),
        name="ada_mod",
    )(c_pad, ada_w.reshape(n_mod, D_MODEL, 3 * D_MODEL), ada_b.reshape(n_mod, 1, 3 * D_MODEL))
    return out.reshape(n_mod * 8, 1, 3 * D_MODEL)


def _rope_table_body(pos_ref, inv_ref, cos_ref, hi_ref, lo_ref):
    ang = pos_ref[0].astype(F32) * inv_ref[...]
    lane = lax.broadcasted_iota(jnp.int32, ang.shape, 1)
    cos = jnp.cos(ang)
    sin = jnp.sin(ang)
    half = QK_ROPE_DIM // 2
    cos_ref[0] = jnp.where(lane < QK_ROPE_DIM, cos, 0.0)
    hi_ref[0] = jnp.where(lane < half, 0.0, jnp.where(lane < QK_ROPE_DIM, sin, 0.0))
    lo_ref[0] = jnp.where(lane < half, -sin, 0.0)


def _rope_tables(positions):
    tm = 512
    half = QK_ROPE_DIM // 2
    inv = ROPE_THETA ** (-jnp.arange(0, QK_ROPE_DIM, 2, dtype=F32) / QK_ROPE_DIM)
    inv_lanes = jnp.concatenate([inv, inv, jnp.zeros((LANES - 2 * half,), F32)]).reshape(1, LANES)
    tab = jax.ShapeDtypeStruct((BATCH, SEQ, LANES), F32)
    spec = pl.BlockSpec((1, tm, LANES), lambda b, i: (b, i, 0))
    return pl.pallas_call(
        _rope_table_body,
        grid=(BATCH, SEQ // tm),
        in_specs=[pl.BlockSpec((1, tm, 1), lambda b, i: (b, i, 0)),
                  pl.BlockSpec((1, LANES), lambda b, i: (0, 0))],
        out_specs=(spec, spec, spec),
        out_shape=(tab, tab, tab),
        compiler_params=_params(("parallel", "parallel")),
        name="rope_tables",
    )(positions.reshape(BATCH, SEQ, 1), inv_lanes)


def _inproj_body(x_ref, sh_ref, sc_ref, wq_ref, wkv_ref, wkr_ref, wu_ref, wv_ref,
                 cq_ref, ckv_ref, kr_ref, zu_ref, zv_ref):
    hm = _modulate(x_ref[0], sh_ref[0], sc_ref[0]).astype(BF16)
    cq_ref[0] = _dot(hm, wq_ref[...])
    ckv_ref[0] = _dot(hm, wkv_ref[...])
    kr_ref[0] = _dot(hm, wkr_ref[...])
    zu_ref[0] = _dot(hm, wu_ref[...])
    zv_ref[0] = _dot(hm, wv_ref[...])


def _inproj(x, mod, k, w_in):
    tm = 512
    a, b, c_, d = (Q_LORA_RANK, Q_LORA_RANK + KV_LORA_RANK,
                   Q_LORA_RANK + KV_LORA_RANK + QK_ROPE_DIM,
                   Q_LORA_RANK + KV_LORA_RANK + QK_ROPE_DIM + SGU_WIDTH)
    wq = w_in[:, :a].astype(BF16)
    wkv = w_in[:, a:b].astype(BF16)
    wkr = jnp.pad(w_in[:, b:c_], ((0, 0), (0, LANES - QK_ROPE_DIM))).astype(BF16)
    wu = w_in[:, c_:d].astype(BF16)
    wv = w_in[:, d:].astype(BF16)
    shift, scale, _ = _mod_specs(k, 2)
    widths = (Q_LORA_RANK, KV_LORA_RANK, LANES, SGU_WIDTH, SGU_WIDTH)
    return pl.pallas_call(
        _inproj_body,
        grid=(BATCH, SEQ // tm),
        in_specs=[pl.BlockSpec((1, tm, D_MODEL), lambda b, i: (b, i, 0)), shift, scale]
                 + [_const_spec((D_MODEL, w)) for w in widths],
        out_specs=tuple(pl.BlockSpec((1, tm, w), lambda b, i: (b, i, 0)) for w in widths),
        out_shape=tuple(jax.ShapeDtypeStruct((BATCH, SEQ, w), F32) for w in widths),
        compiler_params=_params(("parallel", "parallel")),
        name="inproj",
    )(x, mod, mod, wq, wkv, wkr, wu, wv)


def _qkv_body(cq_ref, ckv_ref, kr_ref, cos_ref, hi_ref, lo_ref, qg_ref, kvg_ref,
              wqn_ref, wqr_ref, wkn_ref, wvv_ref, q_ref, k_ref, v_ref):
    cos_t, sin_hi, sin_lo = cos_ref[0], hi_ref[0], lo_ref[0]
    cqn = _rms(cq_ref[0], qg_ref[...]).astype(BF16)
    qn = _dot(cqn, wqn_ref[...])
    qr = _dot(cqn, wqr_ref[...])
    ckvn = _rms(ckv_ref[0], kvg_ref[...]).astype(BF16)
    kn = _dot(ckvn, wkn_ref[...])
    v_ref[0] = _dot(ckvn, wvv_ref[...]).astype(BF16)
    k_rope = _rope(kr_ref[0], cos_t, sin_hi, sin_lo).astype(BF16)
    for h in range(MLA_HEADS):
        nope = slice(h * LANES, (h + 1) * LANES)
        lo = h * HEAD_LANES
        q_ref[0, :, lo:lo + LANES] = qn[:, nope].astype(BF16)
        q_ref[0, :, lo + LANES:lo + HEAD_LANES] = _rope(qr[:, nope], cos_t, sin_hi, sin_lo).astype(BF16)
        k_ref[0, :, lo:lo + LANES] = kn[:, nope].astype(BF16)
        k_ref[0, :, lo + LANES:lo + HEAD_LANES] = k_rope


def _qkv(cq, ckv, kr, tables, q_norm_g, w_q_up, kv_norm_g, w_kv_up):
    tm = 512
    wq = w_q_up.reshape(Q_LORA_RANK, MLA_HEADS, QK_NOPE_DIM + QK_ROPE_DIM)
    wqn = wq[:, :, :QK_NOPE_DIM].reshape(Q_LORA_RANK, MLA_HEADS * LANES).astype(BF16)
    wqr = jnp.pad(wq[:, :, QK_NOPE_DIM:], ((0, 0), (0, 0), (0, LANES - QK_ROPE_DIM)))
    wqr = wqr.reshape(Q_LORA_RANK, MLA_HEADS * LANES).astype(BF16)
    wkv = w_kv_up.reshape(KV_LORA_RANK, MLA_HEADS, QK_NOPE_DIM + V_HEAD_DIM)
    wkn = wkv[:, :, :QK_NOPE_DIM].reshape(KV_LORA_RANK, MLA_HEADS * LANES).astype(BF16)
    wvv = wkv[:, :, QK_NOPE_DIM:].reshape(KV_LORA_RANK, ATTN_WIDTH).astype(BF16)

    def rows(w):
        return pl.BlockSpec((1, tm, w), lambda b, i: (b, i, 0))

    return pl.pallas_call(
        _qkv_body,
        grid=(BATCH, SEQ // tm),
        in_specs=[rows(Q_LORA_RANK), rows(KV_LORA_RANK), rows(LANES),
                  rows(LANES), rows(LANES), rows(LANES),
                  _const_spec((1, Q_LORA_RANK)), _const_spec((1, KV_LORA_RANK)),
                  _const_spec((Q_LORA_RANK, MLA_HEADS * LANES)),
                  _const_spec((Q_LORA_RANK, MLA_HEADS * LANES)),
                  _const_spec((KV_LORA_RANK, MLA_HEADS * LANES)),
                  _const_spec((KV_LORA_RANK, ATTN_WIDTH))],
        out_specs=(rows(QK_WIDTH), rows(QK_WIDTH), rows(ATTN_WIDTH)),
        out_shape=(jax.ShapeDtypeStruct((BATCH, SEQ, QK_WIDTH), BF16),
                   jax.ShapeDtypeStruct((BATCH, SEQ, QK_WIDTH), BF16),
                   jax.ShapeDtypeStruct((BATCH, SEQ, ATTN_WIDTH), BF16)),
        compiler_params=_params(("parallel", "parallel")),
        name="qkv_up",
    )(cq, ckv, kr, *tables, q_norm_g.reshape(1, -1), kv_norm_g.reshape(1, -1), wqn, wqr, wkn, wvv)


def _attn_body(qi_ref, kj_ref, q_ref, k_ref, v_ref, o_ref, m_sc, l_sc, acc_sc, *, tq):
    p = pl.program_id(1)
    i = qi_ref[p]
    j = kj_ref[p]

    @pl.when(j == 0)
    def _():
        m_sc[...] = jnp.full(m_sc.shape, -jnp.inf, F32)
        l_sc[...] = jnp.zeros(l_sc.shape, F32)
        acc_sc[...] = jnp.zeros(acc_sc.shape, F32)

    row = lax.broadcasted_iota(jnp.int32, (tq, tq), 0)
    col = lax.broadcasted_iota(jnp.int32, (tq, tq), 1)
    visible = jnp.logical_or(j < i, col <= row)
    for h in range(MLA_HEADS):
        qk = slice(h * HEAD_LANES, (h + 1) * HEAD_LANES)
        hv = slice(h * V_HEAD_DIM, (h + 1) * V_HEAD_DIM)
        s = lax.dot_general(q_ref[0, :, qk], k_ref[0, :, qk], (((1,), (1,)), ((), ())),
                            preferred_element_type=F32) * SM_SCALE
        s = jnp.where(visible, s, -jnp.inf)
        m_prev = m_sc[h]
        m_new = jnp.maximum(m_prev, jnp.max(s, axis=-1, keepdims=True))
        alpha = jnp.exp(m_prev - m_new)
        pr = jnp.exp(s - m_new)
        l_sc[h] = alpha * l_sc[h] + jnp.sum(pr, axis=-1, keepdims=True)
        acc_sc[:, hv] = alpha * acc_sc[:, hv] + _dot(pr.astype(BF16), v_ref[0, :, hv])
        m_sc[h] = m_new

    @pl.when(j == i)
    def _():
        for h in range(MLA_HEADS):
            hv = slice(h * V_HEAD_DIM, (h + 1) * V_HEAD_DIM)
            o_ref[0, :, hv] = (acc_sc[:, hv] / l_sc[h]).astype(BF16)


def _attention(q, k, v):
    tq = 512
    nb = SEQ // tq
    pairs = [(i, j) for i in range(nb) for j in range(i + 1)]
    qi = jnp.asarray(np.array([p[0] for p in pairs], np.int32))
    kj = jnp.asarray(np.array([p[1] for p in pairs], np.int32))
    grid_spec = pltpu.PrefetchScalarGridSpec(
        num_scalar_prefetch=2,
        grid=(BATCH, len(pairs)),
        in_specs=[pl.BlockSpec((1, tq, QK_WIDTH), lambda b, p, qi, kj: (b, qi[p], 0)),
                  pl.BlockSpec((1, tq, QK_WIDTH), lambda b, p, qi, kj: (b, kj[p], 0)),
                  pl.BlockSpec((1, tq, ATTN_WIDTH), lambda b, p, qi, kj: (b, kj[p], 0))],
        out_specs=pl.BlockSpec((1, tq, ATTN_WIDTH), lambda b, p, qi, kj: (b, qi[p], 0)),
        scratch_shapes=[pltpu.VMEM((MLA_HEADS, tq, 1), F32),
                        pltpu.VMEM((MLA_HEADS, tq, 1), F32),
                        pltpu.VMEM((tq, ATTN_WIDTH), F32)],
    )
    return pl.pallas_call(
        functools.partial(_attn_body, tq=tq),
        grid_spec=grid_spec,
        out_shape=jax.ShapeDtypeStruct((BATCH, SEQ, ATTN_WIDTH), BF16),
        compiler_params=_params(("parallel", "arbitrary")),
        name="mla_attention",
    )(qi, kj, q, k, v)


def _sgu_body(zu_ref, zv_ref, g_ref, b_ref, w_ref, bias_ref, o_ref, *, tm):
    row = lax.broadcasted_iota(jnp.int32, (CHUNK, CHUNK), 0)
    col = lax.broadcasted_iota(jnp.int32, (CHUNK, CHUNK), 1)
    for g in range(SGU_GROUPS):
        gs = slice(g * SGU_GROUP_DIM, (g + 1) * SGU_GROUP_DIM)
        v = jax.nn.gelu(zv_ref[0, :, gs])
        mu = jnp.mean(v, axis=-1, keepdims=True)
        vc = v - mu
        var = jnp.mean(vc * vc, axis=-1, keepdims=True)
        vn = (vc * lax.rsqrt(var + LN_EPS) * g_ref[:, gs] + b_ref[:, gs]).astype(BF16)
        w = jnp.where(col <= row, w_ref[g], 0.0).astype(BF16)
        for n in range(tm // CHUNK):
            rs = slice(n * CHUNK, (n + 1) * CHUNK)
            s = _dot(w, vn[rs]) + bias_ref[:, gs]
            o_ref[0, rs, gs] = (jax.nn.gelu(zu_ref[0, rs, gs]) * s).astype(BF16)


def _sgu(zu, zv, norm_g, norm_b, sgu_w, sgu_b):
    tm = 512
    bias = jnp.repeat(sgu_b.T, SGU_GROUP_DIM, axis=1)
    rows = pl.BlockSpec((1, tm, SGU_WIDTH), lambda b, i: (b, i, 0))
    return pl.pallas_call(
        functools.partial(_sgu_body, tm=tm),
        grid=(BATCH, SEQ // tm),
        in_specs=[rows, rows, _const_spec((1, SGU_WIDTH)), _const_spec((1, SGU_WIDTH)),
                  _const_spec((SGU_GROUPS, CHUNK, CHUNK)), _const_spec((CHUNK, SGU_WIDTH))],
        out_specs=rows,
        out_shape=jax.ShapeDtypeStruct((BATCH, SEQ, SGU_WIDTH), BF16),
        compiler_params=_params(("parallel", "parallel")),
        name="sgu",
    )(zu, zv, norm_g.reshape(1, -1), norm_b.reshape(1, -1), sgu_w, bias)


def _outproj_body(a_ref, s_ref, x_ref, gate_ref, wa_ref, ws_ref, lg_ref, lb_ref, o_ref):
    y = _dot(a_ref[0], wa_ref[...]) + _dot(s_ref[0], ws_ref[...])
    o_ref[0] = _deepnorm(x_ref[0], y, gate_ref[0], lg_ref[...], lb_ref[...])


def _outproj(attn, sgu, x, mod, k, w_out, ln_g, ln_b):
    tm = 512
    wa = w_out[:ATTN_WIDTH].astype(BF16)
    ws = w_out[ATTN_WIDTH:].astype(BF16)
    _, _, gate = _mod_specs(k, 2)

    def rows(w):
        return pl.BlockSpec((1, tm, w), lambda b, i: (b, i, 0))

    return pl.pallas_call(
        _outproj_body,
        grid=(BATCH, SEQ // tm),
        in_specs=[rows(ATTN_WIDTH), rows(SGU_WIDTH), rows(D_MODEL), gate,
                  _const_spec((ATTN_WIDTH, D_MODEL)), _const_spec((SGU_WIDTH, D_MODEL)),
                  _const_spec((1, D_MODEL)), _const_spec((1, D_MODEL))],
        out_specs=rows(D_MODEL),
        out_shape=jax.ShapeDtypeStruct((BATCH, SEQ, D_MODEL), F32),
        compiler_params=_params(("parallel", "parallel")),
        name="outproj_ab",
    )(attn, sgu, x, mod, wa, ws, ln_g.reshape(1, -1), ln_b.reshape(1, -1))


def _ffn_body(x_ref, sh_ref, sc_ref, gate_ref, wg_ref, wu_ref, wd_ref, lg_ref, lb_ref,
              o_ref, hm_sc, *, n_f):
    f = pl.program_id(2)

    @pl.when(f == 0)
    def _():
        hm_sc[...] = _modulate(x_ref[0], sh_ref[0], sc_ref[0]).astype(BF16)

    hm = hm_sc[...]
    h = (jax.nn.silu(_dot(hm, wg_ref[...])) * _dot(hm, wu_ref[...])).astype(BF16)
    part = _dot(h, wd_ref[...])

    @pl.when(f == 0)
    def _():
        o_ref[0] = part

    @pl.when(f > 0)
    def _():
        o_ref[0] += part

    @pl.when(f == n_f - 1)
    def _():
        o_ref[0] = _deepnorm(x_ref[0], o_ref[0], gate_ref[0], lg_ref[...], lb_ref[...])


def _ffn(x, mod, k, w_gate, w_up, w_down, ln_g, ln_b):
    tm, tf = 1024, 512
    n_f = D_FF // tf
    shift, scale, gate = _mod_specs(k, 3)
    rows = pl.BlockSpec((1, tm, D_MODEL), lambda b, i, f: (b, i, 0))
    rows_in = pl.BlockSpec((1, tm, D_MODEL), lambda b, i, f: (b, i, 0), pipeline_mode=pl.Buffered(1))
    vec = pl.BlockSpec((1, D_MODEL), lambda b, i, f: (0, 0))
    return pl.pallas_call(
        functools.partial(_ffn_body, n_f=n_f),
        grid=(BATCH, SEQ // tm, n_f),
        in_specs=[rows_in, shift, scale, gate,
                  pl.BlockSpec((D_MODEL, tf), lambda b, i, f: (0, f)),
                  pl.BlockSpec((D_MODEL, tf), lambda b, i, f: (0, f)),
                  pl.BlockSpec((tf, D_MODEL), lambda b, i, f: (f, 0)),
                  vec, vec],
        out_specs=rows,
        out_shape=jax.ShapeDtypeStruct((BATCH, SEQ, D_MODEL), F32),
        scratch_shapes=[pltpu.VMEM((tm, D_MODEL), BF16)],
        compiler_params=_params(("parallel", "parallel", "arbitrary")),
        name="ffn_swiglu",
    )(x, mod, mod, mod, w_gate.astype(BF16), w_up.astype(BF16), w_down.astype(BF16),
      ln_g.reshape(1, -1), ln_b.reshape(1, -1))


def _pool_body(x_ref, halo_ref, sh_ref, sc_ref, gate_ref, pw_ref, ps_ref, wo_ref, lg_ref, lb_ref,
               o_ref, *, tm):
    i = pl.program_id(1)
    x = x_ref[0]
    hm = _modulate(x, sh_ref[0], sc_ref[0])
    halo = jnp.where(i > 0, _modulate(halo_ref[0], sh_ref[0], sc_ref[0]), 0.0)
    ext = jnp.concatenate([halo, hm], axis=0)
    ext_hi = ext.astype(BF16)
    ext_lo = (ext - ext_hi.astype(F32)).astype(BF16)
    t_loc = lax.broadcasted_iota(jnp.int32, (tm, tm + POOL_HALO), 0)
    s_loc = lax.broadcasted_iota(jnp.int32, (tm, tm + POOL_HALO), 1)
    lag = t_loc + POOL_HALO - s_loc
    t_glob = i * tm + lax.broadcasted_iota(jnp.int32, (tm, 1), 0)
    outs = []
    for gi, w in enumerate(POOL_WINDOWS):
        cs = slice(gi * POOL_GROUP_DIM, (gi + 1) * POOL_GROUP_DIM)
        band = jnp.where(lag >= 0, jnp.where(lag < w, 1.0, 0.0), 0.0).astype(BF16)
        wsum = _dot(band, ext_hi[:, cs]) + _dot(band, ext_lo[:, cs])
        count = jnp.minimum(t_glob + 1, w).astype(F32)
        d = (wsum / count - hm[:, cs]).astype(BF16)
        outs.append(_dot(d, pw_ref[gi]))
    pooled = (jnp.concatenate(outs, axis=-1) * ps_ref[...]).astype(BF16)
    y = _dot(pooled, wo_ref[...])
    o_ref[0] = _deepnorm(x, y, gate_ref[0], lg_ref[...], lb_ref[...])


def _pool_mixer(x, mod, k, pool_w, pool_scale, w_out_c, ln_g, ln_b):
    tm = 256
    per = tm // POOL_HALO
    shift, scale, gate = _mod_specs(k, 2)
    rows = pl.BlockSpec((1, tm, D_MODEL), lambda b, i: (b, i, 0))
    halo = pl.BlockSpec((1, POOL_HALO, D_MODEL), lambda b, i: (b, jnp.maximum(i * per - 1, 0), 0))
    return pl.pallas_call(
        functools.partial(_pool_body, tm=tm),
        grid=(BATCH, SEQ // tm),
        in_specs=[rows, halo, shift, scale, gate,
                  _const_spec((len(POOL_WINDOWS), POOL_GROUP_DIM, POOL_GROUP_DIM)),
                  _const_spec((1, D_MODEL)), _const_spec((D_MODEL, D_MODEL)),
                  _const_spec((1, D_MODEL)), _const_spec((1, D_MODEL))],
        out_specs=rows,
        out_shape=jax.ShapeDtypeStruct((BATCH, SEQ, D_MODEL), F32),
        compiler_params=_params(("parallel", "parallel")),
        name="pool_mixer",
    )(x, x, mod, mod, mod, pool_w.astype(BF16), pool_scale.reshape(1, -1),
      w_out_c.astype(BF16), ln_g.reshape(1, -1), ln_b.reshape(1, -1))


def _router_body(x_ref, sh_ref, sc_ref, wr_ref, g_ref, hm_ref):
    hm = _modulate(x_ref[0], sh_ref[0], sc_ref[0])
    hm_ref[0] = hm
    hm_hi = hm.astype(BF16)
    hm_lo = (hm - hm_hi.astype(F32)).astype(BF16)
    w = wr_ref[...]
    w_hi = w.astype(BF16)
    w_lo = (w - w_hi.astype(F32)).astype(BF16)
    logits = _dot(hm_hi, w_hi) + (_dot(hm_lo, w_hi) + _dot(hm_hi, w_lo))
    lane = lax.broadcasted_iota(jnp.int32, logits.shape, 1).astype(F32)
    logits = jnp.where(lane < N_EXPERTS, logits, -jnp.inf)
    v1 = jnp.max(logits, axis=-1, keepdims=True)
    i1 = jnp.min(jnp.where(logits == v1, lane, float(LANES)), axis=-1, keepdims=True)
    rest = jnp.where(lane == i1, -jnp.inf, logits)
    v2 = jnp.max(rest, axis=-1, keepdims=True)
    i2 = jnp.min(jnp.where(rest == v2, lane, float(LANES)), axis=-1, keepdims=True)
    e2 = jnp.exp(v2 - v1)
    den = 1.0 + e2
    g_ref[0] = (jnp.where(lane == 0.0, i1, 0.0) + jnp.where(lane == 1.0, i2, 0.0)
                + jnp.where(lane == 2.0, 1.0 / den, 0.0) + jnp.where(lane == 3.0, e2 / den, 0.0))


def _router(x, mod, k, w_router):
    tm = 512
    shift, scale, _ = _mod_specs(k, 2)
    wr = jnp.pad(w_router, ((0, 0), (0, LANES - N_EXPERTS)))
    rows = pl.BlockSpec((1, tm, D_MODEL), lambda b, i: (b, i, 0))
    return pl.pallas_call(
        _router_body,
        grid=(BATCH, SEQ // tm),
        in_specs=[rows, shift, scale, _const_spec((D_MODEL, LANES))],
        out_specs=(pl.BlockSpec((1, tm, LANES), lambda b, i: (b, i, 0)), rows),
        out_shape=(jax.ShapeDtypeStruct((BATCH, SEQ, LANES), F32),
                   jax.ShapeDtypeStruct((BATCH, SEQ, D_MODEL), F32)),
        compiler_params=_params(("parallel", "parallel")),
        name="router",
    )(x, mod, mod, wr)


MOE_TILE = 1024
MOE_SUB = 256
N_TOKENS = BATCH * SEQ
N_ASSIGN = 2 * N_TOKENS
MOE_ROWS = N_ASSIGN + N_EXPERTS * MOE_TILE
MOE_TILES = MOE_ROWS // MOE_TILE


def _route_plan(routing):
    e_flat = routing.reshape(N_TOKENS, LANES)[:, :2].astype(jnp.int32).T.reshape(N_ASSIGN)
    onehot = (e_flat[:, None] == jnp.arange(N_EXPERTS, dtype=jnp.int32)[None, :]).astype(jnp.int32)
    csum = jnp.cumsum(onehot, axis=0)
    counts = csum[-1]
    rank = jnp.sum(csum * onehot, axis=1) - 1
    padded = (counts + MOE_TILE - 1) // MOE_TILE * MOE_TILE
    group_end = jnp.cumsum(padded)
    group_start = group_end - padded
    pos = (jnp.sum(onehot * group_start[None, :], axis=1) + rank).astype(jnp.int32)
    tok = jnp.tile(jnp.arange(N_TOKENS, dtype=jnp.int32), 2)
    src = jnp.zeros((MOE_ROWS,), jnp.int32).at[pos].set(tok)
    tile_start = jnp.arange(MOE_TILES, dtype=jnp.int32) * MOE_TILE
    n_used = group_end[-1] // MOE_TILE
    tile_expert = jnp.sum((tile_start[:, None] >= group_end[None, :]).astype(jnp.int32), axis=1)
    last_expert = jnp.max(jnp.where(counts > 0, jnp.arange(N_EXPERTS, dtype=jnp.int32), 0))
    tile_expert = jnp.minimum(tile_expert, last_expert)
    tile_rows = jnp.clip(counts[tile_expert] - (tile_start - group_start[tile_expert]), 0, MOE_TILE)
    tile_rows = jnp.where(jnp.arange(MOE_TILES) < n_used, tile_rows, 0).astype(jnp.int32)
    return pos, src, tile_expert.astype(jnp.int32), tile_rows, n_used.reshape(1).astype(jnp.int32)


def _gather_body(src_ref, hm_hbm, o_ref, buf, sem, *, tg):
    base = pl.program_id(0) * tg

    def issue(r, carry):
        t = src_ref[base + r]
        pltpu.make_async_copy(hm_hbm.at[pl.ds(t, 1)], buf.at[pl.ds(r, 1)], sem).start()
        return carry

    lax.fori_loop(0, tg, issue, 0, unroll=8)
    pltpu.make_async_copy(hm_hbm.at[pl.ds(0, tg)], buf, sem).wait()
    o_ref[...] = buf[...].astype(BF16)


def _gather_rows(src, hm):
    tg = 512
    grid_spec = pltpu.PrefetchScalarGridSpec(
        num_scalar_prefetch=1,
        grid=(MOE_ROWS // tg,),
        in_specs=[pl.BlockSpec(memory_space=pl.ANY)],
        out_specs=pl.BlockSpec((tg, D_MODEL), lambda i, src: (i, 0)),
        scratch_shapes=[pltpu.VMEM((tg, D_MODEL), F32), pltpu.SemaphoreType.DMA(())],
    )
    return pl.pallas_call(
        functools.partial(_gather_body, tg=tg),
        grid_spec=grid_spec,
        out_shape=jax.ShapeDtypeStruct((MOE_ROWS, D_MODEL), BF16),
        compiler_params=_params(("arbitrary",)),
        name="moe_gather",
    )(src, hm.reshape(N_TOKENS, D_MODEL))


def _gffn_body(te_ref, rows_ref, nu_ref, xs_ref, wg_ref, wu_ref, wd_ref, o_ref, wgb, wub, wdb, *, tm, sub):
    i = pl.program_id(0)
    f = pl.program_id(1)
    rows = rows_ref[i]

    @pl.when(rows > 0)
    def _():
        wgb[...] = wg_ref[0].astype(BF16)
        wub[...] = wu_ref[0].astype(BF16)
        wdb[...] = wd_ref[0].astype(BF16)

    for s in range(tm // sub):
        rs = slice(s * sub, (s + 1) * sub)

        @pl.when(s * sub < rows)
        def _():
            xsb = xs_ref[rs, :]
            h = (jax.nn.silu(_dot(xsb, wgb[...])) * _dot(xsb, wub[...])).astype(BF16)
            part = _dot(h, wdb[...])

            @pl.when(f == 0)
            def _():
                o_ref[rs, :] = part

            @pl.when(f > 0)
            def _():
                o_ref[rs, :] += part

        @pl.when(jnp.logical_and(s * sub >= rows, f == 0))
        def _():
            o_ref[rs, :] = jnp.zeros((sub, D_MODEL), F32)


def _grouped_ffn(xs, tile_expert, tile_rows, n_used, layer, w_gate, w_up, w_down):
    tm, tf, sub = MOE_TILE, 256, MOE_SUB
    n_f = D_FF_EXPERT // tf
    first = layer * N_EXPERTS
    n_all = w_gate.shape[0] * N_EXPERTS

    def live(i, f, nu):
        return jnp.where(i < nu[0], f, n_f - 1)

    grid_spec = pltpu.PrefetchScalarGridSpec(
        num_scalar_prefetch=3,
        grid=(MOE_TILES, n_f),
        in_specs=[pl.BlockSpec((tm, D_MODEL), lambda i, f, te, tr, nu: (jnp.minimum(i, nu[0] - 1), 0)),
                  pl.BlockSpec((1, D_MODEL, tf), lambda i, f, te, tr, nu: (first + te[i], 0, live(i, f, nu))),
                  pl.BlockSpec((1, D_MODEL, tf), lambda i, f, te, tr, nu: (first + te[i], 0, live(i, f, nu))),
                  pl.BlockSpec((1, tf, D_MODEL), lambda i, f, te, tr, nu: (first + te[i], live(i, f, nu), 0))],
        out_specs=pl.BlockSpec((tm, D_MODEL), lambda i, f, te, tr, nu: (i, 0)),
        scratch_shapes=[pltpu.VMEM((D_MODEL, tf), BF16), pltpu.VMEM((D_MODEL, tf), BF16),
                        pltpu.VMEM((tf, D_MODEL), BF16)],
    )
    return pl.pallas_call(
        functools.partial(_gffn_body, tm=tm, sub=sub),
        grid_spec=grid_spec,
        out_shape=jax.ShapeDtypeStruct((MOE_ROWS, D_MODEL), F32),
        compiler_params=_params(("arbitrary", "arbitrary")),
        name="moe_grouped_ffn",
    )(tile_expert, tile_rows, n_used, xs,
      w_gate.reshape(n_all, D_MODEL, D_FF_EXPERT), w_up.reshape(n_all, D_MODEL, D_FF_EXPERT),
      w_down.reshape(n_all, D_FF_EXPERT, D_MODEL))


def _combine_body(pos_ref, ys_hbm, r_ref, x_ref, gate_ref, lg_ref, lb_ref, o_ref, buf, sem, *, tc):
    base = (pl.program_id(0) * (SEQ // tc) + pl.program_id(1)) * tc

    def issue(r, carry):
        for k in range(2):
            p = pos_ref[k * N_TOKENS + base + r]
            pltpu.make_async_copy(ys_hbm.at[pl.ds(p, 1)], buf.at[k, pl.ds(r, 1)], sem.at[k]).start()
        return carry

    lax.fori_loop(0, tc, issue, 0, unroll=4)
    for k in range(2):
        pltpu.make_async_copy(ys_hbm.at[pl.ds(0, tc)], buf.at[k], sem.at[k]).wait()
    routing = r_ref[0]
    y = routing[:, 2:3] * buf[0] + routing[:, 3:4] * buf[1]
    o_ref[0] = _deepnorm(x_ref[0], y, gate_ref[0], lg_ref[...], lb_ref[...])


def _combine(pos, ys, routing, x, mod, k, ln_g, ln_b):
    tc = 512
    rows = pl.BlockSpec((1, tc, D_MODEL), lambda b, i, pos: (b, i, 0))
    vec = pl.BlockSpec((1, D_MODEL), lambda b, i, pos: (0, 0))
    grid_spec = pltpu.PrefetchScalarGridSpec(
        num_scalar_prefetch=1,
        grid=(BATCH, SEQ // tc),
        in_specs=[pl.BlockSpec(memory_space=pl.ANY),
                  pl.BlockSpec((1, tc, LANES), lambda b, i, pos: (b, i, 0)),
                  rows,
                  pl.BlockSpec((1, 1, D_MODEL), lambda b, i, pos: (k * 8 + b, 0, 2)),
                  vec, vec],
        out_specs=rows,
        scratch_shapes=[pltpu.VMEM((2, tc, D_MODEL), F32), pltpu.SemaphoreType.DMA((2,))],
    )
    return pl.pallas_call(
        functools.partial(_combine_body, tc=tc),
        grid_spec=grid_spec,
        out_shape=jax.ShapeDtypeStruct((BATCH, SEQ, D_MODEL), F32),
        compiler_params=_params(("arbitrary", "arbitrary")),
        name="moe_combine",
    )(pos, ys, routing, x, mod, ln_g.reshape(1, -1), ln_b.reshape(1, -1))


def _moe(x, mod, k, w_router, layer, w_gate, w_up, w_down, ln_g, ln_b):
    routing, hm = _router(x, mod, k, w_router)
    pos, src, tile_expert, tile_rows, n_used = _route_plan(routing)
    xs = _gather_rows(src, hm)
    ys = _grouped_ffn(xs, tile_expert, tile_rows, n_used, layer, w_gate, w_up, w_down)
    return _combine(pos, ys, routing, x, mod, k, ln_g, ln_b)


def kernel(x, c, positions, ada_w, ada_b, ln_g, ln_b, w_in_ab, q_norm_g, w_q_up, kv_norm_g, w_kv_up, sgu_norm_g, sgu_norm_b, sgu_w, sgu_b, w_out_ab, ffn_w_gate, ffn_w_up, ffn_w_down, pool_w, pool_scale, w_out_c, router_w, moe_w_gate, moe_w_up, moe_w_down):
    mod = _ada_modulation(c, ada_w, ada_b)
    tables = _rope_tables(positions)
    for l in range(DEPTH):
        j = l // 2
        k_tok, k_ch = 2 * l, 2 * l + 1
        if l % 2 == 0:
            cq, ckv, kr, zu, zv = _inproj(x, mod, k_tok, w_in_ab[j])
            q, k, v = _qkv(cq, ckv, kr, tables, q_norm_g[j], w_q_up[j], kv_norm_g[j], w_kv_up[j])
            attn = _attention(q, k, v)
            sgu = _sgu(zu, zv, sgu_norm_g[j], sgu_norm_b[j], sgu_w[j], sgu_b[j])
            x = _outproj(attn, sgu, x, mod, k_tok, w_out_ab[j], ln_g[l, 0], ln_b[l, 0])
            x = _ffn(x, mod, k_ch, ffn_w_gate[j], ffn_w_up[j], ffn_w_down[j], ln_g[l, 1], ln_b[l, 1])
        else:
            x = _pool_mixer(x, mod, k_tok, pool_w[j], pool_scale[j], w_out_c[j], ln_g[l, 0], ln_b[l, 0])
            x = _moe(x, mod, k_ch, router_w[j], j, moe_w_gate, moe_w_up, moe_w_down,
                     ln_g[l, 1], ln_b[l, 1])
    return x
```

```python
import functools

import jax
import jax.numpy as jnp
import numpy as np
from jax import lax
from jax.experimental import pallas as pl
from jax.experimental.pallas import tpu as pltpu

F32 = jnp.float32
BF16 = jnp.bfloat16

D_MODEL = 2048
BATCH = 2
SEQ = 4096
DEPTH = 4
MLA_HEADS = 8
QK_NOPE_DIM = 128
QK_ROPE_DIM = 64
V_HEAD_DIM = 128
Q_LORA_RANK = 768
KV_LORA_RANK = 512
ROPE_THETA = 10000.0
SGU_GROUPS = 8
SGU_GROUP_DIM = 128
SGU_WIDTH = SGU_GROUPS * SGU_GROUP_DIM
CHUNK = 128
ATTN_WIDTH = MLA_HEADS * V_HEAD_DIM
POOL_WINDOWS = (2, 4, 8, 16)
POOL_GROUP_DIM = D_MODEL // 4
D_FF = 5632
N_EXPERTS = 8
D_FF_EXPERT = 2816
DEEPNORM_ALPHA = (2 * DEPTH) ** 0.25
LN_EPS = 1e-5
RMS_EPS = 1e-6
SM_SCALE = (QK_NOPE_DIM + QK_ROPE_DIM) ** -0.5

LANES = 128
HEAD_LANES = 2 * LANES
QK_WIDTH = MLA_HEADS * HEAD_LANES
VMEM_LIMIT = 60 * 1024 * 1024
POOL_HALO = 128


def _params(semantics):
    return pltpu.CompilerParams(dimension_semantics=semantics, vmem_limit_bytes=VMEM_LIMIT)


def _const_spec(shape):
    nd = len(shape)
    return pl.BlockSpec(shape, lambda *_: (0,) * nd, pipeline_mode=pl.Buffered(1))


def _dot(a, b):
    return jnp.dot(a, b, preferred_element_type=F32)


def _modulate(x, shift, scale):
    return x * (1.0 + scale) + shift


def _deepnorm(x, y, gate, g, b):
    r = DEEPNORM_ALPHA * x + (1.0 + gate) * y
    mu = jnp.mean(r, axis=-1, keepdims=True)
    rc = r - mu
    var = jnp.mean(rc * rc, axis=-1, keepdims=True)
    return rc * lax.rsqrt(var + LN_EPS) * g + b


def _rms(x, g):
    return x * lax.rsqrt(jnp.mean(x * x, axis=-1, keepdims=True) + RMS_EPS) * g


def _rope(r, cos_t, sin_hi, sin_lo):
    return (r * cos_t + pltpu.roll(r, QK_ROPE_DIM // 2, 1) * sin_hi
            + pltpu.roll(r, LANES - QK_ROPE_DIM // 2, 1) * sin_lo)


def _mod_specs(k, n_grid):
    def spec(part):
        if n_grid == 2:
            return pl.BlockSpec((1, 1, D_MODEL), lambda b, i: (k * 8 + b, 0, part))
        if n_grid == 3:
            return pl.BlockSpec((1, 1, D_MODEL), lambda b, i, f: (k * 8 + b, 0, part))
        return pl.BlockSpec((1, 1, D_MODEL), lambda b, i, e, f: (k * 8 + b, 0, part))
    return spec(0), spec(1), spec(2)


def _ada_body(c_ref, w_ref, b_ref, o_ref):
    s = jax.nn.silu(c_ref[...]).astype(BF16)
    o_ref[0] = _dot(s, w_ref[0].astype(BF16)) + b_ref[0]


def _ada_modulation(c, ada_w, ada_b):
    n_mod = 2 * DEPTH
    tn = 1024
    c_pad = jnp.pad(c, ((0, 8 - BATCH), (0, 0)))
    out = pl.pallas_call(
        _ada_body,
        grid=(n_mod, 3 * D_MODEL // tn),
        in_specs=[
            pl.BlockSpec((8, D_MODEL), lambda k, n: (0, 0)),
            pl.BlockSpec((1, D_MODEL, tn), lambda k, n: (k, 0, n)),
            pl.BlockSpec((1, 1, tn), lambda k, n: (k, 0, n)),
        ],
        out_specs=pl.BlockSpec((1, 8, tn), lambda k, n: (k, 0, n)),
        out_shape=jax.ShapeDtypeStruct((n_mod, 8, 3 * D_MODEL), F32),
        compiler_params=_params(("parallel", "parallel")),
        name="ada_mod",
    )(c_pad, ada_w.reshape(n_mod, D_MODEL, 3 * D_MODEL), ada_b.reshape(n_mod, 1, 3 * D_MODEL))
    return out.reshape(n_mod * 8, 1, 3 * D_MODEL)


def _rope_table_body(pos_ref, inv_ref, cos_ref, hi_ref, lo_ref):
    ang = pos_ref[0].astype(F32) * inv_ref[...]
    lane = lax.broadcasted_iota(jnp.int32, ang.shape, 1)
    cos = jnp.cos(ang)
    sin = jnp.sin(ang)
    half = QK_ROPE_DIM // 2
    cos_ref[0] = jnp.where(lane < QK_ROPE_DIM, cos, 0.0)
    hi_ref[0] = jnp.where(lane < half, 0.0, jnp.where(lane < QK_ROPE_DIM, sin, 0.0))
    lo_ref[0] = jnp.where(lane < half, -sin, 0.0)


def _rope_tables(positions):
    tm = 512
    half = QK_ROPE_DIM // 2
    inv = ROPE_THETA ** (-jnp.arange(0, QK_ROPE_DIM, 2, dtype=F32) / QK_ROPE_DIM)
    inv_lanes = jnp.concatenate([inv, inv, jnp.zeros((LANES - 2 * half,), F32)]).reshape(1, LANES)
    tab = jax.ShapeDtypeStruct((BATCH, SEQ, LANES), F32)
    spec = pl.BlockSpec((1, tm, LANES), lambda b, i: (b, i, 0))
    return pl.pallas_call(
        _rope_table_body,
        grid=(BATCH, SEQ // tm),
        in_specs=[pl.BlockSpec((1, tm, 1), lambda b, i: (b, i, 0)),
                  pl.BlockSpec((1, LANES), lambda b, i: (0, 0))],
        out_specs=(spec, spec, spec),
        out_shape=(tab, tab, tab),
        compiler_params=_params(("parallel", "parallel")),
        name="rope_tables",
    )(positions.reshape(BATCH, SEQ, 1), inv_lanes)


def _inproj_body(x_ref, sh_ref, sc_ref, wq_ref, wkv_ref, wkr_ref, wu_ref, wv_ref,
                 cq_ref, ckv_ref, kr_ref, zu_ref, zv_ref):
    hm = _modulate(x_ref[0], sh_ref[0], sc_ref[0]).astype(BF16)
    cq_ref[0] = _dot(hm, wq_ref[...])
    ckv_ref[0] = _dot(hm, wkv_ref[...])
    kr_ref[0] = _dot(hm, wkr_ref[...])
    zu_ref[0] = _dot(hm, wu_ref[...])
    zv_ref[0] = _dot(hm, wv_ref[...])


def _inproj(x, mod, k, w_in):
    tm = 512
    a, b, c_, d = (Q_LORA_RANK, Q_LORA_RANK + KV_LORA_RANK,
                   Q_LORA_RANK + KV_LORA_RANK + QK_ROPE_DIM,
                   Q_LORA_RANK + KV_LORA_RANK + QK_ROPE_DIM + SGU_WIDTH)
    wq = w_in[:, :a].astype(BF16)
    wkv = w_in[:, a:b].astype(BF16)
    wkr = jnp.pad(w_in[:, b:c_], ((0, 0), (0, LANES - QK_ROPE_DIM))).astype(BF16)
    wu = w_in[:, c_:d].astype(BF16)
    wv = w_in[:, d:].astype(BF16)
    shift, scale, _ = _mod_specs(k, 2)
    widths = (Q_LORA_RANK, KV_LORA_RANK, LANES, SGU_WIDTH, SGU_WIDTH)
    return pl.pallas_call(
        _inproj_body,
        grid=(BATCH, SEQ // tm),
        in_specs=[pl.BlockSpec((1, tm, D_MODEL), lambda b, i: (b, i, 0)), shift, scale]
                 + [_const_spec((D_MODEL, w)) for w in widths],
        out_specs=tuple(pl.BlockSpec((1, tm, w), lambda b, i: (b, i, 0)) for w in widths),
        out_shape=tuple(jax.ShapeDtypeStruct((BATCH, SEQ, w), F32) for w in widths),
        compiler_params=_params(("parallel", "parallel")),
        name="inproj",
    )(x, mod, mod, wq, wkv, wkr, wu, wv)


def _qkv_body(cq_ref, ckv_ref, kr_ref, cos_ref, hi_ref, lo_ref, qg_ref, kvg_ref,
              wqn_ref, wqr_ref, wkn_ref, wvv_ref, q_ref, k_ref, v_ref):
    cos_t, sin_hi, sin_lo = cos_ref[0], hi_ref[0], lo_ref[0]
    cqn = _rms(cq_ref[0], qg_ref[...]).astype(BF16)
    qn = _dot(cqn, wqn_ref[...])
    qr = _dot(cqn, wqr_ref[...])
    ckvn = _rms(ckv_ref[0], kvg_ref[...]).astype(BF16)
    kn = _dot(ckvn, wkn_ref[...])
    v_ref[0] = _dot(ckvn, wvv_ref[...]).astype(BF16)
    k_rope = _rope(kr_ref[0], cos_t, sin_hi, sin_lo).astype(BF16)
    for h in range(MLA_HEADS):
        nope = slice(h * LANES, (h + 1) * LANES)
        lo = h * HEAD_LANES
        q_ref[0, :, lo:lo + LANES] = qn[:, nope].astype(BF16)
        q_ref[0, :, lo + LANES:lo + HEAD_LANES] = _rope(qr[:, nope], cos_t, sin_hi, sin_lo).astype(BF16)
        k_ref[0, :, lo:lo + LANES] = kn[:, nope].astype(BF16)
        k_ref[0, :, lo + LANES:lo + HEAD_LANES] = k_rope


def _qkv(cq, ckv, kr, tables, q_norm_g, w_q_up, kv_norm_g, w_kv_up):
    tm = 512
    wq = w_q_up.reshape(Q_LORA_RANK, MLA_HEADS, QK_NOPE_DIM + QK_ROPE_DIM)
    wqn = wq[:, :, :QK_NOPE_DIM].reshape(Q_LORA_RANK, MLA_HEADS * LANES).astype(BF16)
    wqr = jnp.pad(wq[:, :, QK_NOPE_DIM:], ((0, 0), (0, 0), (0, LANES - QK_ROPE_DIM)))
    wqr = wqr.reshape(Q_LORA_RANK, MLA_HEADS * LANES).astype(BF16)
    wkv = w_kv_up.reshape(KV_LORA_RANK, MLA_HEADS, QK_NOPE_DIM + V_HEAD_DIM)
    wkn = wkv[:, :, :QK_NOPE_DIM].reshape(KV_LORA_RANK, MLA_HEADS * LANES).astype(BF16)
    wvv = wkv[:, :, QK_NOPE_DIM:].reshape(KV_LORA_RANK, ATTN_WIDTH).astype(BF16)

    def rows(w):
        return pl.BlockSpec((1, tm, w), lambda b, i: (b, i, 0))

    return pl.pallas_call(
        _qkv_body,
        grid=(BATCH, SEQ // tm),
        in_specs=[rows(Q_LORA_RANK), rows(KV_LORA_RANK), rows(LANES),
                  rows(LANES), rows(LANES), rows(LANES),
                  _const_spec((1, Q_LORA_RANK)), _const_spec((1, KV_LORA_RANK)),
                  _const_spec((Q_LORA_RANK, MLA_HEADS * LANES)),
                  _const_spec((Q_LORA_RANK, MLA_HEADS * LANES)),
                  _const_spec((KV_LORA_RANK, MLA_HEADS * LANES)),
                  _const_spec((KV_LORA_RANK, ATTN_WIDTH))],
        out_specs=(rows(QK_WIDTH), rows(QK_WIDTH), rows(ATTN_WIDTH)),
        out_shape=(jax.ShapeDtypeStruct((BATCH, SEQ, QK_WIDTH), BF16),
                   jax.ShapeDtypeStruct((BATCH, SEQ, QK_WIDTH), BF16),
                   jax.ShapeDtypeStruct((BATCH, SEQ, ATTN_WIDTH), BF16)),
        compiler_params=_params(("parallel", "parallel")),
        name="qkv_up",
    )(cq, ckv, kr, *tables, q_norm_g.reshape(1, -1), kv_norm_g.reshape(1, -1), wqn, wqr, wkn, wvv)


def _attn_body(qi_ref, kj_ref, q_ref, k_ref, v_ref, o_ref, m_sc, l_sc, acc_sc, *, tq):
    p = pl.program_id(1)
    i = qi_ref[p]
    j = kj_ref[p]

    @pl.when(j == 0)
    def _():
        m_sc[...] = jnp.full(m_sc.shape, -jnp.inf, F32)
        l_sc[...] = jnp.zeros(l_sc.shape, F32)
        acc_sc[...] = jnp.zeros(acc_sc.shape, F32)

    row = lax.broadcasted_iota(jnp.int32, (tq, tq), 0)
    col = lax.broadcasted_iota(jnp.int32, (tq, tq), 1)
    visible = jnp.logical_or(j < i, col <= row)
    for h in range(MLA_HEADS):
        qk = slice(h * HEAD_LANES, (h + 1) * HEAD_LANES)
        hv = slice(h * V_HEAD_DIM, (h + 1) * V_HEAD_DIM)
        s = lax.dot_general(q_ref[0, :, qk], k_ref[0, :, qk], (((1,), (1,)), ((), ())),
                            preferred_element_type=F32) * SM_SCALE
        s = jnp.where(visible, s, -jnp.inf)
        m_prev = m_sc[h]
        m_new = jnp.maximum(m_prev, jnp.max(s, axis=-1, keepdims=True))
        alpha = jnp.exp(m_prev - m_new)
        pr = jnp.exp(s - m_new)
        l_sc[h] = alpha * l_sc[h] + jnp.sum(pr, axis=-1, keepdims=True)
        acc_sc[:, hv] = alpha * acc_sc[:, hv] + _dot(pr.astype(BF16), v_ref[0, :, hv])
        m_sc[h] = m_new

    @pl.when(j == i)
    def _():
        for h in range(MLA_HEADS):
            hv = slice(h * V_HEAD_DIM, (h + 1) * V_HEAD_DIM)
            o_ref[0, :, hv] = (acc_sc[:, hv] / l_sc[h]).astype(BF16)


def _attention(q, k, v):
    tq = 512
    nb = SEQ // tq
    pairs = [(i, j) for i in range(nb) for j in range(i + 1)]
    qi = jnp.asarray(np.array([p[0] for p in pairs], np.int32))
    kj = jnp.asarray(np.array([p[1] for p in pairs], np.int32))
    grid_spec = pltpu.PrefetchScalarGridSpec(
        num_scalar_prefetch=2,
        grid=(BATCH, len(pairs)),
        in_specs=[pl.BlockSpec((1, tq, QK_WIDTH), lambda b, p, qi, kj: (b, qi[p], 0)),
                  pl.BlockSpec((1, tq, QK_WIDTH), lambda b, p, qi, kj: (b, kj[p], 0)),
                  pl.BlockSpec((1, tq, ATTN_WIDTH), lambda b, p, qi, kj: (b, kj[p], 0))],
        out_specs=pl.BlockSpec((1, tq, ATTN_WIDTH), lambda b, p, qi, kj: (b, qi[p], 0)),
        scratch_shapes=[pltpu.VMEM((MLA_HEADS, tq, 1), F32),
                        pltpu.VMEM((MLA_HEADS, tq, 1), F32),
                        pltpu.VMEM((tq, ATTN_WIDTH), F32)],
    )
    return pl.pallas_call(
        functools.partial(_attn_body, tq=tq),
        grid_spec=grid_spec,
        out_shape=jax.ShapeDtypeStruct((BATCH, SEQ, ATTN_WIDTH), BF16),
        compiler_params=_params(("parallel", "arbitrary")),
        name="mla_attention",
    )(qi, kj, q, k, v)


def _sgu_body(zu_ref, zv_ref, g_ref, b_ref, w_ref, bias_ref, o_ref, *, tm):
    row = lax.broadcasted_iota(jnp.int32, (CHUNK, CHUNK), 0)
    col = lax.broadcasted_iota(jnp.int32, (CHUNK, CHUNK), 1)
    for g in range(SGU_GROUPS):
        gs = slice(g * SGU_GROUP_DIM, (g + 1) * SGU_GROUP_DIM)
        v = jax.nn.gelu(zv_ref[0, :, gs])
        mu = jnp.mean(v, axis=-1, keepdims=True)
        vc = v - mu
        var = jnp.mean(vc * vc, axis=-1, keepdims=True)
        vn = (vc * lax.rsqrt(var + LN_EPS) * g_ref[:, gs] + b_ref[:, gs]).astype(BF16)
        w = jnp.where(col <= row, w_ref[g], 0.0).astype(BF16)
        for n in range(tm // CHUNK):
            rs = slice(n * CHUNK, (n + 1) * CHUNK)
            s = _dot(w, vn[rs]) + bias_ref[:, gs]
            o_ref[0, rs, gs] = (jax.nn.gelu(zu_ref[0, rs, gs]) * s).astype(BF16)


def _sgu(zu, zv, norm_g, norm_b, sgu_w, sgu_b):
    tm = 512
    bias = jnp.repeat(sgu_b.T, SGU_GROUP_DIM, axis=1)
    rows = pl.BlockSpec((1, tm, SGU_WIDTH), lambda b, i: (b, i, 0))
    return pl.pallas_call(
        functools.partial(_sgu_body, tm=tm),
        grid=(BATCH, SEQ // tm),
        in_specs=[rows, rows, _const_spec((1, SGU_WIDTH)), _const_spec((1, SGU_WIDTH)),
                  _const_spec((SGU_GROUPS, CHUNK, CHUNK)), _const_spec((CHUNK, SGU_WIDTH))],
        out_specs=rows,
        out_shape=jax.ShapeDtypeStruct((BATCH, SEQ, SGU_WIDTH), BF16),
        compiler_params=_params(("parallel", "parallel")),
        name="sgu",
    )(zu, zv, norm_g.reshape(1, -1), norm_b.reshape(1, -1), sgu_w, bias)


def _outproj_body(a_ref, s_ref, x_ref, gate_ref, wa_ref, ws_ref, lg_ref, lb_ref, o_ref):
    y = _dot(a_ref[0], wa_ref[...]) + _dot(s_ref[0], ws_ref[...])
    o_ref[0] = _deepnorm(x_ref[0], y, gate_ref[0], lg_ref[...], lb_ref[...])


def _outproj(attn, sgu, x, mod, k, w_out, ln_g, ln_b):
    tm = 512
    wa = w_out[:ATTN_WIDTH].astype(BF16)
    ws = w_out[ATTN_WIDTH:].astype(BF16)
    _, _, gate = _mod_specs(k, 2)

    def rows(w):
        return pl.BlockSpec((1, tm, w), lambda b, i: (b, i, 0))

    return pl.pallas_call(
        _outproj_body,
        grid=(BATCH, SEQ // tm),
        in_specs=[rows(ATTN_WIDTH), rows(SGU_WIDTH), rows(D_MODEL), gate,
                  _const_spec((ATTN_WIDTH, D_MODEL)), _const_spec((SGU_WIDTH, D_MODEL)),
                  _const_spec((1, D_MODEL)), _const_spec((1, D_MODEL))],
        out_specs=rows(D_MODEL),
        out_shape=jax.ShapeDtypeStruct((BATCH, SEQ, D_MODEL), F32),
        compiler_params=_params(("parallel", "parallel")),
        name="outproj_ab",
    )(attn, sgu, x, mod, wa, ws, ln_g.reshape(1, -1), ln_b.reshape(1, -1))


def _ffn_up_body(x_ref, sh_ref, sc_ref, wg_ref, wu_ref, h_ref, hm_sc):
    @pl.when(pl.program_id(2) == 0)
    def _():
        hm_sc[...] = _modulate(x_ref[0], sh_ref[0], sc_ref[0]).astype(BF16)

    hm = hm_sc[...]
    g = _dot(hm, wg_ref[0].astype(BF16))
    u = _dot(hm, wu_ref[0].astype(BF16))
    h_ref[0] = (jax.nn.silu(g) * u).astype(BF16)


def _ffn_down_body(h_ref, x_ref, gate_ref, wd_ref, lg_ref, lb_ref, o_ref, y_sc, *, n_n):
    n = pl.program_id(2)
    y_sc[n] = _dot(h_ref[0], wd_ref[...])

    @pl.when(n == n_n - 1)
    def _():
        y = jnp.concatenate([y_sc[c] for c in range(n_n)], axis=-1)
        o_ref[0] = _deepnorm(x_ref[0], y, gate_ref[0], lg_ref[...], lb_ref[...])


def _ffn(x, mod, k, layer, w_gate, w_up, w_down, ln_g, ln_b):
    tm, tf = 1024, 512
    shift, scale, gate = _mod_specs(k, 3)
    h = pl.pallas_call(
        _ffn_up_body,
        grid=(BATCH, SEQ // tm, D_FF // tf),
        in_specs=[pl.BlockSpec((1, tm, D_MODEL), lambda b, i, f: (b, i, 0), pipeline_mode=pl.Buffered(1)),
                  shift, scale,
                  pl.BlockSpec((1, D_MODEL, tf), lambda b, i, f: (layer, 0, f)),
                  pl.BlockSpec((1, D_MODEL, tf), lambda b, i, f: (layer, 0, f))],
        out_specs=pl.BlockSpec((1, tm, tf), lambda b, i, f: (b, i, f)),
        out_shape=jax.ShapeDtypeStruct((BATCH, SEQ, D_FF), BF16),
        scratch_shapes=[pltpu.VMEM((tm, D_MODEL), BF16)],
        compiler_params=_params(("parallel", "parallel", "arbitrary")),
        name="ffn_up",
    )(x, mod, mod, w_gate, w_up)

    tm, tn = 512, 512
    n_n = D_MODEL // tn
    rows = pl.BlockSpec((1, tm, D_MODEL), lambda b, i, n: (b, i, 0))
    vec = pl.BlockSpec((1, D_MODEL), lambda b, i, n: (0, 0))
    return pl.pallas_call(
        functools.partial(_ffn_down_body, n_n=n_n),
        grid=(BATCH, SEQ // tm, n_n),
        in_specs=[pl.BlockSpec((1, tm, D_FF), lambda b, i, n: (b, i, 0)),
                  pl.BlockSpec((1, tm, D_MODEL), lambda b, i, n: (b, i, 0), pipeline_mode=pl.Buffered(1)),
                  gate,
                  pl.BlockSpec((D_FF, tn), lambda b, i, n: (0, n)),
                  vec, vec],
        out_specs=rows,
        out_shape=jax.ShapeDtypeStruct((BATCH, SEQ, D_MODEL), F32),
        scratch_shapes=[pltpu.VMEM((n_n, tm, tn), F32)],
        compiler_params=_params(("parallel", "parallel", "arbitrary")),
        name="ffn_down",
    )(h, x, mod, w_down.astype(BF16), ln_g.reshape(1, -1), ln_b.reshape(1, -1))


def _pool_body(x_ref, halo_ref, sh_ref, sc_ref, gate_ref, pw_ref, ps_ref, wo_ref, lg_ref, lb_ref,
               o_ref, *, tm):
    i = pl.program_id(1)
    x = x_ref[0]
    hm = _modulate(x, sh_ref[0], sc_ref[0])
    halo = jnp.where(i > 0, _modulate(halo_ref[0], sh_ref[0], sc_ref[0]), 0.0)
    ext = jnp.concatenate([halo, hm], axis=0)
    ext_hi = ext.astype(BF16)
    ext_lo = (ext - ext_hi.astype(F32)).astype(BF16)
    t_loc = lax.broadcasted_iota(jnp.int32, (tm, tm + POOL_HALO), 0)
    s_loc = lax.broadcasted_iota(jnp.int32, (tm, tm + POOL_HALO), 1)
    lag = t_loc + POOL_HALO - s_loc
    t_glob = i * tm + lax.broadcasted_iota(jnp.int32, (tm, 1), 0)
    outs = []
    for gi, w in enumerate(POOL_WINDOWS):
        cs = slice(gi * POOL_GROUP_DIM, (gi + 1) * POOL_GROUP_DIM)
        band = jnp.where(lag >= 0, jnp.where(lag < w, 1.0, 0.0), 0.0).astype(BF16)
        wsum = _dot(band, ext_hi[:, cs]) + _dot(band, ext_lo[:, cs])
        count = jnp.minimum(t_glob + 1, w).astype(F32)
        d = (wsum / count - hm[:, cs]).astype(BF16)
        outs.append(_dot(d, pw_ref[gi]))
    pooled = (jnp.concatenate(outs, axis=-1) * ps_ref[...]).astype(BF16)
    y = _dot(pooled, wo_ref[...])
    o_ref[0] = _deepnorm(x, y, gate_ref[0], lg_ref[...], lb_ref[...])


def _pool_mixer(x, mod, k, pool_w, pool_scale, w_out_c, ln_g, ln_b):
    tm = 256
    per = tm // POOL_HALO
    shift, scale, gate = _mod_specs(k, 2)
    rows = pl.BlockSpec((1, tm, D_MODEL), lambda b, i: (b, i, 0))
    halo = pl.BlockSpec((1, POOL_HALO, D_MODEL), lambda b, i: (b, jnp.maximum(i * per - 1, 0), 0))
    return pl.pallas_call(
        functools.partial(_pool_body, tm=tm),
        grid=(BATCH, SEQ // tm),
        in_specs=[rows, halo, shift, scale, gate,
                  _const_spec((len(POOL_WINDOWS), POOL_GROUP_DIM, POOL_GROUP_DIM)),
                  _const_spec((1, D_MODEL)), _const_spec((D_MODEL, D_MODEL)),
                  _const_spec((1, D_MODEL)), _const_spec((1, D_MODEL))],
        out_specs=rows,
        out_shape=jax.ShapeDtypeStruct((BATCH, SEQ, D_MODEL), F32),
        compiler_params=_params(("parallel", "parallel")),
        name="pool_mixer",
    )(x, x, mod, mod, mod, pool_w.astype(BF16), pool_scale.reshape(1, -1),
      w_out_c.astype(BF16), ln_g.reshape(1, -1), ln_b.reshape(1, -1))


def _router_body(x_ref, sh_ref, sc_ref, wr_ref, g_ref, hm_ref):
    hm = _modulate(x_ref[0], sh_ref[0], sc_ref[0])
    hm_ref[0] = hm
    hm_hi = hm.astype(BF16)
    hm_lo = (hm - hm_hi.astype(F32)).astype(BF16)
    w = wr_ref[...]
    w_hi = w.astype(BF16)
    w_lo = (w - w_hi.astype(F32)).astype(BF16)
    logits = _dot(hm_hi, w_hi) + (_dot(hm_lo, w_hi) + _dot(hm_hi, w_lo))
    lane = lax.broadcasted_iota(jnp.int32, logits.shape, 1).astype(F32)
    logits = jnp.where(lane < N_EXPERTS, logits, -jnp.inf)
    v1 = jnp.max(logits, axis=-1, keepdims=True)
    i1 = jnp.min(jnp.where(logits == v1, lane, float(LANES)), axis=-1, keepdims=True)
    rest = jnp.where(lane == i1, -jnp.inf, logits)
    v2 = jnp.max(rest, axis=-1, keepdims=True)
    i2 = jnp.min(jnp.where(rest == v2, lane, float(LANES)), axis=-1, keepdims=True)
    e2 = jnp.exp(v2 - v1)
    den = 1.0 + e2
    g_ref[0] = (jnp.where(lane == 0.0, i1, 0.0) + jnp.where(lane == 1.0, i2, 0.0)
                + jnp.where(lane == 2.0, 1.0 / den, 0.0) + jnp.where(lane == 3.0, e2 / den, 0.0))


def _router(x, mod, k, w_router):
    tm = 512
    shift, scale, _ = _mod_specs(k, 2)
    wr = jnp.pad(w_router, ((0, 0), (0, LANES - N_EXPERTS)))
    rows = pl.BlockSpec((1, tm, D_MODEL), lambda b, i: (b, i, 0))
    return pl.pallas_call(
        _router_body,
        grid=(BATCH, SEQ // tm),
        in_specs=[rows, shift, scale, _const_spec((D_MODEL, LANES))],
        out_specs=(pl.BlockSpec((1, tm, LANES), lambda b, i: (b, i, 0)), rows),
        out_shape=(jax.ShapeDtypeStruct((BATCH, SEQ, LANES), F32),
                   jax.ShapeDtypeStruct((BATCH, SEQ, D_MODEL), F32)),
        compiler_params=_params(("parallel", "parallel")),
        name="router",
    )(x, mod, mod, wr)


MOE_TILE = 1024
MOE_SUB = 256
N_TOKENS = BATCH * SEQ
N_ASSIGN = 2 * N_TOKENS
MOE_ROWS = N_ASSIGN + N_EXPERTS * MOE_TILE
MOE_TILES = MOE_ROWS // MOE_TILE


def _route_plan(routing):
    e_flat = routing.reshape(N_TOKENS, LANES)[:, :2].astype(jnp.int32).T.reshape(N_ASSIGN)
    onehot = (e_flat[:, None] == jnp.arange(N_EXPERTS, dtype=jnp.int32)[None, :]).astype(jnp.int32)
    csum = jnp.cumsum(onehot, axis=0)
    counts = csum[-1]
    rank = jnp.sum(csum * onehot, axis=1) - 1
    padded = (counts + MOE_TILE - 1) // MOE_TILE * MOE_TILE
    group_end = jnp.cumsum(padded)
    group_start = group_end - padded
    pos = (jnp.sum(onehot * group_start[None, :], axis=1) + rank).astype(jnp.int32)
    tok = jnp.tile(jnp.arange(N_TOKENS, dtype=jnp.int32), 2)
    src = jnp.zeros((MOE_ROWS,), jnp.int32).at[pos].set(tok)
    tile_start = jnp.arange(MOE_TILES, dtype=jnp.int32) * MOE_TILE
    n_used = group_end[-1] // MOE_TILE
    tile_expert = jnp.sum((tile_start[:, None] >= group_end[None, :]).astype(jnp.int32), axis=1)
    last_expert = jnp.max(jnp.where(counts > 0, jnp.arange(N_EXPERTS, dtype=jnp.int32), 0))
    tile_expert = jnp.minimum(tile_expert, last_expert)
    tile_rows = jnp.clip(counts[tile_expert] - (tile_start - group_start[tile_expert]), 0, MOE_TILE)
    tile_rows = jnp.where(jnp.arange(MOE_TILES) < n_used, tile_rows, 0).astype(jnp.int32)
    return pos, src, tile_expert.astype(jnp.int32), tile_rows, n_used.reshape(1).astype(jnp.int32)


GATHER_UNROLL = 8


def _gather_body(src_ref, rows_ref, hm_hbm, o_ref, buf, sem, *, tg):
    i = pl.program_id(0)
    base = i * tg
    per = MOE_TILE // tg
    n_valid = jnp.clip(rows_ref[i // per] - (i % per) * tg, 0, tg)

    @pl.when(n_valid < tg)
    def _():
        buf[...] = jnp.zeros(buf.shape, F32)

    def issue_row(r):
        t = src_ref[base + r]
        pltpu.make_async_copy(hm_hbm.at[pl.ds(t, 1)], buf.at[pl.ds(r, 1)], sem).start()

    def issue_group(g, carry):
        for u in range(GATHER_UNROLL):
            issue_row(g * GATHER_UNROLL + u)
        return carry

    def issue_one(r, carry):
        issue_row(r)
        return carry

    n_groups = n_valid // GATHER_UNROLL
    n_grouped = pl.multiple_of(n_groups * GATHER_UNROLL, GATHER_UNROLL)
    lax.fori_loop(0, n_groups, issue_group, 0)
    lax.fori_loop(n_grouped, n_valid, issue_one, 0)

    @pl.when(n_groups > 0)
    def _():
        pltpu.make_async_copy(hm_hbm.at[pl.ds(0, n_grouped)], buf.at[pl.ds(0, n_grouped)], sem).wait()

    def wait_one(r, carry):
        pltpu.make_async_copy(hm_hbm.at[pl.ds(0, 1)], buf.at[pl.ds(r, 1)], sem).wait()
        return carry

    lax.fori_loop(n_grouped, n_valid, wait_one, 0)

    o_ref[...] = buf[...].astype(BF16)


def _gather_rows(src, tile_rows, hm):
    tg = 512
    grid_spec = pltpu.PrefetchScalarGridSpec(
        num_scalar_prefetch=2,
        grid=(MOE_ROWS // tg,),
        in_specs=[pl.BlockSpec(memory_space=pl.ANY)],
        out_specs=pl.BlockSpec((tg, D_MODEL), lambda i, src, tr: (i, 0)),
        scratch_shapes=[pltpu.VMEM((tg, D_MODEL), F32), pltpu.SemaphoreType.DMA(())],
    )
    return pl.pallas_call(
        functools.partial(_gather_body, tg=tg),
        grid_spec=grid_spec,
        out_shape=jax.ShapeDtypeStruct((MOE_ROWS, D_MODEL), BF16),
        compiler_params=_params(("arbitrary",)),
        name="moe_gather",
    )(src, tile_rows, hm.reshape(N_TOKENS, D_MODEL))


def _for_valid_blocks(rows, tm, sub, full_fn, block_fn, empty_fn):
    @pl.when(rows == tm)
    def _():
        full_fn()

    for s in range(tm // sub):
        rs = slice(s * sub, (s + 1) * sub)

        @pl.when(jnp.logical_and(rows < tm, s * sub < rows))
        def _():
            block_fn(rs)

        @pl.when(s * sub >= rows)
        def _():
            empty_fn(rs)


def _gup_body(te_ref, rows_ref, nu_ref, xs_ref, wg_ref, wu_ref, h_ref, wgb, wub, *, tm, sub):
    rows = rows_ref[pl.program_id(0)]

    @pl.when(rows > 0)
    def _():
        wgb[...] = wg_ref[0].astype(BF16)
        wub[...] = wu_ref[0].astype(BF16)

    def swiglu(xsb):
        return (jax.nn.silu(_dot(xsb, wgb[...])) * _dot(xsb, wub[...])).astype(BF16)

    def full():
        h_ref[...] = swiglu(xs_ref[...])

    def block(rs):
        h_ref[rs, :] = swiglu(xs_ref[rs, :])

    def empty(rs):
        h_ref[rs, :] = jnp.zeros((sub, h_ref.shape[1]), BF16)

    _for_valid_blocks(rows, tm, sub, full, block, empty)


def _gdown_body(te_ref, rows_ref, nu_ref, h_ref, wd_ref, o_ref, wdb, *, tm, sub):
    rows = rows_ref[pl.program_id(0)]

    @pl.when(rows > 0)
    def _():
        wdb[...] = wd_ref[0].astype(BF16)

    def full():
        o_ref[...] = _dot(h_ref[...], wdb[...])

    def block(rs):
        o_ref[rs, :] = _dot(h_ref[rs, :], wdb[...])

    def empty(rs):
        o_ref[rs, :] = jnp.zeros((sub, o_ref.shape[1]), F32)

    _for_valid_blocks(rows, tm, sub, full, block, empty)


def _grouped_ffn(xs, tile_expert, tile_rows, n_used, layer, w_gate, w_up, w_down):
    tm, sub = MOE_TILE, MOE_SUB
    first = layer * N_EXPERTS
    n_all = w_gate.shape[0] * N_EXPERTS

    def tile(i, nu):
        return jnp.minimum(i, nu[0] - 1)

    def chunk(i, c, nu, n_chunks):
        return jnp.where(i < nu[0], c, n_chunks - 1)

    tf = 256
    n_f = D_FF_EXPERT // tf
    w_up_spec = pl.BlockSpec((1, D_MODEL, tf),
                             lambda i, f, te, tr, nu: (first + te[i], 0, chunk(i, f, nu, n_f)))
    up_spec = pltpu.PrefetchScalarGridSpec(
        num_scalar_prefetch=3,
        grid=(MOE_TILES, n_f),
        in_specs=[pl.BlockSpec((tm, D_MODEL), lambda i, f, te, tr, nu: (tile(i, nu), 0)),
                  w_up_spec, w_up_spec],
        out_specs=pl.BlockSpec((tm, tf), lambda i, f, te, tr, nu: (i, f)),
        scratch_shapes=[pltpu.VMEM((D_MODEL, tf), BF16), pltpu.VMEM((D_MODEL, tf), BF16)],
    )
    h = pl.pallas_call(
        functools.partial(_gup_body, tm=tm, sub=sub),
        grid_spec=up_spec,
        out_shape=jax.ShapeDtypeStruct((MOE_ROWS, D_FF_EXPERT), BF16),
        compiler_params=_params(("arbitrary", "arbitrary")),
        name="moe_up",
    )(tile_expert, tile_rows, n_used, xs,
      w_gate.reshape(n_all, D_MODEL, D_FF_EXPERT), w_up.reshape(n_all, D_MODEL, D_FF_EXPERT))

    tn = 512
    n_n = D_MODEL // tn
    down_spec = pltpu.PrefetchScalarGridSpec(
        num_scalar_prefetch=3,
        grid=(MOE_TILES, n_n),
        in_specs=[pl.BlockSpec((tm, D_FF_EXPERT), lambda i, n, te, tr, nu: (tile(i, nu), 0)),
                  pl.BlockSpec((1, D_FF_EXPERT, tn),
                               lambda i, n, te, tr, nu: (first + te[i], 0, chunk(i, n, nu, n_n)))],
        out_specs=pl.BlockSpec((tm, tn), lambda i, n, te, tr, nu: (i, n)),
        scratch_shapes=[pltpu.VMEM((D_FF_EXPERT, tn), BF16)],
    )
    return pl.pallas_call(
        functools.partial(_gdown_body, tm=tm, sub=sub),
        grid_spec=down_spec,
        out_shape=jax.ShapeDtypeStruct((MOE_ROWS, D_MODEL), F32),
        compiler_params=_params(("arbitrary", "arbitrary")),
        name="moe_down",
    )(tile_expert, tile_rows, n_used, h, w_down.reshape(n_all, D_FF_EXPERT, D_MODEL))


def _combine_body(pos_ref, ys_hbm, r_ref, x_ref, gate_ref, lg_ref, lb_ref, o_ref, buf, sem, *, tc):
    base = (pl.program_id(0) * (SEQ // tc) + pl.program_id(1)) * tc

    def issue(r, carry):
        for k in range(2):
            p = pos_ref[k * N_TOKENS + base + r]
            pltpu.make_async_copy(ys_hbm.at[pl.ds(p, 1)], buf.at[k, pl.ds(r, 1)], sem.at[k]).start()
        return carry

    lax.fori_loop(0, tc, issue, 0, unroll=4)
    for k in range(2):
        pltpu.make_async_copy(ys_hbm.at[pl.ds(0, tc)], buf.at[k], sem.at[k]).wait()
    routing = r_ref[0]
    y = routing[:, 2:3] * buf[0] + routing[:, 3:4] * buf[1]
    o_ref[0] = _deepnorm(x_ref[0], y, gate_ref[0], lg_ref[...], lb_ref[...])


def _combine(pos, ys, routing, x, mod, k, ln_g, ln_b):
    tc = 512
    rows = pl.BlockSpec((1, tc, D_MODEL), lambda b, i, pos: (b, i, 0))
    vec = pl.BlockSpec((1, D_MODEL), lambda b, i, pos: (0, 0))
    grid_spec = pltpu.PrefetchScalarGridSpec(
        num_scalar_prefetch=1,
        grid=(BATCH, SEQ // tc),
        in_specs=[pl.BlockSpec(memory_space=pl.ANY),
                  pl.BlockSpec((1, tc, LANES), lambda b, i, pos: (b, i, 0)),
                  rows,
                  pl.BlockSpec((1, 1, D_MODEL), lambda b, i, pos: (k * 8 + b, 0, 2)),
                  vec, vec],
        out_specs=rows,
        scratch_shapes=[pltpu.VMEM((2, tc, D_MODEL), F32), pltpu.SemaphoreType.DMA((2,))],
    )
    return pl.pallas_call(
        functools.partial(_combine_body, tc=tc),
        grid_spec=grid_spec,
        out_shape=jax.ShapeDtypeStruct((BATCH, SEQ, D_MODEL), F32),
        compiler_params=_params(("arbitrary", "arbitrary")),
        name="moe_combine",
    )(pos, ys, routing, x, mod, ln_g.reshape(1, -1), ln_b.reshape(1, -1))


def _moe(x, mod, k, w_router, layer, w_gate, w_up, w_down, ln_g, ln_b):
    routing, hm = _router(x, mod, k, w_router)
    pos, src, tile_expert, tile_rows, n_used = _route_plan(routing)
    xs = _gather_rows(src, tile_rows, hm)
    ys = _grouped_ffn(xs, tile_expert, tile_rows, n_used, layer, w_gate, w_up, w_down)
    return _combine(pos, ys, routing, x, mod, k, ln_g, ln_b)


def kernel(x, c, positions, ada_w, ada_b, ln_g, ln_b, w_in_ab, q_norm_g, w_q_up, kv_norm_g, w_kv_up, sgu_norm_g, sgu_norm_b, sgu_w, sgu_b, w_out_ab, ffn_w_gate, ffn_w_up, ffn_w_down, pool_w, pool_scale, w_out_c, router_w, moe_w_gate, moe_w_up, moe_w_down):
    mod = _ada_modulation(c, ada_w, ada_b)
    tables = _rope_tables(positions)
    for l in range(DEPTH):
        j = l // 2
        k_tok, k_ch = 2 * l, 2 * l + 1
        if l % 2 == 0:
            cq, ckv, kr, zu, zv = _inproj(x, mod, k_tok, w_in_ab[j])
            q, k, v = _qkv(cq, ckv, kr, tables, q_norm_g[j], w_q_up[j], kv_norm_g[j], w_kv_up[j])
            attn = _attention(q, k, v)
            sgu = _sgu(zu, zv, sgu_norm_g[j], sgu_norm_b[j], sgu_w[j], sgu_b[j])
            x = _outproj(attn, sgu, x, mod, k_tok, w_out_ab[j], ln_g[l, 0], ln_b[l, 0])
            x = _ffn(x, mod, k_ch, j, ffn_w_gate, ffn_w_up, ffn_w_down[j], ln_g[l, 1], ln_b[l, 1])
        else:
            x = _pool_mixer(x, mod, k_tok, pool_w[j], pool_scale[j], w_out_c[j], ln_g[l, 0], ln_b[l, 0])
            x = _moe(x, mod, k_ch, router_w[j], j, moe_w_gate, moe_w_up, moe_w_down,
                     ln_g[l, 1], ln_b[l, 1])
    return x
```

```python
import functools

import jax
import jax.numpy as jnp
import numpy as np
from jax import lax
from jax.experimental import pallas as pl
from jax.experimental.pallas import tpu as pltpu

F32 = jnp.float32
BF16 = jnp.bfloat16

D_MODEL = 2048
BATCH = 2
SEQ = 4096
DEPTH = 4
MLA_HEADS = 8
QK_NOPE_DIM = 128
QK_ROPE_DIM = 64
V_HEAD_DIM = 128
Q_LORA_RANK = 768
KV_LORA_RANK = 512
ROPE_THETA = 10000.0
SGU_GROUPS = 8
SGU_GROUP_DIM = 128
SGU_WIDTH = SGU_GROUPS * SGU_GROUP_DIM
CHUNK = 128
ATTN_WIDTH = MLA_HEADS * V_HEAD_DIM
POOL_WINDOWS = (2, 4, 8, 16)
POOL_GROUP_DIM = D_MODEL // 4
D_FF = 5632
N_EXPERTS = 8
D_FF_EXPERT = 2816
DEEPNORM_ALPHA = (2 * DEPTH) ** 0.25
LN_EPS = 1e-5
RMS_EPS = 1e-6
SM_SCALE = (QK_NOPE_DIM + QK_ROPE_DIM) ** -0.5
LOG2_E = 1.4426950408889634

LANES = 128
HEAD_LANES = 2 * LANES
QK_WIDTH = MLA_HEADS * HEAD_LANES
VMEM_LIMIT = 60 * 1024 * 1024
POOL_HALO = 128


def _params(semantics):
    return pltpu.CompilerParams(dimension_semantics=semantics, vmem_limit_bytes=VMEM_LIMIT)


def _const_spec(shape):
    nd = len(shape)
    return pl.BlockSpec(shape, lambda *_: (0,) * nd, pipeline_mode=pl.Buffered(1))


def _dot(a, b):
    return jnp.dot(a, b, preferred_element_type=F32)


def _modulate(x, shift, scale):
    return x * (1.0 + scale) + shift


def _deepnorm(x, y, gate, g, b):
    r = DEEPNORM_ALPHA * x + (1.0 + gate) * y
    mu = jnp.mean(r, axis=-1, keepdims=True)
    rc = r - mu
    var = jnp.mean(rc * rc, axis=-1, keepdims=True)
    return rc * lax.rsqrt(var + LN_EPS) * g + b


def _rms(x, g):
    return x * lax.rsqrt(jnp.mean(x * x, axis=-1, keepdims=True) + RMS_EPS) * g


def _rope(r, cos_t, sin_hi, sin_lo):
    return (r * cos_t + pltpu.roll(r, QK_ROPE_DIM // 2, 1) * sin_hi
            + pltpu.roll(r, LANES - QK_ROPE_DIM // 2, 1) * sin_lo)


def _mod_specs(k, n_grid):
    def spec(part):
        if n_grid == 2:
            return pl.BlockSpec((1, 1, D_MODEL), lambda b, i: (k * 8 + b, 0, part))
        if n_grid == 3:
            return pl.BlockSpec((1, 1, D_MODEL), lambda b, i, f: (k * 8 + b, 0, part))
        return pl.BlockSpec((1, 1, D_MODEL), lambda b, i, e, f: (k * 8 + b, 0, part))
    return spec(0), spec(1), spec(2)


def _ada_body(c_ref, w_ref, b_ref, o_ref):
    s = jax.nn.silu(c_ref[...]).astype(BF16)
    o_ref[0] = _dot(s, w_ref[0].astype(BF16)) + b_ref[0]


def _ada_modulation(c, ada_w, ada_b):
    n_mod = 2 * DEPTH
    tn = 1024
    c_pad = jnp.pad(c, ((0, 8 - BATCH), (0, 0)))
    out = pl.pallas_call(
        _ada_body,
        grid=(n_mod, 3 * D_MODEL // tn),
        in_specs=[
            pl.BlockSpec((8, D_MODEL), lambda k, n: (0, 0)),
            pl.BlockSpec((1, D_MODEL, tn), lambda k, n: (k, 0, n)),
            pl.BlockSpec((1, 1, tn), lambda k, n: (k, 0, n)),
        ],
        out_specs=pl.BlockSpec((1, 8, tn), lambda k, n: (k, 0, n)),
        out_shape=jax.ShapeDtypeStruct((n_mod, 8, 3 * D_MODEL), F32),
        compiler_params=_params(("parallel", "parallel")),
        name="ada_mod",
    )(c_pad, ada_w.reshape(n_mod, D_MODEL, 3 * D_MODEL), ada_b.reshape(n_mod, 1, 3 * D_MODEL))
    return out.reshape(n_mod * 8, 1, 3 * D_MODEL)


def _rope_table_body(pos_ref, inv_ref, cos_ref, hi_ref, lo_ref):
    ang = pos_ref[0].astype(F32) * inv_ref[...]
    lane = lax.broadcasted_iota(jnp.int32, ang.shape, 1)
    cos = jnp.cos(ang)
    sin = jnp.sin(ang)
    half = QK_ROPE_DIM // 2
    cos_ref[0] = jnp.where(lane < QK_ROPE_DIM, cos, 0.0)
    hi_ref[0] = jnp.where(lane < half, 0.0, jnp.where(lane < QK_ROPE_DIM, sin, 0.0))
    lo_ref[0] = jnp.where(lane < half, -sin, 0.0)


def _rope_tables(positions):
    tm = 512
    half = QK_ROPE_DIM // 2
    inv = ROPE_THETA ** (-jnp.arange(0, QK_ROPE_DIM, 2, dtype=F32) / QK_ROPE_DIM)
    inv_lanes = jnp.concatenate([inv, inv, jnp.zeros((LANES - 2 * half,), F32)]).reshape(1, LANES)
    tab = jax.ShapeDtypeStruct((BATCH, SEQ, LANES), F32)
    spec = pl.BlockSpec((1, tm, LANES), lambda b, i: (b, i, 0))
    return pl.pallas_call(
        _rope_table_body,
        grid=(BATCH, SEQ // tm),
        in_specs=[pl.BlockSpec((1, tm, 1), lambda b, i: (b, i, 0)),
                  pl.BlockSpec((1, LANES), lambda b, i: (0, 0))],
        out_specs=(spec, spec, spec),
        out_shape=(tab, tab, tab),
        compiler_params=_params(("parallel", "parallel")),
        name="rope_tables",
    )(positions.reshape(BATCH, SEQ, 1), inv_lanes)


def _inproj_body(x_ref, sh_ref, sc_ref, wq_ref, wkv_ref, wkr_ref, wu_ref, wv_ref,
                 cq_ref, ckv_ref, kr_ref, zu_ref, zv_ref):
    hm = _modulate(x_ref[0], sh_ref[0], sc_ref[0]).astype(BF16)
    cq_ref[0] = _dot(hm, wq_ref[...])
    ckv_ref[0] = _dot(hm, wkv_ref[...])
    kr_ref[0] = _dot(hm, wkr_ref[...])
    zu_ref[0] = _dot(hm, wu_ref[...])
    zv_ref[0] = _dot(hm, wv_ref[...])


def _inproj(x, mod, k, w_in):
    tm = 512
    a, b, c_, d = (Q_LORA_RANK, Q_LORA_RANK + KV_LORA_RANK,
                   Q_LORA_RANK + KV_LORA_RANK + QK_ROPE_DIM,
                   Q_LORA_RANK + KV_LORA_RANK + QK_ROPE_DIM + SGU_WIDTH)
    wq = w_in[:, :a].astype(BF16)
    wkv = w_in[:, a:b].astype(BF16)
    wkr = jnp.pad(w_in[:, b:c_], ((0, 0), (0, LANES - QK_ROPE_DIM))).astype(BF16)
    wu = w_in[:, c_:d].astype(BF16)
    wv = w_in[:, d:].astype(BF16)
    shift, scale, _ = _mod_specs(k, 2)
    widths = (Q_LORA_RANK, KV_LORA_RANK, LANES, SGU_WIDTH, SGU_WIDTH)
    return pl.pallas_call(
        _inproj_body,
        grid=(BATCH, SEQ // tm),
        in_specs=[pl.BlockSpec((1, tm, D_MODEL), lambda b, i: (b, i, 0)), shift, scale]
                 + [_const_spec((D_MODEL, w)) for w in widths],
        out_specs=tuple(pl.BlockSpec((1, tm, w), lambda b, i: (b, i, 0)) for w in widths),
        out_shape=tuple(jax.ShapeDtypeStruct((BATCH, SEQ, w), F32) for w in widths),
        compiler_params=_params(("parallel", "parallel")),
        name="inproj",
    )(x, mod, mod, wq, wkv, wkr, wu, wv)


def _qkv_body(cq_ref, ckv_ref, kr_ref, cos_ref, hi_ref, lo_ref, qg_ref, kvg_ref,
              wqn_ref, wqr_ref, wkn_ref, wvt_ref, q_ref, k_ref, vt_ref):
    cos_t, sin_hi, sin_lo = cos_ref[0], hi_ref[0], lo_ref[0]
    cqn = _rms(cq_ref[0], qg_ref[...]).astype(BF16)
    qn = _dot(cqn, wqn_ref[...])
    qr = _dot(cqn, wqr_ref[...])
    ckvn = _rms(ckv_ref[0], kvg_ref[...]).astype(BF16)
    kn = _dot(ckvn, wkn_ref[...])
    vt_ref[0] = lax.dot_general(wvt_ref[...], ckvn, (((1,), (1,)), ((), ())),
                                preferred_element_type=F32).astype(BF16)
    k_rope = _rope(kr_ref[0], cos_t, sin_hi, sin_lo).astype(BF16)
    for h in range(MLA_HEADS):
        nope = slice(h * LANES, (h + 1) * LANES)
        lo = h * HEAD_LANES
        q_ref[0, :, lo:lo + LANES] = qn[:, nope].astype(BF16)
        q_ref[0, :, lo + LANES:lo + HEAD_LANES] = _rope(qr[:, nope], cos_t, sin_hi, sin_lo).astype(BF16)
        k_ref[0, :, lo:lo + LANES] = kn[:, nope].astype(BF16)
        k_ref[0, :, lo + LANES:lo + HEAD_LANES] = k_rope


def _qkv(cq, ckv, kr, tables, q_norm_g, w_q_up, kv_norm_g, w_kv_up):
    tm = 512
    wq = w_q_up.reshape(Q_LORA_RANK, MLA_HEADS, QK_NOPE_DIM + QK_ROPE_DIM)
    wqn = wq[:, :, :QK_NOPE_DIM].reshape(Q_LORA_RANK, MLA_HEADS * LANES).astype(BF16)
    wqr = jnp.pad(wq[:, :, QK_NOPE_DIM:], ((0, 0), (0, 0), (0, LANES - QK_ROPE_DIM)))
    wqr = wqr.reshape(Q_LORA_RANK, MLA_HEADS * LANES).astype(BF16)
    wkv = w_kv_up.reshape(KV_LORA_RANK, MLA_HEADS, QK_NOPE_DIM + V_HEAD_DIM)
    wkn = wkv[:, :, :QK_NOPE_DIM].reshape(KV_LORA_RANK, MLA_HEADS * LANES).astype(BF16)
    wvt = wkv[:, :, QK_NOPE_DIM:].reshape(KV_LORA_RANK, ATTN_WIDTH).T.astype(BF16)

    def rows(w):
        return pl.BlockSpec((1, tm, w), lambda b, i: (b, i, 0))

    return pl.pallas_call(
        _qkv_body,
        grid=(BATCH, SEQ // tm),
        in_specs=[rows(Q_LORA_RANK), rows(KV_LORA_RANK), rows(LANES),
                  rows(LANES), rows(LANES), rows(LANES),
                  _const_spec((1, Q_LORA_RANK)), _const_spec((1, KV_LORA_RANK)),
                  _const_spec((Q_LORA_RANK, MLA_HEADS * LANES)),
                  _const_spec((Q_LORA_RANK, MLA_HEADS * LANES)),
                  _const_spec((KV_LORA_RANK, MLA_HEADS * LANES)),
                  _const_spec((ATTN_WIDTH, KV_LORA_RANK))],
        out_specs=(rows(QK_WIDTH), rows(QK_WIDTH),
                   pl.BlockSpec((1, ATTN_WIDTH, tm), lambda b, i: (b, 0, i))),
        out_shape=(jax.ShapeDtypeStruct((BATCH, SEQ, QK_WIDTH), BF16),
                   jax.ShapeDtypeStruct((BATCH, SEQ, QK_WIDTH), BF16),
                   jax.ShapeDtypeStruct((BATCH, ATTN_WIDTH, SEQ), BF16)),
        compiler_params=_params(("parallel", "parallel")),
        name="qkv_up",
    )(cq, ckv, kr, *tables, q_norm_g.reshape(1, -1), kv_norm_g.reshape(1, -1), wqn, wqr, wkn, wvt)


def _attn_body(qi_ref, kj_ref, q_ref, k_ref, vt_ref, o_ref, m_sc, l_sc, acc_sc, *, tq):
    p = pl.program_id(1)
    i = qi_ref[p]
    j = kj_ref[p]

    @pl.when(j == 0)
    def _():
        m_sc[...] = jnp.full(m_sc.shape, -jnp.inf, F32)
        l_sc[...] = jnp.zeros(l_sc.shape, F32)
        acc_sc[...] = jnp.zeros(acc_sc.shape, F32)

    def accumulate(diagonal):
        if diagonal:
            key = lax.broadcasted_iota(jnp.int32, (tq, tq), 0)
            qry = lax.broadcasted_iota(jnp.int32, (tq, tq), 1)
            visible = key <= qry
        for h in range(MLA_HEADS):
            qk = slice(h * HEAD_LANES, (h + 1) * HEAD_LANES)
            hv = slice(h * V_HEAD_DIM, (h + 1) * V_HEAD_DIM)
            s = lax.dot_general(k_ref[0, :, qk], q_ref[0, :, qk], (((1,), (1,)), ((), ())),
                                preferred_element_type=F32) * (SM_SCALE * LOG2_E)
            if diagonal:
                s = jnp.where(visible, s, -jnp.inf)
            m_prev = m_sc[h]
            m_new = jnp.maximum(m_prev, jnp.max(s, axis=0, keepdims=True))
            alpha = jnp.exp2(m_prev - m_new)
            pr = jnp.exp2(s - m_new)
            l_sc[h] = alpha * l_sc[h] + jnp.sum(pr, axis=0, keepdims=True)
            acc_sc[hv, :] = alpha * acc_sc[hv, :] + _dot(vt_ref[0, hv, :], pr.astype(BF16))
            m_sc[h] = m_new

    @pl.when(j < i)
    def _():
        accumulate(False)

    @pl.when(j == i)
    def _():
        accumulate(True)
        for h in range(MLA_HEADS):
            hv = slice(h * V_HEAD_DIM, (h + 1) * V_HEAD_DIM)
            o_ref[0, :, hv] = (acc_sc[hv, :] / l_sc[h]).T.astype(BF16)


def _attention(q, k, vt):
    tq = 512
    nb = SEQ // tq
    pairs = [(i, j) for i in range(nb) for j in range(i + 1)]
    qi = jnp.asarray(np.array([p[0] for p in pairs], np.int32))
    kj = jnp.asarray(np.array([p[1] for p in pairs], np.int32))
    grid_spec = pltpu.PrefetchScalarGridSpec(
        num_scalar_prefetch=2,
        grid=(BATCH, len(pairs)),
        in_specs=[pl.BlockSpec((1, tq, QK_WIDTH), lambda b, p, qi, kj: (b, qi[p], 0)),
                  pl.BlockSpec((1, tq, QK_WIDTH), lambda b, p, qi, kj: (b, kj[p], 0)),
                  pl.BlockSpec((1, ATTN_WIDTH, tq), lambda b, p, qi, kj: (b, 0, kj[p]))],
        out_specs=pl.BlockSpec((1, tq, ATTN_WIDTH), lambda b, p, qi, kj: (b, qi[p], 0)),
        scratch_shapes=[pltpu.VMEM((MLA_HEADS, 1, tq), F32),
                        pltpu.VMEM((MLA_HEADS, 1, tq), F32),
                        pltpu.VMEM((ATTN_WIDTH, tq), F32)],
    )
    return pl.pallas_call(
        functools.partial(_attn_body, tq=tq),
        grid_spec=grid_spec,
        out_shape=jax.ShapeDtypeStruct((BATCH, SEQ, ATTN_WIDTH), BF16),
        compiler_params=_params(("parallel", "arbitrary")),
        name="mla_attention",
    )(qi, kj, q, k, vt)


def _sgu_body(zu_ref, zv_ref, g_ref, b_ref, w_ref, bias_ref, o_ref, *, tm):
    row = lax.broadcasted_iota(jnp.int32, (CHUNK, CHUNK), 0)
    col = lax.broadcasted_iota(jnp.int32, (CHUNK, CHUNK), 1)
    for g in range(SGU_GROUPS):
        gs = slice(g * SGU_GROUP_DIM, (g + 1) * SGU_GROUP_DIM)
        v = jax.nn.gelu(zv_ref[0, :, gs])
        mu = jnp.mean(v, axis=-1, keepdims=True)
        vc = v - mu
        var = jnp.mean(vc * vc, axis=-1, keepdims=True)
        vn = (vc * lax.rsqrt(var + LN_EPS) * g_ref[:, gs] + b_ref[:, gs]).astype(BF16)
        w = jnp.where(col <= row, w_ref[g], 0.0).astype(BF16)
        for n in range(tm // CHUNK):
            rs = slice(n * CHUNK, (n + 1) * CHUNK)
            s = _dot(w, vn[rs]) + bias_ref[:, gs]
            o_ref[0, rs, gs] = (jax.nn.gelu(zu_ref[0, rs, gs]) * s).astype(BF16)


def _sgu(zu, zv, norm_g, norm_b, sgu_w, sgu_b):
    tm = 512
    bias = jnp.repeat(sgu_b.T, SGU_GROUP_DIM, axis=1)
    rows = pl.BlockSpec((1, tm, SGU_WIDTH), lambda b, i: (b, i, 0))
    return pl.pallas_call(
        functools.partial(_sgu_body, tm=tm),
        grid=(BATCH, SEQ // tm),
        in_specs=[rows, rows, _const_spec((1, SGU_WIDTH)), _const_spec((1, SGU_WIDTH)),
                  _const_spec((SGU_GROUPS, CHUNK, CHUNK)), _const_spec((CHUNK, SGU_WIDTH))],
        out_specs=rows,
        out_shape=jax.ShapeDtypeStruct((BATCH, SEQ, SGU_WIDTH), BF16),
        compiler_params=_params(("parallel", "parallel")),
        name="sgu",
    )(zu, zv, norm_g.reshape(1, -1), norm_b.reshape(1, -1), sgu_w, bias)


def _outproj_body(a_ref, s_ref, x_ref, gate_ref, wa_ref, ws_ref, lg_ref, lb_ref, o_ref):
    y = _dot(a_ref[0], wa_ref[...]) + _dot(s_ref[0], ws_ref[...])
    o_ref[0] = _deepnorm(x_ref[0], y, gate_ref[0], lg_ref[...], lb_ref[...])


def _outproj(attn, sgu, x, mod, k, w_out, ln_g, ln_b):
    tm = 512
    wa = w_out[:ATTN_WIDTH].astype(BF16)
    ws = w_out[ATTN_WIDTH:].astype(BF16)
    _, _, gate = _mod_specs(k, 2)

    def rows(w):
        return pl.BlockSpec((1, tm, w), lambda b, i: (b, i, 0))

    return pl.pallas_call(
        _outproj_body,
        grid=(BATCH, SEQ // tm),
        in_specs=[rows(ATTN_WIDTH), rows(SGU_WIDTH), rows(D_MODEL), gate,
                  _const_spec((ATTN_WIDTH, D_MODEL)), _const_spec((SGU_WIDTH, D_MODEL)),
                  _const_spec((1, D_MODEL)), _const_spec((1, D_MODEL))],
        out_specs=rows(D_MODEL),
        out_shape=jax.ShapeDtypeStruct((BATCH, SEQ, D_MODEL), F32),
        compiler_params=_params(("parallel", "parallel")),
        name="outproj_ab",
    )(attn, sgu, x, mod, wa, ws, ln_g.reshape(1, -1), ln_b.reshape(1, -1))


def _ffn_up_body(x_ref, sh_ref, sc_ref, wg_ref, wu_ref, h_ref, hm_sc):
    @pl.when(pl.program_id(2) == 0)
    def _():
        hm_sc[...] = _modulate(x_ref[0], sh_ref[0], sc_ref[0]).astype(BF16)

    hm = hm_sc[...]
    g = _dot(hm, wg_ref[0].astype(BF16))
    u = _dot(hm, wu_ref[0].astype(BF16))
    h_ref[0] = (jax.nn.silu(g) * u).astype(BF16)


def _ffn_down_body(h_ref, x_ref, gate_ref, wd_ref, lg_ref, lb_ref, o_ref, y_sc, *, n_n):
    n = pl.program_id(2)
    y_sc[n] = _dot(h_ref[0], wd_ref[...])

    @pl.when(n == n_n - 1)
    def _():
        y = jnp.concatenate([y_sc[c] for c in range(n_n)], axis=-1)
        o_ref[0] = _deepnorm(x_ref[0], y, gate_ref[0], lg_ref[...], lb_ref[...])


def _ffn(x, mod, k, layer, w_gate, w_up, w_down, ln_g, ln_b):
    tm, tf = 1024, 512
    shift, scale, gate = _mod_specs(k, 3)
    h = pl.pallas_call(
        _ffn_up_body,
        grid=(BATCH, SEQ // tm, D_FF // tf),
        in_specs=[pl.BlockSpec((1, tm, D_MODEL), lambda b, i, f: (b, i, 0), pipeline_mode=pl.Buffered(1)),
                  shift, scale,
                  pl.BlockSpec((1, D_MODEL, tf), lambda b, i, f: (layer, 0, f)),
                  pl.BlockSpec((1, D_MODEL, tf), lambda b, i, f: (layer, 0, f))],
        out_specs=pl.BlockSpec((1, tm, tf), lambda b, i, f: (b, i, f)),
        out_shape=jax.ShapeDtypeStruct((BATCH, SEQ, D_FF), BF16),
        scratch_shapes=[pltpu.VMEM((tm, D_MODEL), BF16)],
        compiler_params=_params(("parallel", "parallel", "arbitrary")),
        name="ffn_up",
    )(x, mod, mod, w_gate, w_up)

    tm, tn = 512, 512
    n_n = D_MODEL // tn
    rows = pl.BlockSpec((1, tm, D_MODEL), lambda b, i, n: (b, i, 0))
    vec = pl.BlockSpec((1, D_MODEL), lambda b, i, n: (0, 0))
    return pl.pallas_call(
        functools.partial(_ffn_down_body, n_n=n_n),
        grid=(BATCH, SEQ // tm, n_n),
        in_specs=[pl.BlockSpec((1, tm, D_FF), lambda b, i, n: (b, i, 0)),
                  pl.BlockSpec((1, tm, D_MODEL), lambda b, i, n: (b, i, 0), pipeline_mode=pl.Buffered(1)),
                  gate,
                  pl.BlockSpec((D_FF, tn), lambda b, i, n: (0, n)),
                  vec, vec],
        out_specs=rows,
        out_shape=jax.ShapeDtypeStruct((BATCH, SEQ, D_MODEL), F32),
        scratch_shapes=[pltpu.VMEM((n_n, tm, tn), F32)],
        compiler_params=_params(("parallel", "parallel", "arbitrary")),
        name="ffn_down",
    )(h, x, mod, w_down.astype(BF16), ln_g.reshape(1, -1), ln_b.reshape(1, -1))


def _pool_body(x_ref, halo_ref, sh_ref, sc_ref, gate_ref, pw_ref, ps_ref, wo_ref, lg_ref, lb_ref,
               o_ref, *, tm):
    i = pl.program_id(1)
    x = x_ref[0]
    hm = _modulate(x, sh_ref[0], sc_ref[0])
    halo = jnp.where(i > 0, _modulate(halo_ref[0], sh_ref[0], sc_ref[0]), 0.0)
    ext = jnp.concatenate([halo, hm], axis=0)
    ext_hi = ext.astype(BF16)
    ext_lo = (ext - ext_hi.astype(F32)).astype(BF16)
    t_loc = lax.broadcasted_iota(jnp.int32, (tm, tm + POOL_HALO), 0)
    s_loc = lax.broadcasted_iota(jnp.int32, (tm, tm + POOL_HALO), 1)
    lag = t_loc + POOL_HALO - s_loc
    t_glob = i * tm + lax.broadcasted_iota(jnp.int32, (tm, 1), 0)
    outs = []
    for gi, w in enumerate(POOL_WINDOWS):
        cs = slice(gi * POOL_GROUP_DIM, (gi + 1) * POOL_GROUP_DIM)
        band = jnp.where(lag >= 0, jnp.where(lag < w, 1.0, 0.0), 0.0).astype(BF16)
        wsum = _dot(band, ext_hi[:, cs]) + _dot(band, ext_lo[:, cs])
        count = jnp.minimum(t_glob + 1, w).astype(F32)
        d = (wsum / count - hm[:, cs]).astype(BF16)
        outs.append(_dot(d, pw_ref[gi]))
    pooled = (jnp.concatenate(outs, axis=-1) * ps_ref[...]).astype(BF16)
    y = _dot(pooled, wo_ref[...])
    o_ref[0] = _deepnorm(x, y, gate_ref[0], lg_ref[...], lb_ref[...])


def _pool_mixer(x, mod, k, pool_w, pool_scale, w_out_c, ln_g, ln_b):
    tm = 256
    per = tm // POOL_HALO
    shift, scale, gate = _mod_specs(k, 2)
    rows = pl.BlockSpec((1, tm, D_MODEL), lambda b, i: (b, i, 0))
    halo = pl.BlockSpec((1, POOL_HALO, D_MODEL), lambda b, i: (b, jnp.maximum(i * per - 1, 0), 0))
    return pl.pallas_call(
        functools.partial(_pool_body, tm=tm),
        grid=(BATCH, SEQ // tm),
        in_specs=[rows, halo, shift, scale, gate,
                  _const_spec((len(POOL_WINDOWS), POOL_GROUP_DIM, POOL_GROUP_DIM)),
                  _const_spec((1, D_MODEL)), _const_spec((D_MODEL, D_MODEL)),
                  _const_spec((1, D_MODEL)), _const_spec((1, D_MODEL))],
        out_specs=rows,
        out_shape=jax.ShapeDtypeStruct((BATCH, SEQ, D_MODEL), F32),
        compiler_params=_params(("parallel", "parallel")),
        name="pool_mixer",
    )(x, x, mod, mod, mod, pool_w.astype(BF16), pool_scale.reshape(1, -1),
      w_out_c.astype(BF16), ln_g.reshape(1, -1), ln_b.reshape(1, -1))


def _router_body(x_ref, sh_ref, sc_ref, wr_ref, g_ref, hm_ref):
    hm = _modulate(x_ref[0], sh_ref[0], sc_ref[0])
    hm_ref[0] = hm
    hm_hi = hm.astype(BF16)
    hm_lo = (hm - hm_hi.astype(F32)).astype(BF16)
    w = wr_ref[...]
    w_hi = w.astype(BF16)
    w_lo = (w - w_hi.astype(F32)).astype(BF16)
    logits = _dot(hm_hi, w_hi) + (_dot(hm_lo, w_hi) + _dot(hm_hi, w_lo))
    lane = lax.broadcasted_iota(jnp.int32, logits.shape, 1).astype(F32)
    logits = jnp.where(lane < N_EXPERTS, logits, -jnp.inf)
    v1 = jnp.max(logits, axis=-1, keepdims=True)
    i1 = jnp.min(jnp.where(logits == v1, lane, float(LANES)), axis=-1, keepdims=True)
    rest = jnp.where(lane == i1, -jnp.inf, logits)
    v2 = jnp.max(rest, axis=-1, keepdims=True)
    i2 = jnp.min(jnp.where(rest == v2, lane, float(LANES)), axis=-1, keepdims=True)
    e2 = jnp.exp(v2 - v1)
    den = 1.0 + e2
    g_ref[0] = (jnp.where(lane == 0.0, i1, 0.0) + jnp.where(lane == 1.0, i2, 0.0)
                + jnp.where(lane == 2.0, 1.0 / den, 0.0) + jnp.where(lane == 3.0, e2 / den, 0.0))


def _router(x, mod, k, w_router):
    tm = 512
    shift, scale, _ = _mod_specs(k, 2)
    wr = jnp.pad(w_router, ((0, 0), (0, LANES - N_EXPERTS)))
    rows = pl.BlockSpec((1, tm, D_MODEL), lambda b, i: (b, i, 0))
    return pl.pallas_call(
        _router_body,
        grid=(BATCH, SEQ // tm),
        in_specs=[rows, shift, scale, _const_spec((D_MODEL, LANES))],
        out_specs=(pl.BlockSpec((1, tm, LANES), lambda b, i: (b, i, 0)), rows),
        out_shape=(jax.ShapeDtypeStruct((BATCH, SEQ, LANES), F32),
                   jax.ShapeDtypeStruct((BATCH, SEQ, D_MODEL), F32)),
        compiler_params=_params(("parallel", "parallel")),
        name="router",
    )(x, mod, mod, wr)


MOE_TILE = 1024
MOE_SUB = 256
N_TOKENS = BATCH * SEQ
N_ASSIGN = 2 * N_TOKENS
MOE_ROWS = N_ASSIGN + N_EXPERTS * MOE_TILE
MOE_TILES = MOE_ROWS // MOE_TILE


def _route_plan(routing):
    e_flat = routing.reshape(N_TOKENS, LANES)[:, :2].astype(jnp.int32).T.reshape(N_ASSIGN)
    onehot = (e_flat[:, None] == jnp.arange(N_EXPERTS, dtype=jnp.int32)[None, :]).astype(jnp.int32)
    csum = jnp.cumsum(onehot, axis=0)
    counts = csum[-1]
    rank = jnp.sum(csum * onehot, axis=1) - 1
    padded = (counts + MOE_TILE - 1) // MOE_TILE * MOE_TILE
    group_end = jnp.cumsum(padded)
    group_start = group_end - padded
    pos = (jnp.sum(onehot * group_start[None, :], axis=1) + rank).astype(jnp.int32)
    tok = jnp.tile(jnp.arange(N_TOKENS, dtype=jnp.int32), 2)
    src = jnp.zeros((MOE_ROWS,), jnp.int32).at[pos].set(tok)
    tile_start = jnp.arange(MOE_TILES, dtype=jnp.int32) * MOE_TILE
    n_used = group_end[-1] // MOE_TILE
    tile_expert = jnp.sum((tile_start[:, None] >= group_end[None, :]).astype(jnp.int32), axis=1)
    last_expert = jnp.max(jnp.where(counts > 0, jnp.arange(N_EXPERTS, dtype=jnp.int32), 0))
    tile_expert = jnp.minimum(tile_expert, last_expert)
    tile_rows = jnp.clip(counts[tile_expert] - (tile_start - group_start[tile_expert]), 0, MOE_TILE)
    tile_rows = jnp.where(jnp.arange(MOE_TILES) < n_used, tile_rows, 0).astype(jnp.int32)
    return pos, src, tile_expert.astype(jnp.int32), tile_rows, n_used.reshape(1).astype(jnp.int32)


GATHER_UNROLL = 8


def _gather_body(src_ref, rows_ref, hm_hbm, o_ref, buf, sem, *, tg):
    i = pl.program_id(0)
    base = i * tg
    per = MOE_TILE // tg
    n_valid = jnp.clip(rows_ref[i // per] - (i % per) * tg, 0, tg)

    @pl.when(n_valid < tg)
    def _():
        buf[...] = jnp.zeros(buf.shape, F32)

    def issue_row(r):
        t = src_ref[base + r]
        pltpu.make_async_copy(hm_hbm.at[pl.ds(t, 1)], buf.at[pl.ds(r, 1)], sem).start()

    def issue_group(g, carry):
        for u in range(GATHER_UNROLL):
            issue_row(g * GATHER_UNROLL + u)
        return carry

    def issue_one(r, carry):
        issue_row(r)
        return carry

    n_groups = n_valid // GATHER_UNROLL
    n_grouped = pl.multiple_of(n_groups * GATHER_UNROLL, GATHER_UNROLL)
    lax.fori_loop(0, n_groups, issue_group, 0)
    lax.fori_loop(n_grouped, n_valid, issue_one, 0)

    @pl.when(n_groups > 0)
    def _():
        pltpu.make_async_copy(hm_hbm.at[pl.ds(0, n_grouped)], buf.at[pl.ds(0, n_grouped)], sem).wait()

    def wait_one(r, carry):
        pltpu.make_async_copy(hm_hbm.at[pl.ds(0, 1)], buf.at[pl.ds(r, 1)], sem).wait()
        return carry

    lax.fori_loop(n_grouped, n_valid, wait_one, 0)

    o_ref[...] = buf[...].astype(BF16)


def _gather_rows(src, tile_rows, hm):
    tg = 512
    grid_spec = pltpu.PrefetchScalarGridSpec(
        num_scalar_prefetch=2,
        grid=(MOE_ROWS // tg,),
        in_specs=[pl.BlockSpec(memory_space=pl.ANY)],
        out_specs=pl.BlockSpec((tg, D_MODEL), lambda i, src, tr: (i, 0)),
        scratch_shapes=[pltpu.VMEM((tg, D_MODEL), F32), pltpu.SemaphoreType.DMA(())],
    )
    return pl.pallas_call(
        functools.partial(_gather_body, tg=tg),
        grid_spec=grid_spec,
        out_shape=jax.ShapeDtypeStruct((MOE_ROWS, D_MODEL), BF16),
        compiler_params=_params(("arbitrary",)),
        name="moe_gather",
    )(src, tile_rows, hm.reshape(N_TOKENS, D_MODEL))


def _for_valid_blocks(rows, tm, sub, full_fn, block_fn, empty_fn):
    @pl.when(rows == tm)
    def _():
        full_fn()

    for s in range(tm // sub):
        rs = slice(s * sub, (s + 1) * sub)

        @pl.when(jnp.logical_and(rows < tm, s * sub < rows))
        def _():
            block_fn(rs)

        @pl.when(s * sub >= rows)
        def _():
            empty_fn(rs)


def _gup_body(te_ref, rows_ref, nu_ref, xs_ref, wg_ref, wu_ref, h_ref, wgb, wub, *, tm, sub):
    rows = rows_ref[pl.program_id(0)]

    @pl.when(rows > 0)
    def _():
        wgb[...] = wg_ref[0].astype(BF16)
        wub[...] = wu_ref[0].astype(BF16)

    def swiglu(xsb):
        return (jax.nn.silu(_dot(xsb, wgb[...])) * _dot(xsb, wub[...])).astype(BF16)

    def full():
        h_ref[...] = swiglu(xs_ref[...])

    def block(rs):
        h_ref[rs, :] = swiglu(xs_ref[rs, :])

    def empty(rs):
        h_ref[rs, :] = jnp.zeros((sub, h_ref.shape[1]), BF16)

    _for_valid_blocks(rows, tm, sub, full, block, empty)


def _gdown_body(te_ref, rows_ref, nu_ref, h_ref, wd_ref, o_ref, wdb, *, tm, sub):
    rows = rows_ref[pl.program_id(0)]

    @pl.when(rows > 0)
    def _():
        wdb[...] = wd_ref[0].astype(BF16)

    def full():
        o_ref[...] = _dot(h_ref[...], wdb[...])

    def block(rs):
        o_ref[rs, :] = _dot(h_ref[rs, :], wdb[...])

    def empty(rs):
        o_ref[rs, :] = jnp.zeros((sub, o_ref.shape[1]), F32)

    _for_valid_blocks(rows, tm, sub, full, block, empty)


def _grouped_ffn(xs, tile_expert, tile_rows, n_used, layer, w_gate, w_up, w_down):
    tm, sub = MOE_TILE, MOE_SUB
    first = layer * N_EXPERTS
    n_all = w_gate.shape[0] * N_EXPERTS

    def tile(i, nu):
        return jnp.minimum(i, nu[0] - 1)

    def chunk(i, c, nu, n_chunks):
        return jnp.where(i < nu[0], c, n_chunks - 1)

    tf = 256
    n_f = D_FF_EXPERT // tf
    w_up_spec = pl.BlockSpec((1, D_MODEL, tf),
                             lambda i, f, te, tr, nu: (first + te[i], 0, chunk(i, f, nu, n_f)))
    up_spec = pltpu.PrefetchScalarGridSpec(
        num_scalar_prefetch=3,
        grid=(MOE_TILES, n_f),
        in_specs=[pl.BlockSpec((tm, D_MODEL), lambda i, f, te, tr, nu: (tile(i, nu), 0)),
                  w_up_spec, w_up_spec],
        out_specs=pl.BlockSpec((tm, tf), lambda i, f, te, tr, nu: (i, f)),
        scratch_shapes=[pltpu.VMEM((D_MODEL, tf), BF16), pltpu.VMEM((D_MODEL, tf), BF16)],
    )
    h = pl.pallas_call(
        functools.partial(_gup_body, tm=tm, sub=sub),
        grid_spec=up_spec,
        out_shape=jax.ShapeDtypeStruct((MOE_ROWS, D_FF_EXPERT), BF16),
        compiler_params=_params(("arbitrary", "arbitrary")),
        name="moe_up",
    )(tile_expert, tile_rows, n_used, xs,
      w_gate.reshape(n_all, D_MODEL, D_FF_EXPERT), w_up.reshape(n_all, D_MODEL, D_FF_EXPERT))

    tn = 512
    n_n = D_MODEL // tn
    down_spec = pltpu.PrefetchScalarGridSpec(
        num_scalar_prefetch=3,
        grid=(MOE_TILES, n_n),
        in_specs=[pl.BlockSpec((tm, D_FF_EXPERT), lambda i, n, te, tr, nu: (tile(i, nu), 0)),
                  pl.BlockSpec((1, D_FF_EXPERT, tn),
                               lambda i, n, te, tr, nu: (first + te[i], 0, chunk(i, n, nu, n_n)))],
        out_specs=pl.BlockSpec((tm, tn), lambda i, n, te, tr, nu: (i, n)),
        scratch_shapes=[pltpu.VMEM((D_FF_EXPERT, tn), BF16)],
    )
    return pl.pallas_call(
        functools.partial(_gdown_body, tm=tm, sub=sub),
        grid_spec=down_spec,
        out_shape=jax.ShapeDtypeStruct((MOE_ROWS, D_MODEL), F32),
        compiler_params=_params(("arbitrary", "arbitrary")),
        name="moe_down",
    )(tile_expert, tile_rows, n_used, h, w_down.reshape(n_all, D_FF_EXPERT, D_MODEL))


def _combine_body(pos_ref, ys_hbm, r_ref, x_ref, gate_ref, lg_ref, lb_ref, o_ref, buf, sem, *, tc):
    base = (pl.program_id(0) * (SEQ // tc) + pl.program_id(1)) * tc

    def issue(r, carry):
        for k in range(2):
            p = pos_ref[k * N_TOKENS + base + r]
            pltpu.make_async_copy(ys_hbm.at[pl.ds(p, 1)], buf.at[k, pl.ds(r, 1)], sem.at[k]).start()
        return carry

    lax.fori_loop(0, tc, issue, 0, unroll=4)
    for k in range(2):
        pltpu.make_async_copy(ys_hbm.at[pl.ds(0, tc)], buf.at[k], sem.at[k]).wait()
    routing = r_ref[0]
    y = routing[:, 2:3] * buf[0] + routing[:, 3:4] * buf[1]
    o_ref[0] = _deepnorm(x_ref[0], y, gate_ref[0], lg_ref[...], lb_ref[...])


def _combine(pos, ys, routing, x, mod, k, ln_g, ln_b):
    tc = 512
    rows = pl.BlockSpec((1, tc, D_MODEL), lambda b, i, pos: (b, i, 0))
    vec = pl.BlockSpec((1, D_MODEL), lambda b, i, pos: (0, 0))
    grid_spec = pltpu.PrefetchScalarGridSpec(
        num_scalar_prefetch=1,
        grid=(BATCH, SEQ // tc),
        in_specs=[pl.BlockSpec(memory_space=pl.ANY),
                  pl.BlockSpec((1, tc, LANES), lambda b, i, pos: (b, i, 0)),
                  rows,
                  pl.BlockSpec((1, 1, D_MODEL), lambda b, i, pos: (k * 8 + b, 0, 2)),
                  vec, vec],
        out_specs=rows,
        scratch_shapes=[pltpu.VMEM((2, tc, D_MODEL), F32), pltpu.SemaphoreType.DMA((2,))],
    )
    return pl.pallas_call(
        functools.partial(_combine_body, tc=tc),
        grid_spec=grid_spec,
        out_shape=jax.ShapeDtypeStruct((BATCH, SEQ, D_MODEL), F32),
        compiler_params=_params(("arbitrary", "arbitrary")),
        name="moe_combine",
    )(pos, ys, routing, x, mod, ln_g.reshape(1, -1), ln_b.reshape(1, -1))


def _moe(x, mod, k, w_router, layer, w_gate, w_up, w_down, ln_g, ln_b):
    routing, hm = _router(x, mod, k, w_router)
    pos, src, tile_expert, tile_rows, n_used = _route_plan(routing)
    xs = _gather_rows(src, tile_rows, hm)
    ys = _grouped_ffn(xs, tile_expert, tile_rows, n_used, layer, w_gate, w_up, w_down)
    return _combine(pos, ys, routing, x, mod, k, ln_g, ln_b)


def kernel(x, c, positions, ada_w, ada_b, ln_g, ln_b, w_in_ab, q_norm_g, w_q_up, kv_norm_g, w_kv_up, sgu_norm_g, sgu_norm_b, sgu_w, sgu_b, w_out_ab, ffn_w_gate, ffn_w_up, ffn_w_down, pool_w, pool_scale, w_out_c, router_w, moe_w_gate, moe_w_up, moe_w_down):
    mod = _ada_modulation(c, ada_w, ada_b)
    tables = _rope_tables(positions)
    for l in range(DEPTH):
        j = l // 2
        k_tok, k_ch = 2 * l, 2 * l + 1
        if l % 2 == 0:
            cq, ckv, kr, zu, zv = _inproj(x, mod, k_tok, w_in_ab[j])
            q, k, vt = _qkv(cq, ckv, kr, tables, q_norm_g[j], w_q_up[j], kv_norm_g[j], w_kv_up[j])
            attn = _attention(q, k, vt)
            sgu = _sgu(zu, zv, sgu_norm_g[j], sgu_norm_b[j], sgu_w[j], sgu_b[j])
            x = _outproj(attn, sgu, x, mod, k_tok, w_out_ab[j], ln_g[l, 0], ln_b[l, 0])
            x = _ffn(x, mod, k_ch, j, ffn_w_gate, ffn_w_up, ffn_w_down[j], ln_g[l, 1], ln_b[l, 1])
        else:
            x = _pool_mixer(x, mod, k_tok, pool_w[j], pool_scale[j], w_out_c[j], ln_g[l, 0], ln_b[l, 0])
            x = _moe(x, mod, k_ch, router_w[j], j, moe_w_gate, moe_w_up, moe_w_down,
                     ln_g[l, 1], ln_b[l, 1])
    return x
```

```python
import functools

import jax
import jax.numpy as jnp
import numpy as np
from jax import lax
from jax.experimental import pallas as pl
from jax.experimental.pallas import tpu as pltpu

F32 = jnp.float32
BF16 = jnp.bfloat16

D_MODEL = 2048
BATCH = 2
SEQ = 4096
DEPTH = 4
MLA_HEADS = 8
QK_NOPE_DIM = 128
QK_ROPE_DIM = 64
V_HEAD_DIM = 128
Q_LORA_RANK = 768
KV_LORA_RANK = 512
ROPE_THETA = 10000.0
SGU_GROUPS = 8
SGU_GROUP_DIM = 128
SGU_WIDTH = SGU_GROUPS * SGU_GROUP_DIM
CHUNK = 128
ATTN_WIDTH = MLA_HEADS * V_HEAD_DIM
POOL_WINDOWS = (2, 4, 8, 16)
POOL_GROUP_DIM = D_MODEL // 4
D_FF = 5632
N_EXPERTS = 8
D_FF_EXPERT = 2816
DEEPNORM_ALPHA = (2 * DEPTH) ** 0.25
LN_EPS = 1e-5
RMS_EPS = 1e-6
SM_SCALE = (QK_NOPE_DIM + QK_ROPE_DIM) ** -0.5
LOG2_E = 1.4426950408889634

LANES = 128
HEAD_LANES = 2 * LANES
QK_WIDTH = MLA_HEADS * HEAD_LANES
VMEM_LIMIT = 60 * 1024 * 1024
POOL_HALO = 128


def _params(semantics):
    return pltpu.CompilerParams(dimension_semantics=semantics, vmem_limit_bytes=VMEM_LIMIT)


def _const_spec(shape):
    nd = len(shape)
    return pl.BlockSpec(shape, lambda *_: (0,) * nd, pipeline_mode=pl.Buffered(1))


def _dot(a, b):
    return jnp.dot(a, b, preferred_element_type=F32)


def _modulate(x, shift, scale):
    return x * (1.0 + scale) + shift


def _deepnorm(x, y, gate, g, b):
    r = DEEPNORM_ALPHA * x + (1.0 + gate) * y
    mu = jnp.mean(r, axis=-1, keepdims=True)
    rc = r - mu
    var = jnp.mean(rc * rc, axis=-1, keepdims=True)
    return rc * lax.rsqrt(var + LN_EPS) * g + b


def _rms(x, g):
    return x * lax.rsqrt(jnp.mean(x * x, axis=-1, keepdims=True) + RMS_EPS) * g


def _rope(r, cos_t, sin_hi, sin_lo):
    return (r * cos_t + pltpu.roll(r, QK_ROPE_DIM // 2, 1) * sin_hi
            + pltpu.roll(r, LANES - QK_ROPE_DIM // 2, 1) * sin_lo)


def _mod_specs(k, n_grid):
    def spec(part):
        if n_grid == 2:
            return pl.BlockSpec((1, 1, D_MODEL), lambda b, i: (k * 8 + b, 0, part))
        if n_grid == 3:
            return pl.BlockSpec((1, 1, D_MODEL), lambda b, i, f: (k * 8 + b, 0, part))
        return pl.BlockSpec((1, 1, D_MODEL), lambda b, i, e, f: (k * 8 + b, 0, part))
    return spec(0), spec(1), spec(2)


def _ada_body(c_ref, w_ref, b_ref, o_ref):
    s = jax.nn.silu(c_ref[...]).astype(BF16)
    o_ref[0] = _dot(s, w_ref[0].astype(BF16)) + b_ref[0]


def _ada_modulation(c, ada_w, ada_b):
    n_mod = 2 * DEPTH
    tn = 1024
    c_pad = jnp.pad(c, ((0, 8 - BATCH), (0, 0)))
    out = pl.pallas_call(
        _ada_body,
        grid=(n_mod, 3 * D_MODEL // tn),
        in_specs=[
            pl.BlockSpec((8, D_MODEL), lambda k, n: (0, 0)),
            pl.BlockSpec((1, D_MODEL, tn), lambda k, n: (k, 0, n)),
            pl.BlockSpec((1, 1, tn), lambda k, n: (k, 0, n)),
        ],
        out_specs=pl.BlockSpec((1, 8, tn), lambda k, n: (k, 0, n)),
        out_shape=jax.ShapeDtypeStruct((n_mod, 8, 3 * D_MODEL), F32),
        compiler_params=_params(("parallel", "parallel")),
        name="ada_mod",
    )(c_pad, ada_w.reshape(n_mod, D_MODEL, 3 * D_MODEL), ada_b.reshape(n_mod, 1, 3 * D_MODEL))
    return out.reshape(n_mod * 8, 1, 3 * D_MODEL)


def _rope_table_body(pos_ref, inv_ref, cos_ref, hi_ref, lo_ref):
    ang = pos_ref[0].astype(F32) * inv_ref[...]
    lane = lax.broadcasted_iota(jnp.int32, ang.shape, 1)
    cos = jnp.cos(ang)
    sin = jnp.sin(ang)
    half = QK_ROPE_DIM // 2
    cos_ref[0] = jnp.where(lane < QK_ROPE_DIM, cos, 0.0)
    hi_ref[0] = jnp.where(lane < half, 0.0, jnp.where(lane < QK_ROPE_DIM, sin, 0.0))
    lo_ref[0] = jnp.where(lane < half, -sin, 0.0)


def _rope_tables(positions):
    tm = 512
    half = QK_ROPE_DIM // 2
    inv = ROPE_THETA ** (-jnp.arange(0, QK_ROPE_DIM, 2, dtype=F32) / QK_ROPE_DIM)
    inv_lanes = jnp.concatenate([inv, inv, jnp.zeros((LANES - 2 * half,), F32)]).reshape(1, LANES)
    tab = jax.ShapeDtypeStruct((BATCH, SEQ, LANES), F32)
    spec = pl.BlockSpec((1, tm, LANES), lambda b, i: (b, i, 0))
    return pl.pallas_call(
        _rope_table_body,
        grid=(BATCH, SEQ // tm),
        in_specs=[pl.BlockSpec((1, tm, 1), lambda b, i: (b, i, 0)),
                  pl.BlockSpec((1, LANES), lambda b, i: (0, 0))],
        out_specs=(spec, spec, spec),
        out_shape=(tab, tab, tab),
        compiler_params=_params(("parallel", "parallel")),
        name="rope_tables",
    )(positions.reshape(BATCH, SEQ, 1), inv_lanes)


def _inproj_body(x_ref, sh_ref, sc_ref, wq_ref, wkv_ref, wkr_ref, wu_ref, wv_ref,
                 cq_ref, ckv_ref, kr_ref, zu_ref, zv_ref):
    hm = _modulate(x_ref[0], sh_ref[0], sc_ref[0]).astype(BF16)
    cq_ref[0] = _dot(hm, wq_ref[...])
    ckv_ref[0] = _dot(hm, wkv_ref[...])
    kr_ref[0] = _dot(hm, wkr_ref[...])
    zu_ref[0] = _dot(hm, wu_ref[...])
    zv_ref[0] = _dot(hm, wv_ref[...])


def _inproj(x, mod, k, w_in):
    tm = 512
    a, b, c_, d = (Q_LORA_RANK, Q_LORA_RANK + KV_LORA_RANK,
                   Q_LORA_RANK + KV_LORA_RANK + QK_ROPE_DIM,
                   Q_LORA_RANK + KV_LORA_RANK + QK_ROPE_DIM + SGU_WIDTH)
    wq = w_in[:, :a].astype(BF16)
    wkv = w_in[:, a:b].astype(BF16)
    wkr = jnp.pad(w_in[:, b:c_], ((0, 0), (0, LANES - QK_ROPE_DIM))).astype(BF16)
    wu = w_in[:, c_:d].astype(BF16)
    wv = w_in[:, d:].astype(BF16)
    shift, scale, _ = _mod_specs(k, 2)
    widths = (Q_LORA_RANK, KV_LORA_RANK, LANES, SGU_WIDTH, SGU_WIDTH)
    return pl.pallas_call(
        _inproj_body,
        grid=(BATCH, SEQ // tm),
        in_specs=[pl.BlockSpec((1, tm, D_MODEL), lambda b, i: (b, i, 0)), shift, scale]
                 + [_const_spec((D_MODEL, w)) for w in widths],
        out_specs=tuple(pl.BlockSpec((1, tm, w), lambda b, i: (b, i, 0)) for w in widths),
        out_shape=tuple(jax.ShapeDtypeStruct((BATCH, SEQ, w), F32) for w in widths),
        compiler_params=_params(("parallel", "parallel")),
        name="inproj",
    )(x, mod, mod, wq, wkv, wkr, wu, wv)


def _qkv_body(cq_ref, ckv_ref, kr_ref, cos_ref, hi_ref, lo_ref, qg_ref, kvg_ref,
              wqn_ref, wqr_ref, wkn_ref, wvt_ref, q_ref, k_ref, vt_ref):
    cos_t, sin_hi, sin_lo = cos_ref[0], hi_ref[0], lo_ref[0]
    cqn = _rms(cq_ref[0], qg_ref[...]).astype(BF16)
    qn = _dot(cqn, wqn_ref[...])
    qr = _dot(cqn, wqr_ref[...])
    ckvn = _rms(ckv_ref[0], kvg_ref[...]).astype(BF16)
    kn = _dot(ckvn, wkn_ref[...])
    vt_ref[0] = lax.dot_general(wvt_ref[...], ckvn, (((1,), (1,)), ((), ())),
                                preferred_element_type=F32).astype(BF16)
    k_rope = _rope(kr_ref[0], cos_t, sin_hi, sin_lo).astype(BF16)
    for h in range(MLA_HEADS):
        nope = slice(h * LANES, (h + 1) * LANES)
        lo = h * HEAD_LANES
        q_ref[0, :, lo:lo + LANES] = qn[:, nope].astype(BF16)
        q_ref[0, :, lo + LANES:lo + HEAD_LANES] = _rope(qr[:, nope], cos_t, sin_hi, sin_lo).astype(BF16)
        k_ref[0, :, lo:lo + LANES] = kn[:, nope].astype(BF16)
        k_ref[0, :, lo + LANES:lo + HEAD_LANES] = k_rope


def _qkv(cq, ckv, kr, tables, q_norm_g, w_q_up, kv_norm_g, w_kv_up):
    tm = 512
    wq = w_q_up.reshape(Q_LORA_RANK, MLA_HEADS, QK_NOPE_DIM + QK_ROPE_DIM)
    wqn = wq[:, :, :QK_NOPE_DIM].reshape(Q_LORA_RANK, MLA_HEADS * LANES).astype(BF16)
    wqr = jnp.pad(wq[:, :, QK_NOPE_DIM:], ((0, 0), (0, 0), (0, LANES - QK_ROPE_DIM)))
    wqr = wqr.reshape(Q_LORA_RANK, MLA_HEADS * LANES).astype(BF16)
    wkv = w_kv_up.reshape(KV_LORA_RANK, MLA_HEADS, QK_NOPE_DIM + V_HEAD_DIM)
    wkn = wkv[:, :, :QK_NOPE_DIM].reshape(KV_LORA_RANK, MLA_HEADS * LANES).astype(BF16)
    wvt = wkv[:, :, QK_NOPE_DIM:].reshape(KV_LORA_RANK, ATTN_WIDTH).T.astype(BF16)

    def rows(w):
        return pl.BlockSpec((1, tm, w), lambda b, i: (b, i, 0))

    return pl.pallas_call(
        _qkv_body,
        grid=(BATCH, SEQ // tm),
        in_specs=[rows(Q_LORA_RANK), rows(KV_LORA_RANK), rows(LANES),
                  rows(LANES), rows(LANES), rows(LANES),
                  _const_spec((1, Q_LORA_RANK)), _const_spec((1, KV_LORA_RANK)),
                  _const_spec((Q_LORA_RANK, MLA_HEADS * LANES)),
                  _const_spec((Q_LORA_RANK, MLA_HEADS * LANES)),
                  _const_spec((KV_LORA_RANK, MLA_HEADS * LANES)),
                  _const_spec((ATTN_WIDTH, KV_LORA_RANK))],
        out_specs=(rows(QK_WIDTH), rows(QK_WIDTH),
                   pl.BlockSpec((1, ATTN_WIDTH, tm), lambda b, i: (b, 0, i))),
        out_shape=(jax.ShapeDtypeStruct((BATCH, SEQ, QK_WIDTH), BF16),
                   jax.ShapeDtypeStruct((BATCH, SEQ, QK_WIDTH), BF16),
                   jax.ShapeDtypeStruct((BATCH, ATTN_WIDTH, SEQ), BF16)),
        compiler_params=_params(("parallel", "parallel")),
        name="qkv_up",
    )(cq, ckv, kr, *tables, q_norm_g.reshape(1, -1), kv_norm_g.reshape(1, -1), wqn, wqr, wkn, wvt)


def _attn_body(qi_ref, kj_ref, q_ref, k_ref, vt_ref, o_ref, m_sc, l_sc, acc_sc, *, tq):
    p = pl.program_id(1)
    i = qi_ref[p]
    j = kj_ref[p]

    @pl.when(j == 0)
    def _():
        m_sc[...] = jnp.full(m_sc.shape, -jnp.inf, F32)
        l_sc[...] = jnp.zeros(l_sc.shape, F32)
        acc_sc[...] = jnp.zeros(acc_sc.shape, F32)

    def accumulate(diagonal):
        if diagonal:
            key = lax.broadcasted_iota(jnp.int32, (tq, tq), 0)
            qry = lax.broadcasted_iota(jnp.int32, (tq, tq), 1)
            visible = key <= qry
        for h in range(MLA_HEADS):
            qk = slice(h * HEAD_LANES, (h + 1) * HEAD_LANES)
            hv = slice(h * V_HEAD_DIM, (h + 1) * V_HEAD_DIM)
            s = lax.dot_general(k_ref[0, :, qk], q_ref[0, :, qk], (((1,), (1,)), ((), ())),
                                preferred_element_type=F32) * (SM_SCALE * LOG2_E)
            if diagonal:
                s = jnp.where(visible, s, -jnp.inf)
            m_prev = m_sc[h]
            m_new = jnp.maximum(m_prev, jnp.max(s, axis=0, keepdims=True))
            alpha = jnp.exp2(m_prev - m_new)
            pr = jnp.exp2(s - m_new)
            l_sc[h] = alpha * l_sc[h] + jnp.sum(pr, axis=0, keepdims=True)
            acc_sc[hv, :] = alpha * acc_sc[hv, :] + _dot(vt_ref[0, hv, :], pr.astype(BF16))
            m_sc[h] = m_new

    @pl.when(j < i)
    def _():
        accumulate(False)

    @pl.when(j == i)
    def _():
        accumulate(True)
        for h in range(MLA_HEADS):
            hv = slice(h * V_HEAD_DIM, (h + 1) * V_HEAD_DIM)
            o_ref[0, :, hv] = (acc_sc[hv, :] / l_sc[h]).T.astype(BF16)


def _attention(q, k, vt):
    tq = 512
    nb = SEQ // tq
    pairs = [(i, j) for i in range(nb) for j in range(i + 1)]
    qi = jnp.asarray(np.array([p[0] for p in pairs], np.int32))
    kj = jnp.asarray(np.array([p[1] for p in pairs], np.int32))
    grid_spec = pltpu.PrefetchScalarGridSpec(
        num_scalar_prefetch=2,
        grid=(BATCH, len(pairs)),
        in_specs=[pl.BlockSpec((1, tq, QK_WIDTH), lambda b, p, qi, kj: (b, qi[p], 0)),
                  pl.BlockSpec((1, tq, QK_WIDTH), lambda b, p, qi, kj: (b, kj[p], 0)),
                  pl.BlockSpec((1, ATTN_WIDTH, tq), lambda b, p, qi, kj: (b, 0, kj[p]))],
        out_specs=pl.BlockSpec((1, tq, ATTN_WIDTH), lambda b, p, qi, kj: (b, qi[p], 0)),
        scratch_shapes=[pltpu.VMEM((MLA_HEADS, 1, tq), F32),
                        pltpu.VMEM((MLA_HEADS, 1, tq), F32),
                        pltpu.VMEM((ATTN_WIDTH, tq), F32)],
    )
    return pl.pallas_call(
        functools.partial(_attn_body, tq=tq),
        grid_spec=grid_spec,
        out_shape=jax.ShapeDtypeStruct((BATCH, SEQ, ATTN_WIDTH), BF16),
        compiler_params=_params(("parallel", "arbitrary")),
        name="mla_attention",
    )(qi, kj, q, k, vt)


def _sgu_body(zu_ref, zv_ref, g_ref, b_ref, w_ref, bias_ref, o_ref, *, tm):
    row = lax.broadcasted_iota(jnp.int32, (CHUNK, CHUNK), 0)
    col = lax.broadcasted_iota(jnp.int32, (CHUNK, CHUNK), 1)
    for g in range(SGU_GROUPS):
        gs = slice(g * SGU_GROUP_DIM, (g + 1) * SGU_GROUP_DIM)
        v = jax.nn.gelu(zv_ref[0, :, gs])
        mu = jnp.mean(v, axis=-1, keepdims=True)
        vc = v - mu
        var = jnp.mean(vc * vc, axis=-1, keepdims=True)
        vn = (vc * lax.rsqrt(var + LN_EPS) * g_ref[:, gs] + b_ref[:, gs]).astype(BF16)
        w = jnp.where(col <= row, w_ref[g], 0.0).astype(BF16)
        for n in range(tm // CHUNK):
            rs = slice(n * CHUNK, (n + 1) * CHUNK)
            s = _dot(w, vn[rs]) + bias_ref[:, gs]
            o_ref[0, rs, gs] = (jax.nn.gelu(zu_ref[0, rs, gs]) * s).astype(BF16)


def _sgu(zu, zv, norm_g, norm_b, sgu_w, sgu_b):
    tm = 512
    bias = jnp.repeat(sgu_b.T, SGU_GROUP_DIM, axis=1)
    rows = pl.BlockSpec((1, tm, SGU_WIDTH), lambda b, i: (b, i, 0))
    return pl.pallas_call(
        functools.partial(_sgu_body, tm=tm),
        grid=(BATCH, SEQ // tm),
        in_specs=[rows, rows, _const_spec((1, SGU_WIDTH)), _const_spec((1, SGU_WIDTH)),
                  _const_spec((SGU_GROUPS, CHUNK, CHUNK)), _const_spec((CHUNK, SGU_WIDTH))],
        out_specs=rows,
        out_shape=jax.ShapeDtypeStruct((BATCH, SEQ, SGU_WIDTH), BF16),
        compiler_params=_params(("parallel", "parallel")),
        name="sgu",
    )(zu, zv, norm_g.reshape(1, -1), norm_b.reshape(1, -1), sgu_w, bias)


def _outproj_body(a_ref, s_ref, x_ref, gate_ref, wa_ref, ws_ref, lg_ref, lb_ref, o_ref):
    y = _dot(a_ref[0], wa_ref[...]) + _dot(s_ref[0], ws_ref[...])
    o_ref[0] = _deepnorm(x_ref[0], y, gate_ref[0], lg_ref[...], lb_ref[...])


def _outproj(attn, sgu, x, mod, k, w_out, ln_g, ln_b):
    tm = 512
    wa = w_out[:ATTN_WIDTH].astype(BF16)
    ws = w_out[ATTN_WIDTH:].astype(BF16)
    _, _, gate = _mod_specs(k, 2)

    def rows(w):
        return pl.BlockSpec((1, tm, w), lambda b, i: (b, i, 0))

    return pl.pallas_call(
        _outproj_body,
        grid=(BATCH, SEQ // tm),
        in_specs=[rows(ATTN_WIDTH), rows(SGU_WIDTH), rows(D_MODEL), gate,
                  _const_spec((ATTN_WIDTH, D_MODEL)), _const_spec((SGU_WIDTH, D_MODEL)),
                  _const_spec((1, D_MODEL)), _const_spec((1, D_MODEL))],
        out_specs=rows(D_MODEL),
        out_shape=jax.ShapeDtypeStruct((BATCH, SEQ, D_MODEL), F32),
        compiler_params=_params(("parallel", "parallel")),
        name="outproj_ab",
    )(attn, sgu, x, mod, wa, ws, ln_g.reshape(1, -1), ln_b.reshape(1, -1))


def _ffn_up_body(x_ref, sh_ref, sc_ref, wg_ref, wu_ref, h_ref, hm_sc):
    @pl.when(pl.program_id(2) == 0)
    def _():
        hm_sc[...] = _modulate(x_ref[0], sh_ref[0], sc_ref[0]).astype(BF16)

    hm = hm_sc[...]
    g = _dot(hm, wg_ref[0].astype(BF16))
    u = _dot(hm, wu_ref[0].astype(BF16))
    h_ref[0] = (jax.nn.silu(g) * u).astype(BF16)


def _ffn_down_body(h_ref, x_ref, gate_ref, wd_ref, lg_ref, lb_ref, o_ref, y_sc, *, n_n):
    n = pl.program_id(2)
    y_sc[n] = _dot(h_ref[0], wd_ref[...])

    @pl.when(n == n_n - 1)
    def _():
        y = jnp.concatenate([y_sc[c] for c in range(n_n)], axis=-1)
        o_ref[0] = _deepnorm(x_ref[0], y, gate_ref[0], lg_ref[...], lb_ref[...])


def _ffn(x, mod, k, layer, w_gate, w_up, w_down, ln_g, ln_b):
    tm, tf = 1024, 512
    shift, scale, gate = _mod_specs(k, 3)
    h = pl.pallas_call(
        _ffn_up_body,
        grid=(BATCH, SEQ // tm, D_FF // tf),
        in_specs=[pl.BlockSpec((1, tm, D_MODEL), lambda b, i, f: (b, i, 0), pipeline_mode=pl.Buffered(1)),
                  shift, scale,
                  pl.BlockSpec((1, D_MODEL, tf), lambda b, i, f: (layer, 0, f)),
                  pl.BlockSpec((1, D_MODEL, tf), lambda b, i, f: (layer, 0, f))],
        out_specs=pl.BlockSpec((1, tm, tf), lambda b, i, f: (b, i, f)),
        out_shape=jax.ShapeDtypeStruct((BATCH, SEQ, D_FF), BF16),
        scratch_shapes=[pltpu.VMEM((tm, D_MODEL), BF16)],
        compiler_params=_params(("parallel", "parallel", "arbitrary")),
        name="ffn_up",
    )(x, mod, mod, w_gate, w_up)

    tm, tn = 512, 512
    n_n = D_MODEL // tn
    rows = pl.BlockSpec((1, tm, D_MODEL), lambda b, i, n: (b, i, 0))
    vec = pl.BlockSpec((1, D_MODEL), lambda b, i, n: (0, 0))
    return pl.pallas_call(
        functools.partial(_ffn_down_body, n_n=n_n),
        grid=(BATCH, SEQ // tm, n_n),
        in_specs=[pl.BlockSpec((1, tm, D_FF), lambda b, i, n: (b, i, 0)),
                  rows,
                  gate,
                  pl.BlockSpec((D_FF, tn), lambda b, i, n: (0, n)),
                  vec, vec],
        out_specs=rows,
        out_shape=jax.ShapeDtypeStruct((BATCH, SEQ, D_MODEL), F32),
        scratch_shapes=[pltpu.VMEM((n_n, tm, tn), F32)],
        compiler_params=_params(("parallel", "parallel", "arbitrary")),
        name="ffn_down",
    )(h, x, mod, w_down.astype(BF16), ln_g.reshape(1, -1), ln_b.reshape(1, -1))


def _pool_body(x_ref, halo_ref, sh_ref, sc_ref, gate_ref, pw_ref, ps_ref, wo_ref, lg_ref, lb_ref,
               o_ref, *, tm):
    i = pl.program_id(1)
    x = x_ref[0]
    hm = _modulate(x, sh_ref[0], sc_ref[0])
    halo = jnp.where(i > 0, _modulate(halo_ref[0], sh_ref[0], sc_ref[0]), 0.0)
    ext = jnp.concatenate([halo, hm], axis=0)
    ext_hi = ext.astype(BF16)
    ext_lo = (ext - ext_hi.astype(F32)).astype(BF16)
    t_loc = lax.broadcasted_iota(jnp.int32, (tm, tm + POOL_HALO), 0)
    s_loc = lax.broadcasted_iota(jnp.int32, (tm, tm + POOL_HALO), 1)
    lag = t_loc + POOL_HALO - s_loc
    t_glob = i * tm + lax.broadcasted_iota(jnp.int32, (tm, 1), 0)
    outs = []
    for gi, w in enumerate(POOL_WINDOWS):
        cs = slice(gi * POOL_GROUP_DIM, (gi + 1) * POOL_GROUP_DIM)
        band = jnp.where(lag >= 0, jnp.where(lag < w, 1.0, 0.0), 0.0).astype(BF16)
        wsum = _dot(band, ext_hi[:, cs]) + _dot(band, ext_lo[:, cs])
        count = jnp.minimum(t_glob + 1, w).astype(F32)
        d = (wsum / count - hm[:, cs]).astype(BF16)
        outs.append(_dot(d, pw_ref[gi]))
    pooled = (jnp.concatenate(outs, axis=-1) * ps_ref[...]).astype(BF16)
    y = _dot(pooled, wo_ref[...])
    o_ref[0] = _deepnorm(x, y, gate_ref[0], lg_ref[...], lb_ref[...])


def _pool_mixer(x, mod, k, pool_w, pool_scale, w_out_c, ln_g, ln_b):
    tm = 512
    per = tm // POOL_HALO
    shift, scale, gate = _mod_specs(k, 2)
    rows = pl.BlockSpec((1, tm, D_MODEL), lambda b, i: (b, i, 0))
    halo = pl.BlockSpec((1, POOL_HALO, D_MODEL), lambda b, i: (b, jnp.maximum(i * per - 1, 0), 0))
    return pl.pallas_call(
        functools.partial(_pool_body, tm=tm),
        grid=(BATCH, SEQ // tm),
        in_specs=[rows, halo, shift, scale, gate,
                  _const_spec((len(POOL_WINDOWS), POOL_GROUP_DIM, POOL_GROUP_DIM)),
                  _const_spec((1, D_MODEL)), _const_spec((D_MODEL, D_MODEL)),
                  _const_spec((1, D_MODEL)), _const_spec((1, D_MODEL))],
        out_specs=rows,
        out_shape=jax.ShapeDtypeStruct((BATCH, SEQ, D_MODEL), F32),
        compiler_params=_params(("parallel", "parallel")),
        name="pool_mixer",
    )(x, x, mod, mod, mod, pool_w.astype(BF16), pool_scale.reshape(1, -1),
      w_out_c.astype(BF16), ln_g.reshape(1, -1), ln_b.reshape(1, -1))


def _router_body(x_ref, sh_ref, sc_ref, wr_ref, g_ref, hm_ref):
    hm = _modulate(x_ref[0], sh_ref[0], sc_ref[0])
    hm_ref[0] = hm
    hm_hi = hm.astype(BF16)
    hm_lo = (hm - hm_hi.astype(F32)).astype(BF16)
    w = wr_ref[...]
    w_hi = w.astype(BF16)
    w_lo = (w - w_hi.astype(F32)).astype(BF16)
    logits = _dot(hm_hi, w_hi) + (_dot(hm_lo, w_hi) + _dot(hm_hi, w_lo))
    lane = lax.broadcasted_iota(jnp.int32, logits.shape, 1).astype(F32)
    logits = jnp.where(lane < N_EXPERTS, logits, -jnp.inf)
    v1 = jnp.max(logits, axis=-1, keepdims=True)
    i1 = jnp.min(jnp.where(logits == v1, lane, float(LANES)), axis=-1, keepdims=True)
    rest = jnp.where(lane == i1, -jnp.inf, logits)
    v2 = jnp.max(rest, axis=-1, keepdims=True)
    i2 = jnp.min(jnp.where(rest == v2, lane, float(LANES)), axis=-1, keepdims=True)
    e2 = jnp.exp(v2 - v1)
    den = 1.0 + e2
    g_ref[0] = (jnp.where(lane == 0.0, i1, 0.0) + jnp.where(lane == 1.0, i2, 0.0)
                + jnp.where(lane == 2.0, 1.0 / den, 0.0) + jnp.where(lane == 3.0, e2 / den, 0.0))


def _router(x, mod, k, w_router):
    tm = 512
    shift, scale, _ = _mod_specs(k, 2)
    wr = jnp.pad(w_router, ((0, 0), (0, LANES - N_EXPERTS)))
    rows = pl.BlockSpec((1, tm, D_MODEL), lambda b, i: (b, i, 0))
    return pl.pallas_call(
        _router_body,
        grid=(BATCH, SEQ // tm),
        in_specs=[rows, shift, scale, _const_spec((D_MODEL, LANES))],
        out_specs=(pl.BlockSpec((1, tm, LANES), lambda b, i: (b, i, 0)), rows),
        out_shape=(jax.ShapeDtypeStruct((BATCH, SEQ, LANES), F32),
                   jax.ShapeDtypeStruct((BATCH, SEQ, D_MODEL), F32)),
        compiler_params=_params(("parallel", "parallel")),
        name="router",
    )(x, mod, mod, wr)


MOE_TILE = 2048
MOE_BLOCK = 1024
MOE_SUB = 256
N_TOKENS = BATCH * SEQ
N_ASSIGN = 2 * N_TOKENS
MOE_ROWS = N_ASSIGN + N_EXPERTS * MOE_TILE
MOE_TILES = MOE_ROWS // MOE_TILE


def _route_plan(routing):
    e_flat = routing.reshape(N_TOKENS, LANES)[:, :2].astype(jnp.int32).T.reshape(N_ASSIGN)
    onehot = (e_flat[:, None] == jnp.arange(N_EXPERTS, dtype=jnp.int32)[None, :]).astype(jnp.int32)
    csum = jnp.cumsum(onehot, axis=0)
    counts = csum[-1]
    rank = jnp.sum(csum * onehot, axis=1) - 1
    padded = (counts + MOE_TILE - 1) // MOE_TILE * MOE_TILE
    group_end = jnp.cumsum(padded)
    group_start = group_end - padded
    pos = (jnp.sum(onehot * group_start[None, :], axis=1) + rank).astype(jnp.int32)
    tok = jnp.tile(jnp.arange(N_TOKENS, dtype=jnp.int32), 2)
    src = jnp.zeros((MOE_ROWS,), jnp.int32).at[pos].set(tok)
    tile_start = jnp.arange(MOE_TILES, dtype=jnp.int32) * MOE_TILE
    n_used = group_end[-1] // MOE_TILE
    tile_expert = jnp.sum((tile_start[:, None] >= group_end[None, :]).astype(jnp.int32), axis=1)
    last_expert = jnp.max(jnp.where(counts > 0, jnp.arange(N_EXPERTS, dtype=jnp.int32), 0))
    tile_expert = jnp.minimum(tile_expert, last_expert)
    tile_rows = jnp.clip(counts[tile_expert] - (tile_start - group_start[tile_expert]), 0, MOE_TILE)
    tile_rows = jnp.where(jnp.arange(MOE_TILES) < n_used, tile_rows, 0).astype(jnp.int32)
    return pos, src, tile_expert.astype(jnp.int32), tile_rows, n_used.reshape(1).astype(jnp.int32)


GATHER_UNROLL = 8


def _gather_body(src_ref, rows_ref, hm_hbm, o_ref, buf, sem, *, tg):
    i = pl.program_id(0)
    base = i * tg
    per = MOE_TILE // tg
    n_valid = jnp.clip(rows_ref[i // per] - (i % per) * tg, 0, tg)

    @pl.when(n_valid < tg)
    def _():
        buf[...] = jnp.zeros(buf.shape, F32)

    def issue_row(r):
        t = src_ref[base + r]
        pltpu.make_async_copy(hm_hbm.at[pl.ds(t, 1)], buf.at[pl.ds(r, 1)], sem).start()

    def issue_group(g, carry):
        for u in range(GATHER_UNROLL):
            issue_row(g * GATHER_UNROLL + u)
        return carry

    def issue_one(r, carry):
        issue_row(r)
        return carry

    n_groups = n_valid // GATHER_UNROLL
    n_grouped = pl.multiple_of(n_groups * GATHER_UNROLL, GATHER_UNROLL)
    lax.fori_loop(0, n_groups, issue_group, 0)
    lax.fori_loop(n_grouped, n_valid, issue_one, 0)

    @pl.when(n_groups > 0)
    def _():
        pltpu.make_async_copy(hm_hbm.at[pl.ds(0, n_grouped)], buf.at[pl.ds(0, n_grouped)], sem).wait()

    def wait_one(r, carry):
        pltpu.make_async_copy(hm_hbm.at[pl.ds(0, 1)], buf.at[pl.ds(r, 1)], sem).wait()
        return carry

    lax.fori_loop(n_grouped, n_valid, wait_one, 0)

    o_ref[...] = buf[...].astype(BF16)


def _gather_rows(src, tile_rows, hm):
    tg = 512
    grid_spec = pltpu.PrefetchScalarGridSpec(
        num_scalar_prefetch=2,
        grid=(MOE_ROWS // tg,),
        in_specs=[pl.BlockSpec(memory_space=pl.ANY)],
        out_specs=pl.BlockSpec((tg, D_MODEL), lambda i, src, tr: (i, 0)),
        scratch_shapes=[pltpu.VMEM((tg, D_MODEL), F32), pltpu.SemaphoreType.DMA(())],
    )
    return pl.pallas_call(
        functools.partial(_gather_body, tg=tg),
        grid_spec=grid_spec,
        out_shape=jax.ShapeDtypeStruct((MOE_ROWS, D_MODEL), BF16),
        compiler_params=_params(("arbitrary",)),
        name="moe_gather",
    )(src, tile_rows, hm.reshape(N_TOKENS, D_MODEL))


def _for_valid_blocks(rows, tm, block_fn, empty_fn):
    for b0 in range(0, tm, MOE_BLOCK):
        @pl.when(rows >= b0 + MOE_BLOCK)
        def _():
            block_fn(slice(b0, b0 + MOE_BLOCK))

        for r0 in range(b0, b0 + MOE_BLOCK, MOE_SUB):
            rs = slice(r0, r0 + MOE_SUB)

            @pl.when(jnp.logical_and(rows < b0 + MOE_BLOCK, r0 < rows))
            def _():
                block_fn(rs)

            @pl.when(r0 >= rows)
            def _():
                empty_fn(rs)


def _gup_body(te_ref, rows_ref, nu_ref, xs_ref, wg_ref, wu_ref, h_ref, wgb, wub, *, tm):
    rows = rows_ref[pl.program_id(0)]

    @pl.when(rows > 0)
    def _():
        wgb[...] = wg_ref[0].astype(BF16)
        wub[...] = wu_ref[0].astype(BF16)

    def block(rs):
        xsb = xs_ref[rs, :]
        h_ref[rs, :] = (jax.nn.silu(_dot(xsb, wgb[...])) * _dot(xsb, wub[...])).astype(BF16)

    def empty(rs):
        h_ref[rs, :] = jnp.zeros((rs.stop - rs.start, h_ref.shape[1]), BF16)

    _for_valid_blocks(rows, tm, block, empty)


def _gdown_body(te_ref, rows_ref, nu_ref, h_ref, wd_ref, o_ref, wdb, *, tm):
    rows = rows_ref[pl.program_id(0)]

    @pl.when(rows > 0)
    def _():
        wdb[...] = wd_ref[0].astype(BF16)

    def block(rs):
        o_ref[rs, :] = _dot(h_ref[rs, :], wdb[...])

    def empty(rs):
        o_ref[rs, :] = jnp.zeros((rs.stop - rs.start, o_ref.shape[1]), F32)

    _for_valid_blocks(rows, tm, block, empty)


def _grouped_ffn(xs, tile_expert, tile_rows, n_used, layer, w_gate, w_up, w_down):
    tm = MOE_TILE
    first = layer * N_EXPERTS
    n_all = w_gate.shape[0] * N_EXPERTS

    def tile(i, nu):
        return jnp.minimum(i, nu[0] - 1)

    def chunk(i, c, nu, n_chunks):
        return jnp.where(i < nu[0], c, n_chunks - 1)

    tf = 256
    n_f = D_FF_EXPERT // tf
    w_up_spec = pl.BlockSpec((1, D_MODEL, tf),
                             lambda i, f, te, tr, nu: (first + te[i], 0, chunk(i, f, nu, n_f)))
    up_spec = pltpu.PrefetchScalarGridSpec(
        num_scalar_prefetch=3,
        grid=(MOE_TILES, n_f),
        in_specs=[pl.BlockSpec((tm, D_MODEL), lambda i, f, te, tr, nu: (tile(i, nu), 0)),
                  w_up_spec, w_up_spec],
        out_specs=pl.BlockSpec((tm, tf), lambda i, f, te, tr, nu: (i, f)),
        scratch_shapes=[pltpu.VMEM((D_MODEL, tf), BF16), pltpu.VMEM((D_MODEL, tf), BF16)],
    )
    h = pl.pallas_call(
        functools.partial(_gup_body, tm=tm),
        grid_spec=up_spec,
        out_shape=jax.ShapeDtypeStruct((MOE_ROWS, D_FF_EXPERT), BF16),
        compiler_params=_params(("arbitrary", "arbitrary")),
        name="moe_up",
    )(tile_expert, tile_rows, n_used, xs,
      w_gate.reshape(n_all, D_MODEL, D_FF_EXPERT), w_up.reshape(n_all, D_MODEL, D_FF_EXPERT))

    tn = 512
    n_n = D_MODEL // tn
    down_spec = pltpu.PrefetchScalarGridSpec(
        num_scalar_prefetch=3,
        grid=(MOE_TILES, n_n),
        in_specs=[pl.BlockSpec((tm, D_FF_EXPERT), lambda i, n, te, tr, nu: (tile(i, nu), 0)),
                  pl.BlockSpec((1, D_FF_EXPERT, tn),
                               lambda i, n, te, tr, nu: (first + te[i], 0, chunk(i, n, nu, n_n)))],
        out_specs=pl.BlockSpec((tm, tn), lambda i, n, te, tr, nu: (i, n)),
        scratch_shapes=[pltpu.VMEM((D_FF_EXPERT, tn), BF16)],
    )
    return pl.pallas_call(
        functools.partial(_gdown_body, tm=tm),
        grid_spec=down_spec,
        out_shape=jax.ShapeDtypeStruct((MOE_ROWS, D_MODEL), F32),
        compiler_params=_params(("arbitrary", "arbitrary")),
        name="moe_down",
    )(tile_expert, tile_rows, n_used, h, w_down.reshape(n_all, D_FF_EXPERT, D_MODEL))


def _combine_body(pos_ref, ys_hbm, r_ref, x_ref, gate_ref, lg_ref, lb_ref, o_ref, buf, sem, *, tc):
    base = (pl.program_id(0) * (SEQ // tc) + pl.program_id(1)) * tc

    def issue(r, carry):
        for k in range(2):
            p = pos_ref[k * N_TOKENS + base + r]
            pltpu.make_async_copy(ys_hbm.at[pl.ds(p, 1)], buf.at[k, pl.ds(r, 1)], sem.at[k]).start()
        return carry

    lax.fori_loop(0, tc, issue, 0, unroll=4)
    for k in range(2):
        pltpu.make_async_copy(ys_hbm.at[pl.ds(0, tc)], buf.at[k], sem.at[k]).wait()
    routing = r_ref[0]
    y = routing[:, 2:3] * buf[0] + routing[:, 3:4] * buf[1]
    o_ref[0] = _deepnorm(x_ref[0], y, gate_ref[0], lg_ref[...], lb_ref[...])


def _combine(pos, ys, routing, x, mod, k, ln_g, ln_b):
    tc = 512
    rows = pl.BlockSpec((1, tc, D_MODEL), lambda b, i, pos: (b, i, 0))
    vec = pl.BlockSpec((1, D_MODEL), lambda b, i, pos: (0, 0))
    grid_spec = pltpu.PrefetchScalarGridSpec(
        num_scalar_prefetch=1,
        grid=(BATCH, SEQ // tc),
        in_specs=[pl.BlockSpec(memory_space=pl.ANY),
                  pl.BlockSpec((1, tc, LANES), lambda b, i, pos: (b, i, 0)),
                  rows,
                  pl.BlockSpec((1, 1, D_MODEL), lambda b, i, pos: (k * 8 + b, 0, 2)),
                  vec, vec],
        out_specs=rows,
        scratch_shapes=[pltpu.VMEM((2, tc, D_MODEL), F32), pltpu.SemaphoreType.DMA((2,))],
    )
    return pl.pallas_call(
        functools.partial(_combine_body, tc=tc),
        grid_spec=grid_spec,
        out_shape=jax.ShapeDtypeStruct((BATCH, SEQ, D_MODEL), F32),
        compiler_params=_params(("arbitrary", "arbitrary")),
        name="moe_combine",
    )(pos, ys, routing, x, mod, ln_g.reshape(1, -1), ln_b.reshape(1, -1))


def _moe(x, mod, k, w_router, layer, w_gate, w_up, w_down, ln_g, ln_b):
    routing, hm = _router(x, mod, k, w_router)
    pos, src, tile_expert, tile_rows, n_used = _route_plan(routing)
    xs = _gather_rows(src, tile_rows, hm)
    ys = _grouped_ffn(xs, tile_expert, tile_rows, n_used, layer, w_gate, w_up, w_down)
    return _combine(pos, ys, routing, x, mod, k, ln_g, ln_b)


def kernel(x, c, positions, ada_w, ada_b, ln_g, ln_b, w_in_ab, q_norm_g, w_q_up, kv_norm_g, w_kv_up, sgu_norm_g, sgu_norm_b, sgu_w, sgu_b, w_out_ab, ffn_w_gate, ffn_w_up, ffn_w_down, pool_w, pool_scale, w_out_c, router_w, moe_w_gate, moe_w_up, moe_w_down):
    mod = _ada_modulation(c, ada_w, ada_b)
    tables = _rope_tables(positions)
    for l in range(DEPTH):
        j = l // 2
        k_tok, k_ch = 2 * l, 2 * l + 1
        if l % 2 == 0:
            cq, ckv, kr, zu, zv = _inproj(x, mod, k_tok, w_in_ab[j])
            q, k, vt = _qkv(cq, ckv, kr, tables, q_norm_g[j], w_q_up[j], kv_norm_g[j], w_kv_up[j])
            attn = _attention(q, k, vt)
            sgu = _sgu(zu, zv, sgu_norm_g[j], sgu_norm_b[j], sgu_w[j], sgu_b[j])
            x = _outproj(attn, sgu, x, mod, k_tok, w_out_ab[j], ln_g[l, 0], ln_b[l, 0])
            x = _ffn(x, mod, k_ch, j, ffn_w_gate, ffn_w_up, ffn_w_down[j], ln_g[l, 1], ln_b[l, 1])
        else:
            x = _pool_mixer(x, mod, k_tok, pool_w[j], pool_scale[j], w_out_c[j], ln_g[l, 0], ln_b[l, 0])
            x = _moe(x, mod, k_ch, router_w[j], j, moe_w_gate, moe_w_up, moe_w_down,
                     ln_g[l, 1], ln_b[l, 1])
    return x
```

```python
import functools

import jax
import jax.numpy as jnp
import numpy as np
from jax import lax
from jax.experimental import pallas as pl
from jax.experimental.pallas import tpu as pltpu

F32 = jnp.float32
BF16 = jnp.bfloat16

D_MODEL = 2048
BATCH = 2
SEQ = 4096
DEPTH = 4
MLA_HEADS = 8
QK_NOPE_DIM = 128
QK_ROPE_DIM = 64
V_HEAD_DIM = 128
Q_LORA_RANK = 768
KV_LORA_RANK = 512
ROPE_THETA = 10000.0
SGU_GROUPS = 8
SGU_GROUP_DIM = 128
SGU_WIDTH = SGU_GROUPS * SGU_GROUP_DIM
CHUNK = 128
ATTN_WIDTH = MLA_HEADS * V_HEAD_DIM
POOL_WINDOWS = (2, 4, 8, 16)
POOL_GROUP_DIM = D_MODEL // 4
D_FF = 5632
N_EXPERTS = 8
D_FF_EXPERT = 2816
DEEPNORM_ALPHA = (2 * DEPTH) ** 0.25
LN_EPS = 1e-5
RMS_EPS = 1e-6
SM_SCALE = (QK_NOPE_DIM + QK_ROPE_DIM) ** -0.5
LOG2_E = 1.4426950408889634

LANES = 128
HEAD_LANES = 2 * LANES
QK_WIDTH = MLA_HEADS * HEAD_LANES
VMEM_LIMIT = 60 * 1024 * 1024
POOL_HALO = 128


def _params(semantics):
    return pltpu.CompilerParams(dimension_semantics=semantics, vmem_limit_bytes=VMEM_LIMIT)


def _const_spec(shape):
    nd = len(shape)
    return pl.BlockSpec(shape, lambda *_: (0,) * nd, pipeline_mode=pl.Buffered(1))


def _dot(a, b):
    return jnp.dot(a, b, preferred_element_type=F32)


def _modulate(x, shift, scale):
    return x * (1.0 + scale) + shift


def _deepnorm(x, y, gate, g, b):
    r = DEEPNORM_ALPHA * x + (1.0 + gate) * y
    mu = jnp.mean(r, axis=-1, keepdims=True)
    rc = r - mu
    var = jnp.mean(rc * rc, axis=-1, keepdims=True)
    return rc * lax.rsqrt(var + LN_EPS) * g + b


def _rms(x, g):
    return x * lax.rsqrt(jnp.mean(x * x, axis=-1, keepdims=True) + RMS_EPS) * g


def _rope(r, cos_t, sin_hi, sin_lo):
    return (r * cos_t + pltpu.roll(r, QK_ROPE_DIM // 2, 1) * sin_hi
            + pltpu.roll(r, LANES - QK_ROPE_DIM // 2, 1) * sin_lo)


def _mod_specs(k, n_grid):
    def spec(part):
        if n_grid == 2:
            return pl.BlockSpec((1, 1, D_MODEL), lambda b, i: (k * 8 + b, 0, part))
        if n_grid == 3:
            return pl.BlockSpec((1, 1, D_MODEL), lambda b, i, f: (k * 8 + b, 0, part))
        return pl.BlockSpec((1, 1, D_MODEL), lambda b, i, e, f: (k * 8 + b, 0, part))
    return spec(0), spec(1), spec(2)


def _ada_body(c_ref, w_ref, b_ref, o_ref):
    s = jax.nn.silu(c_ref[...]).astype(BF16)
    o_ref[0] = _dot(s, w_ref[0].astype(BF16)) + b_ref[0]


def _ada_modulation(c, ada_w, ada_b):
    n_mod = 2 * DEPTH
    tn = 1024
    c_pad = jnp.pad(c, ((0, 8 - BATCH), (0, 0)))
    out = pl.pallas_call(
        _ada_body,
        grid=(n_mod, 3 * D_MODEL // tn),
        in_specs=[
            pl.BlockSpec((8, D_MODEL), lambda k, n: (0, 0)),
            pl.BlockSpec((1, D_MODEL, tn), lambda k, n: (k, 0, n)),
            pl.BlockSpec((1, 1, tn), lambda k, n: (k, 0, n)),
        ],
        out_specs=pl.BlockSpec((1, 8, tn), lambda k, n: (k, 0, n)),
        out_shape=jax.ShapeDtypeStruct((n_mod, 8, 3 * D_MODEL), F32),
        compiler_params=_params(("parallel", "parallel")),
        name="ada_mod",
    )(c_pad, ada_w.reshape(n_mod, D_MODEL, 3 * D_MODEL), ada_b.reshape(n_mod, 1, 3 * D_MODEL))
    return out.reshape(n_mod * 8, 1, 3 * D_MODEL)


def _rope_table_body(pos_ref, inv_ref, cos_ref, hi_ref, lo_ref):
    ang = pos_ref[0].astype(F32) * inv_ref[...]
    lane = lax.broadcasted_iota(jnp.int32, ang.shape, 1)
    cos = jnp.cos(ang)
    sin = jnp.sin(ang)
    half = QK_ROPE_DIM // 2
    cos_ref[0] = jnp.where(lane < QK_ROPE_DIM, cos, 0.0)
    hi_ref[0] = jnp.where(lane < half, 0.0, jnp.where(lane < QK_ROPE_DIM, sin, 0.0))
    lo_ref[0] = jnp.where(lane < half, -sin, 0.0)


def _rope_tables(positions):
    tm = 512
    half = QK_ROPE_DIM // 2
    inv = ROPE_THETA ** (-jnp.arange(0, QK_ROPE_DIM, 2, dtype=F32) / QK_ROPE_DIM)
    inv_lanes = jnp.concatenate([inv, inv, jnp.zeros((LANES - 2 * half,), F32)]).reshape(1, LANES)
    tab = jax.ShapeDtypeStruct((BATCH, SEQ, LANES), F32)
    spec = pl.BlockSpec((1, tm, LANES), lambda b, i: (b, i, 0))
    return pl.pallas_call(
        _rope_table_body,
        grid=(BATCH, SEQ // tm),
        in_specs=[pl.BlockSpec((1, tm, 1), lambda b, i: (b, i, 0)),
                  pl.BlockSpec((1, LANES), lambda b, i: (0, 0))],
        out_specs=(spec, spec, spec),
        out_shape=(tab, tab, tab),
        compiler_params=_params(("parallel", "parallel")),
        name="rope_tables",
    )(positions.reshape(BATCH, SEQ, 1), inv_lanes)


def _inproj_body(x_ref, sh_ref, sc_ref, wq_ref, wkv_ref, wkr_ref, wu_ref, wv_ref,
                 cq_ref, ckv_ref, kr_ref, zu_ref, zv_ref):
    hm = _modulate(x_ref[0], sh_ref[0], sc_ref[0]).astype(BF16)
    cq_ref[0] = _dot(hm, wq_ref[...])
    ckv_ref[0] = _dot(hm, wkv_ref[...])
    kr_ref[0] = _dot(hm, wkr_ref[...])
    zu_ref[0] = _dot(hm, wu_ref[...])
    zv_ref[0] = _dot(hm, wv_ref[...])


def _inproj(x, mod, k, w_in):
    tm = 512
    a, b, c_, d = (Q_LORA_RANK, Q_LORA_RANK + KV_LORA_RANK,
                   Q_LORA_RANK + KV_LORA_RANK + QK_ROPE_DIM,
                   Q_LORA_RANK + KV_LORA_RANK + QK_ROPE_DIM + SGU_WIDTH)
    wq = w_in[:, :a].astype(BF16)
    wkv = w_in[:, a:b].astype(BF16)
    wkr = jnp.pad(w_in[:, b:c_], ((0, 0), (0, LANES - QK_ROPE_DIM))).astype(BF16)
    wu = w_in[:, c_:d].astype(BF16)
    wv = w_in[:, d:].astype(BF16)
    shift, scale, _ = _mod_specs(k, 2)
    widths = (Q_LORA_RANK, KV_LORA_RANK, LANES, SGU_WIDTH, SGU_WIDTH)
    return pl.pallas_call(
        _inproj_body,
        grid=(BATCH, SEQ // tm),
        in_specs=[pl.BlockSpec((1, tm, D_MODEL), lambda b, i: (b, i, 0)), shift, scale]
                 + [_const_spec((D_MODEL, w)) for w in widths],
        out_specs=tuple(pl.BlockSpec((1, tm, w), lambda b, i: (b, i, 0)) for w in widths),
        out_shape=tuple(jax.ShapeDtypeStruct((BATCH, SEQ, w), F32) for w in widths),
        compiler_params=_params(("parallel", "parallel")),
        name="inproj",
    )(x, mod, mod, wq, wkv, wkr, wu, wv)


def _qkv_body(cq_ref, ckv_ref, kr_ref, cos_ref, hi_ref, lo_ref, qg_ref, kvg_ref,
              wqn_ref, wqr_ref, wkn_ref, wvt_ref, q_ref, k_ref, vt_ref):
    cos_t, sin_hi, sin_lo = cos_ref[0], hi_ref[0], lo_ref[0]
    cqn = _rms(cq_ref[0], qg_ref[...]).astype(BF16)
    qn = _dot(cqn, wqn_ref[...])
    qr = _dot(cqn, wqr_ref[...])
    ckvn = _rms(ckv_ref[0], kvg_ref[...]).astype(BF16)
    kn = _dot(ckvn, wkn_ref[...])
    vt_ref[0] = lax.dot_general(wvt_ref[...], ckvn, (((1,), (1,)), ((), ())),
                                preferred_element_type=F32).astype(BF16)
    k_rope = _rope(kr_ref[0], cos_t, sin_hi, sin_lo).astype(BF16)
    for h in range(MLA_HEADS):
        nope = slice(h * LANES, (h + 1) * LANES)
        lo = h * HEAD_LANES
        q_ref[0, :, lo:lo + LANES] = qn[:, nope].astype(BF16)
        q_ref[0, :, lo + LANES:lo + HEAD_LANES] = _rope(qr[:, nope], cos_t, sin_hi, sin_lo).astype(BF16)
        k_ref[0, :, lo:lo + LANES] = kn[:, nope].astype(BF16)
        k_ref[0, :, lo + LANES:lo + HEAD_LANES] = k_rope


def _qkv(cq, ckv, kr, tables, q_norm_g, w_q_up, kv_norm_g, w_kv_up):
    tm = 512
    wq = w_q_up.reshape(Q_LORA_RANK, MLA_HEADS, QK_NOPE_DIM + QK_ROPE_DIM)
    wqn = wq[:, :, :QK_NOPE_DIM].reshape(Q_LORA_RANK, MLA_HEADS * LANES).astype(BF16)
    wqr = jnp.pad(wq[:, :, QK_NOPE_DIM:], ((0, 0), (0, 0), (0, LANES - QK_ROPE_DIM)))
    wqr = wqr.reshape(Q_LORA_RANK, MLA_HEADS * LANES).astype(BF16)
    wkv = w_kv_up.reshape(KV_LORA_RANK, MLA_HEADS, QK_NOPE_DIM + V_HEAD_DIM)
    wkn = wkv[:, :, :QK_NOPE_DIM].reshape(KV_LORA_RANK, MLA_HEADS * LANES).astype(BF16)
    wvt = wkv[:, :, QK_NOPE_DIM:].reshape(KV_LORA_RANK, ATTN_WIDTH).T.astype(BF16)

    def rows(w):
        return pl.BlockSpec((1, tm, w), lambda b, i: (b, i, 0))

    return pl.pallas_call(
        _qkv_body,
        grid=(BATCH, SEQ // tm),
        in_specs=[rows(Q_LORA_RANK), rows(KV_LORA_RANK), rows(LANES),
                  rows(LANES), rows(LANES), rows(LANES),
                  _const_spec((1, Q_LORA_RANK)), _const_spec((1, KV_LORA_RANK)),
                  _const_spec((Q_LORA_RANK, MLA_HEADS * LANES)),
                  _const_spec((Q_LORA_RANK, MLA_HEADS * LANES)),
                  _const_spec((KV_LORA_RANK, MLA_HEADS * LANES)),
                  _const_spec((ATTN_WIDTH, KV_LORA_RANK))],
        out_specs=(rows(QK_WIDTH), rows(QK_WIDTH),
                   pl.BlockSpec((1, ATTN_WIDTH, tm), lambda b, i: (b, 0, i))),
        out_shape=(jax.ShapeDtypeStruct((BATCH, SEQ, QK_WIDTH), BF16),
                   jax.ShapeDtypeStruct((BATCH, SEQ, QK_WIDTH), BF16),
                   jax.ShapeDtypeStruct((BATCH, ATTN_WIDTH, SEQ), BF16)),
        compiler_params=_params(("parallel", "parallel")),
        name="qkv_up",
    )(cq, ckv, kr, *tables, q_norm_g.reshape(1, -1), kv_norm_g.reshape(1, -1), wqn, wqr, wkn, wvt)


def _attn_body(qi_ref, kj_ref, q_ref, k_ref, vt_ref, o_ref, m_sc, l_sc, acc_sc, *, tq):
    p = pl.program_id(1)
    i = qi_ref[p]
    j = kj_ref[p]

    @pl.when(j == 0)
    def _():
        m_sc[...] = jnp.full(m_sc.shape, -jnp.inf, F32)
        l_sc[...] = jnp.zeros(l_sc.shape, F32)
        acc_sc[...] = jnp.zeros(acc_sc.shape, F32)

    def accumulate(diagonal):
        if diagonal:
            key = lax.broadcasted_iota(jnp.int32, (tq, tq), 0)
            qry = lax.broadcasted_iota(jnp.int32, (tq, tq), 1)
            visible = key <= qry
        for h in range(MLA_HEADS):
            qk = slice(h * HEAD_LANES, (h + 1) * HEAD_LANES)
            hv = slice(h * V_HEAD_DIM, (h + 1) * V_HEAD_DIM)
            s = lax.dot_general(k_ref[0, :, qk], q_ref[0, :, qk], (((1,), (1,)), ((), ())),
                                preferred_element_type=F32) * (SM_SCALE * LOG2_E)
            if diagonal:
                s = jnp.where(visible, s, -jnp.inf)
            m_prev = m_sc[h]
            m_new = jnp.maximum(m_prev, jnp.max(s, axis=0, keepdims=True))
            alpha = jnp.exp2(m_prev - m_new)
            pr = jnp.exp2(s - m_new)
            l_sc[h] = alpha * l_sc[h] + jnp.sum(pr, axis=0, keepdims=True)
            acc_sc[hv, :] = alpha * acc_sc[hv, :] + _dot(vt_ref[0, hv, :], pr.astype(BF16))
            m_sc[h] = m_new

    @pl.when(j < i)
    def _():
        accumulate(False)

    @pl.when(j == i)
    def _():
        accumulate(True)
        for h in range(MLA_HEADS):
            hv = slice(h * V_HEAD_DIM, (h + 1) * V_HEAD_DIM)
            o_ref[0, :, hv] = (acc_sc[hv, :] / l_sc[h]).T.astype(BF16)


def _attention(q, k, vt):
    tq = 512
    nb = SEQ // tq
    pairs = [(i, j) for i in range(nb) for j in range(i + 1)]
    qi = jnp.asarray(np.array([p[0] for p in pairs], np.int32))
    kj = jnp.asarray(np.array([p[1] for p in pairs], np.int32))
    grid_spec = pltpu.PrefetchScalarGridSpec(
        num_scalar_prefetch=2,
        grid=(BATCH, len(pairs)),
        in_specs=[pl.BlockSpec((1, tq, QK_WIDTH), lambda b, p, qi, kj: (b, qi[p], 0)),
                  pl.BlockSpec((1, tq, QK_WIDTH), lambda b, p, qi, kj: (b, kj[p], 0)),
                  pl.BlockSpec((1, ATTN_WIDTH, tq), lambda b, p, qi, kj: (b, 0, kj[p]))],
        out_specs=pl.BlockSpec((1, tq, ATTN_WIDTH), lambda b, p, qi, kj: (b, qi[p], 0)),
        scratch_shapes=[pltpu.VMEM((MLA_HEADS, 1, tq), F32),
                        pltpu.VMEM((MLA_HEADS, 1, tq), F32),
                        pltpu.VMEM((ATTN_WIDTH, tq), F32)],
    )
    return pl.pallas_call(
        functools.partial(_attn_body, tq=tq),
        grid_spec=grid_spec,
        out_shape=jax.ShapeDtypeStruct((BATCH, SEQ, ATTN_WIDTH), BF16),
        compiler_params=_params(("parallel", "arbitrary")),
        name="mla_attention",
    )(qi, kj, q, k, vt)


def _sgu_body(zu_ref, zv_ref, g_ref, b_ref, w_ref, bias_ref, o_ref, *, tm):
    row = lax.broadcasted_iota(jnp.int32, (CHUNK, CHUNK), 0)
    col = lax.broadcasted_iota(jnp.int32, (CHUNK, CHUNK), 1)
    for g in range(SGU_GROUPS):
        gs = slice(g * SGU_GROUP_DIM, (g + 1) * SGU_GROUP_DIM)
        v = jax.nn.gelu(zv_ref[0, :, gs])
        mu = jnp.mean(v, axis=-1, keepdims=True)
        vc = v - mu
        var = jnp.mean(vc * vc, axis=-1, keepdims=True)
        vn = (vc * lax.rsqrt(var + LN_EPS) * g_ref[:, gs] + b_ref[:, gs]).astype(BF16)
        w = jnp.where(col <= row, w_ref[g], 0.0).astype(BF16)
        for n in range(tm // CHUNK):
            rs = slice(n * CHUNK, (n + 1) * CHUNK)
            s = _dot(w, vn[rs]) + bias_ref[:, gs]
            o_ref[0, rs, gs] = (jax.nn.gelu(zu_ref[0, rs, gs]) * s).astype(BF16)


def _sgu(zu, zv, norm_g, norm_b, sgu_w, sgu_b):
    tm = 512
    bias = jnp.repeat(sgu_b.T, SGU_GROUP_DIM, axis=1)
    rows = pl.BlockSpec((1, tm, SGU_WIDTH), lambda b, i: (b, i, 0))
    return pl.pallas_call(
        functools.partial(_sgu_body, tm=tm),
        grid=(BATCH, SEQ // tm),
        in_specs=[rows, rows, _const_spec((1, SGU_WIDTH)), _const_spec((1, SGU_WIDTH)),
                  _const_spec((SGU_GROUPS, CHUNK, CHUNK)), _const_spec((CHUNK, SGU_WIDTH))],
        out_specs=rows,
        out_shape=jax.ShapeDtypeStruct((BATCH, SEQ, SGU_WIDTH), BF16),
        compiler_params=_params(("parallel", "parallel")),
        name="sgu",
    )(zu, zv, norm_g.reshape(1, -1), norm_b.reshape(1, -1), sgu_w, bias)


def _outproj_body(a_ref, s_ref, x_ref, gate_ref, wa_ref, ws_ref, lg_ref, lb_ref, o_ref):
    y = _dot(a_ref[0], wa_ref[...]) + _dot(s_ref[0], ws_ref[...])
    o_ref[0] = _deepnorm(x_ref[0], y, gate_ref[0], lg_ref[...], lb_ref[...])


def _outproj(attn, sgu, x, mod, k, w_out, ln_g, ln_b):
    tm = 512
    wa = w_out[:ATTN_WIDTH].astype(BF16)
    ws = w_out[ATTN_WIDTH:].astype(BF16)
    _, _, gate = _mod_specs(k, 2)

    def rows(w):
        return pl.BlockSpec((1, tm, w), lambda b, i: (b, i, 0))

    return pl.pallas_call(
        _outproj_body,
        grid=(BATCH, SEQ // tm),
        in_specs=[rows(ATTN_WIDTH), rows(SGU_WIDTH), rows(D_MODEL), gate,
                  _const_spec((ATTN_WIDTH, D_MODEL)), _const_spec((SGU_WIDTH, D_MODEL)),
                  _const_spec((1, D_MODEL)), _const_spec((1, D_MODEL))],
        out_specs=rows(D_MODEL),
        out_shape=jax.ShapeDtypeStruct((BATCH, SEQ, D_MODEL), F32),
        compiler_params=_params(("parallel", "parallel")),
        name="outproj_ab",
    )(attn, sgu, x, mod, wa, ws, ln_g.reshape(1, -1), ln_b.reshape(1, -1))


def _ffn_up_body(x_ref, sh_ref, sc_ref, wg_ref, wu_ref, h_ref, hm_sc):
    @pl.when(pl.program_id(2) == 0)
    def _():
        hm_sc[...] = _modulate(x_ref[0], sh_ref[0], sc_ref[0]).astype(BF16)

    hm = hm_sc[...]
    g = _dot(hm, wg_ref[0].astype(BF16))
    u = _dot(hm, wu_ref[0].astype(BF16))
    h_ref[0] = (jax.nn.silu(g) * u).astype(BF16)


def _ffn_down_body(h_ref, x_ref, gate_ref, wd_ref, lg_ref, lb_ref, o_ref, y_sc, *, n_n):
    n = pl.program_id(2)
    y_sc[n] = _dot(h_ref[0], wd_ref[...])

    @pl.when(n == n_n - 1)
    def _():
        y = jnp.concatenate([y_sc[c] for c in range(n_n)], axis=-1)
        o_ref[0] = _deepnorm(x_ref[0], y, gate_ref[0], lg_ref[...], lb_ref[...])


def _ffn(x, mod, k, layer, w_gate, w_up, w_down, ln_g, ln_b):
    tm, tf = 1024, 512
    shift, scale, gate = _mod_specs(k, 3)
    h = pl.pallas_call(
        _ffn_up_body,
        grid=(BATCH, SEQ // tm, D_FF // tf),
        in_specs=[pl.BlockSpec((1, tm, D_MODEL), lambda b, i, f: (b, i, 0), pipeline_mode=pl.Buffered(1)),
                  shift, scale,
                  pl.BlockSpec((1, D_MODEL, tf), lambda b, i, f: (layer, 0, f)),
                  pl.BlockSpec((1, D_MODEL, tf), lambda b, i, f: (layer, 0, f))],
        out_specs=pl.BlockSpec((1, tm, tf), lambda b, i, f: (b, i, f)),
        out_shape=jax.ShapeDtypeStruct((BATCH, SEQ, D_FF), BF16),
        scratch_shapes=[pltpu.VMEM((tm, D_MODEL), BF16)],
        compiler_params=_params(("parallel", "parallel", "arbitrary")),
        name="ffn_up",
    )(x, mod, mod, w_gate, w_up)

    tm, tn = 512, 512
    n_n = D_MODEL // tn
    rows = pl.BlockSpec((1, tm, D_MODEL), lambda b, i, n: (b, i, 0))
    vec = pl.BlockSpec((1, D_MODEL), lambda b, i, n: (0, 0))
    return pl.pallas_call(
        functools.partial(_ffn_down_body, n_n=n_n),
        grid=(BATCH, SEQ // tm, n_n),
        in_specs=[pl.BlockSpec((1, tm, D_FF), lambda b, i, n: (b, i, 0)),
                  rows,
                  gate,
                  pl.BlockSpec((D_FF, tn), lambda b, i, n: (0, n)),
                  vec, vec],
        out_specs=rows,
        out_shape=jax.ShapeDtypeStruct((BATCH, SEQ, D_MODEL), F32),
        scratch_shapes=[pltpu.VMEM((n_n, tm, tn), F32)],
        compiler_params=_params(("parallel", "parallel", "arbitrary")),
        name="ffn_down",
    )(h, x, mod, w_down.astype(BF16), ln_g.reshape(1, -1), ln_b.reshape(1, -1))


def _pool_body(x_ref, halo_ref, sh_ref, sc_ref, gate_ref, pw_ref, ps_ref, wo_ref, lg_ref, lb_ref,
               o_ref, *, tm):
    i = pl.program_id(1)
    x = x_ref[0]
    hm = _modulate(x, sh_ref[0], sc_ref[0])
    halo = jnp.where(i > 0, _modulate(halo_ref[0], sh_ref[0], sc_ref[0]), 0.0)
    ext = jnp.concatenate([halo, hm], axis=0)
    ext_hi = ext.astype(BF16)
    ext_lo = (ext - ext_hi.astype(F32)).astype(BF16)
    t_loc = lax.broadcasted_iota(jnp.int32, (tm, tm + POOL_HALO), 0)
    s_loc = lax.broadcasted_iota(jnp.int32, (tm, tm + POOL_HALO), 1)
    lag = t_loc + POOL_HALO - s_loc
    t_glob = i * tm + lax.broadcasted_iota(jnp.int32, (tm, 1), 0)
    outs = []
    for gi, w in enumerate(POOL_WINDOWS):
        cs = slice(gi * POOL_GROUP_DIM, (gi + 1) * POOL_GROUP_DIM)
        band = jnp.where(lag >= 0, jnp.where(lag < w, 1.0, 0.0), 0.0).astype(BF16)
        wsum = _dot(band, ext_hi[:, cs]) + _dot(band, ext_lo[:, cs])
        count = jnp.minimum(t_glob + 1, w).astype(F32)
        d = (wsum / count - hm[:, cs]).astype(BF16)
        outs.append(_dot(d, pw_ref[gi]))
    pooled = (jnp.concatenate(outs, axis=-1) * ps_ref[...]).astype(BF16)
    y = _dot(pooled, wo_ref[...])
    o_ref[0] = _deepnorm(x, y, gate_ref[0], lg_ref[...], lb_ref[...])


def _pool_mixer(x, mod, k, pool_w, pool_scale, w_out_c, ln_g, ln_b):
    tm = 512
    per = tm // POOL_HALO
    shift, scale, gate = _mod_specs(k, 2)
    rows = pl.BlockSpec((1, tm, D_MODEL), lambda b, i: (b, i, 0))
    halo = pl.BlockSpec((1, POOL_HALO, D_MODEL), lambda b, i: (b, jnp.maximum(i * per - 1, 0), 0))
    return pl.pallas_call(
        functools.partial(_pool_body, tm=tm),
        grid=(BATCH, SEQ // tm),
        in_specs=[rows, halo, shift, scale, gate,
                  _const_spec((len(POOL_WINDOWS), POOL_GROUP_DIM, POOL_GROUP_DIM)),
                  _const_spec((1, D_MODEL)), _const_spec((D_MODEL, D_MODEL)),
                  _const_spec((1, D_MODEL)), _const_spec((1, D_MODEL))],
        out_specs=rows,
        out_shape=jax.ShapeDtypeStruct((BATCH, SEQ, D_MODEL), F32),
        compiler_params=_params(("parallel", "parallel")),
        name="pool_mixer",
    )(x, x, mod, mod, mod, pool_w.astype(BF16), pool_scale.reshape(1, -1),
      w_out_c.astype(BF16), ln_g.reshape(1, -1), ln_b.reshape(1, -1))


def _router_body(x_ref, sh_ref, sc_ref, wr_ref, g_ref, hm_ref):
    hm = _modulate(x_ref[0], sh_ref[0], sc_ref[0])
    hm_ref[0] = hm
    hm_hi = hm.astype(BF16)
    hm_lo = (hm - hm_hi.astype(F32)).astype(BF16)
    w = wr_ref[...]
    w_hi = w.astype(BF16)
    w_lo = (w - w_hi.astype(F32)).astype(BF16)
    logits = _dot(hm_hi, w_hi) + (_dot(hm_lo, w_hi) + _dot(hm_hi, w_lo))
    lane = lax.broadcasted_iota(jnp.int32, logits.shape, 1).astype(F32)
    logits = jnp.where(lane < N_EXPERTS, logits, -jnp.inf)
    v1 = jnp.max(logits, axis=-1, keepdims=True)
    i1 = jnp.min(jnp.where(logits == v1, lane, float(LANES)), axis=-1, keepdims=True)
    rest = jnp.where(lane == i1, -jnp.inf, logits)
    v2 = jnp.max(rest, axis=-1, keepdims=True)
    i2 = jnp.min(jnp.where(rest == v2, lane, float(LANES)), axis=-1, keepdims=True)
    e2 = jnp.exp(v2 - v1)
    den = 1.0 + e2
    g_ref[0] = (jnp.where(lane == 0.0, i1, 0.0) + jnp.where(lane == 1.0, i2, 0.0)
                + jnp.where(lane == 2.0, 1.0 / den, 0.0) + jnp.where(lane == 3.0, e2 / den, 0.0))


def _router(x, mod, k, w_router):
    tm = 512
    shift, scale, _ = _mod_specs(k, 2)
    wr = jnp.pad(w_router, ((0, 0), (0, LANES - N_EXPERTS)))
    rows = pl.BlockSpec((1, tm, D_MODEL), lambda b, i: (b, i, 0))
    return pl.pallas_call(
        _router_body,
        grid=(BATCH, SEQ // tm),
        in_specs=[rows, shift, scale, _const_spec((D_MODEL, LANES))],
        out_specs=(pl.BlockSpec((1, tm, LANES), lambda b, i: (b, i, 0)), rows),
        out_shape=(jax.ShapeDtypeStruct((BATCH, SEQ, LANES), F32),
                   jax.ShapeDtypeStruct((BATCH, SEQ, D_MODEL), F32)),
        compiler_params=_params(("parallel", "parallel")),
        name="router",
    )(x, mod, mod, wr)


N_TOKENS = BATCH * SEQ
N_ASSIGN = 2 * N_TOKENS
MOE_BLOCK = 1024
MOE_SUB = 256
MOE_TILE = N_ASSIGN // N_EXPERTS + MOE_SUB
MOE_TILES = -(-(N_ASSIGN + N_EXPERTS * (MOE_TILE - 1)) // MOE_TILE)
MOE_ROWS = MOE_TILES * MOE_TILE
GATHER_ROWS = MOE_TILE // 3
assert MOE_TILE % GATHER_ROWS == 0 and GATHER_ROWS % 8 == 0


def _route_plan(routing):
    e_flat = routing.reshape(N_TOKENS, LANES)[:, :2].astype(jnp.int32).T.reshape(N_ASSIGN)
    onehot = (e_flat[:, None] == jnp.arange(N_EXPERTS, dtype=jnp.int32)[None, :]).astype(jnp.int32)
    csum = jnp.cumsum(onehot, axis=0)
    counts = csum[-1]
    rank = jnp.sum(csum * onehot, axis=1) - 1
    padded = (counts + MOE_TILE - 1) // MOE_TILE * MOE_TILE
    group_end = jnp.cumsum(padded)
    group_start = group_end - padded
    pos = (jnp.sum(onehot * group_start[None, :], axis=1) + rank).astype(jnp.int32)
    tok = jnp.tile(jnp.arange(N_TOKENS, dtype=jnp.int32), 2)
    src = jnp.zeros((MOE_ROWS,), jnp.int32).at[pos].set(tok)
    tile_start = jnp.arange(MOE_TILES, dtype=jnp.int32) * MOE_TILE
    n_used = group_end[-1] // MOE_TILE
    tile_expert = jnp.sum((tile_start[:, None] >= group_end[None, :]).astype(jnp.int32), axis=1)
    last_expert = jnp.max(jnp.where(counts > 0, jnp.arange(N_EXPERTS, dtype=jnp.int32), 0))
    tile_expert = jnp.minimum(tile_expert, last_expert)
    tile_rows = jnp.clip(counts[tile_expert] - (tile_start - group_start[tile_expert]), 0, MOE_TILE)
    tile_rows = jnp.where(jnp.arange(MOE_TILES) < n_used, tile_rows, 0).astype(jnp.int32)
    return pos, src, tile_expert.astype(jnp.int32), tile_rows, n_used.reshape(1).astype(jnp.int32)


GATHER_UNROLL = 8


def _gather_body(src_ref, rows_ref, hm_hbm, o_ref, buf, sem, *, tg):
    i = pl.program_id(0)
    base = i * tg
    per = MOE_TILE // tg
    n_valid = jnp.clip(rows_ref[i // per] - (i % per) * tg, 0, tg)

    @pl.when(n_valid < tg)
    def _():
        buf[...] = jnp.zeros(buf.shape, F32)

    def issue_row(r):
        t = src_ref[base + r]
        pltpu.make_async_copy(hm_hbm.at[pl.ds(t, 1)], buf.at[pl.ds(r, 1)], sem).start()

    def issue_group(g, carry):
        for u in range(GATHER_UNROLL):
            issue_row(g * GATHER_UNROLL + u)
        return carry

    def issue_one(r, carry):
        issue_row(r)
        return carry

    n_groups = n_valid // GATHER_UNROLL
    n_grouped = pl.multiple_of(n_groups * GATHER_UNROLL, GATHER_UNROLL)
    lax.fori_loop(0, n_groups, issue_group, 0)
    lax.fori_loop(n_grouped, n_valid, issue_one, 0)

    @pl.when(n_groups > 0)
    def _():
        pltpu.make_async_copy(hm_hbm.at[pl.ds(0, n_grouped)], buf.at[pl.ds(0, n_grouped)], sem).wait()

    def wait_one(r, carry):
        pltpu.make_async_copy(hm_hbm.at[pl.ds(0, 1)], buf.at[pl.ds(r, 1)], sem).wait()
        return carry

    lax.fori_loop(n_grouped, n_valid, wait_one, 0)

    o_ref[...] = buf[...].astype(BF16)


def _gather_rows(src, tile_rows, hm):
    tg = GATHER_ROWS
    grid_spec = pltpu.PrefetchScalarGridSpec(
        num_scalar_prefetch=2,
        grid=(MOE_ROWS // tg,),
        in_specs=[pl.BlockSpec(memory_space=pl.ANY)],
        out_specs=pl.BlockSpec((tg, D_MODEL), lambda i, src, tr: (i, 0)),
        scratch_shapes=[pltpu.VMEM((tg, D_MODEL), F32), pltpu.SemaphoreType.DMA(())],
    )
    return pl.pallas_call(
        functools.partial(_gather_body, tg=tg),
        grid_spec=grid_spec,
        out_shape=jax.ShapeDtypeStruct((MOE_ROWS, D_MODEL), BF16),
        compiler_params=_params(("arbitrary",)),
        name="moe_gather",
    )(src, tile_rows, hm.reshape(N_TOKENS, D_MODEL))


def _for_valid_blocks(rows, tm, block_fn, empty_fn):
    for b0 in range(0, tm, MOE_BLOCK):
        b1 = min(b0 + MOE_BLOCK, tm)
        whole = b1 if b1 - b0 == MOE_BLOCK else tm + 1
        if whole <= tm:
            pl.when(rows >= whole)(functools.partial(block_fn, slice(b0, b1)))

        for r0 in range(b0, b1, MOE_SUB):
            rs = slice(r0, r0 + MOE_SUB)

            @pl.when(jnp.logical_and(rows < whole, r0 < rows))
            def _():
                block_fn(rs)

            @pl.when(r0 >= rows)
            def _():
                empty_fn(rs)


def _gup_body(te_ref, rows_ref, nu_ref, xs_ref, wg_ref, wu_ref, h_ref, wgb, wub, *, tm):
    rows = rows_ref[pl.program_id(0)]

    @pl.when(rows > 0)
    def _():
        wgb[...] = wg_ref[0].astype(BF16)
        wub[...] = wu_ref[0].astype(BF16)

    def block(rs):
        xsb = xs_ref[rs, :]
        h_ref[rs, :] = (jax.nn.silu(_dot(xsb, wgb[...])) * _dot(xsb, wub[...])).astype(BF16)

    def empty(rs):
        h_ref[rs, :] = jnp.zeros((rs.stop - rs.start, h_ref.shape[1]), BF16)

    _for_valid_blocks(rows, tm, block, empty)


def _gdown_body(te_ref, rows_ref, nu_ref, h_ref, wd_ref, o_ref, wdb, *, tm):
    rows = rows_ref[pl.program_id(0)]

    @pl.when(rows > 0)
    def _():
        wdb[...] = wd_ref[0].astype(BF16)

    def block(rs):
        o_ref[rs, :] = _dot(h_ref[rs, :], wdb[...])

    def empty(rs):
        o_ref[rs, :] = jnp.zeros((rs.stop - rs.start, o_ref.shape[1]), F32)

    _for_valid_blocks(rows, tm, block, empty)


def _grouped_ffn(xs, tile_expert, tile_rows, n_used, layer, w_gate, w_up, w_down):
    tm = MOE_TILE
    first = layer * N_EXPERTS
    n_all = w_gate.shape[0] * N_EXPERTS

    def tile(i, nu):
        return jnp.minimum(i, nu[0] - 1)

    def chunk(i, c, nu, n_chunks):
        return jnp.where(i < nu[0], c, n_chunks - 1)

    tf = 256
    n_f = D_FF_EXPERT // tf
    w_up_spec = pl.BlockSpec((1, D_MODEL, tf),
                             lambda i, f, te, tr, nu: (first + te[i], 0, chunk(i, f, nu, n_f)))
    up_spec = pltpu.PrefetchScalarGridSpec(
        num_scalar_prefetch=3,
        grid=(MOE_TILES, n_f),
        in_specs=[pl.BlockSpec((tm, D_MODEL), lambda i, f, te, tr, nu: (tile(i, nu), 0)),
                  w_up_spec, w_up_spec],
        out_specs=pl.BlockSpec((tm, tf), lambda i, f, te, tr, nu: (i, f)),
        scratch_shapes=[pltpu.VMEM((D_MODEL, tf), BF16), pltpu.VMEM((D_MODEL, tf), BF16)],
    )
    h = pl.pallas_call(
        functools.partial(_gup_body, tm=tm),
        grid_spec=up_spec,
        out_shape=jax.ShapeDtypeStruct((MOE_ROWS, D_FF_EXPERT), BF16),
        compiler_params=_params(("arbitrary", "arbitrary")),
        name="moe_up",
    )(tile_expert, tile_rows, n_used, xs,
      w_gate.reshape(n_all, D_MODEL, D_FF_EXPERT), w_up.reshape(n_all, D_MODEL, D_FF_EXPERT))

    tn = 512
    n_n = D_MODEL // tn
    down_spec = pltpu.PrefetchScalarGridSpec(
        num_scalar_prefetch=3,
        grid=(MOE_TILES, n_n),
        in_specs=[pl.BlockSpec((tm, D_FF_EXPERT), lambda i, n, te, tr, nu: (tile(i, nu), 0)),
                  pl.BlockSpec((1, D_FF_EXPERT, tn),
                               lambda i, n, te, tr, nu: (first + te[i], 0, chunk(i, n, nu, n_n)))],
        out_specs=pl.BlockSpec((tm, tn), lambda i, n, te, tr, nu: (i, n)),
        scratch_shapes=[pltpu.VMEM((D_FF_EXPERT, tn), BF16)],
    )
    return pl.pallas_call(
        functools.partial(_gdown_body, tm=tm),
        grid_spec=down_spec,
        out_shape=jax.ShapeDtypeStruct((MOE_ROWS, D_MODEL), F32),
        compiler_params=_params(("arbitrary", "arbitrary")),
        name="moe_down",
    )(tile_expert, tile_rows, n_used, h, w_down.reshape(n_all, D_FF_EXPERT, D_MODEL))


def _combine_body(pos_ref, ys_hbm, r_ref, x_ref, gate_ref, lg_ref, lb_ref, o_ref, buf, sem, *, tc):
    base = (pl.program_id(0) * (SEQ // tc) + pl.program_id(1)) * tc

    def issue(r, carry):
        for k in range(2):
            p = pos_ref[k * N_TOKENS + base + r]
            pltpu.make_async_copy(ys_hbm.at[pl.ds(p, 1)], buf.at[k, pl.ds(r, 1)], sem.at[k]).start()
        return carry

    lax.fori_loop(0, tc, issue, 0, unroll=4)
    for k in range(2):
        pltpu.make_async_copy(ys_hbm.at[pl.ds(0, tc)], buf.at[k], sem.at[k]).wait()
    routing = r_ref[0]
    y = routing[:, 2:3] * buf[0] + routing[:, 3:4] * buf[1]
    o_ref[0] = _deepnorm(x_ref[0], y, gate_ref[0], lg_ref[...], lb_ref[...])


def _combine(pos, ys, routing, x, mod, k, ln_g, ln_b):
    tc = 512
    rows = pl.BlockSpec((1, tc, D_MODEL), lambda b, i, pos: (b, i, 0))
    vec = pl.BlockSpec((1, D_MODEL), lambda b, i, pos: (0, 0))
    grid_spec = pltpu.PrefetchScalarGridSpec(
        num_scalar_prefetch=1,
        grid=(BATCH, SEQ // tc),
        in_specs=[pl.BlockSpec(memory_space=pl.ANY),
                  pl.BlockSpec((1, tc, LANES), lambda b, i, pos: (b, i, 0)),
                  rows,
                  pl.BlockSpec((1, 1, D_MODEL), lambda b, i, pos: (k * 8 + b, 0, 2)),
                  vec, vec],
        out_specs=rows,
        scratch_shapes=[pltpu.VMEM((2, tc, D_MODEL), F32), pltpu.SemaphoreType.DMA((2,))],
    )
    return pl.pallas_call(
        functools.partial(_combine_body, tc=tc),
        grid_spec=grid_spec,
        out_shape=jax.ShapeDtypeStruct((BATCH, SEQ, D_MODEL), F32),
        compiler_params=_params(("arbitrary", "arbitrary")),
        name="moe_combine",
    )(pos, ys, routing, x, mod, ln_g.reshape(1, -1), ln_b.reshape(1, -1))


def _moe(x, mod, k, w_router, layer, w_gate, w_up, w_down, ln_g, ln_b):
    routing, hm = _router(x, mod, k, w_router)
    pos, src, tile_expert, tile_rows, n_used = _route_plan(routing)
    xs = _gather_rows(src, tile_rows, hm)
    ys = _grouped_ffn(xs, tile_expert, tile_rows, n_used, layer, w_gate, w_up, w_down)
    return _combine(pos, ys, routing, x, mod, k, ln_g, ln_b)


def kernel(x, c, positions, ada_w, ada_b, ln_g, ln_b, w_in_ab, q_norm_g, w_q_up, kv_norm_g, w_kv_up, sgu_norm_g, sgu_norm_b, sgu_w, sgu_b, w_out_ab, ffn_w_gate, ffn_w_up, ffn_w_down, pool_w, pool_scale, w_out_c, router_w, moe_w_gate, moe_w_up, moe_w_down):
    mod = _ada_modulation(c, ada_w, ada_b)
    tables = _rope_tables(positions)
    for l in range(DEPTH):
        j = l // 2
        k_tok, k_ch = 2 * l, 2 * l + 1
        if l % 2 == 0:
            cq, ckv, kr, zu, zv = _inproj(x, mod, k_tok, w_in_ab[j])
            q, k, vt = _qkv(cq, ckv, kr, tables, q_norm_g[j], w_q_up[j], kv_norm_g[j], w_kv_up[j])
            attn = _attention(q, k, vt)
            sgu = _sgu(zu, zv, sgu_norm_g[j], sgu_norm_b[j], sgu_w[j], sgu_b[j])
            x = _outproj(attn, sgu, x, mod, k_tok, w_out_ab[j], ln_g[l, 0], ln_b[l, 0])
            x = _ffn(x, mod, k_ch, j, ffn_w_gate, ffn_w_up, ffn_w_down[j], ln_g[l, 1], ln_b[l, 1])
        else:
            x = _pool_mixer(x, mod, k_tok, pool_w[j], pool_scale[j], w_out_c[j], ln_g[l, 0], ln_b[l, 0])
            x = _moe(x, mod, k_ch, router_w[j], j, moe_w_gate, moe_w_up, moe_w_down,
                     ln_g[l, 1], ln_b[l, 1])
    return x
```

```python
import functools

import jax
import jax.numpy as jnp
import numpy as np
from jax import lax
from jax.experimental import pallas as pl
from jax.experimental.pallas import tpu as pltpu

F32 = jnp.float32
BF16 = jnp.bfloat16

D_MODEL = 2048
BATCH = 2
SEQ = 4096
DEPTH = 4
MLA_HEADS = 8
QK_NOPE_DIM = 128
QK_ROPE_DIM = 64
V_HEAD_DIM = 128
Q_LORA_RANK = 768
KV_LORA_RANK = 512
ROPE_THETA = 10000.0
SGU_GROUPS = 8
SGU_GROUP_DIM = 128
SGU_WIDTH = SGU_GROUPS * SGU_GROUP_DIM
CHUNK = 128
ATTN_WIDTH = MLA_HEADS * V_HEAD_DIM
POOL_WINDOWS = (2, 4, 8, 16)
POOL_GROUP_DIM = D_MODEL // 4
D_FF = 5632
N_EXPERTS = 8
D_FF_EXPERT = 2816
DEEPNORM_ALPHA = (2 * DEPTH) ** 0.25
LN_EPS = 1e-5
RMS_EPS = 1e-6
SM_SCALE = (QK_NOPE_DIM + QK_ROPE_DIM) ** -0.5
LOG2_E = 1.4426950408889634

LANES = 128
HEAD_LANES = 2 * LANES
QK_WIDTH = MLA_HEADS * HEAD_LANES
VMEM_LIMIT = 60 * 1024 * 1024
POOL_HALO = 128


def _params(semantics):
    return pltpu.CompilerParams(dimension_semantics=semantics, vmem_limit_bytes=VMEM_LIMIT)


def _const_spec(shape):
    nd = len(shape)
    return pl.BlockSpec(shape, lambda *_: (0,) * nd, pipeline_mode=pl.Buffered(1))


def _dot(a, b):
    return jnp.dot(a, b, preferred_element_type=F32)


def _modulate(x, shift, scale):
    return x * (1.0 + scale) + shift


def _deepnorm(x, y, gate, g, b):
    r = DEEPNORM_ALPHA * x + (1.0 + gate) * y
    mu = jnp.mean(r, axis=-1, keepdims=True)
    rc = r - mu
    var = jnp.mean(rc * rc, axis=-1, keepdims=True)
    return rc * lax.rsqrt(var + LN_EPS) * g + b


def _rms(x, g):
    return x * lax.rsqrt(jnp.mean(x * x, axis=-1, keepdims=True) + RMS_EPS) * g


def _rope(r, cos_t, sin_hi, sin_lo):
    return (r * cos_t + pltpu.roll(r, QK_ROPE_DIM // 2, 1) * sin_hi
            + pltpu.roll(r, LANES - QK_ROPE_DIM // 2, 1) * sin_lo)


def _mod_specs(k, n_grid):
    def spec(part):
        if n_grid == 2:
            return pl.BlockSpec((1, 1, D_MODEL), lambda b, i: (k * 8 + b, 0, part))
        if n_grid == 3:
            return pl.BlockSpec((1, 1, D_MODEL), lambda b, i, f: (k * 8 + b, 0, part))
        return pl.BlockSpec((1, 1, D_MODEL), lambda b, i, e, f: (k * 8 + b, 0, part))
    return spec(0), spec(1), spec(2)


def _ada_body(c_ref, w_ref, b_ref, o_ref):
    s = jax.nn.silu(c_ref[...]).astype(BF16)
    o_ref[0] = _dot(s, w_ref[0].astype(BF16)) + b_ref[0]


def _ada_modulation(c, ada_w, ada_b):
    n_mod = 2 * DEPTH
    tn = 1024
    c_pad = jnp.pad(c, ((0, 8 - BATCH), (0, 0)))
    out = pl.pallas_call(
        _ada_body,
        grid=(n_mod, 3 * D_MODEL // tn),
        in_specs=[
            pl.BlockSpec((8, D_MODEL), lambda k, n: (0, 0)),
            pl.BlockSpec((1, D_MODEL, tn), lambda k, n: (k, 0, n)),
            pl.BlockSpec((1, 1, tn), lambda k, n: (k, 0, n)),
        ],
        out_specs=pl.BlockSpec((1, 8, tn), lambda k, n: (k, 0, n)),
        out_shape=jax.ShapeDtypeStruct((n_mod, 8, 3 * D_MODEL), F32),
        compiler_params=_params(("parallel", "parallel")),
        name="ada_mod",
    )(c_pad, ada_w.reshape(n_mod, D_MODEL, 3 * D_MODEL), ada_b.reshape(n_mod, 1, 3 * D_MODEL))
    return out.reshape(n_mod * 8, 1, 3 * D_MODEL)


def _rope_table_body(pos_ref, inv_ref, cos_ref, hi_ref, lo_ref):
    ang = pos_ref[0].astype(F32) * inv_ref[...]
    lane = lax.broadcasted_iota(jnp.int32, ang.shape, 1)
    cos = jnp.cos(ang)
    sin = jnp.sin(ang)
    half = QK_ROPE_DIM // 2
    cos_ref[0] = jnp.where(lane < QK_ROPE_DIM, cos, 0.0)
    hi_ref[0] = jnp.where(lane < half, 0.0, jnp.where(lane < QK_ROPE_DIM, sin, 0.0))
    lo_ref[0] = jnp.where(lane < half, -sin, 0.0)


def _rope_tables(positions):
    tm = 512
    half = QK_ROPE_DIM // 2
    inv = ROPE_THETA ** (-jnp.arange(0, QK_ROPE_DIM, 2, dtype=F32) / QK_ROPE_DIM)
    inv_lanes = jnp.concatenate([inv, inv, jnp.zeros((LANES - 2 * half,), F32)]).reshape(1, LANES)
    tab = jax.ShapeDtypeStruct((BATCH, SEQ, LANES), F32)
    spec = pl.BlockSpec((1, tm, LANES), lambda b, i: (b, i, 0))
    return pl.pallas_call(
        _rope_table_body,
        grid=(BATCH, SEQ // tm),
        in_specs=[pl.BlockSpec((1, tm, 1), lambda b, i: (b, i, 0)),
                  pl.BlockSpec((1, LANES), lambda b, i: (0, 0))],
        out_specs=(spec, spec, spec),
        out_shape=(tab, tab, tab),
        compiler_params=_params(("parallel", "parallel")),
        name="rope_tables",
    )(positions.reshape(BATCH, SEQ, 1), inv_lanes)


def _inproj_body(x_ref, sh_ref, sc_ref, wq_ref, wkv_ref, wkr_ref, wu_ref, wv_ref,
                 cq_ref, ckv_ref, kr_ref, zu_ref, zv_ref):
    hm = _modulate(x_ref[0], sh_ref[0], sc_ref[0]).astype(BF16)
    cq_ref[0] = _dot(hm, wq_ref[...])
    ckv_ref[0] = _dot(hm, wkv_ref[...])
    kr_ref[0] = _dot(hm, wkr_ref[...])
    zu_ref[0] = _dot(hm, wu_ref[...])
    zv_ref[0] = _dot(hm, wv_ref[...])


def _inproj(x, mod, k, w_in):
    tm = 512
    a, b, c_, d = (Q_LORA_RANK, Q_LORA_RANK + KV_LORA_RANK,
                   Q_LORA_RANK + KV_LORA_RANK + QK_ROPE_DIM,
                   Q_LORA_RANK + KV_LORA_RANK + QK_ROPE_DIM + SGU_WIDTH)
    wq = w_in[:, :a].astype(BF16)
    wkv = w_in[:, a:b].astype(BF16)
    wkr = jnp.pad(w_in[:, b:c_], ((0, 0), (0, LANES - QK_ROPE_DIM))).astype(BF16)
    wu = w_in[:, c_:d].astype(BF16)
    wv = w_in[:, d:].astype(BF16)
    shift, scale, _ = _mod_specs(k, 2)
    widths = (Q_LORA_RANK, KV_LORA_RANK, LANES, SGU_WIDTH, SGU_WIDTH)
    return pl.pallas_call(
        _inproj_body,
        grid=(BATCH, SEQ // tm),
        in_specs=[pl.BlockSpec((1, tm, D_MODEL), lambda b, i: (b, i, 0)), shift, scale]
                 + [_const_spec((D_MODEL, w)) for w in widths],
        out_specs=tuple(pl.BlockSpec((1, tm, w), lambda b, i: (b, i, 0)) for w in widths),
        out_shape=tuple(jax.ShapeDtypeStruct((BATCH, SEQ, w), F32) for w in widths),
        compiler_params=_params(("parallel", "parallel")),
        name="inproj",
    )(x, mod, mod, wq, wkv, wkr, wu, wv)


def _qkv_body(cq_ref, ckv_ref, kr_ref, cos_ref, hi_ref, lo_ref, qg_ref, kvg_ref,
              wqn_ref, wqr_ref, wkn_ref, wvt_ref, q_ref, k_ref, vt_ref):
    cos_t, sin_hi, sin_lo = cos_ref[0], hi_ref[0], lo_ref[0]
    cqn = _rms(cq_ref[0], qg_ref[...]).astype(BF16)
    qn = _dot(cqn, wqn_ref[...])
    qr = _dot(cqn, wqr_ref[...])
    ckvn = _rms(ckv_ref[0], kvg_ref[...]).astype(BF16)
    kn = _dot(ckvn, wkn_ref[...])
    vt_ref[0] = lax.dot_general(wvt_ref[...], ckvn, (((1,), (1,)), ((), ())),
                                preferred_element_type=F32).astype(BF16)
    k_rope = _rope(kr_ref[0], cos_t, sin_hi, sin_lo).astype(BF16)
    for h in range(MLA_HEADS):
        nope = slice(h * LANES, (h + 1) * LANES)
        lo = h * HEAD_LANES
        q_ref[0, :, lo:lo + LANES] = qn[:, nope].astype(BF16)
        q_ref[0, :, lo + LANES:lo + HEAD_LANES] = _rope(qr[:, nope], cos_t, sin_hi, sin_lo).astype(BF16)
        k_ref[0, :, lo:lo + LANES] = kn[:, nope].astype(BF16)
        k_ref[0, :, lo + LANES:lo + HEAD_LANES] = k_rope


def _qkv(cq, ckv, kr, tables, q_norm_g, w_q_up, kv_norm_g, w_kv_up):
    tm = 512
    wq = w_q_up.reshape(Q_LORA_RANK, MLA_HEADS, QK_NOPE_DIM + QK_ROPE_DIM)
    wqn = wq[:, :, :QK_NOPE_DIM].reshape(Q_LORA_RANK, MLA_HEADS * LANES).astype(BF16)
    wqr = jnp.pad(wq[:, :, QK_NOPE_DIM:], ((0, 0), (0, 0), (0, LANES - QK_ROPE_DIM)))
    wqr = wqr.reshape(Q_LORA_RANK, MLA_HEADS * LANES).astype(BF16)
    wkv = w_kv_up.reshape(KV_LORA_RANK, MLA_HEADS, QK_NOPE_DIM + V_HEAD_DIM)
    wkn = wkv[:, :, :QK_NOPE_DIM].reshape(KV_LORA_RANK, MLA_HEADS * LANES).astype(BF16)
    wvt = wkv[:, :, QK_NOPE_DIM:].reshape(KV_LORA_RANK, ATTN_WIDTH).T.astype(BF16)

    def rows(w):
        return pl.BlockSpec((1, tm, w), lambda b, i: (b, i, 0))

    return pl.pallas_call(
        _qkv_body,
        grid=(BATCH, SEQ // tm),
        in_specs=[rows(Q_LORA_RANK), rows(KV_LORA_RANK), rows(LANES),
                  rows(LANES), rows(LANES), rows(LANES),
                  _const_spec((1, Q_LORA_RANK)), _const_spec((1, KV_LORA_RANK)),
                  _const_spec((Q_LORA_RANK, MLA_HEADS * LANES)),
                  _const_spec((Q_LORA_RANK, MLA_HEADS * LANES)),
                  _const_spec((KV_LORA_RANK, MLA_HEADS * LANES)),
                  _const_spec((ATTN_WIDTH, KV_LORA_RANK))],
        out_specs=(rows(QK_WIDTH), rows(QK_WIDTH),
                   pl.BlockSpec((1, ATTN_WIDTH, tm), lambda b, i: (b, 0, i))),
        out_shape=(jax.ShapeDtypeStruct((BATCH, SEQ, QK_WIDTH), BF16),
                   jax.ShapeDtypeStruct((BATCH, SEQ, QK_WIDTH), BF16),
                   jax.ShapeDtypeStruct((BATCH, ATTN_WIDTH, SEQ), BF16)),
        compiler_params=_params(("parallel", "parallel")),
        name="qkv_up",
    )(cq, ckv, kr, *tables, q_norm_g.reshape(1, -1), kv_norm_g.reshape(1, -1), wqn, wqr, wkn, wvt)


def _attn_body(qi_ref, kj_ref, q_ref, k_ref, vt_ref, o_ref, m_sc, l_sc, acc_sc, *, tq):
    p = pl.program_id(1)
    i = qi_ref[p]
    j = kj_ref[p]

    @pl.when(j == 0)
    def _():
        m_sc[...] = jnp.full(m_sc.shape, -jnp.inf, F32)
        l_sc[...] = jnp.zeros(l_sc.shape, F32)
        acc_sc[...] = jnp.zeros(acc_sc.shape, F32)

    def accumulate(diagonal):
        if diagonal:
            key = lax.broadcasted_iota(jnp.int32, (tq, tq), 0)
            qry = lax.broadcasted_iota(jnp.int32, (tq, tq), 1)
            visible = key <= qry
        for h in range(MLA_HEADS):
            qk = slice(h * HEAD_LANES, (h + 1) * HEAD_LANES)
            hv = slice(h * V_HEAD_DIM, (h + 1) * V_HEAD_DIM)
            s = lax.dot_general(k_ref[0, :, qk], q_ref[0, :, qk], (((1,), (1,)), ((), ())),
                                preferred_element_type=F32) * (SM_SCALE * LOG2_E)
            if diagonal:
                s = jnp.where(visible, s, -jnp.inf)
            m_prev = m_sc[h]
            m_new = jnp.maximum(m_prev, jnp.max(s, axis=0, keepdims=True))
            alpha = jnp.exp2(m_prev - m_new)
            pr = jnp.exp2(s - m_new)
            l_sc[h] = alpha * l_sc[h] + jnp.sum(pr, axis=0, keepdims=True)
            acc_sc[hv, :] = alpha * acc_sc[hv, :] + _dot(vt_ref[0, hv, :], pr.astype(BF16))
            m_sc[h] = m_new

    @pl.when(j < i)
    def _():
        accumulate(False)

    @pl.when(j == i)
    def _():
        accumulate(True)
        for h in range(MLA_HEADS):
            hv = slice(h * V_HEAD_DIM, (h + 1) * V_HEAD_DIM)
            o_ref[0, :, hv] = (acc_sc[hv, :] / l_sc[h]).T.astype(BF16)


def _attention(q, k, vt):
    tq = 512
    nb = SEQ // tq
    pairs = [(i, j) for i in range(nb) for j in range(i + 1)]
    qi = jnp.asarray(np.array([p[0] for p in pairs], np.int32))
    kj = jnp.asarray(np.array([p[1] for p in pairs], np.int32))
    grid_spec = pltpu.PrefetchScalarGridSpec(
        num_scalar_prefetch=2,
        grid=(BATCH, len(pairs)),
        in_specs=[pl.BlockSpec((1, tq, QK_WIDTH), lambda b, p, qi, kj: (b, qi[p], 0)),
                  pl.BlockSpec((1, tq, QK_WIDTH), lambda b, p, qi, kj: (b, kj[p], 0)),
                  pl.BlockSpec((1, ATTN_WIDTH, tq), lambda b, p, qi, kj: (b, 0, kj[p]))],
        out_specs=pl.BlockSpec((1, tq, ATTN_WIDTH), lambda b, p, qi, kj: (b, qi[p], 0)),
        scratch_shapes=[pltpu.VMEM((MLA_HEADS, 1, tq), F32),
                        pltpu.VMEM((MLA_HEADS, 1, tq), F32),
                        pltpu.VMEM((ATTN_WIDTH, tq), F32)],
    )
    return pl.pallas_call(
        functools.partial(_attn_body, tq=tq),
        grid_spec=grid_spec,
        out_shape=jax.ShapeDtypeStruct((BATCH, SEQ, ATTN_WIDTH), BF16),
        compiler_params=_params(("parallel", "arbitrary")),
        name="mla_attention",
    )(qi, kj, q, k, vt)


def _sgu_body(zu_ref, zv_ref, g_ref, b_ref, w_ref, bias_ref, o_ref, *, tm):
    row = lax.broadcasted_iota(jnp.int32, (CHUNK, CHUNK), 0)
    col = lax.broadcasted_iota(jnp.int32, (CHUNK, CHUNK), 1)
    for g in range(SGU_GROUPS):
        gs = slice(g * SGU_GROUP_DIM, (g + 1) * SGU_GROUP_DIM)
        v = jax.nn.gelu(zv_ref[0, :, gs])
        mu = jnp.mean(v, axis=-1, keepdims=True)
        vc = v - mu
        var = jnp.mean(vc * vc, axis=-1, keepdims=True)
        vn = (vc * lax.rsqrt(var + LN_EPS) * g_ref[:, gs] + b_ref[:, gs]).astype(BF16)
        w = jnp.where(col <= row, w_ref[g], 0.0).astype(BF16)
        for n in range(tm // CHUNK):
            rs = slice(n * CHUNK, (n + 1) * CHUNK)
            s = _dot(w, vn[rs]) + bias_ref[:, gs]
            o_ref[0, rs, gs] = (jax.nn.gelu(zu_ref[0, rs, gs]) * s).astype(BF16)


def _sgu(zu, zv, norm_g, norm_b, sgu_w, sgu_b):
    tm = 512
    bias = jnp.repeat(sgu_b.T, SGU_GROUP_DIM, axis=1)
    rows = pl.BlockSpec((1, tm, SGU_WIDTH), lambda b, i: (b, i, 0))
    return pl.pallas_call(
        functools.partial(_sgu_body, tm=tm),
        grid=(BATCH, SEQ // tm),
        in_specs=[rows, rows, _const_spec((1, SGU_WIDTH)), _const_spec((1, SGU_WIDTH)),
                  _const_spec((SGU_GROUPS, CHUNK, CHUNK)), _const_spec((CHUNK, SGU_WIDTH))],
        out_specs=rows,
        out_shape=jax.ShapeDtypeStruct((BATCH, SEQ, SGU_WIDTH), BF16),
        compiler_params=_params(("parallel", "parallel")),
        name="sgu",
    )(zu, zv, norm_g.reshape(1, -1), norm_b.reshape(1, -1), sgu_w, bias)


def _outproj_body(a_ref, s_ref, x_ref, gate_ref, wa_ref, ws_ref, lg_ref, lb_ref, o_ref):
    y = _dot(a_ref[0], wa_ref[...]) + _dot(s_ref[0], ws_ref[...])
    o_ref[0] = _deepnorm(x_ref[0], y, gate_ref[0], lg_ref[...], lb_ref[...])


def _outproj(attn, sgu, x, mod, k, w_out, ln_g, ln_b):
    tm = 512
    wa = w_out[:ATTN_WIDTH].astype(BF16)
    ws = w_out[ATTN_WIDTH:].astype(BF16)
    _, _, gate = _mod_specs(k, 2)

    def rows(w):
        return pl.BlockSpec((1, tm, w), lambda b, i: (b, i, 0))

    return pl.pallas_call(
        _outproj_body,
        grid=(BATCH, SEQ // tm),
        in_specs=[rows(ATTN_WIDTH), rows(SGU_WIDTH), rows(D_MODEL), gate,
                  _const_spec((ATTN_WIDTH, D_MODEL)), _const_spec((SGU_WIDTH, D_MODEL)),
                  _const_spec((1, D_MODEL)), _const_spec((1, D_MODEL))],
        out_specs=rows(D_MODEL),
        out_shape=jax.ShapeDtypeStruct((BATCH, SEQ, D_MODEL), F32),
        compiler_params=_params(("parallel", "parallel")),
        name="outproj_ab",
    )(attn, sgu, x, mod, wa, ws, ln_g.reshape(1, -1), ln_b.reshape(1, -1))


def _ffn_up_body(x_ref, sh_ref, sc_ref, wg_ref, wu_ref, h_ref, hm_sc):
    @pl.when(pl.program_id(2) == 0)
    def _():
        hm_sc[...] = _modulate(x_ref[0], sh_ref[0], sc_ref[0]).astype(BF16)

    hm = hm_sc[...]
    g = _dot(hm, wg_ref[0].astype(BF16))
    u = _dot(hm, wu_ref[0].astype(BF16))
    h_ref[0] = (jax.nn.silu(g) * u).astype(BF16)


def _ffn_down_body(h_ref, x_ref, gate_ref, wd_ref, lg_ref, lb_ref, o_ref, *, n_n, tn):
    n = pl.program_id(2)
    o_ref[0, :, pl.ds(pl.multiple_of(n * tn, tn), tn)] = _dot(h_ref[0], wd_ref[...])

    @pl.when(n == n_n - 1)
    def _():
        o_ref[0] = _deepnorm(x_ref[0], o_ref[0], gate_ref[0], lg_ref[...], lb_ref[...])


def _ffn(x, mod, k, layer, w_gate, w_up, w_down, ln_g, ln_b):
    tm, tf = 2048, 256
    shift, scale, gate = _mod_specs(k, 3)
    h = pl.pallas_call(
        _ffn_up_body,
        grid=(BATCH, SEQ // tm, D_FF // tf),
        in_specs=[pl.BlockSpec((1, tm, D_MODEL), lambda b, i, f: (b, i, 0), pipeline_mode=pl.Buffered(1)),
                  shift, scale,
                  pl.BlockSpec((1, D_MODEL, tf), lambda b, i, f: (layer, 0, f)),
                  pl.BlockSpec((1, D_MODEL, tf), lambda b, i, f: (layer, 0, f))],
        out_specs=pl.BlockSpec((1, tm, tf), lambda b, i, f: (b, i, f)),
        out_shape=jax.ShapeDtypeStruct((BATCH, SEQ, D_FF), BF16),
        scratch_shapes=[pltpu.VMEM((tm, D_MODEL), BF16)],
        compiler_params=_params(("parallel", "parallel", "arbitrary")),
        name="ffn_up",
    )(x, mod, mod, w_gate, w_up)

    tm, tn = 1024, 256
    n_n = D_MODEL // tn
    rows = pl.BlockSpec((1, tm, D_MODEL), lambda b, i, n: (b, i, 0))
    vec = pl.BlockSpec((1, D_MODEL), lambda b, i, n: (0, 0))
    return pl.pallas_call(
        functools.partial(_ffn_down_body, n_n=n_n, tn=tn),
        grid=(BATCH, SEQ // tm, n_n),
        in_specs=[pl.BlockSpec((1, tm, D_FF), lambda b, i, n: (b, i, 0)),
                  pl.BlockSpec((1, tm, D_MODEL), lambda b, i, n: (b, i, 0), pipeline_mode=pl.Buffered(1)),
                  gate,
                  pl.BlockSpec((D_FF, tn), lambda b, i, n: (0, n)),
                  vec, vec],
        out_specs=rows,
        out_shape=jax.ShapeDtypeStruct((BATCH, SEQ, D_MODEL), F32),
        compiler_params=_params(("parallel", "parallel", "arbitrary")),
        name="ffn_down",
    )(h, x, mod, w_down.astype(BF16), ln_g.reshape(1, -1), ln_b.reshape(1, -1))


def _pool_body(x_ref, halo_ref, sh_ref, sc_ref, gate_ref, pw_ref, ps_ref, wo_ref, lg_ref, lb_ref,
               o_ref, *, tm):
    i = pl.program_id(1)
    x = x_ref[0]
    hm = _modulate(x, sh_ref[0], sc_ref[0])
    halo = jnp.where(i > 0, _modulate(halo_ref[0], sh_ref[0], sc_ref[0]), 0.0)
    ext = jnp.concatenate([halo, hm], axis=0)
    ext_hi = ext.astype(BF16)
    ext_lo = (ext - ext_hi.astype(F32)).astype(BF16)
    t_loc = lax.broadcasted_iota(jnp.int32, (tm, tm + POOL_HALO), 0)
    s_loc = lax.broadcasted_iota(jnp.int32, (tm, tm + POOL_HALO), 1)
    lag = t_loc + POOL_HALO - s_loc
    t_glob = i * tm + lax.broadcasted_iota(jnp.int32, (tm, 1), 0)
    outs = []
    for gi, w in enumerate(POOL_WINDOWS):
        cs = slice(gi * POOL_GROUP_DIM, (gi + 1) * POOL_GROUP_DIM)
        band = jnp.where(lag >= 0, jnp.where(lag < w, 1.0, 0.0), 0.0).astype(BF16)
        wsum = _dot(band, ext_hi[:, cs]) + _dot(band, ext_lo[:, cs])
        count = jnp.minimum(t_glob + 1, w).astype(F32)
        d = (wsum / count - hm[:, cs]).astype(BF16)
        outs.append(_dot(d, pw_ref[gi]))
    pooled = (jnp.concatenate(outs, axis=-1) * ps_ref[...]).astype(BF16)
    y = _dot(pooled, wo_ref[...])
    o_ref[0] = _deepnorm(x, y, gate_ref[0], lg_ref[...], lb_ref[...])


def _pool_mixer(x, mod, k, pool_w, pool_scale, w_out_c, ln_g, ln_b):
    tm = 512
    per = tm // POOL_HALO
    shift, scale, gate = _mod_specs(k, 2)
    rows = pl.BlockSpec((1, tm, D_MODEL), lambda b, i: (b, i, 0))
    halo = pl.BlockSpec((1, POOL_HALO, D_MODEL), lambda b, i: (b, jnp.maximum(i * per - 1, 0), 0))
    return pl.pallas_call(
        functools.partial(_pool_body, tm=tm),
        grid=(BATCH, SEQ // tm),
        in_specs=[rows, halo, shift, scale, gate,
                  _const_spec((len(POOL_WINDOWS), POOL_GROUP_DIM, POOL_GROUP_DIM)),
                  _const_spec((1, D_MODEL)), _const_spec((D_MODEL, D_MODEL)),
                  _const_spec((1, D_MODEL)), _const_spec((1, D_MODEL))],
        out_specs=rows,
        out_shape=jax.ShapeDtypeStruct((BATCH, SEQ, D_MODEL), F32),
        compiler_params=_params(("parallel", "parallel")),
        name="pool_mixer",
    )(x, x, mod, mod, mod, pool_w.astype(BF16), pool_scale.reshape(1, -1),
      w_out_c.astype(BF16), ln_g.reshape(1, -1), ln_b.reshape(1, -1))


def _router_body(x_ref, sh_ref, sc_ref, wr_ref, g_ref, hm_ref):
    hm = _modulate(x_ref[0], sh_ref[0], sc_ref[0])
    hm_ref[0] = hm
    hm_hi = hm.astype(BF16)
    hm_lo = (hm - hm_hi.astype(F32)).astype(BF16)
    w = wr_ref[...]
    w_hi = w.astype(BF16)
    w_lo = (w - w_hi.astype(F32)).astype(BF16)
    logits = _dot(hm_hi, w_hi) + (_dot(hm_lo, w_hi) + _dot(hm_hi, w_lo))
    lane = lax.broadcasted_iota(jnp.int32, logits.shape, 1).astype(F32)
    logits = jnp.where(lane < N_EXPERTS, logits, -jnp.inf)
    v1 = jnp.max(logits, axis=-1, keepdims=True)
    i1 = jnp.min(jnp.where(logits == v1, lane, float(LANES)), axis=-1, keepdims=True)
    rest = jnp.where(lane == i1, -jnp.inf, logits)
    v2 = jnp.max(rest, axis=-1, keepdims=True)
    i2 = jnp.min(jnp.where(rest == v2, lane, float(LANES)), axis=-1, keepdims=True)
    e2 = jnp.exp(v2 - v1)
    den = 1.0 + e2
    g_ref[0] = (jnp.where(lane == 0.0, i1, 0.0) + jnp.where(lane == 1.0, i2, 0.0)
                + jnp.where(lane == 2.0, 1.0 / den, 0.0) + jnp.where(lane == 3.0, e2 / den, 0.0))


def _router(x, mod, k, w_router):
    tm = 512
    shift, scale, _ = _mod_specs(k, 2)
    wr = jnp.pad(w_router, ((0, 0), (0, LANES - N_EXPERTS)))
    rows = pl.BlockSpec((1, tm, D_MODEL), lambda b, i: (b, i, 0))
    return pl.pallas_call(
        _router_body,
        grid=(BATCH, SEQ // tm),
        in_specs=[rows, shift, scale, _const_spec((D_MODEL, LANES))],
        out_specs=(pl.BlockSpec((1, tm, LANES), lambda b, i: (b, i, 0)), rows),
        out_shape=(jax.ShapeDtypeStruct((BATCH, SEQ, LANES), F32),
                   jax.ShapeDtypeStruct((BATCH, SEQ, D_MODEL), F32)),
        compiler_params=_params(("parallel", "parallel")),
        name="router",
    )(x, mod, mod, wr)


N_TOKENS = BATCH * SEQ
N_ASSIGN = 2 * N_TOKENS
MOE_BLOCK = 1024
MOE_SUB = 256
MOE_TILE = N_ASSIGN // N_EXPERTS + MOE_SUB
MOE_TILES = -(-(N_ASSIGN + N_EXPERTS * (MOE_TILE - 1)) // MOE_TILE)
MOE_ROWS = MOE_TILES * MOE_TILE
GATHER_ROWS = MOE_TILE // 3
assert MOE_TILE % GATHER_ROWS == 0 and GATHER_ROWS % 8 == 0


def _route_plan(routing):
    e_flat = routing.reshape(N_TOKENS, LANES)[:, :2].astype(jnp.int32).T.reshape(N_ASSIGN)
    onehot = (e_flat[:, None] == jnp.arange(N_EXPERTS, dtype=jnp.int32)[None, :]).astype(jnp.int32)
    csum = jnp.cumsum(onehot, axis=0)
    counts = csum[-1]
    rank = jnp.sum(csum * onehot, axis=1) - 1
    padded = (counts + MOE_TILE - 1) // MOE_TILE * MOE_TILE
    group_end = jnp.cumsum(padded)
    group_start = group_end - padded
    pos = (jnp.sum(onehot * group_start[None, :], axis=1) + rank).astype(jnp.int32)
    tok = jnp.tile(jnp.arange(N_TOKENS, dtype=jnp.int32), 2)
    src = jnp.zeros((MOE_ROWS,), jnp.int32).at[pos].set(tok, unique_indices=True)
    tile_start = jnp.arange(MOE_TILES, dtype=jnp.int32) * MOE_TILE
    n_used = group_end[-1] // MOE_TILE
    tile_expert = jnp.sum((tile_start[:, None] >= group_end[None, :]).astype(jnp.int32), axis=1)
    last_expert = jnp.max(jnp.where(counts > 0, jnp.arange(N_EXPERTS, dtype=jnp.int32), 0))
    tile_expert = jnp.minimum(tile_expert, last_expert)
    tile_rows = jnp.clip(counts[tile_expert] - (tile_start - group_start[tile_expert]), 0, MOE_TILE)
    tile_rows = jnp.where(jnp.arange(MOE_TILES) < n_used, tile_rows, 0).astype(jnp.int32)
    return pos, src, tile_expert.astype(jnp.int32), tile_rows, n_used.reshape(1).astype(jnp.int32)


GATHER_UNROLL = 8


def _gather_body(src_ref, rows_ref, hm_hbm, o_ref, buf, sem, *, tg):
    i = pl.program_id(0)
    base = i * tg
    per = MOE_TILE // tg
    n_valid = jnp.clip(rows_ref[i // per] - (i % per) * tg, 0, tg)

    @pl.when(n_valid < tg)
    def _():
        buf[...] = jnp.zeros(buf.shape, F32)

    def issue_row(r):
        t = src_ref[base + r]
        pltpu.make_async_copy(hm_hbm.at[pl.ds(t, 1)], buf.at[pl.ds(r, 1)], sem).start()

    def issue_group(g, carry):
        for u in range(GATHER_UNROLL):
            issue_row(g * GATHER_UNROLL + u)
        return carry

    def issue_one(r, carry):
        issue_row(r)
        return carry

    n_groups = n_valid // GATHER_UNROLL
    n_grouped = pl.multiple_of(n_groups * GATHER_UNROLL, GATHER_UNROLL)
    lax.fori_loop(0, n_groups, issue_group, 0)
    lax.fori_loop(n_grouped, n_valid, issue_one, 0)

    @pl.when(n_groups > 0)
    def _():
        pltpu.make_async_copy(hm_hbm.at[pl.ds(0, n_grouped)], buf.at[pl.ds(0, n_grouped)], sem).wait()

    def wait_one(r, carry):
        pltpu.make_async_copy(hm_hbm.at[pl.ds(0, 1)], buf.at[pl.ds(r, 1)], sem).wait()
        return carry

    lax.fori_loop(n_grouped, n_valid, wait_one, 0)

    o_ref[...] = buf[...].astype(BF16)


def _gather_rows(src, tile_rows, hm):
    tg = GATHER_ROWS
    grid_spec = pltpu.PrefetchScalarGridSpec(
        num_scalar_prefetch=2,
        grid=(MOE_ROWS // tg,),
        in_specs=[pl.BlockSpec(memory_space=pl.ANY)],
        out_specs=pl.BlockSpec((tg, D_MODEL), lambda i, src, tr: (i, 0)),
        scratch_shapes=[pltpu.VMEM((tg, D_MODEL), F32), pltpu.SemaphoreType.DMA(())],
    )
    return pl.pallas_call(
        functools.partial(_gather_body, tg=tg),
        grid_spec=grid_spec,
        out_shape=jax.ShapeDtypeStruct((MOE_ROWS, D_MODEL), BF16),
        compiler_params=_params(("arbitrary",)),
        name="moe_gather",
    )(src, tile_rows, hm.reshape(N_TOKENS, D_MODEL))


def _for_valid_blocks(rows, tm, block_fn, empty_fn):
    for b0 in range(0, tm, MOE_BLOCK):
        b1 = min(b0 + MOE_BLOCK, tm)
        whole = b1 if b1 - b0 == MOE_BLOCK else tm + 1
        if whole <= tm:
            pl.when(rows >= whole)(functools.partial(block_fn, slice(b0, b1)))

        for r0 in range(b0, b1, MOE_SUB):
            rs = slice(r0, r0 + MOE_SUB)

            @pl.when(jnp.logical_and(rows < whole, r0 < rows))
            def _():
                block_fn(rs)

            @pl.when(r0 >= rows)
            def _():
                empty_fn(rs)


def _gup_body(te_ref, rows_ref, nu_ref, xs_ref, wg_ref, wu_ref, h_ref, wgb, wub, *, tm):
    rows = rows_ref[pl.program_id(0)]

    @pl.when(rows > 0)
    def _():
        wgb[...] = wg_ref[0].astype(BF16)
        wub[...] = wu_ref[0].astype(BF16)

    def block(rs):
        xsb = xs_ref[rs, :]
        h_ref[rs, :] = (jax.nn.silu(_dot(xsb, wgb[...])) * _dot(xsb, wub[...])).astype(BF16)

    def empty(rs):
        h_ref[rs, :] = jnp.zeros((rs.stop - rs.start, h_ref.shape[1]), BF16)

    _for_valid_blocks(rows, tm, block, empty)


def _gdown_body(te_ref, rows_ref, nu_ref, h_ref, wd_ref, o_ref, wdb, *, tm):
    rows = rows_ref[pl.program_id(0)]

    @pl.when(rows > 0)
    def _():
        wdb[...] = wd_ref[0].astype(BF16)

    def block(rs):
        o_ref[rs, :] = _dot(h_ref[rs, :], wdb[...])

    def empty(rs):
        o_ref[rs, :] = jnp.zeros((rs.stop - rs.start, o_ref.shape[1]), F32)

    _for_valid_blocks(rows, tm, block, empty)


def _grouped_ffn(xs, tile_expert, tile_rows, n_used, layer, w_gate, w_up, w_down):
    tm = MOE_TILE
    first = layer * N_EXPERTS
    n_all = w_gate.shape[0] * N_EXPERTS

    def tile(i, nu):
        return jnp.minimum(i, nu[0] - 1)

    def chunk(i, c, nu, n_chunks):
        return jnp.where(i < nu[0], c, n_chunks - 1)

    tf = 256
    n_f = D_FF_EXPERT // tf
    w_up_spec = pl.BlockSpec((1, D_MODEL, tf),
                             lambda i, f, te, tr, nu: (first + te[i], 0, chunk(i, f, nu, n_f)))
    up_spec = pltpu.PrefetchScalarGridSpec(
        num_scalar_prefetch=3,
        grid=(MOE_TILES, n_f),
        in_specs=[pl.BlockSpec((tm, D_MODEL), lambda i, f, te, tr, nu: (tile(i, nu), 0)),
                  w_up_spec, w_up_spec],
        out_specs=pl.BlockSpec((tm, tf), lambda i, f, te, tr, nu: (i, f)),
        scratch_shapes=[pltpu.VMEM((D_MODEL, tf), BF16), pltpu.VMEM((D_MODEL, tf), BF16)],
    )
    h = pl.pallas_call(
        functools.partial(_gup_body, tm=tm),
        grid_spec=up_spec,
        out_shape=jax.ShapeDtypeStruct((MOE_ROWS, D_FF_EXPERT), BF16),
        compiler_params=_params(("arbitrary", "arbitrary")),
        name="moe_up",
    )(tile_expert, tile_rows, n_used, xs,
      w_gate.reshape(n_all, D_MODEL, D_FF_EXPERT), w_up.reshape(n_all, D_MODEL, D_FF_EXPERT))

    tn = 512
    n_n = D_MODEL // tn
    down_spec = pltpu.PrefetchScalarGridSpec(
        num_scalar_prefetch=3,
        grid=(MOE_TILES, n_n),
        in_specs=[pl.BlockSpec((tm, D_FF_EXPERT), lambda i, n, te, tr, nu: (tile(i, nu), 0)),
                  pl.BlockSpec((1, D_FF_EXPERT, tn),
                               lambda i, n, te, tr, nu: (first + te[i], 0, chunk(i, n, nu, n_n)))],
        out_specs=pl.BlockSpec((tm, tn), lambda i, n, te, tr, nu: (i, n)),
        scratch_shapes=[pltpu.VMEM((D_FF_EXPERT, tn), BF16)],
    )
    return pl.pallas_call(
        functools.partial(_gdown_body, tm=tm),
        grid_spec=down_spec,
        out_shape=jax.ShapeDtypeStruct((MOE_ROWS, D_MODEL), F32),
        compiler_params=_params(("arbitrary", "arbitrary")),
        name="moe_down",
    )(tile_expert, tile_rows, n_used, h, w_down.reshape(n_all, D_FF_EXPERT, D_MODEL))


def _combine_body(pos_ref, ys_hbm, r_ref, x_ref, gate_ref, lg_ref, lb_ref, o_ref, buf, sem, *, tc):
    base = (pl.program_id(0) * (SEQ // tc) + pl.program_id(1)) * tc

    def issue(r, carry):
        for k in range(2):
            p = pos_ref[k * N_TOKENS + base + r]
            pltpu.make_async_copy(ys_hbm.at[pl.ds(p, 1)], buf.at[k, pl.ds(r, 1)], sem.at[k]).start()
        return carry

    lax.fori_loop(0, tc, issue, 0, unroll=4)
    for k in range(2):
        pltpu.make_async_copy(ys_hbm.at[pl.ds(0, tc)], buf.at[k], sem.at[k]).wait()
    routing = r_ref[0]
    y = routing[:, 2:3] * buf[0] + routing[:, 3:4] * buf[1]
    o_ref[0] = _deepnorm(x_ref[0], y, gate_ref[0], lg_ref[...], lb_ref[...])


def _combine(pos, ys, routing, x, mod, k, ln_g, ln_b):
    tc = 512
    rows = pl.BlockSpec((1, tc, D_MODEL), lambda b, i, pos: (b, i, 0))
    vec = pl.BlockSpec((1, D_MODEL), lambda b, i, pos: (0, 0))
    grid_spec = pltpu.PrefetchScalarGridSpec(
        num_scalar_prefetch=1,
        grid=(BATCH, SEQ // tc),
        in_specs=[pl.BlockSpec(memory_space=pl.ANY),
                  pl.BlockSpec((1, tc, LANES), lambda b, i, pos: (b, i, 0)),
                  rows,
                  pl.BlockSpec((1, 1, D_MODEL), lambda b, i, pos: (k * 8 + b, 0, 2)),
                  vec, vec],
        out_specs=rows,
        scratch_shapes=[pltpu.VMEM((2, tc, D_MODEL), F32), pltpu.SemaphoreType.DMA((2,))],
    )
    return pl.pallas_call(
        functools.partial(_combine_body, tc=tc),
        grid_spec=grid_spec,
        out_shape=jax.ShapeDtypeStruct((BATCH, SEQ, D_MODEL), F32),
        compiler_params=_params(("arbitrary", "arbitrary")),
        name="moe_combine",
    )(pos, ys, routing, x, mod, ln_g.reshape(1, -1), ln_b.reshape(1, -1))


def _moe(x, mod, k, w_router, layer, w_gate, w_up, w_down, ln_g, ln_b):
    routing, hm = _router(x, mod, k, w_router)
    pos, src, tile_expert, tile_rows, n_used = _route_plan(routing)
    xs = _gather_rows(src, tile_rows, hm)
    ys = _grouped_ffn(xs, tile_expert, tile_rows, n_used, layer, w_gate, w_up, w_down)
    return _combine(pos, ys, routing, x, mod, k, ln_g, ln_b)


def kernel(x, c, positions, ada_w, ada_b, ln_g, ln_b, w_in_ab, q_norm_g, w_q_up, kv_norm_g, w_kv_up, sgu_norm_g, sgu_norm_b, sgu_w, sgu_b, w_out_ab, ffn_w_gate, ffn_w_up, ffn_w_down, pool_w, pool_scale, w_out_c, router_w, moe_w_gate, moe_w_up, moe_w_down):
    mod = _ada_modulation(c, ada_w, ada_b)
    tables = _rope_tables(positions)
    for l in range(DEPTH):
        j = l // 2
        k_tok, k_ch = 2 * l, 2 * l + 1
        if l % 2 == 0:
            cq, ckv, kr, zu, zv = _inproj(x, mod, k_tok, w_in_ab[j])
            q, k, vt = _qkv(cq, ckv, kr, tables, q_norm_g[j], w_q_up[j], kv_norm_g[j], w_kv_up[j])
            attn = _attention(q, k, vt)
            sgu = _sgu(zu, zv, sgu_norm_g[j], sgu_norm_b[j], sgu_w[j], sgu_b[j])
            x = _outproj(attn, sgu, x, mod, k_tok, w_out_ab[j], ln_g[l, 0], ln_b[l, 0])
            x = _ffn(x, mod, k_ch, j, ffn_w_gate, ffn_w_up, ffn_w_down[j], ln_g[l, 1], ln_b[l, 1])
        else:
            x = _pool_mixer(x, mod, k_tok, pool_w[j], pool_scale[j], w_out_c[j], ln_g[l, 0], ln_b[l, 0])
            x = _moe(x, mod, k_ch, router_w[j], j, moe_w_gate, moe_w_up, moe_w_down,
                     ln_g[l, 1], ln_b[l, 1])
    return x
```

```python
import functools

import jax
import jax.numpy as jnp
import numpy as np
from jax import lax
from jax.experimental import pallas as pl
from jax.experimental.pallas import tpu as pltpu

F32 = jnp.float32
BF16 = jnp.bfloat16

D_MODEL = 2048
BATCH = 2
SEQ = 4096
DEPTH = 4
MLA_HEADS = 8
QK_NOPE_DIM = 128
QK_ROPE_DIM = 64
V_HEAD_DIM = 128
Q_LORA_RANK = 768
KV_LORA_RANK = 512
ROPE_THETA = 10000.0
SGU_GROUPS = 8
SGU_GROUP_DIM = 128
SGU_WIDTH = SGU_GROUPS * SGU_GROUP_DIM
CHUNK = 128
ATTN_WIDTH = MLA_HEADS * V_HEAD_DIM
POOL_WINDOWS = (2, 4, 8, 16)
POOL_GROUP_DIM = D_MODEL // 4
D_FF = 5632
N_EXPERTS = 8
D_FF_EXPERT = 2816
DEEPNORM_ALPHA = (2 * DEPTH) ** 0.25
LN_EPS = 1e-5
RMS_EPS = 1e-6
SM_SCALE = (QK_NOPE_DIM + QK_ROPE_DIM) ** -0.5
LOG2_E = 1.4426950408889634

LANES = 128
HEAD_LANES = 2 * LANES
QK_WIDTH = MLA_HEADS * HEAD_LANES
VMEM_LIMIT = 60 * 1024 * 1024
POOL_HALO = 128


def _params(semantics):
    return pltpu.CompilerParams(dimension_semantics=semantics, vmem_limit_bytes=VMEM_LIMIT)


def _const_spec(shape):
    nd = len(shape)
    return pl.BlockSpec(shape, lambda *_: (0,) * nd, pipeline_mode=pl.Buffered(1))


def _dot(a, b):
    return jnp.dot(a, b, preferred_element_type=F32)


def _modulate(x, shift, scale):
    return x * (1.0 + scale) + shift


def _deepnorm(x, y, gate, g, b):
    r = DEEPNORM_ALPHA * x + (1.0 + gate) * y
    mu = jnp.mean(r, axis=-1, keepdims=True)
    rc = r - mu
    var = jnp.mean(rc * rc, axis=-1, keepdims=True)
    return rc * lax.rsqrt(var + LN_EPS) * g + b


def _rms(x, g):
    return x * lax.rsqrt(jnp.mean(x * x, axis=-1, keepdims=True) + RMS_EPS) * g


def _rope(r, cos_t, sin_hi, sin_lo):
    return (r * cos_t + pltpu.roll(r, QK_ROPE_DIM // 2, 1) * sin_hi
            + pltpu.roll(r, LANES - QK_ROPE_DIM // 2, 1) * sin_lo)


def _mod_specs(k, n_grid):
    def spec(part):
        if n_grid == 2:
            return pl.BlockSpec((1, 1, D_MODEL), lambda b, i: (k * 8 + b, 0, part))
        if n_grid == 3:
            return pl.BlockSpec((1, 1, D_MODEL), lambda b, i, f: (k * 8 + b, 0, part))
        return pl.BlockSpec((1, 1, D_MODEL), lambda b, i, e, f: (k * 8 + b, 0, part))
    return spec(0), spec(1), spec(2)


def _ada_body(c_ref, w_ref, b_ref, o_ref):
    s = jax.nn.silu(c_ref[...]).astype(BF16)
    o_ref[0] = _dot(s, w_ref[0].astype(BF16)) + b_ref[0]


def _ada_modulation(c, ada_w, ada_b):
    n_mod = 2 * DEPTH
    tn = 1024
    c_pad = jnp.pad(c, ((0, 8 - BATCH), (0, 0)))
    out = pl.pallas_call(
        _ada_body,
        grid=(n_mod, 3 * D_MODEL // tn),
        in_specs=[
            pl.BlockSpec((8, D_MODEL), lambda k, n: (0, 0)),
            pl.BlockSpec((1, D_MODEL, tn), lambda k, n: (k, 0, n)),
            pl.BlockSpec((1, 1, tn), lambda k, n: (k, 0, n)),
        ],
        out_specs=pl.BlockSpec((1, 8, tn), lambda k, n: (k, 0, n)),
        out_shape=jax.ShapeDtypeStruct((n_mod, 8, 3 * D_MODEL), F32),
        compiler_params=_params(("parallel", "parallel")),
        name="ada_mod",
    )(c_pad, ada_w.reshape(n_mod, D_MODEL, 3 * D_MODEL), ada_b.reshape(n_mod, 1, 3 * D_MODEL))
    return out.reshape(n_mod * 8, 1, 3 * D_MODEL)


def _rope_table_body(pos_ref, inv_ref, cos_ref, hi_ref, lo_ref):
    ang = pos_ref[0].astype(F32) * inv_ref[...]
    lane = lax.broadcasted_iota(jnp.int32, ang.shape, 1)
    cos = jnp.cos(ang)
    sin = jnp.sin(ang)
    half = QK_ROPE_DIM // 2
    cos_ref[0] = jnp.where(lane < QK_ROPE_DIM, cos, 0.0)
    hi_ref[0] = jnp.where(lane < half, 0.0, jnp.where(lane < QK_ROPE_DIM, sin, 0.0))
    lo_ref[0] = jnp.where(lane < half, -sin, 0.0)


def _rope_tables(positions):
    tm = 512
    half = QK_ROPE_DIM // 2
    inv = ROPE_THETA ** (-jnp.arange(0, QK_ROPE_DIM, 2, dtype=F32) / QK_ROPE_DIM)
    inv_lanes = jnp.concatenate([inv, inv, jnp.zeros((LANES - 2 * half,), F32)]).reshape(1, LANES)
    tab = jax.ShapeDtypeStruct((BATCH, SEQ, LANES), F32)
    spec = pl.BlockSpec((1, tm, LANES), lambda b, i: (b, i, 0))
    return pl.pallas_call(
        _rope_table_body,
        grid=(BATCH, SEQ // tm),
        in_specs=[pl.BlockSpec((1, tm, 1), lambda b, i: (b, i, 0)),
                  pl.BlockSpec((1, LANES), lambda b, i: (0, 0))],
        out_specs=(spec, spec, spec),
        out_shape=(tab, tab, tab),
        compiler_params=_params(("parallel", "parallel")),
        name="rope_tables",
    )(positions.reshape(BATCH, SEQ, 1), inv_lanes)


def _front_body(x_ref, sh_ref, sc_ref, wq_ref, wkv_ref, wkr_ref, wu_ref, wv_ref,
                cos_ref, hi_ref, lo_ref, qg_ref, kvg_ref, wqn_ref, wqr_ref, wkn_ref, wvt_ref,
                sg_ref, sb_ref, sw_ref, sbias_ref,
                q_ref, k_ref, vt_ref, sgu_ref, *, tm):
    hm = _modulate(x_ref[0], sh_ref[0], sc_ref[0]).astype(BF16)
    _qkv_compute(_dot(hm, wq_ref[...]), _dot(hm, wkv_ref[...]), _dot(hm, wkr_ref[...]),
                 cos_ref[0], hi_ref[0], lo_ref[0], qg_ref, kvg_ref,
                 wqn_ref, wqr_ref, wkn_ref, wvt_ref, q_ref, k_ref, vt_ref)
    _sgu_compute(_dot(hm, wu_ref[...]), _dot(hm, wv_ref[...]),
                 sg_ref, sb_ref, sw_ref, sbias_ref, sgu_ref, tm)


def _front(x, mod, k, tables, w_in, q_norm_g, w_q_up, kv_norm_g, w_kv_up,
           sgu_norm_g, sgu_norm_b, sgu_w, sgu_b):
    tm = 512
    a, b, c_, d = (Q_LORA_RANK, Q_LORA_RANK + KV_LORA_RANK,
                   Q_LORA_RANK + KV_LORA_RANK + QK_ROPE_DIM,
                   Q_LORA_RANK + KV_LORA_RANK + QK_ROPE_DIM + SGU_WIDTH)
    wq = w_in[:, :a].astype(BF16)
    wkv = w_in[:, a:b].astype(BF16)
    wkr = jnp.pad(w_in[:, b:c_], ((0, 0), (0, LANES - QK_ROPE_DIM))).astype(BF16)
    wu = w_in[:, c_:d].astype(BF16)
    wv = w_in[:, d:].astype(BF16)
    in_widths = (Q_LORA_RANK, KV_LORA_RANK, LANES, SGU_WIDTH, SGU_WIDTH)

    wq_up = w_q_up.reshape(Q_LORA_RANK, MLA_HEADS, QK_NOPE_DIM + QK_ROPE_DIM)
    wqn = wq_up[:, :, :QK_NOPE_DIM].reshape(Q_LORA_RANK, MLA_HEADS * LANES).astype(BF16)
    wqr = jnp.pad(wq_up[:, :, QK_NOPE_DIM:], ((0, 0), (0, 0), (0, LANES - QK_ROPE_DIM)))
    wqr = wqr.reshape(Q_LORA_RANK, MLA_HEADS * LANES).astype(BF16)
    wkv_up = w_kv_up.reshape(KV_LORA_RANK, MLA_HEADS, QK_NOPE_DIM + V_HEAD_DIM)
    wkn = wkv_up[:, :, :QK_NOPE_DIM].reshape(KV_LORA_RANK, MLA_HEADS * LANES).astype(BF16)
    wvt = wkv_up[:, :, QK_NOPE_DIM:].reshape(KV_LORA_RANK, ATTN_WIDTH).T.astype(BF16)
    sgu_bias = jnp.repeat(sgu_b.T, SGU_GROUP_DIM, axis=1)

    shift, scale, _ = _mod_specs(k, 2)

    def rows(w):
        return pl.BlockSpec((1, tm, w), lambda b, i: (b, i, 0))

    return pl.pallas_call(
        functools.partial(_front_body, tm=tm),
        grid=(BATCH, SEQ // tm),
        in_specs=[rows(D_MODEL), shift, scale]
                 + [_const_spec((D_MODEL, w)) for w in in_widths]
                 + [rows(LANES), rows(LANES), rows(LANES),
                    _const_spec((1, Q_LORA_RANK)), _const_spec((1, KV_LORA_RANK)),
                    _const_spec((Q_LORA_RANK, MLA_HEADS * LANES)),
                    _const_spec((Q_LORA_RANK, MLA_HEADS * LANES)),
                    _const_spec((KV_LORA_RANK, MLA_HEADS * LANES)),
                    _const_spec((ATTN_WIDTH, KV_LORA_RANK)),
                    _const_spec((1, SGU_WIDTH)), _const_spec((1, SGU_WIDTH)),
                    _const_spec((SGU_GROUPS, CHUNK, CHUNK)), _const_spec((CHUNK, SGU_WIDTH))],
        out_specs=(rows(QK_WIDTH), rows(QK_WIDTH),
                   pl.BlockSpec((1, ATTN_WIDTH, tm), lambda b, i: (b, 0, i)),
                   rows(SGU_WIDTH)),
        out_shape=(jax.ShapeDtypeStruct((BATCH, SEQ, QK_WIDTH), BF16),
                   jax.ShapeDtypeStruct((BATCH, SEQ, QK_WIDTH), BF16),
                   jax.ShapeDtypeStruct((BATCH, ATTN_WIDTH, SEQ), BF16),
                   jax.ShapeDtypeStruct((BATCH, SEQ, SGU_WIDTH), BF16)),
        compiler_params=_params(("parallel", "parallel")),
        name="front_ab",
    )(x, mod, mod, wq, wkv, wkr, wu, wv, *tables,
      q_norm_g.reshape(1, -1), kv_norm_g.reshape(1, -1), wqn, wqr, wkn, wvt,
      sgu_norm_g.reshape(1, -1), sgu_norm_b.reshape(1, -1), sgu_w, sgu_bias)


def _qkv_compute(cq, ckv, kr, cos_t, sin_hi, sin_lo, qg_ref, kvg_ref,
                 wqn_ref, wqr_ref, wkn_ref, wvt_ref, q_ref, k_ref, vt_ref):
    cqn = _rms(cq, qg_ref[...]).astype(BF16)
    qn = _dot(cqn, wqn_ref[...])
    qr = _dot(cqn, wqr_ref[...])
    ckvn = _rms(ckv, kvg_ref[...]).astype(BF16)
    kn = _dot(ckvn, wkn_ref[...])
    vt_ref[0] = lax.dot_general(wvt_ref[...], ckvn, (((1,), (1,)), ((), ())),
                                preferred_element_type=F32).astype(BF16)
    k_rope = _rope(kr, cos_t, sin_hi, sin_lo).astype(BF16)
    for h in range(MLA_HEADS):
        nope = slice(h * LANES, (h + 1) * LANES)
        lo = h * HEAD_LANES
        q_ref[0, :, lo:lo + LANES] = qn[:, nope].astype(BF16)
        q_ref[0, :, lo + LANES:lo + HEAD_LANES] = _rope(qr[:, nope], cos_t, sin_hi, sin_lo).astype(BF16)
        k_ref[0, :, lo:lo + LANES] = kn[:, nope].astype(BF16)
        k_ref[0, :, lo + LANES:lo + HEAD_LANES] = k_rope


def _attn_body(qi_ref, kj_ref, q_ref, k_ref, vt_ref, o_ref, m_sc, l_sc, acc_sc, *, tq):
    p = pl.program_id(1)
    i = qi_ref[p]
    j = kj_ref[p]

    @pl.when(j == 0)
    def _():
        m_sc[...] = jnp.full(m_sc.shape, -jnp.inf, F32)
        l_sc[...] = jnp.zeros(l_sc.shape, F32)
        acc_sc[...] = jnp.zeros(acc_sc.shape, F32)

    def accumulate(diagonal):
        if diagonal:
            key = lax.broadcasted_iota(jnp.int32, (tq, tq), 0)
            qry = lax.broadcasted_iota(jnp.int32, (tq, tq), 1)
            visible = key <= qry
        for h in range(MLA_HEADS):
            qk = slice(h * HEAD_LANES, (h + 1) * HEAD_LANES)
            hv = slice(h * V_HEAD_DIM, (h + 1) * V_HEAD_DIM)
            s = lax.dot_general(k_ref[0, :, qk], q_ref[0, :, qk], (((1,), (1,)), ((), ())),
                                preferred_element_type=F32) * (SM_SCALE * LOG2_E)
            if diagonal:
                s = jnp.where(visible, s, -jnp.inf)
            m_prev = m_sc[h]
            m_new = jnp.maximum(m_prev, jnp.max(s, axis=0, keepdims=True))
            alpha = jnp.exp2(m_prev - m_new)
            pr = jnp.exp2(s - m_new)
            l_sc[h] = alpha * l_sc[h] + jnp.sum(pr, axis=0, keepdims=True)
            acc_sc[hv, :] = alpha * acc_sc[hv, :] + _dot(vt_ref[0, hv, :], pr.astype(BF16))
            m_sc[h] = m_new

    @pl.when(j < i)
    def _():
        accumulate(False)

    @pl.when(j == i)
    def _():
        accumulate(True)
        for h in range(MLA_HEADS):
            hv = slice(h * V_HEAD_DIM, (h + 1) * V_HEAD_DIM)
            o_ref[0, :, hv] = (acc_sc[hv, :] / l_sc[h]).T.astype(BF16)


def _attention(q, k, vt):
    tq = 512
    nb = SEQ // tq
    pairs = [(i, j) for i in range(nb) for j in range(i + 1)]
    qi = jnp.asarray(np.array([p[0] for p in pairs], np.int32))
    kj = jnp.asarray(np.array([p[1] for p in pairs], np.int32))
    grid_spec = pltpu.PrefetchScalarGridSpec(
        num_scalar_prefetch=2,
        grid=(BATCH, len(pairs)),
        in_specs=[pl.BlockSpec((1, tq, QK_WIDTH), lambda b, p, qi, kj: (b, qi[p], 0)),
                  pl.BlockSpec((1, tq, QK_WIDTH), lambda b, p, qi, kj: (b, kj[p], 0)),
                  pl.BlockSpec((1, ATTN_WIDTH, tq), lambda b, p, qi, kj: (b, 0, kj[p]))],
        out_specs=pl.BlockSpec((1, tq, ATTN_WIDTH), lambda b, p, qi, kj: (b, qi[p], 0)),
        scratch_shapes=[pltpu.VMEM((MLA_HEADS, 1, tq), F32),
                        pltpu.VMEM((MLA_HEADS, 1, tq), F32),
                        pltpu.VMEM((ATTN_WIDTH, tq), F32)],
    )
    return pl.pallas_call(
        functools.partial(_attn_body, tq=tq),
        grid_spec=grid_spec,
        out_shape=jax.ShapeDtypeStruct((BATCH, SEQ, ATTN_WIDTH), BF16),
        compiler_params=_params(("parallel", "arbitrary")),
        name="mla_attention",
    )(qi, kj, q, k, vt)


def _sgu_compute(zu, zv, g_ref, b_ref, w_ref, bias_ref, o_ref, tm):
    row = lax.broadcasted_iota(jnp.int32, (CHUNK, CHUNK), 0)
    col = lax.broadcasted_iota(jnp.int32, (CHUNK, CHUNK), 1)
    for g in range(SGU_GROUPS):
        gs = slice(g * SGU_GROUP_DIM, (g + 1) * SGU_GROUP_DIM)
        v = jax.nn.gelu(zv[:, gs])
        mu = jnp.mean(v, axis=-1, keepdims=True)
        vc = v - mu
        var = jnp.mean(vc * vc, axis=-1, keepdims=True)
        vn = (vc * lax.rsqrt(var + LN_EPS) * g_ref[:, gs] + b_ref[:, gs]).astype(BF16)
        w = jnp.where(col <= row, w_ref[g], 0.0).astype(BF16)
        for n in range(tm // CHUNK):
            rs = slice(n * CHUNK, (n + 1) * CHUNK)
            s = _dot(w, vn[rs]) + bias_ref[:, gs]
            o_ref[0, rs, gs] = (jax.nn.gelu(zu[rs, gs]) * s).astype(BF16)


def _outproj_body(a_ref, s_ref, x_ref, gate_ref, wa_ref, ws_ref, lg_ref, lb_ref, o_ref):
    y = _dot(a_ref[0], wa_ref[...]) + _dot(s_ref[0], ws_ref[...])
    o_ref[0] = _deepnorm(x_ref[0], y, gate_ref[0], lg_ref[...], lb_ref[...])


def _outproj(attn, sgu, x, mod, k, w_out, ln_g, ln_b):
    tm = 512
    wa = w_out[:ATTN_WIDTH].astype(BF16)
    ws = w_out[ATTN_WIDTH:].astype(BF16)
    _, _, gate = _mod_specs(k, 2)

    def rows(w):
        return pl.BlockSpec((1, tm, w), lambda b, i: (b, i, 0))

    return pl.pallas_call(
        _outproj_body,
        grid=(BATCH, SEQ // tm),
        in_specs=[rows(ATTN_WIDTH), rows(SGU_WIDTH), rows(D_MODEL), gate,
                  _const_spec((ATTN_WIDTH, D_MODEL)), _const_spec((SGU_WIDTH, D_MODEL)),
                  _const_spec((1, D_MODEL)), _const_spec((1, D_MODEL))],
        out_specs=rows(D_MODEL),
        out_shape=jax.ShapeDtypeStruct((BATCH, SEQ, D_MODEL), F32),
        compiler_params=_params(("parallel", "parallel")),
        name="outproj_ab",
    )(attn, sgu, x, mod, wa, ws, ln_g.reshape(1, -1), ln_b.reshape(1, -1))


def _ffn_up_body(x_ref, sh_ref, sc_ref, wg_ref, wu_ref, h_ref, hm_sc):
    @pl.when(pl.program_id(2) == 0)
    def _():
        hm_sc[...] = _modulate(x_ref[0], sh_ref[0], sc_ref[0]).astype(BF16)

    hm = hm_sc[...]
    g = _dot(hm, wg_ref[0].astype(BF16))
    u = _dot(hm, wu_ref[0].astype(BF16))
    h_ref[0] = (jax.nn.silu(g) * u).astype(BF16)


def _ffn_down_body(h_ref, x_ref, gate_ref, wd_ref, lg_ref, lb_ref, o_ref, y_sc, *, n_n):
    n = pl.program_id(2)
    y_sc[n] = _dot(h_ref[0], wd_ref[...])

    @pl.when(n == n_n - 1)
    def _():
        y = jnp.concatenate([y_sc[c] for c in range(n_n)], axis=-1)
        o_ref[0] = _deepnorm(x_ref[0], y, gate_ref[0], lg_ref[...], lb_ref[...])


def _ffn(x, mod, k, layer, w_gate, w_up, w_down, ln_g, ln_b):
    tm, tf = 2048, 256
    shift, scale, gate = _mod_specs(k, 3)
    h = pl.pallas_call(
        _ffn_up_body,
        grid=(BATCH, SEQ // tm, D_FF // tf),
        in_specs=[pl.BlockSpec((1, tm, D_MODEL), lambda b, i, f: (b, i, 0), pipeline_mode=pl.Buffered(1)),
                  shift, scale,
                  pl.BlockSpec((1, D_MODEL, tf), lambda b, i, f: (layer, 0, f)),
                  pl.BlockSpec((1, D_MODEL, tf), lambda b, i, f: (layer, 0, f))],
        out_specs=pl.BlockSpec((1, tm, tf), lambda b, i, f: (b, i, f)),
        out_shape=jax.ShapeDtypeStruct((BATCH, SEQ, D_FF), BF16),
        scratch_shapes=[pltpu.VMEM((tm, D_MODEL), BF16)],
        compiler_params=_params(("parallel", "parallel", "arbitrary")),
        name="ffn_up",
    )(x, mod, mod, w_gate, w_up)

    tm, tn = 512, 512
    n_n = D_MODEL // tn
    rows = pl.BlockSpec((1, tm, D_MODEL), lambda b, i, n: (b, i, 0))
    vec = pl.BlockSpec((1, D_MODEL), lambda b, i, n: (0, 0))
    return pl.pallas_call(
        functools.partial(_ffn_down_body, n_n=n_n),
        grid=(BATCH, SEQ // tm, n_n),
        in_specs=[pl.BlockSpec((1, tm, D_FF), lambda b, i, n: (b, i, 0)),
                  rows,
                  gate,
                  pl.BlockSpec((D_FF, tn), lambda b, i, n: (0, n)),
                  vec, vec],
        out_specs=rows,
        out_shape=jax.ShapeDtypeStruct((BATCH, SEQ, D_MODEL), F32),
        scratch_shapes=[pltpu.VMEM((n_n, tm, tn), F32)],
        compiler_params=_params(("parallel", "parallel", "arbitrary")),
        name="ffn_down",
    )(h, x, mod, w_down.astype(BF16), ln_g.reshape(1, -1), ln_b.reshape(1, -1))


def _pool_body(x_ref, halo_ref, sh_ref, sc_ref, gate_ref, pw_ref, ps_ref, wo_ref, lg_ref, lb_ref,
               o_ref, *, tm):
    i = pl.program_id(1)
    x = x_ref[0]
    hm = _modulate(x, sh_ref[0], sc_ref[0])
    halo = jnp.where(i > 0, _modulate(halo_ref[0], sh_ref[0], sc_ref[0]), 0.0)
    ext = jnp.concatenate([halo, hm], axis=0)
    ext_hi = ext.astype(BF16)
    ext_lo = (ext - ext_hi.astype(F32)).astype(BF16)
    t_loc = lax.broadcasted_iota(jnp.int32, (tm, tm + POOL_HALO), 0)
    s_loc = lax.broadcasted_iota(jnp.int32, (tm, tm + POOL_HALO), 1)
    lag = t_loc + POOL_HALO - s_loc
    t_glob = i * tm + lax.broadcasted_iota(jnp.int32, (tm, 1), 0)
    outs = []
    for gi, w in enumerate(POOL_WINDOWS):
        cs = slice(gi * POOL_GROUP_DIM, (gi + 1) * POOL_GROUP_DIM)
        band = jnp.where(lag >= 0, jnp.where(lag < w, 1.0, 0.0), 0.0).astype(BF16)
        wsum = _dot(band, ext_hi[:, cs]) + _dot(band, ext_lo[:, cs])
        count = jnp.minimum(t_glob + 1, w).astype(F32)
        d = (wsum / count - hm[:, cs]).astype(BF16)
        outs.append(_dot(d, pw_ref[gi]))
    pooled = (jnp.concatenate(outs, axis=-1) * ps_ref[...]).astype(BF16)
    y = _dot(pooled, wo_ref[...])
    o_ref[0] = _deepnorm(x, y, gate_ref[0], lg_ref[...], lb_ref[...])


def _pool_mixer(x, mod, k, pool_w, pool_scale, w_out_c, ln_g, ln_b):
    tm = 512
    per = tm // POOL_HALO
    shift, scale, gate = _mod_specs(k, 2)
    rows = pl.BlockSpec((1, tm, D_MODEL), lambda b, i: (b, i, 0))
    halo = pl.BlockSpec((1, POOL_HALO, D_MODEL), lambda b, i: (b, jnp.maximum(i * per - 1, 0), 0))
    return pl.pallas_call(
        functools.partial(_pool_body, tm=tm),
        grid=(BATCH, SEQ // tm),
        in_specs=[rows, halo, shift, scale, gate,
                  _const_spec((len(POOL_WINDOWS), POOL_GROUP_DIM, POOL_GROUP_DIM)),
                  _const_spec((1, D_MODEL)), _const_spec((D_MODEL, D_MODEL)),
                  _const_spec((1, D_MODEL)), _const_spec((1, D_MODEL))],
        out_specs=rows,
        out_shape=jax.ShapeDtypeStruct((BATCH, SEQ, D_MODEL), F32),
        compiler_params=_params(("parallel", "parallel")),
        name="pool_mixer",
    )(x, x, mod, mod, mod, pool_w.astype(BF16), pool_scale.reshape(1, -1),
      w_out_c.astype(BF16), ln_g.reshape(1, -1), ln_b.reshape(1, -1))


def _router_body(x_ref, sh_ref, sc_ref, wr_ref, g_ref, hm_ref):
    hm = _modulate(x_ref[0], sh_ref[0], sc_ref[0])
    hm_ref[0] = hm
    hm_hi = hm.astype(BF16)
    hm_lo = (hm - hm_hi.astype(F32)).astype(BF16)
    w = wr_ref[...]
    w_hi = w.astype(BF16)
    w_lo = (w - w_hi.astype(F32)).astype(BF16)
    logits = _dot(hm_hi, w_hi) + (_dot(hm_lo, w_hi) + _dot(hm_hi, w_lo))
    lane = lax.broadcasted_iota(jnp.int32, logits.shape, 1).astype(F32)
    logits = jnp.where(lane < N_EXPERTS, logits, -jnp.inf)
    v1 = jnp.max(logits, axis=-1, keepdims=True)
    i1 = jnp.min(jnp.where(logits == v1, lane, float(LANES)), axis=-1, keepdims=True)
    rest = jnp.where(lane == i1, -jnp.inf, logits)
    v2 = jnp.max(rest, axis=-1, keepdims=True)
    i2 = jnp.min(jnp.where(rest == v2, lane, float(LANES)), axis=-1, keepdims=True)
    e2 = jnp.exp(v2 - v1)
    den = 1.0 + e2
    g_ref[0] = (jnp.where(lane == 0.0, i1, 0.0) + jnp.where(lane == 1.0, i2, 0.0)
                + jnp.where(lane == 2.0, 1.0 / den, 0.0) + jnp.where(lane == 3.0, e2 / den, 0.0))


def _router(x, mod, k, w_router):
    tm = 512
    shift, scale, _ = _mod_specs(k, 2)
    wr = jnp.pad(w_router, ((0, 0), (0, LANES - N_EXPERTS)))
    rows = pl.BlockSpec((1, tm, D_MODEL), lambda b, i: (b, i, 0))
    return pl.pallas_call(
        _router_body,
        grid=(BATCH, SEQ // tm),
        in_specs=[rows, shift, scale, _const_spec((D_MODEL, LANES))],
        out_specs=(pl.BlockSpec((1, tm, LANES), lambda b, i: (b, i, 0)), rows),
        out_shape=(jax.ShapeDtypeStruct((BATCH, SEQ, LANES), F32),
                   jax.ShapeDtypeStruct((BATCH, SEQ, D_MODEL), F32)),
        compiler_params=_params(("parallel", "parallel")),
        name="router",
    )(x, mod, mod, wr)


N_TOKENS = BATCH * SEQ
N_ASSIGN = 2 * N_TOKENS
MOE_BLOCK = 1024
MOE_SUB = 256
MOE_TILE = N_ASSIGN // N_EXPERTS + MOE_SUB
MOE_TILES = -(-(N_ASSIGN + N_EXPERTS * (MOE_TILE - 1)) // MOE_TILE)
MOE_ROWS = MOE_TILES * MOE_TILE
GATHER_ROWS = MOE_TILE // 3
assert MOE_TILE % GATHER_ROWS == 0 and GATHER_ROWS % 8 == 0


def _route_plan(routing):
    e_flat = routing.reshape(N_TOKENS, LANES)[:, :2].astype(jnp.int32).T.reshape(N_ASSIGN)
    onehot = (e_flat[:, None] == jnp.arange(N_EXPERTS, dtype=jnp.int32)[None, :]).astype(jnp.int32)
    csum = jnp.cumsum(onehot, axis=0)
    counts = csum[-1]
    rank = jnp.sum(csum * onehot, axis=1) - 1
    padded = (counts + MOE_TILE - 1) // MOE_TILE * MOE_TILE
    group_end = jnp.cumsum(padded)
    group_start = group_end - padded
    pos = (jnp.sum(onehot * group_start[None, :], axis=1) + rank).astype(jnp.int32)
    tok = jnp.tile(jnp.arange(N_TOKENS, dtype=jnp.int32), 2)
    src = jnp.zeros((MOE_ROWS,), jnp.int32).at[pos].set(tok, unique_indices=True)
    tile_start = jnp.arange(MOE_TILES, dtype=jnp.int32) * MOE_TILE
    n_used = group_end[-1] // MOE_TILE
    tile_expert = jnp.sum((tile_start[:, None] >= group_end[None, :]).astype(jnp.int32), axis=1)
    last_expert = jnp.max(jnp.where(counts > 0, jnp.arange(N_EXPERTS, dtype=jnp.int32), 0))
    tile_expert = jnp.minimum(tile_expert, last_expert)
    tile_rows = jnp.clip(counts[tile_expert] - (tile_start - group_start[tile_expert]), 0, MOE_TILE)
    tile_rows = jnp.where(jnp.arange(MOE_TILES) < n_used, tile_rows, 0).astype(jnp.int32)
    return pos, src, tile_expert.astype(jnp.int32), tile_rows, n_used.reshape(1).astype(jnp.int32)


GATHER_UNROLL = 8


def _gather_body(src_ref, rows_ref, hm_hbm, o_ref, buf, sem, *, tg):
    i = pl.program_id(0)
    base = i * tg
    per = MOE_TILE // tg
    n_valid = jnp.clip(rows_ref[i // per] - (i % per) * tg, 0, tg)

    @pl.when(n_valid < tg)
    def _():
        buf[...] = jnp.zeros(buf.shape, F32)

    def issue_row(r):
        t = src_ref[base + r]
        pltpu.make_async_copy(hm_hbm.at[pl.ds(t, 1)], buf.at[pl.ds(r, 1)], sem).start()

    def issue_group(g, carry):
        for u in range(GATHER_UNROLL):
            issue_row(g * GATHER_UNROLL + u)
        return carry

    def issue_one(r, carry):
        issue_row(r)
        return carry

    n_groups = n_valid // GATHER_UNROLL
    n_grouped = pl.multiple_of(n_groups * GATHER_UNROLL, GATHER_UNROLL)
    lax.fori_loop(0, n_groups, issue_group, 0)
    lax.fori_loop(n_grouped, n_valid, issue_one, 0)

    @pl.when(n_groups > 0)
    def _():
        pltpu.make_async_copy(hm_hbm.at[pl.ds(0, n_grouped)], buf.at[pl.ds(0, n_grouped)], sem).wait()

    def wait_one(r, carry):
        pltpu.make_async_copy(hm_hbm.at[pl.ds(0, 1)], buf.at[pl.ds(r, 1)], sem).wait()
        return carry

    lax.fori_loop(n_grouped, n_valid, wait_one, 0)

    o_ref[...] = buf[...].astype(BF16)


def _gather_rows(src, tile_rows, hm):
    tg = GATHER_ROWS
    grid_spec = pltpu.PrefetchScalarGridSpec(
        num_scalar_prefetch=2,
        grid=(MOE_ROWS // tg,),
        in_specs=[pl.BlockSpec(memory_space=pl.ANY)],
        out_specs=pl.BlockSpec((tg, D_MODEL), lambda i, src, tr: (i, 0)),
        scratch_shapes=[pltpu.VMEM((tg, D_MODEL), F32), pltpu.SemaphoreType.DMA(())],
    )
    return pl.pallas_call(
        functools.partial(_gather_body, tg=tg),
        grid_spec=grid_spec,
        out_shape=jax.ShapeDtypeStruct((MOE_ROWS, D_MODEL), BF16),
        compiler_params=_params(("arbitrary",)),
        name="moe_gather",
    )(src, tile_rows, hm.reshape(N_TOKENS, D_MODEL))


def _for_valid_blocks(rows, tm, block_fn, empty_fn):
    for b0 in range(0, tm, MOE_BLOCK):
        b1 = min(b0 + MOE_BLOCK, tm)
        whole = b1 if b1 - b0 == MOE_BLOCK else tm + 1
        if whole <= tm:
            pl.when(rows >= whole)(functools.partial(block_fn, slice(b0, b1)))

        for r0 in range(b0, b1, MOE_SUB):
            rs = slice(r0, r0 + MOE_SUB)

            @pl.when(jnp.logical_and(rows < whole, r0 < rows))
            def _():
                block_fn(rs)

            @pl.when(r0 >= rows)
            def _():
                empty_fn(rs)


def _gup_body(te_ref, rows_ref, nu_ref, xs_ref, wg_ref, wu_ref, h_ref, wgb, wub, *, tm):
    rows = rows_ref[pl.program_id(0)]

    @pl.when(rows > 0)
    def _():
        wgb[...] = wg_ref[0].astype(BF16)
        wub[...] = wu_ref[0].astype(BF16)

    def block(rs):
        xsb = xs_ref[rs, :]
        h_ref[rs, :] = (jax.nn.silu(_dot(xsb, wgb[...])) * _dot(xsb, wub[...])).astype(BF16)

    def empty(rs):
        h_ref[rs, :] = jnp.zeros((rs.stop - rs.start, h_ref.shape[1]), BF16)

    _for_valid_blocks(rows, tm, block, empty)


def _gdown_body(te_ref, rows_ref, nu_ref, h_ref, wd_ref, o_ref, wdb, *, tm):
    rows = rows_ref[pl.program_id(0)]

    @pl.when(rows > 0)
    def _():
        wdb[...] = wd_ref[0].astype(BF16)

    def block(rs):
        o_ref[rs, :] = _dot(h_ref[rs, :], wdb[...])

    def empty(rs):
        o_ref[rs, :] = jnp.zeros((rs.stop - rs.start, o_ref.shape[1]), F32)

    _for_valid_blocks(rows, tm, block, empty)


def _grouped_ffn(xs, tile_expert, tile_rows, n_used, layer, w_gate, w_up, w_down):
    tm = MOE_TILE
    first = layer * N_EXPERTS
    n_all = w_gate.shape[0] * N_EXPERTS

    def tile(i, nu):
        return jnp.minimum(i, nu[0] - 1)

    def chunk(i, c, nu, n_chunks):
        return jnp.where(i < nu[0], c, n_chunks - 1)

    tf = 256
    n_f = D_FF_EXPERT // tf
    w_up_spec = pl.BlockSpec((1, D_MODEL, tf),
                             lambda i, f, te, tr, nu: (first + te[i], 0, chunk(i, f, nu, n_f)))
    up_spec = pltpu.PrefetchScalarGridSpec(
        num_scalar_prefetch=3,
        grid=(MOE_TILES, n_f),
        in_specs=[pl.BlockSpec((tm, D_MODEL), lambda i, f, te, tr, nu: (tile(i, nu), 0)),
                  w_up_spec, w_up_spec],
        out_specs=pl.BlockSpec((tm, tf), lambda i, f, te, tr, nu: (i, f)),
        scratch_shapes=[pltpu.VMEM((D_MODEL, tf), BF16), pltpu.VMEM((D_MODEL, tf), BF16)],
    )
    h = pl.pallas_call(
        functools.partial(_gup_body, tm=tm),
        grid_spec=up_spec,
        out_shape=jax.ShapeDtypeStruct((MOE_ROWS, D_FF_EXPERT), BF16),
        compiler_params=_params(("arbitrary", "arbitrary")),
        name="moe_up",
    )(tile_expert, tile_rows, n_used, xs,
      w_gate.reshape(n_all, D_MODEL, D_FF_EXPERT), w_up.reshape(n_all, D_MODEL, D_FF_EXPERT))

    tn = 512
    n_n = D_MODEL // tn
    down_spec = pltpu.PrefetchScalarGridSpec(
        num_scalar_prefetch=3,
        grid=(MOE_TILES, n_n),
        in_specs=[pl.BlockSpec((tm, D_FF_EXPERT), lambda i, n, te, tr, nu: (tile(i, nu), 0)),
                  pl.BlockSpec((1, D_FF_EXPERT, tn),
                               lambda i, n, te, tr, nu: (first + te[i], 0, chunk(i, n, nu, n_n)))],
        out_specs=pl.BlockSpec((tm, tn), lambda i, n, te, tr, nu: (i, n)),
        scratch_shapes=[pltpu.VMEM((D_FF_EXPERT, tn), BF16)],
    )
    return pl.pallas_call(
        functools.partial(_gdown_body, tm=tm),
        grid_spec=down_spec,
        out_shape=jax.ShapeDtypeStruct((MOE_ROWS, D_MODEL), F32),
        compiler_params=_params(("arbitrary", "arbitrary")),
        name="moe_down",
    )(tile_expert, tile_rows, n_used, h, w_down.reshape(n_all, D_FF_EXPERT, D_MODEL))


def _combine_body(pos_ref, ys_hbm, r_ref, x_ref, gate_ref, lg_ref, lb_ref, o_ref, buf, sem, *, tc):
    base = (pl.program_id(0) * (SEQ // tc) + pl.program_id(1)) * tc

    def issue(r, carry):
        for k in range(2):
            p = pos_ref[k * N_TOKENS + base + r]
            pltpu.make_async_copy(ys_hbm.at[pl.ds(p, 1)], buf.at[k, pl.ds(r, 1)], sem.at[k]).start()
        return carry

    lax.fori_loop(0, tc, issue, 0, unroll=4)
    for k in range(2):
        pltpu.make_async_copy(ys_hbm.at[pl.ds(0, tc)], buf.at[k], sem.at[k]).wait()
    routing = r_ref[0]
    y = routing[:, 2:3] * buf[0] + routing[:, 3:4] * buf[1]
    o_ref[0] = _deepnorm(x_ref[0], y, gate_ref[0], lg_ref[...], lb_ref[...])


def _combine(pos, ys, routing, x, mod, k, ln_g, ln_b):
    tc = 512
    rows = pl.BlockSpec((1, tc, D_MODEL), lambda b, i, pos: (b, i, 0))
    vec = pl.BlockSpec((1, D_MODEL), lambda b, i, pos: (0, 0))
    grid_spec = pltpu.PrefetchScalarGridSpec(
        num_scalar_prefetch=1,
        grid=(BATCH, SEQ // tc),
        in_specs=[pl.BlockSpec(memory_space=pl.ANY),
                  pl.BlockSpec((1, tc, LANES), lambda b, i, pos: (b, i, 0)),
                  rows,
                  pl.BlockSpec((1, 1, D_MODEL), lambda b, i, pos: (k * 8 + b, 0, 2)),
                  vec, vec],
        out_specs=rows,
        scratch_shapes=[pltpu.VMEM((2, tc, D_MODEL), F32), pltpu.SemaphoreType.DMA((2,))],
    )
    return pl.pallas_call(
        functools.partial(_combine_body, tc=tc),
        grid_spec=grid_spec,
        out_shape=jax.ShapeDtypeStruct((BATCH, SEQ, D_MODEL), F32),
        compiler_params=_params(("arbitrary", "arbitrary")),
        name="moe_combine",
    )(pos, ys, routing, x, mod, ln_g.reshape(1, -1), ln_b.reshape(1, -1))


def _moe(x, mod, k, w_router, layer, w_gate, w_up, w_down, ln_g, ln_b):
    routing, hm = _router(x, mod, k, w_router)
    pos, src, tile_expert, tile_rows, n_used = _route_plan(routing)
    xs = _gather_rows(src, tile_rows, hm)
    ys = _grouped_ffn(xs, tile_expert, tile_rows, n_used, layer, w_gate, w_up, w_down)
    return _combine(pos, ys, routing, x, mod, k, ln_g, ln_b)


def kernel(x, c, positions, ada_w, ada_b, ln_g, ln_b, w_in_ab, q_norm_g, w_q_up, kv_norm_g, w_kv_up, sgu_norm_g, sgu_norm_b, sgu_w, sgu_b, w_out_ab, ffn_w_gate, ffn_w_up, ffn_w_down, pool_w, pool_scale, w_out_c, router_w, moe_w_gate, moe_w_up, moe_w_down):
    mod = _ada_modulation(c, ada_w, ada_b)
    tables = _rope_tables(positions)
    for l in range(DEPTH):
        j = l // 2
        k_tok, k_ch = 2 * l, 2 * l + 1
        if l % 2 == 0:
            q, k, vt, sgu = _front(x, mod, k_tok, tables, w_in_ab[j], q_norm_g[j], w_q_up[j],
                                   kv_norm_g[j], w_kv_up[j],
                                   sgu_norm_g[j], sgu_norm_b[j], sgu_w[j], sgu_b[j])
            attn = _attention(q, k, vt)
            x = _outproj(attn, sgu, x, mod, k_tok, w_out_ab[j], ln_g[l, 0], ln_b[l, 0])
            x = _ffn(x, mod, k_ch, j, ffn_w_gate, ffn_w_up, ffn_w_down[j], ln_g[l, 1], ln_b[l, 1])
        else:
            x = _pool_mixer(x, mod, k_tok, pool_w[j], pool_scale[j], w_out_c[j], ln_g[l, 0], ln_b[l, 0])
            x = _moe(x, mod, k_ch, router_w[j], j, moe_w_gate, moe_w_up, moe_w_down,
                     ln_g[l, 1], ln_b[l, 1])
    return x
```

```python
import functools

import jax
import jax.numpy as jnp
import numpy as np
from jax import lax
from jax.experimental import pallas as pl
from jax.experimental.pallas import tpu as pltpu

F32 = jnp.float32
BF16 = jnp.bfloat16

D_MODEL = 2048
BATCH = 2
SEQ = 4096
DEPTH = 4
MLA_HEADS = 8
QK_NOPE_DIM = 128
QK_ROPE_DIM = 64
V_HEAD_DIM = 128
Q_LORA_RANK = 768
KV_LORA_RANK = 512
ROPE_THETA = 10000.0
SGU_GROUPS = 8
SGU_GROUP_DIM = 128
SGU_WIDTH = SGU_GROUPS * SGU_GROUP_DIM
CHUNK = 128
ATTN_WIDTH = MLA_HEADS * V_HEAD_DIM
POOL_WINDOWS = (2, 4, 8, 16)
POOL_GROUP_DIM = D_MODEL // 4
D_FF = 5632
N_EXPERTS = 8
D_FF_EXPERT = 2816
DEEPNORM_ALPHA = (2 * DEPTH) ** 0.25
LN_EPS = 1e-5
RMS_EPS = 1e-6
SM_SCALE = (QK_NOPE_DIM + QK_ROPE_DIM) ** -0.5
LOG2_E = 1.4426950408889634

LANES = 128
HEAD_LANES = 2 * LANES
QK_WIDTH = MLA_HEADS * HEAD_LANES
VMEM_LIMIT = 60 * 1024 * 1024
POOL_HALO = 128


def _params(semantics):
    return pltpu.CompilerParams(dimension_semantics=semantics, vmem_limit_bytes=VMEM_LIMIT)


def _const_spec(shape):
    nd = len(shape)
    return pl.BlockSpec(shape, lambda *_: (0,) * nd, pipeline_mode=pl.Buffered(1))


def _dot(a, b):
    return jnp.dot(a, b, preferred_element_type=F32)


def _modulate(x, shift, scale):
    return x * (1.0 + scale) + shift


def _deepnorm(x, y, gate, g, b):
    r = DEEPNORM_ALPHA * x + (1.0 + gate) * y
    mu = jnp.mean(r, axis=-1, keepdims=True)
    rc = r - mu
    var = jnp.mean(rc * rc, axis=-1, keepdims=True)
    return rc * lax.rsqrt(var + LN_EPS) * g + b


def _rms(x, g):
    return x * lax.rsqrt(jnp.mean(x * x, axis=-1, keepdims=True) + RMS_EPS) * g


def _rope(r, cos_t, sin_hi, sin_lo):
    return (r * cos_t + pltpu.roll(r, QK_ROPE_DIM // 2, 1) * sin_hi
            + pltpu.roll(r, LANES - QK_ROPE_DIM // 2, 1) * sin_lo)


def _mod_specs(k, n_grid):
    def spec(part):
        if n_grid == 2:
            return pl.BlockSpec((1, 1, D_MODEL), lambda b, i: (k * 8 + b, 0, part))
        if n_grid == 3:
            return pl.BlockSpec((1, 1, D_MODEL), lambda b, i, f: (k * 8 + b, 0, part))
        return pl.BlockSpec((1, 1, D_MODEL), lambda b, i, e, f: (k * 8 + b, 0, part))
    return spec(0), spec(1), spec(2)


def _ada_body(c_ref, w_ref, b_ref, o_ref):
    s = jax.nn.silu(c_ref[...]).astype(BF16)
    o_ref[0] = _dot(s, w_ref[0].astype(BF16)) + b_ref[0]


def _ada_modulation(c, ada_w, ada_b):
    n_mod = 2 * DEPTH
    tn = 1024
    c_pad = jnp.pad(c, ((0, 8 - BATCH), (0, 0)))
    out = pl.pallas_call(
        _ada_body,
        grid=(n_mod, 3 * D_MODEL // tn),
        in_specs=[
            pl.BlockSpec((8, D_MODEL), lambda k, n: (0, 0)),
            pl.BlockSpec((1, D_MODEL, tn), lambda k, n: (k, 0, n)),
            pl.BlockSpec((1, 1, tn), lambda k, n: (k, 0, n)),
        ],
        out_specs=pl.BlockSpec((1, 8, tn), lambda k, n: (k, 0, n)),
        out_shape=jax.ShapeDtypeStruct((n_mod, 8, 3 * D_MODEL), F32),
        compiler_params=_params(("parallel", "parallel")),
        name="ada_mod",
    )(c_pad, ada_w.reshape(n_mod, D_MODEL, 3 * D_MODEL), ada_b.reshape(n_mod, 1, 3 * D_MODEL))
    return out.reshape(n_mod * 8, 1, 3 * D_MODEL)


def _rope_table_body(pos_ref, inv_ref, cos_ref, hi_ref, lo_ref):
    ang = pos_ref[0].astype(F32) * inv_ref[...]
    lane = lax.broadcasted_iota(jnp.int32, ang.shape, 1)
    cos = jnp.cos(ang)
    sin = jnp.sin(ang)
    half = QK_ROPE_DIM // 2
    cos_ref[0] = jnp.where(lane < QK_ROPE_DIM, cos, 0.0)
    hi_ref[0] = jnp.where(lane < half, 0.0, jnp.where(lane < QK_ROPE_DIM, sin, 0.0))
    lo_ref[0] = jnp.where(lane < half, -sin, 0.0)


def _rope_tables(positions):
    tm = 512
    half = QK_ROPE_DIM // 2
    inv = ROPE_THETA ** (-jnp.arange(0, QK_ROPE_DIM, 2, dtype=F32) / QK_ROPE_DIM)
    inv_lanes = jnp.concatenate([inv, inv, jnp.zeros((LANES - 2 * half,), F32)]).reshape(1, LANES)
    tab = jax.ShapeDtypeStruct((BATCH, SEQ, LANES), F32)
    spec = pl.BlockSpec((1, tm, LANES), lambda b, i: (b, i, 0))
    return pl.pallas_call(
        _rope_table_body,
        grid=(BATCH, SEQ // tm),
        in_specs=[pl.BlockSpec((1, tm, 1), lambda b, i: (b, i, 0)),
                  pl.BlockSpec((1, LANES), lambda b, i: (0, 0))],
        out_specs=(spec, spec, spec),
        out_shape=(tab, tab, tab),
        compiler_params=_params(("parallel", "parallel")),
        name="rope_tables",
    )(positions.reshape(BATCH, SEQ, 1), inv_lanes)


def _front_body(x_ref, sh_ref, sc_ref, wq_ref, wkv_ref, wkr_ref, wu_ref, wv_ref,
                cos_ref, hi_ref, lo_ref, qg_ref, kvg_ref, wqn_ref, wqr_ref, wkn_ref, wvt_ref,
                sg_ref, sb_ref, sw_ref, sbias_ref,
                q_ref, k_ref, vt_ref, sgu_ref, *, tm):
    hm = _modulate(x_ref[0], sh_ref[0], sc_ref[0]).astype(BF16)
    _qkv_compute(_dot(hm, wq_ref[...]), _dot(hm, wkv_ref[...]), _dot(hm, wkr_ref[...]),
                 cos_ref[0], hi_ref[0], lo_ref[0], qg_ref, kvg_ref,
                 wqn_ref, wqr_ref, wkn_ref, wvt_ref, q_ref, k_ref, vt_ref)
    _sgu_compute(_dot(hm, wu_ref[...]), _dot(hm, wv_ref[...]),
                 sg_ref, sb_ref, sw_ref, sbias_ref, sgu_ref, tm)


def _front(x, mod, k, tables, w_in, q_norm_g, w_q_up, kv_norm_g, w_kv_up,
           sgu_norm_g, sgu_norm_b, sgu_w, sgu_b):
    tm = 512
    a, b, c_, d = (Q_LORA_RANK, Q_LORA_RANK + KV_LORA_RANK,
                   Q_LORA_RANK + KV_LORA_RANK + QK_ROPE_DIM,
                   Q_LORA_RANK + KV_LORA_RANK + QK_ROPE_DIM + SGU_WIDTH)
    wq = w_in[:, :a].astype(BF16)
    wkv = w_in[:, a:b].astype(BF16)
    wkr = jnp.pad(w_in[:, b:c_], ((0, 0), (0, LANES - QK_ROPE_DIM))).astype(BF16)
    wu = w_in[:, c_:d].astype(BF16)
    wv = w_in[:, d:].astype(BF16)
    in_widths = (Q_LORA_RANK, KV_LORA_RANK, LANES, SGU_WIDTH, SGU_WIDTH)

    wq_up = w_q_up.reshape(Q_LORA_RANK, MLA_HEADS, QK_NOPE_DIM + QK_ROPE_DIM)
    wqn = wq_up[:, :, :QK_NOPE_DIM].reshape(Q_LORA_RANK, MLA_HEADS * LANES).astype(BF16)
    wqr = jnp.pad(wq_up[:, :, QK_NOPE_DIM:], ((0, 0), (0, 0), (0, LANES - QK_ROPE_DIM)))
    wqr = wqr.reshape(Q_LORA_RANK, MLA_HEADS * LANES).astype(BF16)
    wkv_up = w_kv_up.reshape(KV_LORA_RANK, MLA_HEADS, QK_NOPE_DIM + V_HEAD_DIM)
    wkn = wkv_up[:, :, :QK_NOPE_DIM].reshape(KV_LORA_RANK, MLA_HEADS * LANES).astype(BF16)
    wvt = wkv_up[:, :, QK_NOPE_DIM:].reshape(KV_LORA_RANK, ATTN_WIDTH).T.astype(BF16)
    sgu_bias = jnp.repeat(sgu_b.T, SGU_GROUP_DIM, axis=1)

    shift, scale, _ = _mod_specs(k, 2)

    def rows(w):
        return pl.BlockSpec((1, tm, w), lambda b, i: (b, i, 0))

    return pl.pallas_call(
        functools.partial(_front_body, tm=tm),
        grid=(BATCH, SEQ // tm),
        in_specs=[rows(D_MODEL), shift, scale]
                 + [_const_spec((D_MODEL, w)) for w in in_widths]
                 + [rows(LANES), rows(LANES), rows(LANES),
                    _const_spec((1, Q_LORA_RANK)), _const_spec((1, KV_LORA_RANK)),
                    _const_spec((Q_LORA_RANK, MLA_HEADS * LANES)),
                    _const_spec((Q_LORA_RANK, MLA_HEADS * LANES)),
                    _const_spec((KV_LORA_RANK, MLA_HEADS * LANES)),
                    _const_spec((ATTN_WIDTH, KV_LORA_RANK)),
                    _const_spec((1, SGU_WIDTH)), _const_spec((1, SGU_WIDTH)),
                    _const_spec((SGU_GROUPS, CHUNK, CHUNK)), _const_spec((CHUNK, SGU_WIDTH))],
        out_specs=(rows(QK_WIDTH), rows(QK_WIDTH),
                   pl.BlockSpec((1, ATTN_WIDTH, tm), lambda b, i: (b, 0, i)),
                   rows(SGU_WIDTH)),
        out_shape=(jax.ShapeDtypeStruct((BATCH, SEQ, QK_WIDTH), BF16),
                   jax.ShapeDtypeStruct((BATCH, SEQ, QK_WIDTH), BF16),
                   jax.ShapeDtypeStruct((BATCH, ATTN_WIDTH, SEQ), BF16),
                   jax.ShapeDtypeStruct((BATCH, SEQ, SGU_WIDTH), BF16)),
        compiler_params=_params(("parallel", "parallel")),
        name="front_ab",
    )(x, mod, mod, wq, wkv, wkr, wu, wv, *tables,
      q_norm_g.reshape(1, -1), kv_norm_g.reshape(1, -1), wqn, wqr, wkn, wvt,
      sgu_norm_g.reshape(1, -1), sgu_norm_b.reshape(1, -1), sgu_w, sgu_bias)


def _qkv_compute(cq, ckv, kr, cos_t, sin_hi, sin_lo, qg_ref, kvg_ref,
                 wqn_ref, wqr_ref, wkn_ref, wvt_ref, q_ref, k_ref, vt_ref):
    cqn = _rms(cq, qg_ref[...]).astype(BF16)
    qn = _dot(cqn, wqn_ref[...])
    qr = _dot(cqn, wqr_ref[...])
    ckvn = _rms(ckv, kvg_ref[...]).astype(BF16)
    kn = _dot(ckvn, wkn_ref[...])
    vt_ref[0] = lax.dot_general(wvt_ref[...], ckvn, (((1,), (1,)), ((), ())),
                                preferred_element_type=F32).astype(BF16)
    k_rope = _rope(kr, cos_t, sin_hi, sin_lo).astype(BF16)
    for h in range(MLA_HEADS):
        nope = slice(h * LANES, (h + 1) * LANES)
        lo = h * HEAD_LANES
        q_ref[0, :, lo:lo + LANES] = qn[:, nope].astype(BF16)
        q_ref[0, :, lo + LANES:lo + HEAD_LANES] = _rope(qr[:, nope], cos_t, sin_hi, sin_lo).astype(BF16)
        k_ref[0, :, lo:lo + LANES] = kn[:, nope].astype(BF16)
        k_ref[0, :, lo + LANES:lo + HEAD_LANES] = k_rope


def _attn_body(qi_ref, kj_ref, q_ref, k_ref, vt_ref, o_ref, m_sc, l_sc, acc_sc, *, tq):
    p = pl.program_id(1)
    i = qi_ref[p]
    j = kj_ref[p]

    @pl.when(j == 0)
    def _():
        m_sc[...] = jnp.full(m_sc.shape, -jnp.inf, F32)
        l_sc[...] = jnp.zeros(l_sc.shape, F32)
        acc_sc[...] = jnp.zeros(acc_sc.shape, F32)

    def accumulate(diagonal):
        if diagonal:
            key = lax.broadcasted_iota(jnp.int32, (tq, tq), 0)
            qry = lax.broadcasted_iota(jnp.int32, (tq, tq), 1)
            visible = key <= qry
        for h in range(MLA_HEADS):
            qk = slice(h * HEAD_LANES, (h + 1) * HEAD_LANES)
            hv = slice(h * V_HEAD_DIM, (h + 1) * V_HEAD_DIM)
            s = lax.dot_general(k_ref[0, :, qk], q_ref[0, :, qk], (((1,), (1,)), ((), ())),
                                preferred_element_type=F32) * (SM_SCALE * LOG2_E)
            if diagonal:
                s = jnp.where(visible, s, -jnp.inf)
            m_prev = m_sc[h]
            m_new = jnp.maximum(m_prev, jnp.max(s, axis=0, keepdims=True))
            alpha = jnp.exp2(m_prev - m_new)
            pr = jnp.exp2(s - m_new)
            l_sc[h] = alpha * l_sc[h] + jnp.sum(pr, axis=0, keepdims=True)
            acc_sc[hv, :] = alpha * acc_sc[hv, :] + _dot(vt_ref[0, hv, :], pr.astype(BF16))
            m_sc[h] = m_new

    @pl.when(j < i)
    def _():
        accumulate(False)

    @pl.when(j == i)
    def _():
        accumulate(True)
        for h in range(MLA_HEADS):
            hv = slice(h * V_HEAD_DIM, (h + 1) * V_HEAD_DIM)
            o_ref[0, :, hv] = (acc_sc[hv, :] / l_sc[h]).T.astype(BF16)


def _attention(q, k, vt):
    tq = 512
    nb = SEQ // tq
    pairs = [(i, j) for i in range(nb) for j in range(i + 1)]
    qi = jnp.asarray(np.array([p[0] for p in pairs], np.int32))
    kj = jnp.asarray(np.array([p[1] for p in pairs], np.int32))
    grid_spec = pltpu.PrefetchScalarGridSpec(
        num_scalar_prefetch=2,
        grid=(BATCH, len(pairs)),
        in_specs=[pl.BlockSpec((1, tq, QK_WIDTH), lambda b, p, qi, kj: (b, qi[p], 0)),
                  pl.BlockSpec((1, tq, QK_WIDTH), lambda b, p, qi, kj: (b, kj[p], 0)),
                  pl.BlockSpec((1, ATTN_WIDTH, tq), lambda b, p, qi, kj: (b, 0, kj[p]))],
        out_specs=pl.BlockSpec((1, tq, ATTN_WIDTH), lambda b, p, qi, kj: (b, qi[p], 0)),
        scratch_shapes=[pltpu.VMEM((MLA_HEADS, 1, tq), F32),
                        pltpu.VMEM((MLA_HEADS, 1, tq), F32),
                        pltpu.VMEM((ATTN_WIDTH, tq), F32)],
    )
    return pl.pallas_call(
        functools.partial(_attn_body, tq=tq),
        grid_spec=grid_spec,
        out_shape=jax.ShapeDtypeStruct((BATCH, SEQ, ATTN_WIDTH), BF16),
        compiler_params=_params(("parallel", "arbitrary")),
        name="mla_attention",
    )(qi, kj, q, k, vt)


def _sgu_compute(zu, zv, g_ref, b_ref, w_ref, bias_ref, o_ref, tm):
    row = lax.broadcasted_iota(jnp.int32, (CHUNK, CHUNK), 0)
    col = lax.broadcasted_iota(jnp.int32, (CHUNK, CHUNK), 1)
    for g in range(SGU_GROUPS):
        gs = slice(g * SGU_GROUP_DIM, (g + 1) * SGU_GROUP_DIM)
        v = jax.nn.gelu(zv[:, gs])
        mu = jnp.mean(v, axis=-1, keepdims=True)
        vc = v - mu
        var = jnp.mean(vc * vc, axis=-1, keepdims=True)
        vn = (vc * lax.rsqrt(var + LN_EPS) * g_ref[:, gs] + b_ref[:, gs]).astype(BF16)
        w = jnp.where(col <= row, w_ref[g], 0.0).astype(BF16)
        for n in range(tm // CHUNK):
            rs = slice(n * CHUNK, (n + 1) * CHUNK)
            s = _dot(w, vn[rs]) + bias_ref[:, gs]
            o_ref[0, rs, gs] = (jax.nn.gelu(zu[rs, gs]) * s).astype(BF16)


def _outproj_body(a_ref, s_ref, x_ref, gate_ref, wa_ref, ws_ref, lg_ref, lb_ref, o_ref):
    y = _dot(a_ref[0], wa_ref[...]) + _dot(s_ref[0], ws_ref[...])
    o_ref[0] = _deepnorm(x_ref[0], y, gate_ref[0], lg_ref[...], lb_ref[...])


def _outproj(attn, sgu, x, mod, k, w_out, ln_g, ln_b):
    tm = 512
    wa = w_out[:ATTN_WIDTH].astype(BF16)
    ws = w_out[ATTN_WIDTH:].astype(BF16)
    _, _, gate = _mod_specs(k, 2)

    def rows(w):
        return pl.BlockSpec((1, tm, w), lambda b, i: (b, i, 0))

    return pl.pallas_call(
        _outproj_body,
        grid=(BATCH, SEQ // tm),
        in_specs=[rows(ATTN_WIDTH), rows(SGU_WIDTH), rows(D_MODEL), gate,
                  _const_spec((ATTN_WIDTH, D_MODEL)), _const_spec((SGU_WIDTH, D_MODEL)),
                  _const_spec((1, D_MODEL)), _const_spec((1, D_MODEL))],
        out_specs=rows(D_MODEL),
        out_shape=jax.ShapeDtypeStruct((BATCH, SEQ, D_MODEL), F32),
        compiler_params=_params(("parallel", "parallel")),
        name="outproj_ab",
    )(attn, sgu, x, mod, wa, ws, ln_g.reshape(1, -1), ln_b.reshape(1, -1))


def _ffn_up_body(x_ref, sh_ref, sc_ref, wg_ref, wu_ref, h_ref, hm_sc):
    @pl.when(pl.program_id(2) == 0)
    def _():
        hm_sc[...] = _modulate(x_ref[0], sh_ref[0], sc_ref[0]).astype(BF16)

    hm = hm_sc[...]
    g = _dot(hm, wg_ref[0].astype(BF16))
    u = _dot(hm, wu_ref[0].astype(BF16))
    h_ref[0] = (jax.nn.silu(g) * u).astype(BF16)


def _ffn_down_body(h_ref, x_ref, gate_ref, wd_ref, lg_ref, lb_ref, o_ref, y_sc, *, n_n):
    n = pl.program_id(2)
    y_sc[n] = _dot(h_ref[0], wd_ref[...])

    @pl.when(n == n_n - 1)
    def _():
        y = jnp.concatenate([y_sc[c] for c in range(n_n)], axis=-1)
        o_ref[0] = _deepnorm(x_ref[0], y, gate_ref[0], lg_ref[...], lb_ref[...])


def _ffn(x, mod, k, layer, w_gate, w_up, w_down, ln_g, ln_b):
    tm, tf = 2048, 256
    shift, scale, gate = _mod_specs(k, 3)
    h = pl.pallas_call(
        _ffn_up_body,
        grid=(BATCH, SEQ // tm, D_FF // tf),
        in_specs=[pl.BlockSpec((1, tm, D_MODEL), lambda b, i, f: (b, i, 0), pipeline_mode=pl.Buffered(1)),
                  shift, scale,
                  pl.BlockSpec((1, D_MODEL, tf), lambda b, i, f: (layer, 0, f)),
                  pl.BlockSpec((1, D_MODEL, tf), lambda b, i, f: (layer, 0, f))],
        out_specs=pl.BlockSpec((1, tm, tf), lambda b, i, f: (b, i, f)),
        out_shape=jax.ShapeDtypeStruct((BATCH, SEQ, D_FF), BF16),
        scratch_shapes=[pltpu.VMEM((tm, D_MODEL), BF16)],
        compiler_params=_params(("parallel", "parallel", "arbitrary")),
        name="ffn_up",
    )(x, mod, mod, w_gate, w_up)

    tm, tn = 512, 512
    n_n = D_MODEL // tn
    rows = pl.BlockSpec((1, tm, D_MODEL), lambda b, i, n: (b, i, 0))
    vec = pl.BlockSpec((1, D_MODEL), lambda b, i, n: (0, 0))
    return pl.pallas_call(
        functools.partial(_ffn_down_body, n_n=n_n),
        grid=(BATCH, SEQ // tm, n_n),
        in_specs=[pl.BlockSpec((1, tm, D_FF), lambda b, i, n: (b, i, 0)),
                  rows,
                  gate,
                  pl.BlockSpec((D_FF, tn), lambda b, i, n: (0, n)),
                  vec, vec],
        out_specs=rows,
        out_shape=jax.ShapeDtypeStruct((BATCH, SEQ, D_MODEL), F32),
        scratch_shapes=[pltpu.VMEM((n_n, tm, tn), F32)],
        compiler_params=_params(("parallel", "parallel", "arbitrary")),
        name="ffn_down",
    )(h, x, mod, w_down.astype(BF16), ln_g.reshape(1, -1), ln_b.reshape(1, -1))


def _pool_body(x_ref, halo_ref, sh_ref, sc_ref, gate_ref, pw_ref, ps_ref, wo_ref, lg_ref, lb_ref,
               o_ref, *, tm):
    i = pl.program_id(1)
    x = x_ref[0]
    hm = _modulate(x, sh_ref[0], sc_ref[0])
    halo = jnp.where(i > 0, _modulate(halo_ref[0], sh_ref[0], sc_ref[0]), 0.0)
    ext = jnp.concatenate([halo, hm], axis=0)
    ext_hi = ext.astype(BF16)
    ext_lo = (ext - ext_hi.astype(F32)).astype(BF16)
    t_loc = lax.broadcasted_iota(jnp.int32, (tm, tm + POOL_HALO), 0)
    s_loc = lax.broadcasted_iota(jnp.int32, (tm, tm + POOL_HALO), 1)
    lag = t_loc + POOL_HALO - s_loc
    t_glob = i * tm + lax.broadcasted_iota(jnp.int32, (tm, 1), 0)
    outs = []
    for gi, w in enumerate(POOL_WINDOWS):
        cs = slice(gi * POOL_GROUP_DIM, (gi + 1) * POOL_GROUP_DIM)
        band = jnp.where(lag >= 0, jnp.where(lag < w, 1.0, 0.0), 0.0).astype(BF16)
        wsum = _dot(band, ext_hi[:, cs]) + _dot(band, ext_lo[:, cs])
        count = jnp.minimum(t_glob + 1, w).astype(F32)
        d = (wsum / count - hm[:, cs]).astype(BF16)
        outs.append(_dot(d, pw_ref[gi]))
    pooled = (jnp.concatenate(outs, axis=-1) * ps_ref[...]).astype(BF16)
    y = _dot(pooled, wo_ref[...])
    o_ref[0] = _deepnorm(x, y, gate_ref[0], lg_ref[...], lb_ref[...])


def _pool_mixer(x, mod, k, pool_w, pool_scale, w_out_c, ln_g, ln_b):
    tm = 512
    per = tm // POOL_HALO
    shift, scale, gate = _mod_specs(k, 2)
    rows = pl.BlockSpec((1, tm, D_MODEL), lambda b, i: (b, i, 0))
    halo = pl.BlockSpec((1, POOL_HALO, D_MODEL), lambda b, i: (b, jnp.maximum(i * per - 1, 0), 0))
    return pl.pallas_call(
        functools.partial(_pool_body, tm=tm),
        grid=(BATCH, SEQ // tm),
        in_specs=[rows, halo, shift, scale, gate,
                  _const_spec((len(POOL_WINDOWS), POOL_GROUP_DIM, POOL_GROUP_DIM)),
                  _const_spec((1, D_MODEL)), _const_spec((D_MODEL, D_MODEL)),
                  _const_spec((1, D_MODEL)), _const_spec((1, D_MODEL))],
        out_specs=rows,
        out_shape=jax.ShapeDtypeStruct((BATCH, SEQ, D_MODEL), F32),
        compiler_params=_params(("parallel", "parallel")),
        name="pool_mixer",
    )(x, x, mod, mod, mod, pool_w.astype(BF16), pool_scale.reshape(1, -1),
      w_out_c.astype(BF16), ln_g.reshape(1, -1), ln_b.reshape(1, -1))


def _router_body(x_ref, sh_ref, sc_ref, wr_ref, g_ref, hm_ref):
    hm = _modulate(x_ref[0], sh_ref[0], sc_ref[0])
    hm_ref[0] = hm
    hm_hi = hm.astype(BF16)
    hm_lo = (hm - hm_hi.astype(F32)).astype(BF16)
    w = wr_ref[...]
    w_hi = w.astype(BF16)
    w_lo = (w - w_hi.astype(F32)).astype(BF16)
    logits = _dot(hm_hi, w_hi) + (_dot(hm_lo, w_hi) + _dot(hm_hi, w_lo))
    lane = lax.broadcasted_iota(jnp.int32, logits.shape, 1).astype(F32)
    logits = jnp.where(lane < N_EXPERTS, logits, -jnp.inf)
    v1 = jnp.max(logits, axis=-1, keepdims=True)
    i1 = jnp.min(jnp.where(logits == v1, lane, float(LANES)), axis=-1, keepdims=True)
    rest = jnp.where(lane == i1, -jnp.inf, logits)
    v2 = jnp.max(rest, axis=-1, keepdims=True)
    i2 = jnp.min(jnp.where(rest == v2, lane, float(LANES)), axis=-1, keepdims=True)
    e2 = jnp.exp(v2 - v1)
    den = 1.0 + e2
    g_ref[0] = (jnp.where(lane == 0.0, i1, 0.0) + jnp.where(lane == 1.0, i2, 0.0)
                + jnp.where(lane == 2.0, 1.0 / den, 0.0) + jnp.where(lane == 3.0, e2 / den, 0.0))


def _router(x, mod, k, w_router):
    tm = 512
    shift, scale, _ = _mod_specs(k, 2)
    wr = jnp.pad(w_router, ((0, 0), (0, LANES - N_EXPERTS)))
    rows = pl.BlockSpec((1, tm, D_MODEL), lambda b, i: (b, i, 0))
    return pl.pallas_call(
        _router_body,
        grid=(BATCH, SEQ // tm),
        in_specs=[rows, shift, scale, _const_spec((D_MODEL, LANES))],
        out_specs=(pl.BlockSpec((1, tm, LANES), lambda b, i: (b, i, 0)), rows),
        out_shape=(jax.ShapeDtypeStruct((BATCH, SEQ, LANES), F32),
                   jax.ShapeDtypeStruct((BATCH, SEQ, D_MODEL), F32)),
        compiler_params=_params(("parallel", "parallel")),
        name="router",
    )(x, mod, mod, wr)


N_TOKENS = BATCH * SEQ
N_ASSIGN = 2 * N_TOKENS
MOE_BLOCK = 1024
MOE_SUB = 256
MOE_TILE = N_ASSIGN // N_EXPERTS + MOE_SUB
MOE_TILES = -(-(N_ASSIGN + N_EXPERTS * (MOE_TILE - 1)) // MOE_TILE)
MOE_ROWS = MOE_TILES * MOE_TILE
GATHER_ROWS = MOE_TILE // 3
assert MOE_TILE % GATHER_ROWS == 0 and GATHER_ROWS % 8 == 0


def _route_plan(routing):
    e_flat = routing.reshape(N_TOKENS, LANES)[:, :2].astype(jnp.int32).T.reshape(N_ASSIGN)
    onehot = (e_flat[:, None] == jnp.arange(N_EXPERTS, dtype=jnp.int32)[None, :]).astype(jnp.int32)
    csum = jnp.cumsum(onehot, axis=0)
    counts = csum[-1]
    rank = jnp.sum(csum * onehot, axis=1) - 1
    padded = (counts + MOE_TILE - 1) // MOE_TILE * MOE_TILE
    group_end = jnp.cumsum(padded)
    group_start = group_end - padded
    pos = (jnp.sum(onehot * group_start[None, :], axis=1) + rank).astype(jnp.int32)
    tok = jnp.tile(jnp.arange(N_TOKENS, dtype=jnp.int32), 2)
    src = jnp.zeros((MOE_ROWS,), jnp.int32).at[pos].set(tok, unique_indices=True)
    tile_start = jnp.arange(MOE_TILES, dtype=jnp.int32) * MOE_TILE
    n_used = group_end[-1] // MOE_TILE
    tile_expert = jnp.sum((tile_start[:, None] >= group_end[None, :]).astype(jnp.int32), axis=1)
    last_expert = jnp.max(jnp.where(counts > 0, jnp.arange(N_EXPERTS, dtype=jnp.int32), 0))
    tile_expert = jnp.minimum(tile_expert, last_expert)
    tile_rows = jnp.clip(counts[tile_expert] - (tile_start - group_start[tile_expert]), 0, MOE_TILE)
    tile_rows = jnp.where(jnp.arange(MOE_TILES) < n_used, tile_rows, 0).astype(jnp.int32)
    return pos, src, tile_expert.astype(jnp.int32), tile_rows, n_used.reshape(1).astype(jnp.int32)


GATHER_UNROLL = 8


def _gather_body(src_ref, rows_ref, hm_hbm, o_ref, buf, sem, *, tg):
    i = pl.program_id(0)
    base = i * tg
    per = MOE_TILE // tg
    n_valid = jnp.clip(rows_ref[i // per] - (i % per) * tg, 0, tg)

    @pl.when(n_valid < tg)
    def _():
        buf[...] = jnp.zeros(buf.shape, F32)

    def issue_row(r, queue):
        t = src_ref[base + r]
        pltpu.make_async_copy(hm_hbm.at[pl.ds(t, 1)], buf.at[pl.ds(r, 1)], sem).start(priority=queue)

    def issue_group(g, carry):
        for u in range(GATHER_UNROLL):
            issue_row(g * GATHER_UNROLL + u, u % 2)
        return carry

    def issue_one(r, carry):
        issue_row(r, 0)
        return carry

    n_groups = n_valid // GATHER_UNROLL
    n_grouped = pl.multiple_of(n_groups * GATHER_UNROLL, GATHER_UNROLL)
    lax.fori_loop(0, n_groups, issue_group, 0)
    lax.fori_loop(n_grouped, n_valid, issue_one, 0)

    @pl.when(n_groups > 0)
    def _():
        pltpu.make_async_copy(hm_hbm.at[pl.ds(0, n_grouped)], buf.at[pl.ds(0, n_grouped)], sem).wait()

    def wait_one(r, carry):
        pltpu.make_async_copy(hm_hbm.at[pl.ds(0, 1)], buf.at[pl.ds(r, 1)], sem).wait()
        return carry

    lax.fori_loop(n_grouped, n_valid, wait_one, 0)

    o_ref[...] = buf[...].astype(BF16)


def _gather_rows(src, tile_rows, hm):
    tg = GATHER_ROWS
    grid_spec = pltpu.PrefetchScalarGridSpec(
        num_scalar_prefetch=2,
        grid=(MOE_ROWS // tg,),
        in_specs=[pl.BlockSpec(memory_space=pl.ANY)],
        out_specs=pl.BlockSpec((tg, D_MODEL), lambda i, src, tr: (i, 0)),
        scratch_shapes=[pltpu.VMEM((tg, D_MODEL), F32), pltpu.SemaphoreType.DMA(())],
    )
    return pl.pallas_call(
        functools.partial(_gather_body, tg=tg),
        grid_spec=grid_spec,
        out_shape=jax.ShapeDtypeStruct((MOE_ROWS, D_MODEL), BF16),
        compiler_params=_params(("arbitrary",)),
        name="moe_gather",
    )(src, tile_rows, hm.reshape(N_TOKENS, D_MODEL))


def _for_valid_blocks(rows, tm, block_fn, empty_fn):
    for b0 in range(0, tm, MOE_BLOCK):
        b1 = min(b0 + MOE_BLOCK, tm)
        whole = b1 if b1 - b0 == MOE_BLOCK else tm + 1
        if whole <= tm:
            pl.when(rows >= whole)(functools.partial(block_fn, slice(b0, b1)))

        for r0 in range(b0, b1, MOE_SUB):
            rs = slice(r0, r0 + MOE_SUB)

            @pl.when(jnp.logical_and(rows < whole, r0 < rows))
            def _():
                block_fn(rs)

            @pl.when(r0 >= rows)
            def _():
                empty_fn(rs)


def _gup_body(te_ref, rows_ref, nu_ref, xs_ref, wg_ref, wu_ref, h_ref, wgb, wub, *, tm):
    rows = rows_ref[pl.program_id(0)]

    @pl.when(rows > 0)
    def _():
        wgb[...] = wg_ref[0].astype(BF16)
        wub[...] = wu_ref[0].astype(BF16)

    def block(rs):
        xsb = xs_ref[rs, :]
        h_ref[rs, :] = (jax.nn.silu(_dot(xsb, wgb[...])) * _dot(xsb, wub[...])).astype(BF16)

    def empty(rs):
        h_ref[rs, :] = jnp.zeros((rs.stop - rs.start, h_ref.shape[1]), BF16)

    _for_valid_blocks(rows, tm, block, empty)


def _gdown_body(te_ref, rows_ref, nu_ref, h_ref, wd_ref, o_ref, wdb, *, tm):
    rows = rows_ref[pl.program_id(0)]

    @pl.when(rows > 0)
    def _():
        wdb[...] = wd_ref[0].astype(BF16)

    def block(rs):
        o_ref[rs, :] = _dot(h_ref[rs, :], wdb[...])

    def empty(rs):
        o_ref[rs, :] = jnp.zeros((rs.stop - rs.start, o_ref.shape[1]), F32)

    _for_valid_blocks(rows, tm, block, empty)


def _grouped_ffn(xs, tile_expert, tile_rows, n_used, layer, w_gate, w_up, w_down):
    tm = MOE_TILE
    first = layer * N_EXPERTS
    n_all = w_gate.shape[0] * N_EXPERTS

    def tile(i, nu):
        return jnp.minimum(i, nu[0] - 1)

    def chunk(i, c, nu, n_chunks):
        return jnp.where(i < nu[0], c, n_chunks - 1)

    tf = 256
    n_f = D_FF_EXPERT // tf
    w_up_spec = pl.BlockSpec((1, D_MODEL, tf),
                             lambda i, f, te, tr, nu: (first + te[i], 0, chunk(i, f, nu, n_f)))
    up_spec = pltpu.PrefetchScalarGridSpec(
        num_scalar_prefetch=3,
        grid=(MOE_TILES, n_f),
        in_specs=[pl.BlockSpec((tm, D_MODEL), lambda i, f, te, tr, nu: (tile(i, nu), 0)),
                  w_up_spec, w_up_spec],
        out_specs=pl.BlockSpec((tm, tf), lambda i, f, te, tr, nu: (i, f)),
        scratch_shapes=[pltpu.VMEM((D_MODEL, tf), BF16), pltpu.VMEM((D_MODEL, tf), BF16)],
    )
    h = pl.pallas_call(
        functools.partial(_gup_body, tm=tm),
        grid_spec=up_spec,
        out_shape=jax.ShapeDtypeStruct((MOE_ROWS, D_FF_EXPERT), BF16),
        compiler_params=_params(("arbitrary", "arbitrary")),
        name="moe_up",
    )(tile_expert, tile_rows, n_used, xs,
      w_gate.reshape(n_all, D_MODEL, D_FF_EXPERT), w_up.reshape(n_all, D_MODEL, D_FF_EXPERT))

    tn = 512
    n_n = D_MODEL // tn
    down_spec = pltpu.PrefetchScalarGridSpec(
        num_scalar_prefetch=3,
        grid=(MOE_TILES, n_n),
        in_specs=[pl.BlockSpec((tm, D_FF_EXPERT), lambda i, n, te, tr, nu: (tile(i, nu), 0)),
                  pl.BlockSpec((1, D_FF_EXPERT, tn),
                               lambda i, n, te, tr, nu: (first + te[i], 0, chunk(i, n, nu, n_n)))],
        out_specs=pl.BlockSpec((tm, tn), lambda i, n, te, tr, nu: (i, n)),
        scratch_shapes=[pltpu.VMEM((D_FF_EXPERT, tn), BF16)],
    )
    return pl.pallas_call(
        functools.partial(_gdown_body, tm=tm),
        grid_spec=down_spec,
        out_shape=jax.ShapeDtypeStruct((MOE_ROWS, D_MODEL), F32),
        compiler_params=_params(("arbitrary", "arbitrary")),
        name="moe_down",
    )(tile_expert, tile_rows, n_used, h, w_down.reshape(n_all, D_FF_EXPERT, D_MODEL))


def _combine_body(pos_ref, ys_hbm, r_ref, x_ref, gate_ref, lg_ref, lb_ref, o_ref, buf, sem, *, tc):
    base = (pl.program_id(0) * (SEQ // tc) + pl.program_id(1)) * tc

    def issue(r, carry):
        for k in range(2):
            p = pos_ref[k * N_TOKENS + base + r]
            pltpu.make_async_copy(ys_hbm.at[pl.ds(p, 1)], buf.at[k, pl.ds(r, 1)],
                                  sem.at[k]).start(priority=k)
        return carry

    lax.fori_loop(0, tc, issue, 0, unroll=4)
    for k in range(2):
        pltpu.make_async_copy(ys_hbm.at[pl.ds(0, tc)], buf.at[k], sem.at[k]).wait()
    routing = r_ref[0]
    y = routing[:, 2:3] * buf[0] + routing[:, 3:4] * buf[1]
    o_ref[0] = _deepnorm(x_ref[0], y, gate_ref[0], lg_ref[...], lb_ref[...])


def _combine(pos, ys, routing, x, mod, k, ln_g, ln_b):
    tc = 512
    rows = pl.BlockSpec((1, tc, D_MODEL), lambda b, i, pos: (b, i, 0))
    vec = pl.BlockSpec((1, D_MODEL), lambda b, i, pos: (0, 0))
    grid_spec = pltpu.PrefetchScalarGridSpec(
        num_scalar_prefetch=1,
        grid=(BATCH, SEQ // tc),
        in_specs=[pl.BlockSpec(memory_space=pl.ANY),
                  pl.BlockSpec((1, tc, LANES), lambda b, i, pos: (b, i, 0)),
                  rows,
                  pl.BlockSpec((1, 1, D_MODEL), lambda b, i, pos: (k * 8 + b, 0, 2)),
                  vec, vec],
        out_specs=rows,
        scratch_shapes=[pltpu.VMEM((2, tc, D_MODEL), F32), pltpu.SemaphoreType.DMA((2,))],
    )
    return pl.pallas_call(
        functools.partial(_combine_body, tc=tc),
        grid_spec=grid_spec,
        out_shape=jax.ShapeDtypeStruct((BATCH, SEQ, D_MODEL), F32),
        compiler_params=_params(("arbitrary", "arbitrary")),
        name="moe_combine",
    )(pos, ys, routing, x, mod, ln_g.reshape(1, -1), ln_b.reshape(1, -1))


def _moe(x, mod, k, w_router, layer, w_gate, w_up, w_down, ln_g, ln_b):
    routing, hm = _router(x, mod, k, w_router)
    pos, src, tile_expert, tile_rows, n_used = _route_plan(routing)
    xs = _gather_rows(src, tile_rows, hm)
    ys = _grouped_ffn(xs, tile_expert, tile_rows, n_used, layer, w_gate, w_up, w_down)
    return _combine(pos, ys, routing, x, mod, k, ln_g, ln_b)


def kernel(x, c, positions, ada_w, ada_b, ln_g, ln_b, w_in_ab, q_norm_g, w_q_up, kv_norm_g, w_kv_up, sgu_norm_g, sgu_norm_b, sgu_w, sgu_b, w_out_ab, ffn_w_gate, ffn_w_up, ffn_w_down, pool_w, pool_scale, w_out_c, router_w, moe_w_gate, moe_w_up, moe_w_down):
    mod = _ada_modulation(c, ada_w, ada_b)
    tables = _rope_tables(positions)
    for l in range(DEPTH):
        j = l // 2
        k_tok, k_ch = 2 * l, 2 * l + 1
        if l % 2 == 0:
            q, k, vt, sgu = _front(x, mod, k_tok, tables, w_in_ab[j], q_norm_g[j], w_q_up[j],
                                   kv_norm_g[j], w_kv_up[j],
                                   sgu_norm_g[j], sgu_norm_b[j], sgu_w[j], sgu_b[j])
            attn = _attention(q, k, vt)
            x = _outproj(attn, sgu, x, mod, k_tok, w_out_ab[j], ln_g[l, 0], ln_b[l, 0])
            x = _ffn(x, mod, k_ch, j, ffn_w_gate, ffn_w_up, ffn_w_down[j], ln_g[l, 1], ln_b[l, 1])
        else:
            x = _pool_mixer(x, mod, k_tok, pool_w[j], pool_scale[j], w_out_c[j], ln_g[l, 0], ln_b[l, 0])
            x = _moe(x, mod, k_ch, router_w[j], j, moe_w_gate, moe_w_up, moe_w_down,
                     ln_g[l, 1], ln_b[l, 1])
    return x
```

```python
import functools

import jax
import jax.numpy as jnp
import numpy as np
from jax import lax
from jax.experimental import pallas as pl
from jax.experimental.pallas import tpu as pltpu

F32 = jnp.float32
BF16 = jnp.bfloat16

D_MODEL = 2048
BATCH = 2
SEQ = 4096
DEPTH = 4
MLA_HEADS = 8
QK_NOPE_DIM = 128
QK_ROPE_DIM = 64
V_HEAD_DIM = 128
Q_LORA_RANK = 768
KV_LORA_RANK = 512
ROPE_THETA = 10000.0
SGU_GROUPS = 8
SGU_GROUP_DIM = 128
SGU_WIDTH = SGU_GROUPS * SGU_GROUP_DIM
CHUNK = 128
ATTN_WIDTH = MLA_HEADS * V_HEAD_DIM
POOL_WINDOWS = (2, 4, 8, 16)
POOL_GROUP_DIM = D_MODEL // 4
D_FF = 5632
N_EXPERTS = 8
D_FF_EXPERT = 2816
DEEPNORM_ALPHA = (2 * DEPTH) ** 0.25
LN_EPS = 1e-5
RMS_EPS = 1e-6
SM_SCALE = (QK_NOPE_DIM + QK_ROPE_DIM) ** -0.5
LOG2_E = 1.4426950408889634

LANES = 128
HEAD_LANES = 2 * LANES
QK_WIDTH = MLA_HEADS * HEAD_LANES
VMEM_LIMIT = 60 * 1024 * 1024
POOL_HALO = 128


def _params(semantics):
    return pltpu.CompilerParams(dimension_semantics=semantics, vmem_limit_bytes=VMEM_LIMIT)


def _const_spec(shape):
    nd = len(shape)
    return pl.BlockSpec(shape, lambda *_: (0,) * nd, pipeline_mode=pl.Buffered(1))


def _dot(a, b):
    return jnp.dot(a, b, preferred_element_type=F32)


def _modulate(x, shift, scale):
    return x * (1.0 + scale) + shift


def _deepnorm(x, y, gate, g, b):
    r = DEEPNORM_ALPHA * x + (1.0 + gate) * y
    mu = jnp.mean(r, axis=-1, keepdims=True)
    rc = r - mu
    var = jnp.mean(rc * rc, axis=-1, keepdims=True)
    return rc * lax.rsqrt(var + LN_EPS) * g + b


def _rms(x, g):
    return x * lax.rsqrt(jnp.mean(x * x, axis=-1, keepdims=True) + RMS_EPS) * g


def _rope(r, cos_t, sin_hi, sin_lo):
    return (r * cos_t + pltpu.roll(r, QK_ROPE_DIM // 2, 1) * sin_hi
            + pltpu.roll(r, LANES - QK_ROPE_DIM // 2, 1) * sin_lo)


def _mod_specs(k, n_grid):
    def spec(part):
        if n_grid == 2:
            return pl.BlockSpec((1, 1, D_MODEL), lambda b, i: (k * 8 + b, 0, part))
        if n_grid == 3:
            return pl.BlockSpec((1, 1, D_MODEL), lambda b, i, f: (k * 8 + b, 0, part))
        return pl.BlockSpec((1, 1, D_MODEL), lambda b, i, e, f: (k * 8 + b, 0, part))
    return spec(0), spec(1), spec(2)


def _ada_body(c_ref, w_ref, b_ref, o_ref):
    s = jax.nn.silu(c_ref[...]).astype(BF16)
    o_ref[0] = _dot(s, w_ref[0].astype(BF16)) + b_ref[0]


def _ada_modulation(c, ada_w, ada_b):
    n_mod = 2 * DEPTH
    tn = 1024
    c_pad = jnp.pad(c, ((0, 8 - BATCH), (0, 0)))
    out = pl.pallas_call(
        _ada_body,
        grid=(n_mod, 3 * D_MODEL // tn),
        in_specs=[
            pl.BlockSpec((8, D_MODEL), lambda k, n: (0, 0)),
            pl.BlockSpec((1, D_MODEL, tn), lambda k, n: (k, 0, n)),
            pl.BlockSpec((1, 1, tn), lambda k, n: (k, 0, n)),
        ],
        out_specs=pl.BlockSpec((1, 8, tn), lambda k, n: (k, 0, n)),
        out_shape=jax.ShapeDtypeStruct((n_mod, 8, 3 * D_MODEL), F32),
        compiler_params=_params(("parallel", "parallel")),
        name="ada_mod",
    )(c_pad, ada_w.reshape(n_mod, D_MODEL, 3 * D_MODEL), ada_b.reshape(n_mod, 1, 3 * D_MODEL))
    return out.reshape(n_mod * 8, 1, 3 * D_MODEL)


def _rope_table_body(pos_ref, inv_ref, cos_ref, hi_ref, lo_ref):
    ang = pos_ref[0].astype(F32) * inv_ref[...]
    lane = lax.broadcasted_iota(jnp.int32, ang.shape, 1)
    cos = jnp.cos(ang)
    sin = jnp.sin(ang)
    half = QK_ROPE_DIM // 2
    cos_ref[0] = jnp.where(lane < QK_ROPE_DIM, cos, 0.0)
    hi_ref[0] = jnp.where(lane < half, 0.0, jnp.where(lane < QK_ROPE_DIM, sin, 0.0))
    lo_ref[0] = jnp.where(lane < half, -sin, 0.0)


def _rope_tables(positions):
    tm = 512
    half = QK_ROPE_DIM // 2
    inv = ROPE_THETA ** (-jnp.arange(0, QK_ROPE_DIM, 2, dtype=F32) / QK_ROPE_DIM)
    inv_lanes = jnp.concatenate([inv, inv, jnp.zeros((LANES - 2 * half,), F32)]).reshape(1, LANES)
    tab = jax.ShapeDtypeStruct((BATCH, SEQ, LANES), F32)
    spec = pl.BlockSpec((1, tm, LANES), lambda b, i: (b, i, 0))
    return pl.pallas_call(
        _rope_table_body,
        grid=(BATCH, SEQ // tm),
        in_specs=[pl.BlockSpec((1, tm, 1), lambda b, i: (b, i, 0)),
                  pl.BlockSpec((1, LANES), lambda b, i: (0, 0))],
        out_specs=(spec, spec, spec),
        out_shape=(tab, tab, tab),
        compiler_params=_params(("parallel", "parallel")),
        name="rope_tables",
    )(positions.reshape(BATCH, SEQ, 1), inv_lanes)


def _front_body(x_ref, sh_ref, sc_ref, wq_ref, wkv_ref, wkr_ref, wu_ref, wv_ref,
                cos_ref, hi_ref, lo_ref, qg_ref, kvg_ref, wqn_ref, wqr_ref, wkn_ref, wvt_ref,
                sg_ref, sb_ref, sw_ref, sbias_ref,
                q_ref, k_ref, vt_ref, sgu_ref, *, tm):
    hm = _modulate(x_ref[0], sh_ref[0], sc_ref[0]).astype(BF16)
    _qkv_compute(_dot(hm, wq_ref[...]), _dot(hm, wkv_ref[...]), _dot(hm, wkr_ref[...]),
                 cos_ref[0], hi_ref[0], lo_ref[0], qg_ref, kvg_ref,
                 wqn_ref, wqr_ref, wkn_ref, wvt_ref, q_ref, k_ref, vt_ref)
    _sgu_compute(_dot(hm, wu_ref[...]), _dot(hm, wv_ref[...]),
                 sg_ref, sb_ref, sw_ref, sbias_ref, sgu_ref, tm)


def _front(x, mod, k, tables, w_in, q_norm_g, w_q_up, kv_norm_g, w_kv_up,
           sgu_norm_g, sgu_norm_b, sgu_w, sgu_b):
    tm = 512
    a, b, c_, d = (Q_LORA_RANK, Q_LORA_RANK + KV_LORA_RANK,
                   Q_LORA_RANK + KV_LORA_RANK + QK_ROPE_DIM,
                   Q_LORA_RANK + KV_LORA_RANK + QK_ROPE_DIM + SGU_WIDTH)
    wq = w_in[:, :a].astype(BF16)
    wkv = w_in[:, a:b].astype(BF16)
    wkr = jnp.pad(w_in[:, b:c_], ((0, 0), (0, LANES - QK_ROPE_DIM))).astype(BF16)
    wu = w_in[:, c_:d].astype(BF16)
    wv = w_in[:, d:].astype(BF16)
    in_widths = (Q_LORA_RANK, KV_LORA_RANK, LANES, SGU_WIDTH, SGU_WIDTH)

    wq_up = w_q_up.reshape(Q_LORA_RANK, MLA_HEADS, QK_NOPE_DIM + QK_ROPE_DIM)
    wqn = wq_up[:, :, :QK_NOPE_DIM].reshape(Q_LORA_RANK, MLA_HEADS * LANES).astype(BF16)
    wqr = jnp.pad(wq_up[:, :, QK_NOPE_DIM:], ((0, 0), (0, 0), (0, LANES - QK_ROPE_DIM)))
    wqr = wqr.reshape(Q_LORA_RANK, MLA_HEADS * LANES).astype(BF16)
    wkv_up = w_kv_up.reshape(KV_LORA_RANK, MLA_HEADS, QK_NOPE_DIM + V_HEAD_DIM)
    wkn = wkv_up[:, :, :QK_NOPE_DIM].reshape(KV_LORA_RANK, MLA_HEADS * LANES).astype(BF16)
    wvt = wkv_up[:, :, QK_NOPE_DIM:].reshape(KV_LORA_RANK, ATTN_WIDTH).T.astype(BF16)
    sgu_bias = jnp.repeat(sgu_b.T, SGU_GROUP_DIM, axis=1)

    shift, scale, _ = _mod_specs(k, 2)

    def rows(w):
        return pl.BlockSpec((1, tm, w), lambda b, i: (b, i, 0))

    return pl.pallas_call(
        functools.partial(_front_body, tm=tm),
        grid=(BATCH, SEQ // tm),
        in_specs=[rows(D_MODEL), shift, scale]
                 + [_const_spec((D_MODEL, w)) for w in in_widths]
                 + [rows(LANES), rows(LANES), rows(LANES),
                    _const_spec((1, Q_LORA_RANK)), _const_spec((1, KV_LORA_RANK)),
                    _const_spec((Q_LORA_RANK, MLA_HEADS * LANES)),
                    _const_spec((Q_LORA_RANK, MLA_HEADS * LANES)),
                    _const_spec((KV_LORA_RANK, MLA_HEADS * LANES)),
                    _const_spec((ATTN_WIDTH, KV_LORA_RANK)),
                    _const_spec((1, SGU_WIDTH)), _const_spec((1, SGU_WIDTH)),
                    _const_spec((SGU_GROUPS, CHUNK, CHUNK)), _const_spec((CHUNK, SGU_WIDTH))],
        out_specs=(rows(QK_WIDTH), rows(QK_WIDTH),
                   pl.BlockSpec((1, ATTN_WIDTH, tm), lambda b, i: (b, 0, i)),
                   rows(SGU_WIDTH)),
        out_shape=(jax.ShapeDtypeStruct((BATCH, SEQ, QK_WIDTH), BF16),
                   jax.ShapeDtypeStruct((BATCH, SEQ, QK_WIDTH), BF16),
                   jax.ShapeDtypeStruct((BATCH, ATTN_WIDTH, SEQ), BF16),
                   jax.ShapeDtypeStruct((BATCH, SEQ, SGU_WIDTH), BF16)),
        compiler_params=_params(("parallel", "parallel")),
        name="front_ab",
    )(x, mod, mod, wq, wkv, wkr, wu, wv, *tables,
      q_norm_g.reshape(1, -1), kv_norm_g.reshape(1, -1), wqn, wqr, wkn, wvt,
      sgu_norm_g.reshape(1, -1), sgu_norm_b.reshape(1, -1), sgu_w, sgu_bias)


def _qkv_compute(cq, ckv, kr, cos_t, sin_hi, sin_lo, qg_ref, kvg_ref,
                 wqn_ref, wqr_ref, wkn_ref, wvt_ref, q_ref, k_ref, vt_ref):
    cqn = _rms(cq, qg_ref[...]).astype(BF16)
    qn = _dot(cqn, wqn_ref[...])
    qr = _dot(cqn, wqr_ref[...])
    ckvn = _rms(ckv, kvg_ref[...]).astype(BF16)
    kn = _dot(ckvn, wkn_ref[...])
    vt_ref[0] = lax.dot_general(wvt_ref[...], ckvn, (((1,), (1,)), ((), ())),
                                preferred_element_type=F32).astype(BF16)
    k_rope = _rope(kr, cos_t, sin_hi, sin_lo).astype(BF16)
    for h in range(MLA_HEADS):
        nope = slice(h * LANES, (h + 1) * LANES)
        lo = h * HEAD_LANES
        q_ref[0, :, lo:lo + LANES] = qn[:, nope].astype(BF16)
        q_ref[0, :, lo + LANES:lo + HEAD_LANES] = _rope(qr[:, nope], cos_t, sin_hi, sin_lo).astype(BF16)
        k_ref[0, :, lo:lo + LANES] = kn[:, nope].astype(BF16)
        k_ref[0, :, lo + LANES:lo + HEAD_LANES] = k_rope


def _attn_body(qi_ref, kj_ref, q_ref, k_ref, vt_ref, o_ref, m_sc, l_sc, acc_sc, *, tq):
    p = pl.program_id(1)
    i = qi_ref[p]
    j = kj_ref[p]

    @pl.when(j == 0)
    def _():
        m_sc[...] = jnp.full(m_sc.shape, -jnp.inf, F32)
        l_sc[...] = jnp.zeros(l_sc.shape, F32)
        acc_sc[...] = jnp.zeros(acc_sc.shape, F32)

    def accumulate(diagonal):
        if diagonal:
            key = lax.broadcasted_iota(jnp.int32, (tq, tq), 0)
            qry = lax.broadcasted_iota(jnp.int32, (tq, tq), 1)
            visible = key <= qry
        for h in range(MLA_HEADS):
            qk = slice(h * HEAD_LANES, (h + 1) * HEAD_LANES)
            hv = slice(h * V_HEAD_DIM, (h + 1) * V_HEAD_DIM)
            s = lax.dot_general(k_ref[0, :, qk], q_ref[0, :, qk], (((1,), (1,)), ((), ())),
                                preferred_element_type=F32) * (SM_SCALE * LOG2_E)
            if diagonal:
                s = jnp.where(visible, s, -jnp.inf)
            m_prev = m_sc[h]
            m_new = jnp.maximum(m_prev, jnp.max(s, axis=0, keepdims=True))
            alpha = jnp.exp2(m_prev - m_new)
            pr = jnp.exp2(s - m_new)
            l_sc[h] = alpha * l_sc[h] + jnp.sum(pr, axis=0, keepdims=True)
            acc_sc[hv, :] = alpha * acc_sc[hv, :] + _dot(vt_ref[0, hv, :], pr.astype(BF16))
            m_sc[h] = m_new

    @pl.when(j < i)
    def _():
        accumulate(False)

    @pl.when(j == i)
    def _():
        accumulate(True)
        for h in range(MLA_HEADS):
            hv = slice(h * V_HEAD_DIM, (h + 1) * V_HEAD_DIM)
            o_ref[0, :, hv] = (acc_sc[hv, :] / l_sc[h]).T.astype(BF16)


def _attention(q, k, vt):
    tq = 512
    nb = SEQ // tq
    pairs = [(i, j) for i in range(nb) for j in range(i + 1)]
    qi = jnp.asarray(np.array([p[0] for p in pairs], np.int32))
    kj = jnp.asarray(np.array([p[1] for p in pairs], np.int32))
    grid_spec = pltpu.PrefetchScalarGridSpec(
        num_scalar_prefetch=2,
        grid=(BATCH, len(pairs)),
        in_specs=[pl.BlockSpec((1, tq, QK_WIDTH), lambda b, p, qi, kj: (b, qi[p], 0)),
                  pl.BlockSpec((1, tq, QK_WIDTH), lambda b, p, qi, kj: (b, kj[p], 0)),
                  pl.BlockSpec((1, ATTN_WIDTH, tq), lambda b, p, qi, kj: (b, 0, kj[p]))],
        out_specs=pl.BlockSpec((1, tq, ATTN_WIDTH), lambda b, p, qi, kj: (b, qi[p], 0)),
        scratch_shapes=[pltpu.VMEM((MLA_HEADS, 1, tq), F32),
                        pltpu.VMEM((MLA_HEADS, 1, tq), F32),
                        pltpu.VMEM((ATTN_WIDTH, tq), F32)],
    )
    return pl.pallas_call(
        functools.partial(_attn_body, tq=tq),
        grid_spec=grid_spec,
        out_shape=jax.ShapeDtypeStruct((BATCH, SEQ, ATTN_WIDTH), BF16),
        compiler_params=_params(("parallel", "arbitrary")),
        name="mla_attention",
    )(qi, kj, q, k, vt)


def _sgu_compute(zu, zv, g_ref, b_ref, w_ref, bias_ref, o_ref, tm):
    row = lax.broadcasted_iota(jnp.int32, (CHUNK, CHUNK), 0)
    col = lax.broadcasted_iota(jnp.int32, (CHUNK, CHUNK), 1)
    for g in range(SGU_GROUPS):
        gs = slice(g * SGU_GROUP_DIM, (g + 1) * SGU_GROUP_DIM)
        v = jax.nn.gelu(zv[:, gs])
        mu = jnp.mean(v, axis=-1, keepdims=True)
        vc = v - mu
        var = jnp.mean(vc * vc, axis=-1, keepdims=True)
        vn = (vc * lax.rsqrt(var + LN_EPS) * g_ref[:, gs] + b_ref[:, gs]).astype(BF16)
        w = jnp.where(col <= row, w_ref[g], 0.0).astype(BF16)
        for n in range(tm // CHUNK):
            rs = slice(n * CHUNK, (n + 1) * CHUNK)
            s = _dot(w, vn[rs]) + bias_ref[:, gs]
            o_ref[0, rs, gs] = (jax.nn.gelu(zu[rs, gs]) * s).astype(BF16)


def _outproj_body(a_ref, s_ref, x_ref, gate_ref, wa_ref, ws_ref, lg_ref, lb_ref, o_ref):
    y = _dot(a_ref[0], wa_ref[...]) + _dot(s_ref[0], ws_ref[...])
    o_ref[0] = _deepnorm(x_ref[0], y, gate_ref[0], lg_ref[...], lb_ref[...])


def _outproj(attn, sgu, x, mod, k, w_out, ln_g, ln_b):
    tm = 512
    wa = w_out[:ATTN_WIDTH].astype(BF16)
    ws = w_out[ATTN_WIDTH:].astype(BF16)
    _, _, gate = _mod_specs(k, 2)

    def rows(w):
        return pl.BlockSpec((1, tm, w), lambda b, i: (b, i, 0))

    return pl.pallas_call(
        _outproj_body,
        grid=(BATCH, SEQ // tm),
        in_specs=[rows(ATTN_WIDTH), rows(SGU_WIDTH), rows(D_MODEL), gate,
                  _const_spec((ATTN_WIDTH, D_MODEL)), _const_spec((SGU_WIDTH, D_MODEL)),
                  _const_spec((1, D_MODEL)), _const_spec((1, D_MODEL))],
        out_specs=rows(D_MODEL),
        out_shape=jax.ShapeDtypeStruct((BATCH, SEQ, D_MODEL), F32),
        compiler_params=_params(("parallel", "parallel")),
        name="outproj_ab",
    )(attn, sgu, x, mod, wa, ws, ln_g.reshape(1, -1), ln_b.reshape(1, -1))


def _ffn_up_body(x_ref, sh_ref, sc_ref, wg_ref, wu_ref, h_ref, hm_sc):
    @pl.when(pl.program_id(2) == 0)
    def _():
        hm_sc[...] = _modulate(x_ref[0], sh_ref[0], sc_ref[0]).astype(BF16)

    hm = hm_sc[...]
    g = _dot(hm, wg_ref[0].astype(BF16))
    u = _dot(hm, wu_ref[0].astype(BF16))
    h_ref[0] = (jax.nn.silu(g) * u).astype(BF16)


def _ffn_down_body(h_ref, x_ref, gate_ref, wd_ref, lg_ref, lb_ref, o_ref, y_sc, *, n_n):
    n = pl.program_id(2)
    y_sc[n] = _dot(h_ref[0], wd_ref[...])

    @pl.when(n == n_n - 1)
    def _():
        y = jnp.concatenate([y_sc[c] for c in range(n_n)], axis=-1)
        o_ref[0] = _deepnorm(x_ref[0], y, gate_ref[0], lg_ref[...], lb_ref[...])


def _ffn(x, mod, k, layer, w_gate, w_up, w_down, ln_g, ln_b):
    tm, tf = 2048, 256
    shift, scale, gate = _mod_specs(k, 3)
    h = pl.pallas_call(
        _ffn_up_body,
        grid=(BATCH, SEQ // tm, D_FF // tf),
        in_specs=[pl.BlockSpec((1, tm, D_MODEL), lambda b, i, f: (b, i, 0), pipeline_mode=pl.Buffered(1)),
                  shift, scale,
                  pl.BlockSpec((1, D_MODEL, tf), lambda b, i, f: (layer, 0, f)),
                  pl.BlockSpec((1, D_MODEL, tf), lambda b, i, f: (layer, 0, f))],
        out_specs=pl.BlockSpec((1, tm, tf), lambda b, i, f: (b, i, f)),
        out_shape=jax.ShapeDtypeStruct((BATCH, SEQ, D_FF), BF16),
        scratch_shapes=[pltpu.VMEM((tm, D_MODEL), BF16)],
        compiler_params=_params(("parallel", "parallel", "arbitrary")),
        name="ffn_up",
    )(x, mod, mod, w_gate, w_up)

    tm, tn = 512, 512
    n_n = D_MODEL // tn
    rows = pl.BlockSpec((1, tm, D_MODEL), lambda b, i, n: (b, i, 0))
    vec = pl.BlockSpec((1, D_MODEL), lambda b, i, n: (0, 0))
    return pl.pallas_call(
        functools.partial(_ffn_down_body, n_n=n_n),
        grid=(BATCH, SEQ // tm, n_n),
        in_specs=[pl.BlockSpec((1, tm, D_FF), lambda b, i, n: (b, i, 0)),
                  rows,
                  gate,
                  pl.BlockSpec((D_FF, tn), lambda b, i, n: (0, n)),
                  vec, vec],
        out_specs=rows,
        out_shape=jax.ShapeDtypeStruct((BATCH, SEQ, D_MODEL), F32),
        scratch_shapes=[pltpu.VMEM((n_n, tm, tn), F32)],
        compiler_params=_params(("parallel", "parallel", "arbitrary")),
        name="ffn_down",
    )(h, x, mod, w_down.astype(BF16), ln_g.reshape(1, -1), ln_b.reshape(1, -1))


def _pool_body(x_ref, halo_ref, sh_ref, sc_ref, gate_ref, pw_ref, ps_ref, wo_ref, lg_ref, lb_ref,
               o_ref, *, tm):
    i = pl.program_id(1)
    x = x_ref[0]
    hm = _modulate(x, sh_ref[0], sc_ref[0])
    halo = jnp.where(i > 0, _modulate(halo_ref[0], sh_ref[0], sc_ref[0]), 0.0)
    ext = jnp.concatenate([halo, hm], axis=0)
    ext_hi = ext.astype(BF16)
    ext_lo = (ext - ext_hi.astype(F32)).astype(BF16)
    t_loc = lax.broadcasted_iota(jnp.int32, (tm, tm + POOL_HALO), 0)
    s_loc = lax.broadcasted_iota(jnp.int32, (tm, tm + POOL_HALO), 1)
    lag = t_loc + POOL_HALO - s_loc
    t_glob = i * tm + lax.broadcasted_iota(jnp.int32, (tm, 1), 0)
    outs = []
    for gi, w in enumerate(POOL_WINDOWS):
        cs = slice(gi * POOL_GROUP_DIM, (gi + 1) * POOL_GROUP_DIM)
        band = jnp.where(lag >= 0, jnp.where(lag < w, 1.0, 0.0), 0.0).astype(BF16)
        wsum = _dot(band, ext_hi[:, cs]) + _dot(band, ext_lo[:, cs])
        count = jnp.minimum(t_glob + 1, w).astype(F32)
        d = (wsum / count - hm[:, cs]).astype(BF16)
        outs.append(_dot(d, pw_ref[gi]))
    pooled = (jnp.concatenate(outs, axis=-1) * ps_ref[...]).astype(BF16)
    y = _dot(pooled, wo_ref[...])
    o_ref[0] = _deepnorm(x, y, gate_ref[0], lg_ref[...], lb_ref[...])


def _pool_mixer(x, mod, k, pool_w, pool_scale, w_out_c, ln_g, ln_b):
    tm = 512
    per = tm // POOL_HALO
    shift, scale, gate = _mod_specs(k, 2)
    rows = pl.BlockSpec((1, tm, D_MODEL), lambda b, i: (b, i, 0))
    halo = pl.BlockSpec((1, POOL_HALO, D_MODEL), lambda b, i: (b, jnp.maximum(i * per - 1, 0), 0))
    return pl.pallas_call(
        functools.partial(_pool_body, tm=tm),
        grid=(BATCH, SEQ // tm),
        in_specs=[rows, halo, shift, scale, gate,
                  _const_spec((len(POOL_WINDOWS), POOL_GROUP_DIM, POOL_GROUP_DIM)),
                  _const_spec((1, D_MODEL)), _const_spec((D_MODEL, D_MODEL)),
                  _const_spec((1, D_MODEL)), _const_spec((1, D_MODEL))],
        out_specs=rows,
        out_shape=jax.ShapeDtypeStruct((BATCH, SEQ, D_MODEL), F32),
        compiler_params=_params(("parallel", "parallel")),
        name="pool_mixer",
    )(x, x, mod, mod, mod, pool_w.astype(BF16), pool_scale.reshape(1, -1),
      w_out_c.astype(BF16), ln_g.reshape(1, -1), ln_b.reshape(1, -1))


def _router_body(x_ref, sh_ref, sc_ref, wr_ref, g_ref, hm_ref):
    hm = _modulate(x_ref[0], sh_ref[0], sc_ref[0])
    hm_ref[0] = hm
    hm_hi = hm.astype(BF16)
    hm_lo = (hm - hm_hi.astype(F32)).astype(BF16)
    w = wr_ref[...]
    w_hi = w.astype(BF16)
    w_lo = (w - w_hi.astype(F32)).astype(BF16)
    logits = _dot(hm_hi, w_hi) + (_dot(hm_lo, w_hi) + _dot(hm_hi, w_lo))
    lane = lax.broadcasted_iota(jnp.int32, logits.shape, 1).astype(F32)
    logits = jnp.where(lane < N_EXPERTS, logits, -jnp.inf)
    v1 = jnp.max(logits, axis=-1, keepdims=True)
    i1 = jnp.min(jnp.where(logits == v1, lane, float(LANES)), axis=-1, keepdims=True)
    rest = jnp.where(lane == i1, -jnp.inf, logits)
    v2 = jnp.max(rest, axis=-1, keepdims=True)
    i2 = jnp.min(jnp.where(rest == v2, lane, float(LANES)), axis=-1, keepdims=True)
    e2 = jnp.exp(v2 - v1)
    den = 1.0 + e2
    g_ref[0] = (jnp.where(lane == 0.0, i1, 0.0) + jnp.where(lane == 1.0, i2, 0.0)
                + jnp.where(lane == 2.0, 1.0 / den, 0.0) + jnp.where(lane == 3.0, e2 / den, 0.0))


def _router(x, mod, k, w_router):
    tm = 512
    shift, scale, _ = _mod_specs(k, 2)
    wr = jnp.pad(w_router, ((0, 0), (0, LANES - N_EXPERTS)))
    rows = pl.BlockSpec((1, tm, D_MODEL), lambda b, i: (b, i, 0))
    return pl.pallas_call(
        _router_body,
        grid=(BATCH, SEQ // tm),
        in_specs=[rows, shift, scale, _const_spec((D_MODEL, LANES))],
        out_specs=(pl.BlockSpec((1, tm, LANES), lambda b, i: (b, i, 0)), rows),
        out_shape=(jax.ShapeDtypeStruct((BATCH, SEQ, LANES), F32),
                   jax.ShapeDtypeStruct((BATCH, SEQ, D_MODEL), F32)),
        compiler_params=_params(("parallel", "parallel")),
        name="router",
    )(x, mod, mod, wr)


N_TOKENS = BATCH * SEQ
N_ASSIGN = 2 * N_TOKENS
MOE_BLOCK = 1024
MOE_SUB = 256
MOE_TILE = N_ASSIGN // N_EXPERTS + MOE_SUB
MOE_TILES = -(-(N_ASSIGN + N_EXPERTS * (MOE_TILE - 1)) // MOE_TILE)
MOE_ROWS = MOE_TILES * MOE_TILE
GATHER_ROWS = MOE_TILE // 3
assert MOE_TILE % GATHER_ROWS == 0 and GATHER_ROWS % 8 == 0


def _route_plan(routing):
    e_flat = routing.reshape(N_TOKENS, LANES)[:, :2].astype(jnp.int32).T.reshape(N_ASSIGN)
    onehot = (e_flat[:, None] == jnp.arange(N_EXPERTS, dtype=jnp.int32)[None, :]).astype(jnp.int32)
    csum = jnp.cumsum(onehot, axis=0)
    counts = csum[-1]
    rank = jnp.sum(csum * onehot, axis=1) - 1
    padded = (counts + MOE_TILE - 1) // MOE_TILE * MOE_TILE
    group_end = jnp.cumsum(padded)
    group_start = group_end - padded
    pos = (jnp.sum(onehot * group_start[None, :], axis=1) + rank).astype(jnp.int32)
    tok = jnp.tile(jnp.arange(N_TOKENS, dtype=jnp.int32), 2)
    src = jnp.zeros((MOE_ROWS,), jnp.int32).at[pos].set(tok, unique_indices=True)
    tile_start = jnp.arange(MOE_TILES, dtype=jnp.int32) * MOE_TILE
    n_used = group_end[-1] // MOE_TILE
    tile_expert = jnp.sum((tile_start[:, None] >= group_end[None, :]).astype(jnp.int32), axis=1)
    last_expert = jnp.max(jnp.where(counts > 0, jnp.arange(N_EXPERTS, dtype=jnp.int32), 0))
    tile_expert = jnp.minimum(tile_expert, last_expert)
    tile_rows = jnp.clip(counts[tile_expert] - (tile_start - group_start[tile_expert]), 0, MOE_TILE)
    tile_rows = jnp.where(jnp.arange(MOE_TILES) < n_used, tile_rows, 0).astype(jnp.int32)
    return pos, src, tile_expert.astype(jnp.int32), tile_rows, n_used.reshape(1).astype(jnp.int32)


GATHER_UNROLL = 8


def _gather_body(src_ref, rows_ref, hm_hbm, o_ref, buf, sem, *, tg):
    i = pl.program_id(0)
    base = i * tg
    per = MOE_TILE // tg
    n_valid = jnp.clip(rows_ref[i // per] - (i % per) * tg, 0, tg)

    @pl.when(n_valid < tg)
    def _():
        buf[...] = jnp.zeros(buf.shape, F32)

    def issue_row(r, queue):
        t = src_ref[base + r]
        pltpu.make_async_copy(hm_hbm.at[pl.ds(t, 1)], buf.at[pl.ds(r, 1)], sem).start(priority=queue)

    def issue_group(g, carry):
        for u in range(GATHER_UNROLL):
            issue_row(g * GATHER_UNROLL + u, u % 2)
        return carry

    def issue_one(r, carry):
        issue_row(r, 0)
        return carry

    n_groups = n_valid // GATHER_UNROLL
    n_grouped = pl.multiple_of(n_groups * GATHER_UNROLL, GATHER_UNROLL)
    lax.fori_loop(0, n_groups, issue_group, 0)
    lax.fori_loop(n_grouped, n_valid, issue_one, 0)

    @pl.when(n_groups > 0)
    def _():
        pltpu.make_async_copy(hm_hbm.at[pl.ds(0, n_grouped)], buf.at[pl.ds(0, n_grouped)], sem).wait()

    def wait_one(r, carry):
        pltpu.make_async_copy(hm_hbm.at[pl.ds(0, 1)], buf.at[pl.ds(r, 1)], sem).wait()
        return carry

    lax.fori_loop(n_grouped, n_valid, wait_one, 0)

    o_ref[...] = buf[...].astype(BF16)


def _gather_rows(src, tile_rows, n_used, hm):
    tg = GATHER_ROWS
    grid_spec = pltpu.PrefetchScalarGridSpec(
        num_scalar_prefetch=2,
        grid=(n_used[0] * (MOE_TILE // tg),),
        in_specs=[pl.BlockSpec(memory_space=pl.ANY)],
        out_specs=pl.BlockSpec((tg, D_MODEL), lambda i, src, tr: (i, 0)),
        scratch_shapes=[pltpu.VMEM((tg, D_MODEL), F32), pltpu.SemaphoreType.DMA(())],
    )
    return pl.pallas_call(
        functools.partial(_gather_body, tg=tg),
        grid_spec=grid_spec,
        out_shape=jax.ShapeDtypeStruct((MOE_ROWS, D_MODEL), BF16),
        compiler_params=_params(("arbitrary",)),
        name="moe_gather",
    )(src, tile_rows, hm.reshape(N_TOKENS, D_MODEL))


def _for_valid_blocks(rows, tm, block_fn, empty_fn):
    for b0 in range(0, tm, MOE_BLOCK):
        b1 = min(b0 + MOE_BLOCK, tm)
        whole = b1 if b1 - b0 == MOE_BLOCK else tm + 1
        if whole <= tm:
            pl.when(rows >= whole)(functools.partial(block_fn, slice(b0, b1)))

        for r0 in range(b0, b1, MOE_SUB):
            rs = slice(r0, r0 + MOE_SUB)

            @pl.when(jnp.logical_and(rows < whole, r0 < rows))
            def _():
                block_fn(rs)

            @pl.when(r0 >= rows)
            def _():
                empty_fn(rs)


def _gup_body(te_ref, rows_ref, nu_ref, xs_ref, wg_ref, wu_ref, h_ref, wgb, wub, *, tm):
    rows = rows_ref[pl.program_id(0)]

    @pl.when(rows > 0)
    def _():
        wgb[...] = wg_ref[0].astype(BF16)
        wub[...] = wu_ref[0].astype(BF16)

    def block(rs):
        xsb = xs_ref[rs, :]
        h_ref[rs, :] = (jax.nn.silu(_dot(xsb, wgb[...])) * _dot(xsb, wub[...])).astype(BF16)

    def empty(rs):
        h_ref[rs, :] = jnp.zeros((rs.stop - rs.start, h_ref.shape[1]), BF16)

    _for_valid_blocks(rows, tm, block, empty)


def _gdown_body(te_ref, rows_ref, nu_ref, h_ref, wd_ref, o_ref, wdb, *, tm):
    rows = rows_ref[pl.program_id(0)]

    @pl.when(rows > 0)
    def _():
        wdb[...] = wd_ref[0].astype(BF16)

    def block(rs):
        o_ref[rs, :] = _dot(h_ref[rs, :], wdb[...])

    def empty(rs):
        o_ref[rs, :] = jnp.zeros((rs.stop - rs.start, o_ref.shape[1]), F32)

    _for_valid_blocks(rows, tm, block, empty)


def _grouped_ffn(xs, tile_expert, tile_rows, n_used, layer, w_gate, w_up, w_down):
    tm = MOE_TILE
    first = layer * N_EXPERTS
    n_all = w_gate.shape[0] * N_EXPERTS

    def tile(i, nu):
        return jnp.minimum(i, nu[0] - 1)

    def chunk(i, c, nu, n_chunks):
        return jnp.where(i < nu[0], c, n_chunks - 1)

    tf = 256
    n_f = D_FF_EXPERT // tf
    w_up_spec = pl.BlockSpec((1, D_MODEL, tf),
                             lambda i, f, te, tr, nu: (first + te[i], 0, chunk(i, f, nu, n_f)))
    up_spec = pltpu.PrefetchScalarGridSpec(
        num_scalar_prefetch=3,
        grid=(n_used[0], n_f),
        in_specs=[pl.BlockSpec((tm, D_MODEL), lambda i, f, te, tr, nu: (tile(i, nu), 0)),
                  w_up_spec, w_up_spec],
        out_specs=pl.BlockSpec((tm, tf), lambda i, f, te, tr, nu: (i, f)),
        scratch_shapes=[pltpu.VMEM((D_MODEL, tf), BF16), pltpu.VMEM((D_MODEL, tf), BF16)],
    )
    h = pl.pallas_call(
        functools.partial(_gup_body, tm=tm),
        grid_spec=up_spec,
        out_shape=jax.ShapeDtypeStruct((MOE_ROWS, D_FF_EXPERT), BF16),
        compiler_params=_params(("arbitrary", "arbitrary")),
        name="moe_up",
    )(tile_expert, tile_rows, n_used, xs,
      w_gate.reshape(n_all, D_MODEL, D_FF_EXPERT), w_up.reshape(n_all, D_MODEL, D_FF_EXPERT))

    tn = 512
    n_n = D_MODEL // tn
    down_spec = pltpu.PrefetchScalarGridSpec(
        num_scalar_prefetch=3,
        grid=(n_used[0], n_n),
        in_specs=[pl.BlockSpec((tm, D_FF_EXPERT), lambda i, n, te, tr, nu: (tile(i, nu), 0)),
                  pl.BlockSpec((1, D_FF_EXPERT, tn),
                               lambda i, n, te, tr, nu: (first + te[i], 0, chunk(i, n, nu, n_n)))],
        out_specs=pl.BlockSpec((tm, tn), lambda i, n, te, tr, nu: (i, n)),
        scratch_shapes=[pltpu.VMEM((D_FF_EXPERT, tn), BF16)],
    )
    return pl.pallas_call(
        functools.partial(_gdown_body, tm=tm),
        grid_spec=down_spec,
        out_shape=jax.ShapeDtypeStruct((MOE_ROWS, D_MODEL), F32),
        compiler_params=_params(("arbitrary", "arbitrary")),
        name="moe_down",
    )(tile_expert, tile_rows, n_used, h, w_down.reshape(n_all, D_FF_EXPERT, D_MODEL))


def _combine_body(pos_ref, ys_hbm, r_ref, x_ref, gate_ref, lg_ref, lb_ref, o_ref, buf, sem, *, tc):
    base = (pl.program_id(0) * (SEQ // tc) + pl.program_id(1)) * tc

    def issue(r, carry):
        for k in range(2):
            p = pos_ref[k * N_TOKENS + base + r]
            pltpu.make_async_copy(ys_hbm.at[pl.ds(p, 1)], buf.at[k, pl.ds(r, 1)],
                                  sem.at[k]).start(priority=k)
        return carry

    lax.fori_loop(0, tc, issue, 0, unroll=4)
    for k in range(2):
        pltpu.make_async_copy(ys_hbm.at[pl.ds(0, tc)], buf.at[k], sem.at[k]).wait()
    routing = r_ref[0]
    y = routing[:, 2:3] * buf[0] + routing[:, 3:4] * buf[1]
    o_ref[0] = _deepnorm(x_ref[0], y, gate_ref[0], lg_ref[...], lb_ref[...])


def _combine(pos, ys, routing, x, mod, k, ln_g, ln_b):
    tc = 512
    rows = pl.BlockSpec((1, tc, D_MODEL), lambda b, i, pos: (b, i, 0))
    vec = pl.BlockSpec((1, D_MODEL), lambda b, i, pos: (0, 0))
    grid_spec = pltpu.PrefetchScalarGridSpec(
        num_scalar_prefetch=1,
        grid=(BATCH, SEQ // tc),
        in_specs=[pl.BlockSpec(memory_space=pl.ANY),
                  pl.BlockSpec((1, tc, LANES), lambda b, i, pos: (b, i, 0)),
                  rows,
                  pl.BlockSpec((1, 1, D_MODEL), lambda b, i, pos: (k * 8 + b, 0, 2)),
                  vec, vec],
        out_specs=rows,
        scratch_shapes=[pltpu.VMEM((2, tc, D_MODEL), F32), pltpu.SemaphoreType.DMA((2,))],
    )
    return pl.pallas_call(
        functools.partial(_combine_body, tc=tc),
        grid_spec=grid_spec,
        out_shape=jax.ShapeDtypeStruct((BATCH, SEQ, D_MODEL), F32),
        compiler_params=_params(("arbitrary", "arbitrary")),
        name="moe_combine",
    )(pos, ys, routing, x, mod, ln_g.reshape(1, -1), ln_b.reshape(1, -1))


def _moe(x, mod, k, w_router, layer, w_gate, w_up, w_down, ln_g, ln_b):
    routing, hm = _router(x, mod, k, w_router)
    pos, src, tile_expert, tile_rows, n_used = _route_plan(routing)
    xs = _gather_rows(src, tile_rows, n_used, hm)
    ys = _grouped_ffn(xs, tile_expert, tile_rows, n_used, layer, w_gate, w_up, w_down)
    return _combine(pos, ys, routing, x, mod, k, ln_g, ln_b)


def kernel(x, c, positions, ada_w, ada_b, ln_g, ln_b, w_in_ab, q_norm_g, w_q_up, kv_norm_g, w_kv_up, sgu_norm_g, sgu_norm_b, sgu_w, sgu_b, w_out_ab, ffn_w_gate, ffn_w_up, ffn_w_down, pool_w, pool_scale, w_out_c, router_w, moe_w_gate, moe_w_up, moe_w_down):
    mod = _ada_modulation(c, ada_w, ada_b)
    tables = _rope_tables(positions)
    for l in range(DEPTH):
        j = l // 2
        k_tok, k_ch = 2 * l, 2 * l + 1
        if l % 2 == 0:
            q, k, vt, sgu = _front(x, mod, k_tok, tables, w_in_ab[j], q_norm_g[j], w_q_up[j],
                                   kv_norm_g[j], w_kv_up[j],
                                   sgu_norm_g[j], sgu_norm_b[j], sgu_w[j], sgu_b[j])
            attn = _attention(q, k, vt)
            x = _outproj(attn, sgu, x, mod, k_tok, w_out_ab[j], ln_g[l, 0], ln_b[l, 0])
            x = _ffn(x, mod, k_ch, j, ffn_w_gate, ffn_w_up, ffn_w_down[j], ln_g[l, 1], ln_b[l, 1])
        else:
            x = _pool_mixer(x, mod, k_tok, pool_w[j], pool_scale[j], w_out_c[j], ln_g[l, 0], ln_b[l, 0])
            x = _moe(x, mod, k_ch, router_w[j], j, moe_w_gate, moe_w_up, moe_w_down,
                     ln_g[l, 1], ln_b[l, 1])
    return x
```

```python
import functools

import jax
import jax.numpy as jnp
import numpy as np
from jax import lax
from jax.experimental import pallas as pl
from jax.experimental.pallas import tpu as pltpu

F32 = jnp.float32
BF16 = jnp.bfloat16

D_MODEL = 2048
BATCH = 2
SEQ = 4096
DEPTH = 4
MLA_HEADS = 8
QK_NOPE_DIM = 128
QK_ROPE_DIM = 64
V_HEAD_DIM = 128
Q_LORA_RANK = 768
KV_LORA_RANK = 512
ROPE_THETA = 10000.0
SGU_GROUPS = 8
SGU_GROUP_DIM = 128
SGU_WIDTH = SGU_GROUPS * SGU_GROUP_DIM
CHUNK = 128
ATTN_WIDTH = MLA_HEADS * V_HEAD_DIM
POOL_WINDOWS = (2, 4, 8, 16)
POOL_GROUP_DIM = D_MODEL // 4
D_FF = 5632
N_EXPERTS = 8
D_FF_EXPERT = 2816
DEEPNORM_ALPHA = (2 * DEPTH) ** 0.25
LN_EPS = 1e-5
RMS_EPS = 1e-6
SM_SCALE = (QK_NOPE_DIM + QK_ROPE_DIM) ** -0.5
LOG2_E = 1.4426950408889634

LANES = 128
HEAD_LANES = 2 * LANES
QK_WIDTH = MLA_HEADS * HEAD_LANES
VMEM_LIMIT = 60 * 1024 * 1024
POOL_HALO = 128


def _params(semantics):
    return pltpu.CompilerParams(dimension_semantics=semantics, vmem_limit_bytes=VMEM_LIMIT)


def _const_spec(shape):
    nd = len(shape)
    return pl.BlockSpec(shape, lambda *_: (0,) * nd, pipeline_mode=pl.Buffered(1))


def _dot(a, b):
    return jnp.dot(a, b, preferred_element_type=F32)


def _modulate(x, shift, scale):
    return x * (1.0 + scale) + shift


def _deepnorm(x, y, gate, g, b):
    r = DEEPNORM_ALPHA * x + (1.0 + gate) * y
    mu = jnp.mean(r, axis=-1, keepdims=True)
    rc = r - mu
    var = jnp.mean(rc * rc, axis=-1, keepdims=True)
    return rc * lax.rsqrt(var + LN_EPS) * g + b


def _rms(x, g):
    return x * lax.rsqrt(jnp.mean(x * x, axis=-1, keepdims=True) + RMS_EPS) * g


def _rope(r, cos_t, sin_hi, sin_lo):
    return (r * cos_t + pltpu.roll(r, QK_ROPE_DIM // 2, 1) * sin_hi
            + pltpu.roll(r, LANES - QK_ROPE_DIM // 2, 1) * sin_lo)


def _mod_specs(k, n_grid):
    def spec(part):
        if n_grid == 2:
            return pl.BlockSpec((1, 1, D_MODEL), lambda b, i: (k * 8 + b, 0, part))
        if n_grid == 3:
            return pl.BlockSpec((1, 1, D_MODEL), lambda b, i, f: (k * 8 + b, 0, part))
        return pl.BlockSpec((1, 1, D_MODEL), lambda b, i, e, f: (k * 8 + b, 0, part))
    return spec(0), spec(1), spec(2)


def _ada_body(c_ref, w_ref, b_ref, o_ref):
    s = jax.nn.silu(c_ref[...]).astype(BF16)
    o_ref[0] = _dot(s, w_ref[0].astype(BF16)) + b_ref[0]


def _ada_modulation(c, ada_w, ada_b):
    n_mod = 2 * DEPTH
    tn = 1024
    c_pad = jnp.pad(c, ((0, 8 - BATCH), (0, 0)))
    out = pl.pallas_call(
        _ada_body,
        grid=(n_mod, 3 * D_MODEL // tn),
        in_specs=[
            pl.BlockSpec((8, D_MODEL), lambda k, n: (0, 0)),
            pl.BlockSpec((1, D_MODEL, tn), lambda k, n: (k, 0, n)),
            pl.BlockSpec((1, 1, tn), lambda k, n: (k, 0, n)),
        ],
        out_specs=pl.BlockSpec((1, 8, tn), lambda k, n: (k, 0, n)),
        out_shape=jax.ShapeDtypeStruct((n_mod, 8, 3 * D_MODEL), F32),
        compiler_params=_params(("parallel", "parallel")),
        name="ada_mod",
    )(c_pad, ada_w.reshape(n_mod, D_MODEL, 3 * D_MODEL), ada_b.reshape(n_mod, 1, 3 * D_MODEL))
    return out.reshape(n_mod * 8, 1, 3 * D_MODEL)


def _rope_table_body(pos_ref, inv_ref, cos_ref, hi_ref, lo_ref):
    ang = pos_ref[0].astype(F32) * inv_ref[...]
    lane = lax.broadcasted_iota(jnp.int32, ang.shape, 1)
    cos = jnp.cos(ang)
    sin = jnp.sin(ang)
    half = QK_ROPE_DIM // 2
    cos_ref[0] = jnp.where(lane < QK_ROPE_DIM, cos, 0.0)
    hi_ref[0] = jnp.where(lane < half, 0.0, jnp.where(lane < QK_ROPE_DIM, sin, 0.0))
    lo_ref[0] = jnp.where(lane < half, -sin, 0.0)


def _rope_tables(positions):
    tm = 512
    half = QK_ROPE_DIM // 2
    inv = ROPE_THETA ** (-jnp.arange(0, QK_ROPE_DIM, 2, dtype=F32) / QK_ROPE_DIM)
    inv_lanes = jnp.concatenate([inv, inv, jnp.zeros((LANES - 2 * half,), F32)]).reshape(1, LANES)
    tab = jax.ShapeDtypeStruct((BATCH, SEQ, LANES), F32)
    spec = pl.BlockSpec((1, tm, LANES), lambda b, i: (b, i, 0))
    return pl.pallas_call(
        _rope_table_body,
        grid=(BATCH, SEQ // tm),
        in_specs=[pl.BlockSpec((1, tm, 1), lambda b, i: (b, i, 0)),
                  pl.BlockSpec((1, LANES), lambda b, i: (0, 0))],
        out_specs=(spec, spec, spec),
        out_shape=(tab, tab, tab),
        compiler_params=_params(("parallel", "parallel")),
        name="rope_tables",
    )(positions.reshape(BATCH, SEQ, 1), inv_lanes)


def _front_body(x_ref, sh_ref, sc_ref, wq_ref, wkv_ref, wkr_ref, wu_ref, wv_ref,
                cos_ref, hi_ref, lo_ref, qg_ref, kvg_ref, wqn_ref, wqr_ref, wkn_ref, wvt_ref,
                sg_ref, sb_ref, sw_ref, sbias_ref,
                q_ref, k_ref, vt_ref, sgu_ref, *, tm):
    hm = _modulate(x_ref[0], sh_ref[0], sc_ref[0]).astype(BF16)
    _qkv_compute(_dot(hm, wq_ref[...]), _dot(hm, wkv_ref[...]), _dot(hm, wkr_ref[...]),
                 cos_ref[0], hi_ref[0], lo_ref[0], qg_ref, kvg_ref,
                 wqn_ref, wqr_ref, wkn_ref, wvt_ref, q_ref, k_ref, vt_ref)
    _sgu_compute(_dot(hm, wu_ref[...]), _dot(hm, wv_ref[...]),
                 sg_ref, sb_ref, sw_ref, sbias_ref, sgu_ref, tm)


def _front(x, mod, k, tables, w_in, q_norm_g, w_q_up, kv_norm_g, w_kv_up,
           sgu_norm_g, sgu_norm_b, sgu_w, sgu_b):
    tm = 512
    a, b, c_, d = (Q_LORA_RANK, Q_LORA_RANK + KV_LORA_RANK,
                   Q_LORA_RANK + KV_LORA_RANK + QK_ROPE_DIM,
                   Q_LORA_RANK + KV_LORA_RANK + QK_ROPE_DIM + SGU_WIDTH)
    wq = w_in[:, :a].astype(BF16)
    wkv = w_in[:, a:b].astype(BF16)
    wkr = jnp.pad(w_in[:, b:c_], ((0, 0), (0, LANES - QK_ROPE_DIM))).astype(BF16)
    wu = w_in[:, c_:d].astype(BF16)
    wv = w_in[:, d:].astype(BF16)
    in_widths = (Q_LORA_RANK, KV_LORA_RANK, LANES, SGU_WIDTH, SGU_WIDTH)

    wq_up = w_q_up.reshape(Q_LORA_RANK, MLA_HEADS, QK_NOPE_DIM + QK_ROPE_DIM)
    wqn = wq_up[:, :, :QK_NOPE_DIM].reshape(Q_LORA_RANK, MLA_HEADS * LANES).astype(BF16)
    wqr = jnp.pad(wq_up[:, :, QK_NOPE_DIM:], ((0, 0), (0, 0), (0, LANES - QK_ROPE_DIM)))
    wqr = wqr.reshape(Q_LORA_RANK, MLA_HEADS * LANES).astype(BF16)
    wkv_up = w_kv_up.reshape(KV_LORA_RANK, MLA_HEADS, QK_NOPE_DIM + V_HEAD_DIM)
    wkn = wkv_up[:, :, :QK_NOPE_DIM].reshape(KV_LORA_RANK, MLA_HEADS * LANES).astype(BF16)
    wvt = wkv_up[:, :, QK_NOPE_DIM:].reshape(KV_LORA_RANK, ATTN_WIDTH).T.astype(BF16)
    sgu_bias = jnp.repeat(sgu_b.T, SGU_GROUP_DIM, axis=1)

    shift, scale, _ = _mod_specs(k, 2)

    def rows(w):
        return pl.BlockSpec((1, tm, w), lambda b, i: (b, i, 0))

    return pl.pallas_call(
        functools.partial(_front_body, tm=tm),
        grid=(BATCH, SEQ // tm),
        in_specs=[rows(D_MODEL), shift, scale]
                 + [_const_spec((D_MODEL, w)) for w in in_widths]
                 + [rows(LANES), rows(LANES), rows(LANES),
                    _const_spec((1, Q_LORA_RANK)), _const_spec((1, KV_LORA_RANK)),
                    _const_spec((Q_LORA_RANK, MLA_HEADS * LANES)),
                    _const_spec((Q_LORA_RANK, MLA_HEADS * LANES)),
                    _const_spec((KV_LORA_RANK, MLA_HEADS * LANES)),
                    _const_spec((ATTN_WIDTH, KV_LORA_RANK)),
                    _const_spec((1, SGU_WIDTH)), _const_spec((1, SGU_WIDTH)),
                    _const_spec((SGU_GROUPS, CHUNK, CHUNK)), _const_spec((CHUNK, SGU_WIDTH))],
        out_specs=(rows(QK_WIDTH), rows(QK_WIDTH),
                   pl.BlockSpec((1, ATTN_WIDTH, tm), lambda b, i: (b, 0, i)),
                   rows(SGU_WIDTH)),
        out_shape=(jax.ShapeDtypeStruct((BATCH, SEQ, QK_WIDTH), BF16),
                   jax.ShapeDtypeStruct((BATCH, SEQ, QK_WIDTH), BF16),
                   jax.ShapeDtypeStruct((BATCH, ATTN_WIDTH, SEQ), BF16),
                   jax.ShapeDtypeStruct((BATCH, SEQ, SGU_WIDTH), BF16)),
        compiler_params=_params(("parallel", "parallel")),
        name="front_ab",
    )(x, mod, mod, wq, wkv, wkr, wu, wv, *tables,
      q_norm_g.reshape(1, -1), kv_norm_g.reshape(1, -1), wqn, wqr, wkn, wvt,
      sgu_norm_g.reshape(1, -1), sgu_norm_b.reshape(1, -1), sgu_w, sgu_bias)


def _qkv_compute(cq, ckv, kr, cos_t, sin_hi, sin_lo, qg_ref, kvg_ref,
                 wqn_ref, wqr_ref, wkn_ref, wvt_ref, q_ref, k_ref, vt_ref):
    cqn = _rms(cq, qg_ref[...]).astype(BF16)
    qn = _dot(cqn, wqn_ref[...])
    qr = _dot(cqn, wqr_ref[...])
    ckvn = _rms(ckv, kvg_ref[...]).astype(BF16)
    kn = _dot(ckvn, wkn_ref[...])
    vt_ref[0] = lax.dot_general(wvt_ref[...], ckvn, (((1,), (1,)), ((), ())),
                                preferred_element_type=F32).astype(BF16)
    k_rope = _rope(kr, cos_t, sin_hi, sin_lo).astype(BF16)
    for h in range(MLA_HEADS):
        nope = slice(h * LANES, (h + 1) * LANES)
        lo = h * HEAD_LANES
        q_ref[0, :, lo:lo + LANES] = qn[:, nope].astype(BF16)
        q_ref[0, :, lo + LANES:lo + HEAD_LANES] = _rope(qr[:, nope], cos_t, sin_hi, sin_lo).astype(BF16)
        k_ref[0, :, lo:lo + LANES] = kn[:, nope].astype(BF16)
        k_ref[0, :, lo + LANES:lo + HEAD_LANES] = k_rope


def _attn_body(qi_ref, kj_ref, q_ref, k_ref, vt_ref, o_ref, m_sc, l_sc, acc_sc, *, tq):
    p = pl.program_id(1)
    i = qi_ref[p]
    j = kj_ref[p]

    @pl.when(j == 0)
    def _():
        m_sc[...] = jnp.full(m_sc.shape, -jnp.inf, F32)
        l_sc[...] = jnp.zeros(l_sc.shape, F32)
        acc_sc[...] = jnp.zeros(acc_sc.shape, F32)

    def accumulate(diagonal):
        if diagonal:
            key = lax.broadcasted_iota(jnp.int32, (tq, tq), 0)
            qry = lax.broadcasted_iota(jnp.int32, (tq, tq), 1)
            visible = key <= qry
        for h in range(MLA_HEADS):
            qk = slice(h * HEAD_LANES, (h + 1) * HEAD_LANES)
            hv = slice(h * V_HEAD_DIM, (h + 1) * V_HEAD_DIM)
            s = lax.dot_general(k_ref[0, :, qk], q_ref[0, :, qk], (((1,), (1,)), ((), ())),
                                preferred_element_type=F32) * (SM_SCALE * LOG2_E)
            if diagonal:
                s = jnp.where(visible, s, -jnp.inf)
            m_prev = m_sc[h]
            m_new = jnp.maximum(m_prev, jnp.max(s, axis=0, keepdims=True))
            alpha = jnp.exp2(m_prev - m_new)
            pr = jnp.exp2(s - m_new)
            l_sc[h] = alpha * l_sc[h] + jnp.sum(pr, axis=0, keepdims=True)
            acc_sc[hv, :] = alpha * acc_sc[hv, :] + _dot(vt_ref[0, hv, :], pr.astype(BF16))
            m_sc[h] = m_new

    @pl.when(j < i)
    def _():
        accumulate(False)

    @pl.when(j == i)
    def _():
        accumulate(True)
        for h in range(MLA_HEADS):
            hv = slice(h * V_HEAD_DIM, (h + 1) * V_HEAD_DIM)
            o_ref[0, :, hv] = (acc_sc[hv, :] / l_sc[h]).T.astype(BF16)


def _attention(q, k, vt):
    tq = 512
    nb = SEQ // tq
    pairs = [(i, j) for i in range(nb) for j in range(i + 1)]
    qi = jnp.asarray(np.array([p[0] for p in pairs], np.int32))
    kj = jnp.asarray(np.array([p[1] for p in pairs], np.int32))
    grid_spec = pltpu.PrefetchScalarGridSpec(
        num_scalar_prefetch=2,
        grid=(BATCH, len(pairs)),
        in_specs=[pl.BlockSpec((1, tq, QK_WIDTH), lambda b, p, qi, kj: (b, qi[p], 0)),
                  pl.BlockSpec((1, tq, QK_WIDTH), lambda b, p, qi, kj: (b, kj[p], 0)),
                  pl.BlockSpec((1, ATTN_WIDTH, tq), lambda b, p, qi, kj: (b, 0, kj[p]))],
        out_specs=pl.BlockSpec((1, tq, ATTN_WIDTH), lambda b, p, qi, kj: (b, qi[p], 0)),
        scratch_shapes=[pltpu.VMEM((MLA_HEADS, 1, tq), F32),
                        pltpu.VMEM((MLA_HEADS, 1, tq), F32),
                        pltpu.VMEM((ATTN_WIDTH, tq), F32)],
    )
    return pl.pallas_call(
        functools.partial(_attn_body, tq=tq),
        grid_spec=grid_spec,
        out_shape=jax.ShapeDtypeStruct((BATCH, SEQ, ATTN_WIDTH), BF16),
        compiler_params=_params(("parallel", "arbitrary")),
        name="mla_attention",
    )(qi, kj, q, k, vt)


def _sgu_compute(zu, zv, g_ref, b_ref, w_ref, bias_ref, o_ref, tm):
    row = lax.broadcasted_iota(jnp.int32, (CHUNK, CHUNK), 0)
    col = lax.broadcasted_iota(jnp.int32, (CHUNK, CHUNK), 1)
    for g in range(SGU_GROUPS):
        gs = slice(g * SGU_GROUP_DIM, (g + 1) * SGU_GROUP_DIM)
        v = jax.nn.gelu(zv[:, gs])
        mu = jnp.mean(v, axis=-1, keepdims=True)
        vc = v - mu
        var = jnp.mean(vc * vc, axis=-1, keepdims=True)
        vn = (vc * lax.rsqrt(var + LN_EPS) * g_ref[:, gs] + b_ref[:, gs]).astype(BF16)
        w = jnp.where(col <= row, w_ref[g], 0.0).astype(BF16)
        for n in range(tm // CHUNK):
            rs = slice(n * CHUNK, (n + 1) * CHUNK)
            s = _dot(w, vn[rs]) + bias_ref[:, gs]
            o_ref[0, rs, gs] = (jax.nn.gelu(zu[rs, gs]) * s).astype(BF16)


def _outproj_body(a_ref, s_ref, x_ref, gate_ref, wa_ref, ws_ref, lg_ref, lb_ref, o_ref):
    y = _dot(a_ref[0], wa_ref[...]) + _dot(s_ref[0], ws_ref[...])
    o_ref[0] = _deepnorm(x_ref[0], y, gate_ref[0], lg_ref[...], lb_ref[...])


def _outproj(attn, sgu, x, mod, k, w_out, ln_g, ln_b):
    tm = 512
    wa = w_out[:ATTN_WIDTH].astype(BF16)
    ws = w_out[ATTN_WIDTH:].astype(BF16)
    _, _, gate = _mod_specs(k, 2)

    def rows(w):
        return pl.BlockSpec((1, tm, w), lambda b, i: (b, i, 0))

    return pl.pallas_call(
        _outproj_body,
        grid=(BATCH, SEQ // tm),
        in_specs=[rows(ATTN_WIDTH), rows(SGU_WIDTH), rows(D_MODEL), gate,
                  _const_spec((ATTN_WIDTH, D_MODEL)), _const_spec((SGU_WIDTH, D_MODEL)),
                  _const_spec((1, D_MODEL)), _const_spec((1, D_MODEL))],
        out_specs=rows(D_MODEL),
        out_shape=jax.ShapeDtypeStruct((BATCH, SEQ, D_MODEL), F32),
        compiler_params=_params(("parallel", "parallel")),
        name="outproj_ab",
    )(attn, sgu, x, mod, wa, ws, ln_g.reshape(1, -1), ln_b.reshape(1, -1))


def _ffn_up_body(x_ref, sh_ref, sc_ref, wg_ref, wu_ref, h_ref, hm_sc):
    @pl.when(pl.program_id(2) == 0)
    def _():
        hm_sc[...] = _modulate(x_ref[0], sh_ref[0], sc_ref[0]).astype(BF16)

    hm = hm_sc[...]
    g = _dot(hm, wg_ref[0].astype(BF16))
    u = _dot(hm, wu_ref[0].astype(BF16))
    h_ref[0] = (jax.nn.silu(g) * u).astype(BF16)


def _ffn_down_body(h_ref, x_ref, gate_ref, wd_ref, lg_ref, lb_ref, o_ref, y_sc, *, n_n):
    n = pl.program_id(2)
    y_sc[n] = _dot(h_ref[0], wd_ref[...])

    @pl.when(n == n_n - 1)
    def _():
        y = jnp.concatenate([y_sc[c] for c in range(n_n)], axis=-1)
        o_ref[0] = _deepnorm(x_ref[0], y, gate_ref[0], lg_ref[...], lb_ref[...])


def _ffn(x, mod, k, layer, w_gate, w_up, w_down, ln_g, ln_b):
    tm, tf = 2048, 256
    shift, scale, gate = _mod_specs(k, 3)
    h = pl.pallas_call(
        _ffn_up_body,
        grid=(BATCH, SEQ // tm, D_FF // tf),
        in_specs=[pl.BlockSpec((1, tm, D_MODEL), lambda b, i, f: (b, i, 0), pipeline_mode=pl.Buffered(1)),
                  shift, scale,
                  pl.BlockSpec((1, D_MODEL, tf), lambda b, i, f: (layer, 0, f)),
                  pl.BlockSpec((1, D_MODEL, tf), lambda b, i, f: (layer, 0, f))],
        out_specs=pl.BlockSpec((1, tm, tf), lambda b, i, f: (b, i, f)),
        out_shape=jax.ShapeDtypeStruct((BATCH, SEQ, D_FF), BF16),
        scratch_shapes=[pltpu.VMEM((tm, D_MODEL), BF16)],
        compiler_params=_params(("parallel", "parallel", "arbitrary")),
        name="ffn_up",
    )(x, mod, mod, w_gate, w_up)

    tm, tn = 512, 512
    n_n = D_MODEL // tn
    rows = pl.BlockSpec((1, tm, D_MODEL), lambda b, i, n: (b, i, 0))
    vec = pl.BlockSpec((1, D_MODEL), lambda b, i, n: (0, 0))
    return pl.pallas_call(
        functools.partial(_ffn_down_body, n_n=n_n),
        grid=(BATCH, SEQ // tm, n_n),
        in_specs=[pl.BlockSpec((1, tm, D_FF), lambda b, i, n: (b, i, 0)),
                  rows,
                  gate,
                  pl.BlockSpec((D_FF, tn), lambda b, i, n: (0, n)),
                  vec, vec],
        out_specs=rows,
        out_shape=jax.ShapeDtypeStruct((BATCH, SEQ, D_MODEL), F32),
        scratch_shapes=[pltpu.VMEM((n_n, tm, tn), F32)],
        compiler_params=_params(("parallel", "parallel", "arbitrary")),
        name="ffn_down",
    )(h, x, mod, w_down.astype(BF16), ln_g.reshape(1, -1), ln_b.reshape(1, -1))


def _pool_body(x_ref, halo_ref, sh_ref, sc_ref, gate_ref, pw_ref, ps_ref, wo_ref, lg_ref, lb_ref,
               o_ref, *, tm):
    i = pl.program_id(1)
    x = x_ref[0]
    hm = _modulate(x, sh_ref[0], sc_ref[0])
    halo = jnp.where(i > 0, _modulate(halo_ref[0], sh_ref[0], sc_ref[0]), 0.0)
    ext = jnp.concatenate([halo, hm], axis=0)
    ext_hi = ext.astype(BF16)
    ext_lo = (ext - ext_hi.astype(F32)).astype(BF16)
    t_loc = lax.broadcasted_iota(jnp.int32, (tm, tm + POOL_HALO), 0)
    s_loc = lax.broadcasted_iota(jnp.int32, (tm, tm + POOL_HALO), 1)
    lag = t_loc + POOL_HALO - s_loc
    t_glob = i * tm + lax.broadcasted_iota(jnp.int32, (tm, 1), 0)
    outs = []
    for gi, w in enumerate(POOL_WINDOWS):
        cs = slice(gi * POOL_GROUP_DIM, (gi + 1) * POOL_GROUP_DIM)
        band = jnp.where(lag >= 0, jnp.where(lag < w, 1.0, 0.0), 0.0).astype(BF16)
        wsum = _dot(band, ext_hi[:, cs]) + _dot(band, ext_lo[:, cs])
        count = jnp.minimum(t_glob + 1, w).astype(F32)
        d = (wsum / count - hm[:, cs]).astype(BF16)
        outs.append(_dot(d, pw_ref[gi]))
    pooled = (jnp.concatenate(outs, axis=-1) * ps_ref[...]).astype(BF16)
    y = _dot(pooled, wo_ref[...])
    o_ref[0] = _deepnorm(x, y, gate_ref[0], lg_ref[...], lb_ref[...])


def _pool_mixer(x, mod, k, pool_w, pool_scale, w_out_c, ln_g, ln_b):
    tm = 512
    per = tm // POOL_HALO
    shift, scale, gate = _mod_specs(k, 2)
    rows = pl.BlockSpec((1, tm, D_MODEL), lambda b, i: (b, i, 0))
    halo = pl.BlockSpec((1, POOL_HALO, D_MODEL), lambda b, i: (b, jnp.maximum(i * per - 1, 0), 0))
    return pl.pallas_call(
        functools.partial(_pool_body, tm=tm),
        grid=(BATCH, SEQ // tm),
        in_specs=[rows, halo, shift, scale, gate,
                  _const_spec((len(POOL_WINDOWS), POOL_GROUP_DIM, POOL_GROUP_DIM)),
                  _const_spec((1, D_MODEL)), _const_spec((D_MODEL, D_MODEL)),
                  _const_spec((1, D_MODEL)), _const_spec((1, D_MODEL))],
        out_specs=rows,
        out_shape=jax.ShapeDtypeStruct((BATCH, SEQ, D_MODEL), F32),
        compiler_params=_params(("parallel", "parallel")),
        name="pool_mixer",
    )(x, x, mod, mod, mod, pool_w.astype(BF16), pool_scale.reshape(1, -1),
      w_out_c.astype(BF16), ln_g.reshape(1, -1), ln_b.reshape(1, -1))


def _router_body(x_ref, sh_ref, sc_ref, wr_ref, g_ref, hm_ref):
    hm = _modulate(x_ref[0], sh_ref[0], sc_ref[0])
    hm_ref[0] = hm
    hm_hi = hm.astype(BF16)
    hm_lo = (hm - hm_hi.astype(F32)).astype(BF16)
    w = wr_ref[...]
    w_hi = w.astype(BF16)
    w_lo = (w - w_hi.astype(F32)).astype(BF16)
    logits = _dot(hm_hi, w_hi) + (_dot(hm_lo, w_hi) + _dot(hm_hi, w_lo))
    lane = lax.broadcasted_iota(jnp.int32, logits.shape, 1).astype(F32)
    logits = jnp.where(lane < N_EXPERTS, logits, -jnp.inf)
    v1 = jnp.max(logits, axis=-1, keepdims=True)
    i1 = jnp.min(jnp.where(logits == v1, lane, float(LANES)), axis=-1, keepdims=True)
    rest = jnp.where(lane == i1, -jnp.inf, logits)
    v2 = jnp.max(rest, axis=-1, keepdims=True)
    i2 = jnp.min(jnp.where(rest == v2, lane, float(LANES)), axis=-1, keepdims=True)
    e2 = jnp.exp(v2 - v1)
    den = 1.0 + e2
    g_ref[0] = (jnp.where(lane == 0.0, i1, 0.0) + jnp.where(lane == 1.0, i2, 0.0)
                + jnp.where(lane == 2.0, 1.0 / den, 0.0) + jnp.where(lane == 3.0, e2 / den, 0.0))


def _router(x, mod, k, w_router):
    tm = 512
    shift, scale, _ = _mod_specs(k, 2)
    wr = jnp.pad(w_router, ((0, 0), (0, LANES - N_EXPERTS)))
    rows = pl.BlockSpec((1, tm, D_MODEL), lambda b, i: (b, i, 0))
    return pl.pallas_call(
        _router_body,
        grid=(BATCH, SEQ // tm),
        in_specs=[rows, shift, scale, _const_spec((D_MODEL, LANES))],
        out_specs=(pl.BlockSpec((1, tm, LANES), lambda b, i: (b, i, 0)), rows),
        out_shape=(jax.ShapeDtypeStruct((BATCH, SEQ, LANES), F32),
                   jax.ShapeDtypeStruct((BATCH, SEQ, D_MODEL), F32)),
        compiler_params=_params(("parallel", "parallel")),
        name="router",
    )(x, mod, mod, wr)


N_TOKENS = BATCH * SEQ
N_ASSIGN = 2 * N_TOKENS
MOE_BLOCK = 1024
MOE_SUB = 256
MOE_TILE = N_ASSIGN // N_EXPERTS + MOE_SUB
MOE_TILES_ANY = -(-(N_ASSIGN + N_EXPERTS * (MOE_TILE - 1)) // MOE_TILE)
MOE_TILES_FIT = N_EXPERTS
GATHER_ROWS = MOE_TILE // 3
assert MOE_TILE % GATHER_ROWS == 0 and GATHER_ROWS % 8 == 0


def _route_counts(routing):
    e_flat = routing.reshape(N_TOKENS, LANES)[:, :2].astype(jnp.int32).T.reshape(N_ASSIGN)
    onehot = (e_flat[:, None] == jnp.arange(N_EXPERTS, dtype=jnp.int32)[None, :]).astype(jnp.int32)
    csum = jnp.cumsum(onehot, axis=0)
    return onehot, jnp.sum(csum * onehot, axis=1) - 1, csum[-1]


def _route_plan(onehot, rank, counts, n_tiles):
    padded = (counts + MOE_TILE - 1) // MOE_TILE * MOE_TILE
    group_end = jnp.cumsum(padded)
    group_start = group_end - padded
    pos = (jnp.sum(onehot * group_start[None, :], axis=1) + rank).astype(jnp.int32)
    tok = jnp.tile(jnp.arange(N_TOKENS, dtype=jnp.int32), 2)
    src = jnp.zeros((n_tiles * MOE_TILE,), jnp.int32).at[pos].set(tok, unique_indices=True)
    tile_start = jnp.arange(n_tiles, dtype=jnp.int32) * MOE_TILE
    n_used = group_end[-1] // MOE_TILE
    tile_expert = jnp.sum((tile_start[:, None] >= group_end[None, :]).astype(jnp.int32), axis=1)
    last_expert = jnp.max(jnp.where(counts > 0, jnp.arange(N_EXPERTS, dtype=jnp.int32), 0))
    tile_expert = jnp.minimum(tile_expert, last_expert)
    tile_rows = jnp.clip(counts[tile_expert] - (tile_start - group_start[tile_expert]), 0, MOE_TILE)
    tile_rows = jnp.where(jnp.arange(n_tiles) < n_used, tile_rows, 0).astype(jnp.int32)
    return pos, src, tile_expert.astype(jnp.int32), tile_rows, n_used.reshape(1).astype(jnp.int32)


GATHER_UNROLL = 8


def _gather_body(src_ref, rows_ref, hm_hbm, o_ref, buf, sem, *, tg):
    i = pl.program_id(0)
    base = i * tg
    per = MOE_TILE // tg
    n_valid = jnp.clip(rows_ref[i // per] - (i % per) * tg, 0, tg)

    @pl.when(n_valid < tg)
    def _():
        buf[...] = jnp.zeros(buf.shape, F32)

    def issue_row(r, queue):
        t = src_ref[base + r]
        pltpu.make_async_copy(hm_hbm.at[pl.ds(t, 1)], buf.at[pl.ds(r, 1)], sem).start(priority=queue)

    def issue_group(g, carry):
        for u in range(GATHER_UNROLL):
            issue_row(g * GATHER_UNROLL + u, u % 2)
        return carry

    def issue_one(r, carry):
        issue_row(r, 0)
        return carry

    n_groups = n_valid // GATHER_UNROLL
    n_grouped = pl.multiple_of(n_groups * GATHER_UNROLL, GATHER_UNROLL)
    lax.fori_loop(0, n_groups, issue_group, 0)
    lax.fori_loop(n_grouped, n_valid, issue_one, 0)

    @pl.when(n_groups > 0)
    def _():
        pltpu.make_async_copy(hm_hbm.at[pl.ds(0, n_grouped)], buf.at[pl.ds(0, n_grouped)], sem).wait()

    def wait_one(r, carry):
        pltpu.make_async_copy(hm_hbm.at[pl.ds(0, 1)], buf.at[pl.ds(r, 1)], sem).wait()
        return carry

    lax.fori_loop(n_grouped, n_valid, wait_one, 0)

    o_ref[...] = buf[...].astype(BF16)


def _gather_rows(src, tile_rows, hm):
    n_rows = src.shape[0]
    tg = GATHER_ROWS
    grid_spec = pltpu.PrefetchScalarGridSpec(
        num_scalar_prefetch=2,
        grid=(n_rows // tg,),
        in_specs=[pl.BlockSpec(memory_space=pl.ANY)],
        out_specs=pl.BlockSpec((tg, D_MODEL), lambda i, src, tr: (i, 0)),
        scratch_shapes=[pltpu.VMEM((tg, D_MODEL), F32), pltpu.SemaphoreType.DMA(())],
    )
    return pl.pallas_call(
        functools.partial(_gather_body, tg=tg),
        grid_spec=grid_spec,
        out_shape=jax.ShapeDtypeStruct((n_rows, D_MODEL), BF16),
        compiler_params=_params(("arbitrary",)),
        name="moe_gather",
    )(src, tile_rows, hm.reshape(N_TOKENS, D_MODEL))


def _for_valid_blocks(rows, tm, block_fn, empty_fn):
    for b0 in range(0, tm, MOE_BLOCK):
        b1 = min(b0 + MOE_BLOCK, tm)
        whole = b1 if b1 - b0 == MOE_BLOCK else tm + 1
        if whole <= tm:
            pl.when(rows >= whole)(functools.partial(block_fn, slice(b0, b1)))

        for r0 in range(b0, b1, MOE_SUB):
            rs = slice(r0, r0 + MOE_SUB)

            @pl.when(jnp.logical_and(rows < whole, r0 < rows))
            def _():
                block_fn(rs)

            @pl.when(r0 >= rows)
            def _():
                empty_fn(rs)


def _gup_body(te_ref, rows_ref, nu_ref, xs_ref, wg_ref, wu_ref, h_ref, wgb, wub, *, tm):
    rows = rows_ref[pl.program_id(0)]

    @pl.when(rows > 0)
    def _():
        wgb[...] = wg_ref[0].astype(BF16)
        wub[...] = wu_ref[0].astype(BF16)

    def block(rs):
        xsb = xs_ref[rs, :]
        h_ref[rs, :] = (jax.nn.silu(_dot(xsb, wgb[...])) * _dot(xsb, wub[...])).astype(BF16)

    def empty(rs):
        h_ref[rs, :] = jnp.zeros((rs.stop - rs.start, h_ref.shape[1]), BF16)

    _for_valid_blocks(rows, tm, block, empty)


def _gdown_body(te_ref, rows_ref, nu_ref, h_ref, wd_ref, o_ref, wdb, *, tm):
    rows = rows_ref[pl.program_id(0)]

    @pl.when(rows > 0)
    def _():
        wdb[...] = wd_ref[0].astype(BF16)

    def block(rs):
        o_ref[rs, :] = _dot(h_ref[rs, :], wdb[...])

    def empty(rs):
        o_ref[rs, :] = jnp.zeros((rs.stop - rs.start, o_ref.shape[1]), F32)

    _for_valid_blocks(rows, tm, block, empty)


def _grouped_ffn(xs, tile_expert, tile_rows, n_used, layer, w_gate, w_up, w_down):
    tm = MOE_TILE
    n_tiles = tile_expert.shape[0]
    first = layer * N_EXPERTS
    n_all = w_gate.shape[0] * N_EXPERTS

    def tile(i, nu):
        return jnp.minimum(i, nu[0] - 1)

    def chunk(i, c, nu, n_chunks):
        return jnp.where(i < nu[0], c, n_chunks - 1)

    tf = 256
    n_f = D_FF_EXPERT // tf
    w_up_spec = pl.BlockSpec((1, D_MODEL, tf),
                             lambda i, f, te, tr, nu: (first + te[i], 0, chunk(i, f, nu, n_f)))
    up_spec = pltpu.PrefetchScalarGridSpec(
        num_scalar_prefetch=3,
        grid=(n_tiles, n_f),
        in_specs=[pl.BlockSpec((tm, D_MODEL), lambda i, f, te, tr, nu: (tile(i, nu), 0)),
                  w_up_spec, w_up_spec],
        out_specs=pl.BlockSpec((tm, tf), lambda i, f, te, tr, nu: (i, f)),
        scratch_shapes=[pltpu.VMEM((D_MODEL, tf), BF16), pltpu.VMEM((D_MODEL, tf), BF16)],
    )
    h = pl.pallas_call(
        functools.partial(_gup_body, tm=tm),
        grid_spec=up_spec,
        out_shape=jax.ShapeDtypeStruct((n_tiles * tm, D_FF_EXPERT), BF16),
        compiler_params=_params(("arbitrary", "arbitrary")),
        name="moe_up",
    )(tile_expert, tile_rows, n_used, xs,
      w_gate.reshape(n_all, D_MODEL, D_FF_EXPERT), w_up.reshape(n_all, D_MODEL, D_FF_EXPERT))

    tn = 512
    n_n = D_MODEL // tn
    down_spec = pltpu.PrefetchScalarGridSpec(
        num_scalar_prefetch=3,
        grid=(n_tiles, n_n),
        in_specs=[pl.BlockSpec((tm, D_FF_EXPERT), lambda i, n, te, tr, nu: (tile(i, nu), 0)),
                  pl.BlockSpec((1, D_FF_EXPERT, tn),
                               lambda i, n, te, tr, nu: (first + te[i], 0, chunk(i, n, nu, n_n)))],
        out_specs=pl.BlockSpec((tm, tn), lambda i, n, te, tr, nu: (i, n)),
        scratch_shapes=[pltpu.VMEM((D_FF_EXPERT, tn), BF16)],
    )
    return pl.pallas_call(
        functools.partial(_gdown_body, tm=tm),
        grid_spec=down_spec,
        out_shape=jax.ShapeDtypeStruct((n_tiles * tm, D_MODEL), F32),
        compiler_params=_params(("arbitrary", "arbitrary")),
        name="moe_down",
    )(tile_expert, tile_rows, n_used, h, w_down.reshape(n_all, D_FF_EXPERT, D_MODEL))


def _combine_body(pos_ref, ys_hbm, r_ref, x_ref, gate_ref, lg_ref, lb_ref, o_ref, buf, sem, *, tc):
    base = (pl.program_id(0) * (SEQ // tc) + pl.program_id(1)) * tc

    def issue(r, carry):
        for k in range(2):
            p = pos_ref[k * N_TOKENS + base + r]
            pltpu.make_async_copy(ys_hbm.at[pl.ds(p, 1)], buf.at[k, pl.ds(r, 1)],
                                  sem.at[k]).start(priority=k)
        return carry

    lax.fori_loop(0, tc, issue, 0, unroll=4)
    for k in range(2):
        pltpu.make_async_copy(ys_hbm.at[pl.ds(0, tc)], buf.at[k], sem.at[k]).wait()
    routing = r_ref[0]
    y = routing[:, 2:3] * buf[0] + routing[:, 3:4] * buf[1]
    o_ref[0] = _deepnorm(x_ref[0], y, gate_ref[0], lg_ref[...], lb_ref[...])


def _combine(pos, ys, routing, x, mod, k, ln_g, ln_b):
    tc = 512
    rows = pl.BlockSpec((1, tc, D_MODEL), lambda b, i, pos: (b, i, 0))
    vec = pl.BlockSpec((1, D_MODEL), lambda b, i, pos: (0, 0))
    grid_spec = pltpu.PrefetchScalarGridSpec(
        num_scalar_prefetch=1,
        grid=(BATCH, SEQ // tc),
        in_specs=[pl.BlockSpec(memory_space=pl.ANY),
                  pl.BlockSpec((1, tc, LANES), lambda b, i, pos: (b, i, 0)),
                  rows,
                  pl.BlockSpec((1, 1, D_MODEL), lambda b, i, pos: (k * 8 + b, 0, 2)),
                  vec, vec],
        out_specs=rows,
        scratch_shapes=[pltpu.VMEM((2, tc, D_MODEL), F32), pltpu.SemaphoreType.DMA((2,))],
    )
    return pl.pallas_call(
        functools.partial(_combine_body, tc=tc),
        grid_spec=grid_spec,
        out_shape=jax.ShapeDtypeStruct((BATCH, SEQ, D_MODEL), F32),
        compiler_params=_params(("arbitrary", "arbitrary")),
        name="moe_combine",
    )(pos, ys, routing, x, mod, ln_g.reshape(1, -1), ln_b.reshape(1, -1))


def _moe(x, mod, k, w_router, layer, w_gate, w_up, w_down, ln_g, ln_b):
    routing, hm = _router(x, mod, k, w_router)
    onehot, rank, counts = _route_counts(routing)

    def experts_over(n_tiles):
        def run(_):
            pos, src, tile_expert, tile_rows, n_used = _route_plan(onehot, rank, counts, n_tiles)
            xs = _gather_rows(src, tile_rows, hm)
            ys = _grouped_ffn(xs, tile_expert, tile_rows, n_used, layer, w_gate, w_up, w_down)
            return _combine(pos, ys, routing, x, mod, k, ln_g, ln_b)
        return run

    return lax.cond(jnp.all(counts <= MOE_TILE),
                    experts_over(MOE_TILES_FIT), experts_over(MOE_TILES_ANY), None)


def kernel(x, c, positions, ada_w, ada_b, ln_g, ln_b, w_in_ab, q_norm_g, w_q_up, kv_norm_g, w_kv_up, sgu_norm_g, sgu_norm_b, sgu_w, sgu_b, w_out_ab, ffn_w_gate, ffn_w_up, ffn_w_down, pool_w, pool_scale, w_out_c, router_w, moe_w_gate, moe_w_up, moe_w_down):
    mod = _ada_modulation(c, ada_w, ada_b)
    tables = _rope_tables(positions)
    for l in range(DEPTH):
        j = l // 2
        k_tok, k_ch = 2 * l, 2 * l + 1
        if l % 2 == 0:
            q, k, vt, sgu = _front(x, mod, k_tok, tables, w_in_ab[j], q_norm_g[j], w_q_up[j],
                                   kv_norm_g[j], w_kv_up[j],
                                   sgu_norm_g[j], sgu_norm_b[j], sgu_w[j], sgu_b[j])
            attn = _attention(q, k, vt)
            x = _outproj(attn, sgu, x, mod, k_tok, w_out_ab[j], ln_g[l, 0], ln_b[l, 0])
            x = _ffn(x, mod, k_ch, j, ffn_w_gate, ffn_w_up, ffn_w_down[j], ln_g[l, 1], ln_b[l, 1])
        else:
            x = _pool_mixer(x, mod, k_tok, pool_w[j], pool_scale[j], w_out_c[j], ln_g[l, 0], ln_b[l, 0])
            x = _moe(x, mod, k_ch, router_w[j], j, moe_w_gate, moe_w_up, moe_w_down,
                     ln_g[l, 1], ln_b[l, 1])
    return x
```

```python
import functools

import jax
import jax.numpy as jnp
import numpy as np
from jax import lax
from jax.experimental import pallas as pl
from jax.experimental.pallas import tpu as pltpu

F32 = jnp.float32
BF16 = jnp.bfloat16

D_MODEL = 2048
BATCH = 2
SEQ = 4096
DEPTH = 4
MLA_HEADS = 8
QK_NOPE_DIM = 128
QK_ROPE_DIM = 64
V_HEAD_DIM = 128
Q_LORA_RANK = 768
KV_LORA_RANK = 512
ROPE_THETA = 10000.0
SGU_GROUPS = 8
SGU_GROUP_DIM = 128
SGU_WIDTH = SGU_GROUPS * SGU_GROUP_DIM
CHUNK = 128
ATTN_WIDTH = MLA_HEADS * V_HEAD_DIM
POOL_WINDOWS = (2, 4, 8, 16)
POOL_GROUP_DIM = D_MODEL // 4
D_FF = 5632
N_EXPERTS = 8
D_FF_EXPERT = 2816
DEEPNORM_ALPHA = (2 * DEPTH) ** 0.25
LN_EPS = 1e-5
RMS_EPS = 1e-6
SM_SCALE = (QK_NOPE_DIM + QK_ROPE_DIM) ** -0.5
LOG2_E = 1.4426950408889634

LANES = 128
HEAD_LANES = 2 * LANES
QK_WIDTH = MLA_HEADS * HEAD_LANES
VMEM_LIMIT = 60 * 1024 * 1024
POOL_HALO = 128


def _params(semantics):
    return pltpu.CompilerParams(dimension_semantics=semantics, vmem_limit_bytes=VMEM_LIMIT)


def _const_spec(shape):
    nd = len(shape)
    return pl.BlockSpec(shape, lambda *_: (0,) * nd, pipeline_mode=pl.Buffered(1))


def _dot(a, b):
    return jnp.dot(a, b, preferred_element_type=F32)


def _modulate(x, shift, scale):
    return x * (1.0 + scale) + shift


def _deepnorm(x, y, gate, g, b):
    r = DEEPNORM_ALPHA * x + (1.0 + gate) * y
    mu = jnp.mean(r, axis=-1, keepdims=True)
    rc = r - mu
    var = jnp.mean(rc * rc, axis=-1, keepdims=True)
    return rc * lax.rsqrt(var + LN_EPS) * g + b


def _rms(x, g):
    return x * lax.rsqrt(jnp.mean(x * x, axis=-1, keepdims=True) + RMS_EPS) * g


def _rope(r, cos_t, sin_hi, sin_lo):
    return (r * cos_t + pltpu.roll(r, QK_ROPE_DIM // 2, 1) * sin_hi
            + pltpu.roll(r, LANES - QK_ROPE_DIM // 2, 1) * sin_lo)


def _mod_specs(k, n_grid):
    def spec(part):
        if n_grid == 2:
            return pl.BlockSpec((1, 1, D_MODEL), lambda b, i: (k * 8 + b, 0, part))
        if n_grid == 3:
            return pl.BlockSpec((1, 1, D_MODEL), lambda b, i, f: (k * 8 + b, 0, part))
        return pl.BlockSpec((1, 1, D_MODEL), lambda b, i, e, f: (k * 8 + b, 0, part))
    return spec(0), spec(1), spec(2)


def _ada_body(c_ref, w_ref, b_ref, o_ref):
    s = jax.nn.silu(c_ref[...]).astype(BF16)
    o_ref[0] = _dot(s, w_ref[0].astype(BF16)) + b_ref[0]


def _ada_modulation(c, ada_w, ada_b):
    n_mod = 2 * DEPTH
    tn = 1024
    c_pad = jnp.pad(c, ((0, 8 - BATCH), (0, 0)))
    out = pl.pallas_call(
        _ada_body,
        grid=(n_mod, 3 * D_MODEL // tn),
        in_specs=[
            pl.BlockSpec((8, D_MODEL), lambda k, n: (0, 0)),
            pl.BlockSpec((1, D_MODEL, tn), lambda k, n: (k, 0, n)),
            pl.BlockSpec((1, 1, tn), lambda k, n: (k, 0, n)),
        ],
        out_specs=pl.BlockSpec((1, 8, tn), lambda k, n: (k, 0, n)),
        out_shape=jax.ShapeDtypeStruct((n_mod, 8, 3 * D_MODEL), F32),
        compiler_params=_params(("parallel", "parallel")),
        name="ada_mod",
    )(c_pad, ada_w.reshape(n_mod, D_MODEL, 3 * D_MODEL), ada_b.reshape(n_mod, 1, 3 * D_MODEL))
    return out.reshape(n_mod * 8, 1, 3 * D_MODEL)


def _rope_table_body(pos_ref, inv_ref, cos_ref, hi_ref, lo_ref):
    ang = pos_ref[0].astype(F32) * inv_ref[...]
    lane = lax.broadcasted_iota(jnp.int32, ang.shape, 1)
    cos = jnp.cos(ang)
    sin = jnp.sin(ang)
    half = QK_ROPE_DIM // 2
    cos_ref[0] = jnp.where(lane < QK_ROPE_DIM, cos, 0.0)
    hi_ref[0] = jnp.where(lane < half, 0.0, jnp.where(lane < QK_ROPE_DIM, sin, 0.0))
    lo_ref[0] = jnp.where(lane < half, -sin, 0.0)


def _rope_tables(positions):
    tm = 512
    half = QK_ROPE_DIM // 2
    inv = ROPE_THETA ** (-jnp.arange(0, QK_ROPE_DIM, 2, dtype=F32) / QK_ROPE_DIM)
    inv_lanes = jnp.concatenate([inv, inv, jnp.zeros((LANES - 2 * half,), F32)]).reshape(1, LANES)
    tab = jax.ShapeDtypeStruct((BATCH, SEQ, LANES), F32)
    spec = pl.BlockSpec((1, tm, LANES), lambda b, i: (b, i, 0))
    return pl.pallas_call(
        _rope_table_body,
        grid=(BATCH, SEQ // tm),
        in_specs=[pl.BlockSpec((1, tm, 1), lambda b, i: (b, i, 0)),
                  pl.BlockSpec((1, LANES), lambda b, i: (0, 0))],
        out_specs=(spec, spec, spec),
        out_shape=(tab, tab, tab),
        compiler_params=_params(("parallel", "parallel")),
        name="rope_tables",
    )(positions.reshape(BATCH, SEQ, 1), inv_lanes)


def _front_body(x_ref, sh_ref, sc_ref, wq_ref, wkv_ref, wkr_ref, wu_ref, wv_ref,
                cos_ref, hi_ref, lo_ref, qg_ref, kvg_ref, wqn_ref, wqr_ref, wkn_ref, wvt_ref,
                sg_ref, sb_ref, sw_ref, sbias_ref,
                q_ref, k_ref, vt_ref, sgu_ref, *, tm):
    hm = _modulate(x_ref[0], sh_ref[0], sc_ref[0]).astype(BF16)
    _qkv_compute(_dot(hm, wq_ref[...]), _dot(hm, wkv_ref[...]), _dot(hm, wkr_ref[...]),
                 cos_ref[0], hi_ref[0], lo_ref[0], qg_ref, kvg_ref,
                 wqn_ref, wqr_ref, wkn_ref, wvt_ref, q_ref, k_ref, vt_ref)
    _sgu_compute(_dot(hm, wu_ref[...]), _dot(hm, wv_ref[...]),
                 sg_ref, sb_ref, sw_ref, sbias_ref, sgu_ref, tm)


def _front(x, mod, k, tables, w_in, q_norm_g, w_q_up, kv_norm_g, w_kv_up,
           sgu_norm_g, sgu_norm_b, sgu_w, sgu_b):
    tm = 512
    a, b, c_, d = (Q_LORA_RANK, Q_LORA_RANK + KV_LORA_RANK,
                   Q_LORA_RANK + KV_LORA_RANK + QK_ROPE_DIM,
                   Q_LORA_RANK + KV_LORA_RANK + QK_ROPE_DIM + SGU_WIDTH)
    wq = w_in[:, :a].astype(BF16)
    wkv = w_in[:, a:b].astype(BF16)
    wkr = jnp.pad(w_in[:, b:c_], ((0, 0), (0, LANES - QK_ROPE_DIM))).astype(BF16)
    wu = w_in[:, c_:d].astype(BF16)
    wv = w_in[:, d:].astype(BF16)
    in_widths = (Q_LORA_RANK, KV_LORA_RANK, LANES, SGU_WIDTH, SGU_WIDTH)

    wq_up = w_q_up.reshape(Q_LORA_RANK, MLA_HEADS, QK_NOPE_DIM + QK_ROPE_DIM)
    wqn = wq_up[:, :, :QK_NOPE_DIM].reshape(Q_LORA_RANK, MLA_HEADS * LANES).astype(BF16)
    wqr = jnp.pad(wq_up[:, :, QK_NOPE_DIM:], ((0, 0), (0, 0), (0, LANES - QK_ROPE_DIM)))
    wqr = wqr.reshape(Q_LORA_RANK, MLA_HEADS * LANES).astype(BF16)
    wkv_up = w_kv_up.reshape(KV_LORA_RANK, MLA_HEADS, QK_NOPE_DIM + V_HEAD_DIM)
    wkn = wkv_up[:, :, :QK_NOPE_DIM].reshape(KV_LORA_RANK, MLA_HEADS * LANES).astype(BF16)
    wvt = wkv_up[:, :, QK_NOPE_DIM:].reshape(KV_LORA_RANK, ATTN_WIDTH).T.astype(BF16)
    sgu_bias = jnp.repeat(sgu_b.T, SGU_GROUP_DIM, axis=1)

    shift, scale, _ = _mod_specs(k, 2)

    def rows(w):
        return pl.BlockSpec((1, tm, w), lambda b, i: (b, i, 0))

    return pl.pallas_call(
        functools.partial(_front_body, tm=tm),
        grid=(BATCH, SEQ // tm),
        in_specs=[rows(D_MODEL), shift, scale]
                 + [_const_spec((D_MODEL, w)) for w in in_widths]
                 + [rows(LANES), rows(LANES), rows(LANES),
                    _const_spec((1, Q_LORA_RANK)), _const_spec((1, KV_LORA_RANK)),
                    _const_spec((Q_LORA_RANK, MLA_HEADS * LANES)),
                    _const_spec((Q_LORA_RANK, MLA_HEADS * LANES)),
                    _const_spec((KV_LORA_RANK, MLA_HEADS * LANES)),
                    _const_spec((ATTN_WIDTH, KV_LORA_RANK)),
                    _const_spec((1, SGU_WIDTH)), _const_spec((1, SGU_WIDTH)),
                    _const_spec((SGU_GROUPS, CHUNK, CHUNK)), _const_spec((CHUNK, SGU_WIDTH))],
        out_specs=(rows(QK_WIDTH), rows(QK_WIDTH),
                   pl.BlockSpec((1, ATTN_WIDTH, tm), lambda b, i: (b, 0, i)),
                   rows(SGU_WIDTH)),
        out_shape=(jax.ShapeDtypeStruct((BATCH, SEQ, QK_WIDTH), BF16),
                   jax.ShapeDtypeStruct((BATCH, SEQ, QK_WIDTH), BF16),
                   jax.ShapeDtypeStruct((BATCH, ATTN_WIDTH, SEQ), BF16),
                   jax.ShapeDtypeStruct((BATCH, SEQ, SGU_WIDTH), BF16)),
        compiler_params=_params(("parallel", "parallel")),
        name="front_ab",
    )(x, mod, mod, wq, wkv, wkr, wu, wv, *tables,
      q_norm_g.reshape(1, -1), kv_norm_g.reshape(1, -1), wqn, wqr, wkn, wvt,
      sgu_norm_g.reshape(1, -1), sgu_norm_b.reshape(1, -1), sgu_w, sgu_bias)


def _qkv_compute(cq, ckv, kr, cos_t, sin_hi, sin_lo, qg_ref, kvg_ref,
                 wqn_ref, wqr_ref, wkn_ref, wvt_ref, q_ref, k_ref, vt_ref):
    cqn = _rms(cq, qg_ref[...]).astype(BF16)
    qn = _dot(cqn, wqn_ref[...])
    qr = _dot(cqn, wqr_ref[...])
    ckvn = _rms(ckv, kvg_ref[...]).astype(BF16)
    kn = _dot(ckvn, wkn_ref[...])
    vt_ref[0] = lax.dot_general(wvt_ref[...], ckvn, (((1,), (1,)), ((), ())),
                                preferred_element_type=F32).astype(BF16)
    k_rope = _rope(kr, cos_t, sin_hi, sin_lo).astype(BF16)
    for h in range(MLA_HEADS):
        nope = slice(h * LANES, (h + 1) * LANES)
        lo = h * HEAD_LANES
        q_ref[0, :, lo:lo + LANES] = qn[:, nope].astype(BF16)
        q_ref[0, :, lo + LANES:lo + HEAD_LANES] = _rope(qr[:, nope], cos_t, sin_hi, sin_lo).astype(BF16)
        k_ref[0, :, lo:lo + LANES] = kn[:, nope].astype(BF16)
        k_ref[0, :, lo + LANES:lo + HEAD_LANES] = k_rope


def _attn_body(qi_ref, kj_ref, q_ref, k_ref, vt_ref, o_ref, m_sc, l_sc, acc_sc, *, tq):
    p = pl.program_id(1)
    i = qi_ref[p]
    j = kj_ref[p]

    @pl.when(j == 0)
    def _():
        m_sc[...] = jnp.full(m_sc.shape, -jnp.inf, F32)
        l_sc[...] = jnp.zeros(l_sc.shape, F32)
        acc_sc[...] = jnp.zeros(acc_sc.shape, F32)

    def accumulate(diagonal):
        if diagonal:
            key = lax.broadcasted_iota(jnp.int32, (tq, tq), 0)
            qry = lax.broadcasted_iota(jnp.int32, (tq, tq), 1)
            visible = key <= qry
        for h in range(MLA_HEADS):
            qk = slice(h * HEAD_LANES, (h + 1) * HEAD_LANES)
            hv = slice(h * V_HEAD_DIM, (h + 1) * V_HEAD_DIM)
            s = lax.dot_general(k_ref[0, :, qk], q_ref[0, :, qk], (((1,), (1,)), ((), ())),
                                preferred_element_type=F32) * (SM_SCALE * LOG2_E)
            if diagonal:
                s = jnp.where(visible, s, -jnp.inf)
            m_prev = m_sc[h]
            m_new = jnp.maximum(m_prev, jnp.max(s, axis=0, keepdims=True))
            alpha = jnp.exp2(m_prev - m_new)
            pr = jnp.exp2(s - m_new)
            l_sc[h] = alpha * l_sc[h] + jnp.sum(pr, axis=0, keepdims=True)
            acc_sc[hv, :] = alpha * acc_sc[hv, :] + _dot(vt_ref[0, hv, :], pr.astype(BF16))
            m_sc[h] = m_new

    @pl.when(j < i)
    def _():
        accumulate(False)

    @pl.when(j == i)
    def _():
        accumulate(True)
        for h in range(MLA_HEADS):
            hv = slice(h * V_HEAD_DIM, (h + 1) * V_HEAD_DIM)
            o_ref[0, :, hv] = (acc_sc[hv, :] / l_sc[h]).T.astype(BF16)


def _attention(q, k, vt):
    tq = 512
    nb = SEQ // tq
    pairs = [(i, j) for i in range(nb) for j in range(i + 1)]
    qi = jnp.asarray(np.array([p[0] for p in pairs], np.int32))
    kj = jnp.asarray(np.array([p[1] for p in pairs], np.int32))
    grid_spec = pltpu.PrefetchScalarGridSpec(
        num_scalar_prefetch=2,
        grid=(BATCH, len(pairs)),
        in_specs=[pl.BlockSpec((1, tq, QK_WIDTH), lambda b, p, qi, kj: (b, qi[p], 0)),
                  pl.BlockSpec((1, tq, QK_WIDTH), lambda b, p, qi, kj: (b, kj[p], 0)),
                  pl.BlockSpec((1, ATTN_WIDTH, tq), lambda b, p, qi, kj: (b, 0, kj[p]))],
        out_specs=pl.BlockSpec((1, tq, ATTN_WIDTH), lambda b, p, qi, kj: (b, qi[p], 0)),
        scratch_shapes=[pltpu.VMEM((MLA_HEADS, 1, tq), F32),
                        pltpu.VMEM((MLA_HEADS, 1, tq), F32),
                        pltpu.VMEM((ATTN_WIDTH, tq), F32)],
    )
    return pl.pallas_call(
        functools.partial(_attn_body, tq=tq),
        grid_spec=grid_spec,
        out_shape=jax.ShapeDtypeStruct((BATCH, SEQ, ATTN_WIDTH), BF16),
        compiler_params=_params(("parallel", "arbitrary")),
        name="mla_attention",
    )(qi, kj, q, k, vt)


def _sgu_compute(zu, zv, g_ref, b_ref, w_ref, bias_ref, o_ref, tm):
    row = lax.broadcasted_iota(jnp.int32, (CHUNK, CHUNK), 0)
    col = lax.broadcasted_iota(jnp.int32, (CHUNK, CHUNK), 1)
    for g in range(SGU_GROUPS):
        gs = slice(g * SGU_GROUP_DIM, (g + 1) * SGU_GROUP_DIM)
        v = jax.nn.gelu(zv[:, gs])
        mu = jnp.mean(v, axis=-1, keepdims=True)
        vc = v - mu
        var = jnp.mean(vc * vc, axis=-1, keepdims=True)
        vn = (vc * lax.rsqrt(var + LN_EPS) * g_ref[:, gs] + b_ref[:, gs]).astype(BF16)
        w = jnp.where(col <= row, w_ref[g], 0.0).astype(BF16)
        for n in range(tm // CHUNK):
            rs = slice(n * CHUNK, (n + 1) * CHUNK)
            s = _dot(w, vn[rs]) + bias_ref[:, gs]
            o_ref[0, rs, gs] = (jax.nn.gelu(zu[rs, gs]) * s).astype(BF16)


def _outproj_body(a_ref, s_ref, x_ref, gate_ref, wa_ref, ws_ref, lg_ref, lb_ref, o_ref):
    y = _dot(a_ref[0], wa_ref[...]) + _dot(s_ref[0], ws_ref[...])
    o_ref[0] = _deepnorm(x_ref[0], y, gate_ref[0], lg_ref[...], lb_ref[...])


def _outproj(attn, sgu, x, mod, k, w_out, ln_g, ln_b):
    tm = 512
    wa = w_out[:ATTN_WIDTH].astype(BF16)
    ws = w_out[ATTN_WIDTH:].astype(BF16)
    _, _, gate = _mod_specs(k, 2)

    def rows(w):
        return pl.BlockSpec((1, tm, w), lambda b, i: (b, i, 0))

    return pl.pallas_call(
        _outproj_body,
        grid=(BATCH, SEQ // tm),
        in_specs=[rows(ATTN_WIDTH), rows(SGU_WIDTH), rows(D_MODEL), gate,
                  _const_spec((ATTN_WIDTH, D_MODEL)), _const_spec((SGU_WIDTH, D_MODEL)),
                  _const_spec((1, D_MODEL)), _const_spec((1, D_MODEL))],
        out_specs=rows(D_MODEL),
        out_shape=jax.ShapeDtypeStruct((BATCH, SEQ, D_MODEL), F32),
        compiler_params=_params(("parallel", "parallel")),
        name="outproj_ab",
    )(attn, sgu, x, mod, wa, ws, ln_g.reshape(1, -1), ln_b.reshape(1, -1))


def _ffn_up_body(x_ref, sh_ref, sc_ref, wg_ref, wu_ref, h_ref, hm_sc):
    @pl.when(pl.program_id(2) == 0)
    def _():
        hm_sc[...] = _modulate(x_ref[0], sh_ref[0], sc_ref[0]).astype(BF16)

    hm = hm_sc[...]
    g = _dot(hm, wg_ref[0].astype(BF16))
    u = _dot(hm, wu_ref[0].astype(BF16))
    h_ref[0] = (jax.nn.silu(g) * u).astype(BF16)


def _ffn_down_body(h_ref, x_ref, gate_ref, wd_ref, lg_ref, lb_ref, o_ref, y_sc, *, n_n):
    n = pl.program_id(2)
    y_sc[n] = _dot(h_ref[0], wd_ref[...])

    @pl.when(n == n_n - 1)
    def _():
        y = jnp.concatenate([y_sc[c] for c in range(n_n)], axis=-1)
        o_ref[0] = _deepnorm(x_ref[0], y, gate_ref[0], lg_ref[...], lb_ref[...])


def _ffn(x, mod, k, layer, w_gate, w_up, w_down, ln_g, ln_b):
    tm, tf = 2048, 256
    shift, scale, gate = _mod_specs(k, 3)
    h = pl.pallas_call(
        _ffn_up_body,
        grid=(BATCH, SEQ // tm, D_FF // tf),
        in_specs=[pl.BlockSpec((1, tm, D_MODEL), lambda b, i, f: (b, i, 0), pipeline_mode=pl.Buffered(1)),
                  shift, scale,
                  pl.BlockSpec((1, D_MODEL, tf), lambda b, i, f: (layer, 0, f)),
                  pl.BlockSpec((1, D_MODEL, tf), lambda b, i, f: (layer, 0, f))],
        out_specs=pl.BlockSpec((1, tm, tf), lambda b, i, f: (b, i, f)),
        out_shape=jax.ShapeDtypeStruct((BATCH, SEQ, D_FF), BF16),
        scratch_shapes=[pltpu.VMEM((tm, D_MODEL), BF16)],
        compiler_params=_params(("parallel", "parallel", "arbitrary")),
        name="ffn_up",
    )(x, mod, mod, w_gate, w_up)

    tm, tn = 512, 512
    n_n = D_MODEL // tn
    rows = pl.BlockSpec((1, tm, D_MODEL), lambda b, i, n: (b, i, 0))
    vec = pl.BlockSpec((1, D_MODEL), lambda b, i, n: (0, 0))
    return pl.pallas_call(
        functools.partial(_ffn_down_body, n_n=n_n),
        grid=(BATCH, SEQ // tm, n_n),
        in_specs=[pl.BlockSpec((1, tm, D_FF), lambda b, i, n: (b, i, 0)),
                  rows,
                  gate,
                  pl.BlockSpec((D_FF, tn), lambda b, i, n: (0, n)),
                  vec, vec],
        out_specs=rows,
        out_shape=jax.ShapeDtypeStruct((BATCH, SEQ, D_MODEL), F32),
        scratch_shapes=[pltpu.VMEM((n_n, tm, tn), F32)],
        compiler_params=_params(("parallel", "parallel", "arbitrary")),
        name="ffn_down",
    )(h, x, mod, w_down.astype(BF16), ln_g.reshape(1, -1), ln_b.reshape(1, -1))


def _pool_body(x_ref, halo_ref, sh_ref, sc_ref, gate_ref, pw_ref, ps_ref, wo_ref, lg_ref, lb_ref,
               o_ref, *, tm):
    i = pl.program_id(1)
    x = x_ref[0]
    hm = _modulate(x, sh_ref[0], sc_ref[0])
    halo = jnp.where(i > 0, _modulate(halo_ref[0], sh_ref[0], sc_ref[0]), 0.0)
    ext = jnp.concatenate([halo, hm], axis=0)
    ext_hi = ext.astype(BF16)
    ext_lo = (ext - ext_hi.astype(F32)).astype(BF16)
    t_loc = lax.broadcasted_iota(jnp.int32, (tm, tm + POOL_HALO), 0)
    s_loc = lax.broadcasted_iota(jnp.int32, (tm, tm + POOL_HALO), 1)
    lag = t_loc + POOL_HALO - s_loc
    t_glob = i * tm + lax.broadcasted_iota(jnp.int32, (tm, 1), 0)
    outs = []
    for gi, w in enumerate(POOL_WINDOWS):
        cs = slice(gi * POOL_GROUP_DIM, (gi + 1) * POOL_GROUP_DIM)
        band = jnp.where(lag >= 0, jnp.where(lag < w, 1.0, 0.0), 0.0).astype(BF16)
        wsum = _dot(band, ext_hi[:, cs]) + _dot(band, ext_lo[:, cs])
        count = jnp.minimum(t_glob + 1, w).astype(F32)
        d = (wsum / count - hm[:, cs]).astype(BF16)
        outs.append(_dot(d, pw_ref[gi]))
    pooled = (jnp.concatenate(outs, axis=-1) * ps_ref[...]).astype(BF16)
    y = _dot(pooled, wo_ref[...])
    o_ref[0] = _deepnorm(x, y, gate_ref[0], lg_ref[...], lb_ref[...])


def _pool_mixer(x, mod, k, pool_w, pool_scale, w_out_c, ln_g, ln_b):
    tm = 512
    per = tm // POOL_HALO
    shift, scale, gate = _mod_specs(k, 2)
    rows = pl.BlockSpec((1, tm, D_MODEL), lambda b, i: (b, i, 0))
    halo = pl.BlockSpec((1, POOL_HALO, D_MODEL), lambda b, i: (b, jnp.maximum(i * per - 1, 0), 0))
    return pl.pallas_call(
        functools.partial(_pool_body, tm=tm),
        grid=(BATCH, SEQ // tm),
        in_specs=[rows, halo, shift, scale, gate,
                  _const_spec((len(POOL_WINDOWS), POOL_GROUP_DIM, POOL_GROUP_DIM)),
                  _const_spec((1, D_MODEL)), _const_spec((D_MODEL, D_MODEL)),
                  _const_spec((1, D_MODEL)), _const_spec((1, D_MODEL))],
        out_specs=rows,
        out_shape=jax.ShapeDtypeStruct((BATCH, SEQ, D_MODEL), F32),
        compiler_params=_params(("parallel", "parallel")),
        name="pool_mixer",
    )(x, x, mod, mod, mod, pool_w.astype(BF16), pool_scale.reshape(1, -1),
      w_out_c.astype(BF16), ln_g.reshape(1, -1), ln_b.reshape(1, -1))


def _router_body(x_ref, sh_ref, sc_ref, wr_ref, g_ref, hm_ref):
    hm = _modulate(x_ref[0], sh_ref[0], sc_ref[0])
    hm_ref[0] = hm
    hm_hi = hm.astype(BF16)
    hm_lo = (hm - hm_hi.astype(F32)).astype(BF16)
    w = wr_ref[...]
    w_hi = w.astype(BF16)
    w_lo = (w - w_hi.astype(F32)).astype(BF16)
    logits = _dot(hm_hi, w_hi) + (_dot(hm_lo, w_hi) + _dot(hm_hi, w_lo))
    lane = lax.broadcasted_iota(jnp.int32, logits.shape, 1).astype(F32)
    logits = jnp.where(lane < N_EXPERTS, logits, -jnp.inf)
    v1 = jnp.max(logits, axis=-1, keepdims=True)
    i1 = jnp.min(jnp.where(logits == v1, lane, float(LANES)), axis=-1, keepdims=True)
    rest = jnp.where(lane == i1, -jnp.inf, logits)
    v2 = jnp.max(rest, axis=-1, keepdims=True)
    i2 = jnp.min(jnp.where(rest == v2, lane, float(LANES)), axis=-1, keepdims=True)
    e2 = jnp.exp(v2 - v1)
    den = 1.0 + e2
    g_ref[0] = (jnp.where(lane == 0.0, i1, 0.0) + jnp.where(lane == 1.0, i2, 0.0)
                + jnp.where(lane == 2.0, 1.0 / den, 0.0) + jnp.where(lane == 3.0, e2 / den, 0.0))


def _router(x, mod, k, w_router):
    tm = 512
    shift, scale, _ = _mod_specs(k, 2)
    wr = jnp.pad(w_router, ((0, 0), (0, LANES - N_EXPERTS)))
    rows = pl.BlockSpec((1, tm, D_MODEL), lambda b, i: (b, i, 0))
    return pl.pallas_call(
        _router_body,
        grid=(BATCH, SEQ // tm),
        in_specs=[rows, shift, scale, _const_spec((D_MODEL, LANES))],
        out_specs=(pl.BlockSpec((1, tm, LANES), lambda b, i: (b, i, 0)), rows),
        out_shape=(jax.ShapeDtypeStruct((BATCH, SEQ, LANES), F32),
                   jax.ShapeDtypeStruct((BATCH, SEQ, D_MODEL), F32)),
        compiler_params=_params(("parallel", "parallel")),
        name="router",
    )(x, mod, mod, wr)


N_TOKENS = BATCH * SEQ
N_ASSIGN = 2 * N_TOKENS
MOE_BLOCK = 1024
MOE_SUB = 256
MOE_TILE = N_ASSIGN // N_EXPERTS * 5 // 4
MOE_TILES_ANY = -(-(N_ASSIGN + N_EXPERTS * (MOE_TILE - 1)) // MOE_TILE)
MOE_TILES_FIT = N_EXPERTS
GATHER_ROWS = MOE_TILE // 4
assert MOE_TILE % GATHER_ROWS == 0 and GATHER_ROWS % 8 == 0


def _route_counts(routing):
    e_flat = routing.reshape(N_TOKENS, LANES)[:, :2].astype(jnp.int32).T.reshape(N_ASSIGN)
    onehot = (e_flat[:, None] == jnp.arange(N_EXPERTS, dtype=jnp.int32)[None, :]).astype(jnp.int32)
    csum = jnp.cumsum(onehot, axis=0)
    return onehot, jnp.sum(csum * onehot, axis=1) - 1, csum[-1]


def _route_plan(onehot, rank, counts, n_tiles):
    padded = (counts + MOE_TILE - 1) // MOE_TILE * MOE_TILE
    group_end = jnp.cumsum(padded)
    group_start = group_end - padded
    pos = (jnp.sum(onehot * group_start[None, :], axis=1) + rank).astype(jnp.int32)
    tok = jnp.tile(jnp.arange(N_TOKENS, dtype=jnp.int32), 2)
    src = jnp.zeros((n_tiles * MOE_TILE,), jnp.int32).at[pos].set(tok, unique_indices=True)
    tile_start = jnp.arange(n_tiles, dtype=jnp.int32) * MOE_TILE
    n_used = group_end[-1] // MOE_TILE
    tile_expert = jnp.sum((tile_start[:, None] >= group_end[None, :]).astype(jnp.int32), axis=1)
    last_expert = jnp.max(jnp.where(counts > 0, jnp.arange(N_EXPERTS, dtype=jnp.int32), 0))
    tile_expert = jnp.minimum(tile_expert, last_expert)
    tile_rows = jnp.clip(counts[tile_expert] - (tile_start - group_start[tile_expert]), 0, MOE_TILE)
    tile_rows = jnp.where(jnp.arange(n_tiles) < n_used, tile_rows, 0).astype(jnp.int32)
    return pos, src, tile_expert.astype(jnp.int32), tile_rows, n_used.reshape(1).astype(jnp.int32)


GATHER_UNROLL = 8


def _gather_body(src_ref, rows_ref, hm_hbm, o_ref, buf, sem, *, tg):
    i = pl.program_id(0)
    base = i * tg
    per = MOE_TILE // tg
    n_valid = jnp.clip(rows_ref[i // per] - (i % per) * tg, 0, tg)

    @pl.when(n_valid < tg)
    def _():
        buf[...] = jnp.zeros(buf.shape, F32)

    def issue_row(r, queue):
        t = src_ref[base + r]
        pltpu.make_async_copy(hm_hbm.at[pl.ds(t, 1)], buf.at[pl.ds(r, 1)], sem).start(priority=queue)

    def issue_group(g, carry):
        for u in range(GATHER_UNROLL):
            issue_row(g * GATHER_UNROLL + u, u % 2)
        return carry

    def issue_one(r, carry):
        issue_row(r, 0)
        return carry

    n_groups = n_valid // GATHER_UNROLL
    n_grouped = pl.multiple_of(n_groups * GATHER_UNROLL, GATHER_UNROLL)
    lax.fori_loop(0, n_groups, issue_group, 0)
    lax.fori_loop(n_grouped, n_valid, issue_one, 0)

    @pl.when(n_groups > 0)
    def _():
        pltpu.make_async_copy(hm_hbm.at[pl.ds(0, n_grouped)], buf.at[pl.ds(0, n_grouped)], sem).wait()

    def wait_one(r, carry):
        pltpu.make_async_copy(hm_hbm.at[pl.ds(0, 1)], buf.at[pl.ds(r, 1)], sem).wait()
        return carry

    lax.fori_loop(n_grouped, n_valid, wait_one, 0)

    o_ref[...] = buf[...].astype(BF16)


def _gather_rows(src, tile_rows, hm):
    n_rows = src.shape[0]
    tg = GATHER_ROWS
    grid_spec = pltpu.PrefetchScalarGridSpec(
        num_scalar_prefetch=2,
        grid=(n_rows // tg,),
        in_specs=[pl.BlockSpec(memory_space=pl.ANY)],
        out_specs=pl.BlockSpec((tg, D_MODEL), lambda i, src, tr: (i, 0)),
        scratch_shapes=[pltpu.VMEM((tg, D_MODEL), F32), pltpu.SemaphoreType.DMA(())],
    )
    return pl.pallas_call(
        functools.partial(_gather_body, tg=tg),
        grid_spec=grid_spec,
        out_shape=jax.ShapeDtypeStruct((n_rows, D_MODEL), BF16),
        compiler_params=_params(("arbitrary",)),
        name="moe_gather",
    )(src, tile_rows, hm.reshape(N_TOKENS, D_MODEL))


def _for_valid_blocks(rows, tm, block_fn, empty_fn):
    for b0 in range(0, tm, MOE_BLOCK):
        b1 = min(b0 + MOE_BLOCK, tm)
        whole = b1 if b1 - b0 == MOE_BLOCK else tm + 1
        if whole <= tm:
            pl.when(rows >= whole)(functools.partial(block_fn, slice(b0, b1)))

        for r0 in range(b0, b1, MOE_SUB):
            rs = slice(r0, r0 + MOE_SUB)

            @pl.when(jnp.logical_and(rows < whole, r0 < rows))
            def _():
                block_fn(rs)

            @pl.when(r0 >= rows)
            def _():
                empty_fn(rs)


def _gup_body(te_ref, rows_ref, nu_ref, xs_ref, wg_ref, wu_ref, h_ref, wgb, wub, *, tm):
    rows = rows_ref[pl.program_id(0)]

    @pl.when(rows > 0)
    def _():
        wgb[...] = wg_ref[0].astype(BF16)
        wub[...] = wu_ref[0].astype(BF16)

    def block(rs):
        xsb = xs_ref[rs, :]
        h_ref[rs, :] = (jax.nn.silu(_dot(xsb, wgb[...])) * _dot(xsb, wub[...])).astype(BF16)

    def empty(rs):
        h_ref[rs, :] = jnp.zeros((rs.stop - rs.start, h_ref.shape[1]), BF16)

    _for_valid_blocks(rows, tm, block, empty)


def _gdown_body(te_ref, rows_ref, nu_ref, h_ref, wd_ref, o_ref, wdb, *, tm):
    rows = rows_ref[pl.program_id(0)]

    @pl.when(rows > 0)
    def _():
        wdb[...] = wd_ref[0].astype(BF16)

    def block(rs):
        o_ref[rs, :] = _dot(h_ref[rs, :], wdb[...])

    def empty(rs):
        o_ref[rs, :] = jnp.zeros((rs.stop - rs.start, o_ref.shape[1]), F32)

    _for_valid_blocks(rows, tm, block, empty)


def _grouped_ffn(xs, tile_expert, tile_rows, n_used, layer, w_gate, w_up, w_down):
    tm = MOE_TILE
    n_tiles = tile_expert.shape[0]
    first = layer * N_EXPERTS
    n_all = w_gate.shape[0] * N_EXPERTS

    def tile(i, nu):
        return jnp.minimum(i, nu[0] - 1)

    def chunk(i, c, nu, n_chunks):
        return jnp.where(i < nu[0], c, n_chunks - 1)

    tf = 256
    n_f = D_FF_EXPERT // tf
    w_up_spec = pl.BlockSpec((1, D_MODEL, tf),
                             lambda i, f, te, tr, nu: (first + te[i], 0, chunk(i, f, nu, n_f)))
    up_spec = pltpu.PrefetchScalarGridSpec(
        num_scalar_prefetch=3,
        grid=(n_tiles, n_f),
        in_specs=[pl.BlockSpec((tm, D_MODEL), lambda i, f, te, tr, nu: (tile(i, nu), 0)),
                  w_up_spec, w_up_spec],
        out_specs=pl.BlockSpec((tm, tf), lambda i, f, te, tr, nu: (i, f)),
        scratch_shapes=[pltpu.VMEM((D_MODEL, tf), BF16), pltpu.VMEM((D_MODEL, tf), BF16)],
    )
    h = pl.pallas_call(
        functools.partial(_gup_body, tm=tm),
        grid_spec=up_spec,
        out_shape=jax.ShapeDtypeStruct((n_tiles * tm, D_FF_EXPERT), BF16),
        compiler_params=_params(("arbitrary", "arbitrary")),
        name="moe_up",
    )(tile_expert, tile_rows, n_used, xs,
      w_gate.reshape(n_all, D_MODEL, D_FF_EXPERT), w_up.reshape(n_all, D_MODEL, D_FF_EXPERT))

    tn = 512
    n_n = D_MODEL // tn
    down_spec = pltpu.PrefetchScalarGridSpec(
        num_scalar_prefetch=3,
        grid=(n_tiles, n_n),
        in_specs=[pl.BlockSpec((tm, D_FF_EXPERT), lambda i, n, te, tr, nu: (tile(i, nu), 0)),
                  pl.BlockSpec((1, D_FF_EXPERT, tn),
                               lambda i, n, te, tr, nu: (first + te[i], 0, chunk(i, n, nu, n_n)))],
        out_specs=pl.BlockSpec((tm, tn), lambda i, n, te, tr, nu: (i, n)),
        scratch_shapes=[pltpu.VMEM((D_FF_EXPERT, tn), BF16)],
    )
    return pl.pallas_call(
        functools.partial(_gdown_body, tm=tm),
        grid_spec=down_spec,
        out_shape=jax.ShapeDtypeStruct((n_tiles * tm, D_MODEL), F32),
        compiler_params=_params(("arbitrary", "arbitrary")),
        name="moe_down",
    )(tile_expert, tile_rows, n_used, h, w_down.reshape(n_all, D_FF_EXPERT, D_MODEL))


def _combine_body(pos_ref, ys_hbm, r_ref, x_ref, gate_ref, lg_ref, lb_ref, o_ref, buf, sem, *, tc):
    base = (pl.program_id(0) * (SEQ // tc) + pl.program_id(1)) * tc

    def issue(r, carry):
        for k in range(2):
            p = pos_ref[k * N_TOKENS + base + r]
            pltpu.make_async_copy(ys_hbm.at[pl.ds(p, 1)], buf.at[k, pl.ds(r, 1)],
                                  sem.at[k]).start(priority=k)
        return carry

    lax.fori_loop(0, tc, issue, 0, unroll=4)
    for k in range(2):
        pltpu.make_async_copy(ys_hbm.at[pl.ds(0, tc)], buf.at[k], sem.at[k]).wait()
    routing = r_ref[0]
    y = routing[:, 2:3] * buf[0] + routing[:, 3:4] * buf[1]
    o_ref[0] = _deepnorm(x_ref[0], y, gate_ref[0], lg_ref[...], lb_ref[...])


def _combine(pos, ys, routing, x, mod, k, ln_g, ln_b):
    tc = 512
    rows = pl.BlockSpec((1, tc, D_MODEL), lambda b, i, pos: (b, i, 0))
    vec = pl.BlockSpec((1, D_MODEL), lambda b, i, pos: (0, 0))
    grid_spec = pltpu.PrefetchScalarGridSpec(
        num_scalar_prefetch=1,
        grid=(BATCH, SEQ // tc),
        in_specs=[pl.BlockSpec(memory_space=pl.ANY),
                  pl.BlockSpec((1, tc, LANES), lambda b, i, pos: (b, i, 0)),
                  rows,
                  pl.BlockSpec((1, 1, D_MODEL), lambda b, i, pos: (k * 8 + b, 0, 2)),
                  vec, vec],
        out_specs=rows,
        scratch_shapes=[pltpu.VMEM((2, tc, D_MODEL), F32), pltpu.SemaphoreType.DMA((2,))],
    )
    return pl.pallas_call(
        functools.partial(_combine_body, tc=tc),
        grid_spec=grid_spec,
        out_shape=jax.ShapeDtypeStruct((BATCH, SEQ, D_MODEL), F32),
        compiler_params=_params(("arbitrary", "arbitrary")),
        name="moe_combine",
    )(pos, ys, routing, x, mod, ln_g.reshape(1, -1), ln_b.reshape(1, -1))


def _moe(x, mod, k, w_router, layer, w_gate, w_up, w_down, ln_g, ln_b):
    routing, hm = _router(x, mod, k, w_router)
    onehot, rank, counts = _route_counts(routing)

    def experts_over(n_tiles):
        def run(_):
            pos, src, tile_expert, tile_rows, n_used = _route_plan(onehot, rank, counts, n_tiles)
            xs = _gather_rows(src, tile_rows, hm)
            ys = _grouped_ffn(xs, tile_expert, tile_rows, n_used, layer, w_gate, w_up, w_down)
            return _combine(pos, ys, routing, x, mod, k, ln_g, ln_b)
        return run

    return lax.cond(jnp.all(counts <= MOE_TILE),
                    experts_over(MOE_TILES_FIT), experts_over(MOE_TILES_ANY), None)


def kernel(x, c, positions, ada_w, ada_b, ln_g, ln_b, w_in_ab, q_norm_g, w_q_up, kv_norm_g, w_kv_up, sgu_norm_g, sgu_norm_b, sgu_w, sgu_b, w_out_ab, ffn_w_gate, ffn_w_up, ffn_w_down, pool_w, pool_scale, w_out_c, router_w, moe_w_gate, moe_w_up, moe_w_down):
    mod = _ada_modulation(c, ada_w, ada_b)
    tables = _rope_tables(positions)
    for l in range(DEPTH):
        j = l // 2
        k_tok, k_ch = 2 * l, 2 * l + 1
        if l % 2 == 0:
            q, k, vt, sgu = _front(x, mod, k_tok, tables, w_in_ab[j], q_norm_g[j], w_q_up[j],
                                   kv_norm_g[j], w_kv_up[j],
                                   sgu_norm_g[j], sgu_norm_b[j], sgu_w[j], sgu_b[j])
            attn = _attention(q, k, vt)
            x = _outproj(attn, sgu, x, mod, k_tok, w_out_ab[j], ln_g[l, 0], ln_b[l, 0])
            x = _ffn(x, mod, k_ch, j, ffn_w_gate, ffn_w_up, ffn_w_down[j], ln_g[l, 1], ln_b[l, 1])
        else:
            x = _pool_mixer(x, mod, k_tok, pool_w[j], pool_scale[j], w_out_c[j], ln_g[l, 0], ln_b[l, 0])
            x = _moe(x, mod, k_ch, router_w[j], j, moe_w_gate, moe_w_up, moe_w_down,
                     ln_g[l, 1], ln_b[l, 1])
    return x
```

```python
import functools

import jax
import jax.numpy as jnp
import numpy as np
from jax import lax
from jax.experimental import pallas as pl
from jax.experimental.pallas import tpu as pltpu

F32 = jnp.float32
BF16 = jnp.bfloat16

D_MODEL = 2048
BATCH = 2
SEQ = 4096
DEPTH = 4
MLA_HEADS = 8
QK_NOPE_DIM = 128
QK_ROPE_DIM = 64
V_HEAD_DIM = 128
Q_LORA_RANK = 768
KV_LORA_RANK = 512
ROPE_THETA = 10000.0
SGU_GROUPS = 8
SGU_GROUP_DIM = 128
SGU_WIDTH = SGU_GROUPS * SGU_GROUP_DIM
CHUNK = 128
ATTN_WIDTH = MLA_HEADS * V_HEAD_DIM
POOL_WINDOWS = (2, 4, 8, 16)
POOL_GROUP_DIM = D_MODEL // 4
D_FF = 5632
N_EXPERTS = 8
D_FF_EXPERT = 2816
DEEPNORM_ALPHA = (2 * DEPTH) ** 0.25
LN_EPS = 1e-5
RMS_EPS = 1e-6
SM_SCALE = (QK_NOPE_DIM + QK_ROPE_DIM) ** -0.5
LOG2_E = 1.4426950408889634

LANES = 128
HEAD_LANES = 2 * LANES
QK_WIDTH = MLA_HEADS * HEAD_LANES
VMEM_LIMIT = 60 * 1024 * 1024
POOL_HALO = 128


def _params(semantics):
    return pltpu.CompilerParams(dimension_semantics=semantics, vmem_limit_bytes=VMEM_LIMIT)


def _const_spec(shape):
    nd = len(shape)
    return pl.BlockSpec(shape, lambda *_: (0,) * nd, pipeline_mode=pl.Buffered(1))


def _dot(a, b):
    return jnp.dot(a, b, preferred_element_type=F32)


def _modulate(x, shift, scale):
    return x * (1.0 + scale) + shift


def _deepnorm(x, y, gate, g, b):
    r = DEEPNORM_ALPHA * x + (1.0 + gate) * y
    mu = jnp.mean(r, axis=-1, keepdims=True)
    rc = r - mu
    var = jnp.mean(rc * rc, axis=-1, keepdims=True)
    return rc * lax.rsqrt(var + LN_EPS) * g + b


def _rms(x, g):
    return x * lax.rsqrt(jnp.mean(x * x, axis=-1, keepdims=True) + RMS_EPS) * g


def _rope(r, cos_t, sin_hi, sin_lo):
    return (r * cos_t + pltpu.roll(r, QK_ROPE_DIM // 2, 1) * sin_hi
            + pltpu.roll(r, LANES - QK_ROPE_DIM // 2, 1) * sin_lo)


def _mod_specs(k, n_grid):
    def spec(part):
        if n_grid == 2:
            return pl.BlockSpec((1, 1, D_MODEL), lambda b, i: (k * 8 + b, 0, part))
        if n_grid == 3:
            return pl.BlockSpec((1, 1, D_MODEL), lambda b, i, f: (k * 8 + b, 0, part))
        return pl.BlockSpec((1, 1, D_MODEL), lambda b, i, e, f: (k * 8 + b, 0, part))
    return spec(0), spec(1), spec(2)


def _ada_body(c_ref, w_ref, b_ref, o_ref):
    s = jax.nn.silu(c_ref[...]).astype(BF16)
    o_ref[0] = _dot(s, w_ref[0].astype(BF16)) + b_ref[0]


def _ada_modulation(c, ada_w, ada_b):
    n_mod = 2 * DEPTH
    tn = 1024
    c_pad = jnp.pad(c, ((0, 8 - BATCH), (0, 0)))
    out = pl.pallas_call(
        _ada_body,
        grid=(n_mod, 3 * D_MODEL // tn),
        in_specs=[
            pl.BlockSpec((8, D_MODEL), lambda k, n: (0, 0)),
            pl.BlockSpec((1, D_MODEL, tn), lambda k, n: (k, 0, n)),
            pl.BlockSpec((1, 1, tn), lambda k, n: (k, 0, n)),
        ],
        out_specs=pl.BlockSpec((1, 8, tn), lambda k, n: (k, 0, n)),
        out_shape=jax.ShapeDtypeStruct((n_mod, 8, 3 * D_MODEL), F32),
        compiler_params=_params(("parallel", "parallel")),
        name="ada_mod",
    )(c_pad, ada_w.reshape(n_mod, D_MODEL, 3 * D_MODEL), ada_b.reshape(n_mod, 1, 3 * D_MODEL))
    return out.reshape(n_mod * 8, 1, 3 * D_MODEL)


def _rope_table_body(pos_ref, inv_ref, cos_ref, hi_ref, lo_ref):
    ang = pos_ref[0].astype(F32) * inv_ref[...]
    lane = lax.broadcasted_iota(jnp.int32, ang.shape, 1)
    cos = jnp.cos(ang)
    sin = jnp.sin(ang)
    half = QK_ROPE_DIM // 2
    cos_ref[0] = jnp.where(lane < QK_ROPE_DIM, cos, 0.0)
    hi_ref[0] = jnp.where(lane < half, 0.0, jnp.where(lane < QK_ROPE_DIM, sin, 0.0))
    lo_ref[0] = jnp.where(lane < half, -sin, 0.0)


def _rope_tables(positions):
    tm = 512
    half = QK_ROPE_DIM // 2
    inv = ROPE_THETA ** (-jnp.arange(0, QK_ROPE_DIM, 2, dtype=F32) / QK_ROPE_DIM)
    inv_lanes = jnp.concatenate([inv, inv, jnp.zeros((LANES - 2 * half,), F32)]).reshape(1, LANES)
    tab = jax.ShapeDtypeStruct((BATCH, SEQ, LANES), F32)
    spec = pl.BlockSpec((1, tm, LANES), lambda b, i: (b, i, 0))
    return pl.pallas_call(
        _rope_table_body,
        grid=(BATCH, SEQ // tm),
        in_specs=[pl.BlockSpec((1, tm, 1), lambda b, i: (b, i, 0)),
                  pl.BlockSpec((1, LANES), lambda b, i: (0, 0))],
        out_specs=(spec, spec, spec),
        out_shape=(tab, tab, tab),
        compiler_params=_params(("parallel", "parallel")),
        name="rope_tables",
    )(positions.reshape(BATCH, SEQ, 1), inv_lanes)


def _front_body(x_ref, sh_ref, sc_ref, wq_ref, wkv_ref, wkr_ref, wu_ref, wv_ref,
                cos_ref, hi_ref, lo_ref, qg_ref, kvg_ref, wqn_ref, wqr_ref, wkn_ref, wvt_ref,
                sg_ref, sb_ref, sw_ref, sbias_ref,
                q_ref, k_ref, vt_ref, sgu_ref, *, tm):
    hm = _modulate(x_ref[0], sh_ref[0], sc_ref[0]).astype(BF16)
    _qkv_compute(_dot(hm, wq_ref[...]), _dot(hm, wkv_ref[...]), _dot(hm, wkr_ref[...]),
                 cos_ref[0], hi_ref[0], lo_ref[0], qg_ref, kvg_ref,
                 wqn_ref, wqr_ref, wkn_ref, wvt_ref, q_ref, k_ref, vt_ref)
    _sgu_compute(_dot(hm, wu_ref[...]), _dot(hm, wv_ref[...]),
                 sg_ref, sb_ref, sw_ref, sbias_ref, sgu_ref, tm)


def _front(x, mod, k, tables, w_in, q_norm_g, w_q_up, kv_norm_g, w_kv_up,
           sgu_norm_g, sgu_norm_b, sgu_w, sgu_b):
    tm = 512
    a, b, c_, d = (Q_LORA_RANK, Q_LORA_RANK + KV_LORA_RANK,
                   Q_LORA_RANK + KV_LORA_RANK + QK_ROPE_DIM,
                   Q_LORA_RANK + KV_LORA_RANK + QK_ROPE_DIM + SGU_WIDTH)
    wq = w_in[:, :a].astype(BF16)
    wkv = w_in[:, a:b].astype(BF16)
    wkr = jnp.pad(w_in[:, b:c_], ((0, 0), (0, LANES - QK_ROPE_DIM))).astype(BF16)
    wu = w_in[:, c_:d].astype(BF16)
    wv = w_in[:, d:].astype(BF16)
    in_widths = (Q_LORA_RANK, KV_LORA_RANK, LANES, SGU_WIDTH, SGU_WIDTH)

    wq_up = w_q_up.reshape(Q_LORA_RANK, MLA_HEADS, QK_NOPE_DIM + QK_ROPE_DIM)
    wqn = wq_up[:, :, :QK_NOPE_DIM].reshape(Q_LORA_RANK, MLA_HEADS * LANES).astype(BF16)
    wqr = jnp.pad(wq_up[:, :, QK_NOPE_DIM:], ((0, 0), (0, 0), (0, LANES - QK_ROPE_DIM)))
    wqr = wqr.reshape(Q_LORA_RANK, MLA_HEADS * LANES).astype(BF16)
    wkv_up = w_kv_up.reshape(KV_LORA_RANK, MLA_HEADS, QK_NOPE_DIM + V_HEAD_DIM)
    wkn = wkv_up[:, :, :QK_NOPE_DIM].reshape(KV_LORA_RANK, MLA_HEADS * LANES).astype(BF16)
    wvt = wkv_up[:, :, QK_NOPE_DIM:].reshape(KV_LORA_RANK, ATTN_WIDTH).T.astype(BF16)
    sgu_bias = jnp.repeat(sgu_b.T, SGU_GROUP_DIM, axis=1)

    shift, scale, _ = _mod_specs(k, 2)

    def rows(w):
        return pl.BlockSpec((1, tm, w), lambda b, i: (b, i, 0))

    return pl.pallas_call(
        functools.partial(_front_body, tm=tm),
        grid=(BATCH, SEQ // tm),
        in_specs=[rows(D_MODEL), shift, scale]
                 + [_const_spec((D_MODEL, w)) for w in in_widths]
                 + [rows(LANES), rows(LANES), rows(LANES),
                    _const_spec((1, Q_LORA_RANK)), _const_spec((1, KV_LORA_RANK)),
                    _const_spec((Q_LORA_RANK, MLA_HEADS * LANES)),
                    _const_spec((Q_LORA_RANK, MLA_HEADS * LANES)),
                    _const_spec((KV_LORA_RANK, MLA_HEADS * LANES)),
                    _const_spec((ATTN_WIDTH, KV_LORA_RANK)),
                    _const_spec((1, SGU_WIDTH)), _const_spec((1, SGU_WIDTH)),
                    _const_spec((SGU_GROUPS, CHUNK, CHUNK)), _const_spec((CHUNK, SGU_WIDTH))],
        out_specs=(rows(QK_WIDTH), rows(QK_WIDTH),
                   pl.BlockSpec((1, ATTN_WIDTH, tm), lambda b, i: (b, 0, i)),
                   rows(SGU_WIDTH)),
        out_shape=(jax.ShapeDtypeStruct((BATCH, SEQ, QK_WIDTH), BF16),
                   jax.ShapeDtypeStruct((BATCH, SEQ, QK_WIDTH), BF16),
                   jax.ShapeDtypeStruct((BATCH, ATTN_WIDTH, SEQ), BF16),
                   jax.ShapeDtypeStruct((BATCH, SEQ, SGU_WIDTH), BF16)),
        compiler_params=_params(("parallel", "parallel")),
        name="front_ab",
    )(x, mod, mod, wq, wkv, wkr, wu, wv, *tables,
      q_norm_g.reshape(1, -1), kv_norm_g.reshape(1, -1), wqn, wqr, wkn, wvt,
      sgu_norm_g.reshape(1, -1), sgu_norm_b.reshape(1, -1), sgu_w, sgu_bias)


def _qkv_compute(cq, ckv, kr, cos_t, sin_hi, sin_lo, qg_ref, kvg_ref,
                 wqn_ref, wqr_ref, wkn_ref, wvt_ref, q_ref, k_ref, vt_ref):
    cqn = _rms(cq, qg_ref[...]).astype(BF16)
    qn = _dot(cqn, wqn_ref[...])
    qr = _dot(cqn, wqr_ref[...])
    ckvn = _rms(ckv, kvg_ref[...]).astype(BF16)
    kn = _dot(ckvn, wkn_ref[...])
    vt_ref[0] = lax.dot_general(wvt_ref[...], ckvn, (((1,), (1,)), ((), ())),
                                preferred_element_type=F32).astype(BF16)
    k_rope = _rope(kr, cos_t, sin_hi, sin_lo).astype(BF16)
    for h in range(MLA_HEADS):
        nope = slice(h * LANES, (h + 1) * LANES)
        lo = h * HEAD_LANES
        q_ref[0, :, lo:lo + LANES] = qn[:, nope].astype(BF16)
        q_ref[0, :, lo + LANES:lo + HEAD_LANES] = _rope(qr[:, nope], cos_t, sin_hi, sin_lo).astype(BF16)
        k_ref[0, :, lo:lo + LANES] = kn[:, nope].astype(BF16)
        k_ref[0, :, lo + LANES:lo + HEAD_LANES] = k_rope


def _attn_body(qi_ref, kj_ref, q_ref, k_ref, vt_ref, o_ref, m_sc, l_sc, acc_sc, *, tq):
    p = pl.program_id(1)
    i = qi_ref[p]
    j = kj_ref[p]

    @pl.when(j == 0)
    def _():
        m_sc[...] = jnp.full(m_sc.shape, -jnp.inf, F32)
        l_sc[...] = jnp.zeros(l_sc.shape, F32)
        acc_sc[...] = jnp.zeros(acc_sc.shape, F32)

    def accumulate(diagonal):
        if diagonal:
            key = lax.broadcasted_iota(jnp.int32, (tq, tq), 0)
            qry = lax.broadcasted_iota(jnp.int32, (tq, tq), 1)
            visible = key <= qry
        for h in range(MLA_HEADS):
            qk = slice(h * HEAD_LANES, (h + 1) * HEAD_LANES)
            hv = slice(h * V_HEAD_DIM, (h + 1) * V_HEAD_DIM)
            s = lax.dot_general(k_ref[0, :, qk], q_ref[0, :, qk], (((1,), (1,)), ((), ())),
                                preferred_element_type=F32) * (SM_SCALE * LOG2_E)
            if diagonal:
                s = jnp.where(visible, s, -jnp.inf)
            m_prev = m_sc[h]
            m_new = jnp.maximum(m_prev, jnp.max(s, axis=0, keepdims=True))
            alpha = jnp.exp2(m_prev - m_new)
            pr = jnp.exp2(s - m_new)
            l_sc[h] = alpha * l_sc[h] + jnp.sum(pr, axis=0, keepdims=True)
            acc_sc[hv, :] = alpha * acc_sc[hv, :] + _dot(vt_ref[0, hv, :], pr.astype(BF16))
            m_sc[h] = m_new

    @pl.when(j < i)
    def _():
        accumulate(False)

    @pl.when(j == i)
    def _():
        accumulate(True)
        for h in range(MLA_HEADS):
            hv = slice(h * V_HEAD_DIM, (h + 1) * V_HEAD_DIM)
            o_ref[0, :, hv] = (acc_sc[hv, :] / l_sc[h]).T.astype(BF16)


def _attention(q, k, vt):
    tq = 512
    nb = SEQ // tq
    pairs = [(i, j) for i in range(nb) for j in range(i + 1)]
    qi = jnp.asarray(np.array([p[0] for p in pairs], np.int32))
    kj = jnp.asarray(np.array([p[1] for p in pairs], np.int32))
    grid_spec = pltpu.PrefetchScalarGridSpec(
        num_scalar_prefetch=2,
        grid=(BATCH, len(pairs)),
        in_specs=[pl.BlockSpec((1, tq, QK_WIDTH), lambda b, p, qi, kj: (b, qi[p], 0)),
                  pl.BlockSpec((1, tq, QK_WIDTH), lambda b, p, qi, kj: (b, kj[p], 0)),
                  pl.BlockSpec((1, ATTN_WIDTH, tq), lambda b, p, qi, kj: (b, 0, kj[p]))],
        out_specs=pl.BlockSpec((1, tq, ATTN_WIDTH), lambda b, p, qi, kj: (b, qi[p], 0)),
        scratch_shapes=[pltpu.VMEM((MLA_HEADS, 1, tq), F32),
                        pltpu.VMEM((MLA_HEADS, 1, tq), F32),
                        pltpu.VMEM((ATTN_WIDTH, tq), F32)],
    )
    return pl.pallas_call(
        functools.partial(_attn_body, tq=tq),
        grid_spec=grid_spec,
        out_shape=jax.ShapeDtypeStruct((BATCH, SEQ, ATTN_WIDTH), BF16),
        compiler_params=_params(("parallel", "arbitrary")),
        name="mla_attention",
    )(qi, kj, q, k, vt)


def _sgu_compute(zu, zv, g_ref, b_ref, w_ref, bias_ref, o_ref, tm):
    row = lax.broadcasted_iota(jnp.int32, (CHUNK, CHUNK), 0)
    col = lax.broadcasted_iota(jnp.int32, (CHUNK, CHUNK), 1)
    for g in range(SGU_GROUPS):
        gs = slice(g * SGU_GROUP_DIM, (g + 1) * SGU_GROUP_DIM)
        v = jax.nn.gelu(zv[:, gs])
        mu = jnp.mean(v, axis=-1, keepdims=True)
        vc = v - mu
        var = jnp.mean(vc * vc, axis=-1, keepdims=True)
        vn = (vc * lax.rsqrt(var + LN_EPS) * g_ref[:, gs] + b_ref[:, gs]).astype(BF16)
        w = jnp.where(col <= row, w_ref[g], 0.0).astype(BF16)
        for n in range(tm // CHUNK):
            rs = slice(n * CHUNK, (n + 1) * CHUNK)
            s = _dot(w, vn[rs]) + bias_ref[:, gs]
            o_ref[0, rs, gs] = (jax.nn.gelu(zu[rs, gs]) * s).astype(BF16)


def _outproj_body(a_ref, s_ref, x_ref, gate_ref, wa_ref, ws_ref, lg_ref, lb_ref, o_ref):
    y = _dot(a_ref[0], wa_ref[...]) + _dot(s_ref[0], ws_ref[...])
    o_ref[0] = _deepnorm(x_ref[0], y, gate_ref[0], lg_ref[...], lb_ref[...])


def _outproj(attn, sgu, x, mod, k, w_out, ln_g, ln_b):
    tm = 512
    wa = w_out[:ATTN_WIDTH].astype(BF16)
    ws = w_out[ATTN_WIDTH:].astype(BF16)
    _, _, gate = _mod_specs(k, 2)

    def rows(w):
        return pl.BlockSpec((1, tm, w), lambda b, i: (b, i, 0))

    return pl.pallas_call(
        _outproj_body,
        grid=(BATCH, SEQ // tm),
        in_specs=[rows(ATTN_WIDTH), rows(SGU_WIDTH), rows(D_MODEL), gate,
                  _const_spec((ATTN_WIDTH, D_MODEL)), _const_spec((SGU_WIDTH, D_MODEL)),
                  _const_spec((1, D_MODEL)), _const_spec((1, D_MODEL))],
        out_specs=rows(D_MODEL),
        out_shape=jax.ShapeDtypeStruct((BATCH, SEQ, D_MODEL), F32),
        compiler_params=_params(("parallel", "parallel")),
        name="outproj_ab",
    )(attn, sgu, x, mod, wa, ws, ln_g.reshape(1, -1), ln_b.reshape(1, -1))


def _ffn_up_body(x_ref, sh_ref, sc_ref, wg_ref, wu_ref, h_ref, hm_sc):
    @pl.when(pl.program_id(2) == 0)
    def _():
        hm_sc[...] = _modulate(x_ref[0], sh_ref[0], sc_ref[0]).astype(BF16)

    hm = hm_sc[...]
    g = _dot(hm, wg_ref[0].astype(BF16))
    u = _dot(hm, wu_ref[0].astype(BF16))
    h_ref[0] = (jax.nn.silu(g) * u).astype(BF16)


def _ffn_down_body(h_ref, x_ref, gate_ref, wd_ref, lg_ref, lb_ref, o_ref, y_sc, *, n_n):
    n = pl.program_id(2)
    y_sc[n] = _dot(h_ref[0], wd_ref[...])

    @pl.when(n == n_n - 1)
    def _():
        y = jnp.concatenate([y_sc[c] for c in range(n_n)], axis=-1)
        o_ref[0] = _deepnorm(x_ref[0], y, gate_ref[0], lg_ref[...], lb_ref[...])


def _ffn(x, mod, k, layer, w_gate, w_up, w_down, ln_g, ln_b):
    tm, tf = 2048, 256
    shift, scale, gate = _mod_specs(k, 3)
    h = pl.pallas_call(
        _ffn_up_body,
        grid=(BATCH, SEQ // tm, D_FF // tf),
        in_specs=[pl.BlockSpec((1, tm, D_MODEL), lambda b, i, f: (b, i, 0), pipeline_mode=pl.Buffered(1)),
                  shift, scale,
                  pl.BlockSpec((1, D_MODEL, tf), lambda b, i, f: (layer, 0, f)),
                  pl.BlockSpec((1, D_MODEL, tf), lambda b, i, f: (layer, 0, f))],
        out_specs=pl.BlockSpec((1, tm, tf), lambda b, i, f: (b, i, f)),
        out_shape=jax.ShapeDtypeStruct((BATCH, SEQ, D_FF), BF16),
        scratch_shapes=[pltpu.VMEM((tm, D_MODEL), BF16)],
        compiler_params=_params(("parallel", "parallel", "arbitrary")),
        name="ffn_up",
    )(x, mod, mod, w_gate, w_up)

    tm, tn = 512, 512
    n_n = D_MODEL // tn
    rows = pl.BlockSpec((1, tm, D_MODEL), lambda b, i, n: (b, i, 0))
    vec = pl.BlockSpec((1, D_MODEL), lambda b, i, n: (0, 0))
    return pl.pallas_call(
        functools.partial(_ffn_down_body, n_n=n_n),
        grid=(BATCH, SEQ // tm, n_n),
        in_specs=[pl.BlockSpec((1, tm, D_FF), lambda b, i, n: (b, i, 0)),
                  rows,
                  gate,
                  pl.BlockSpec((D_FF, tn), lambda b, i, n: (0, n)),
                  vec, vec],
        out_specs=rows,
        out_shape=jax.ShapeDtypeStruct((BATCH, SEQ, D_MODEL), F32),
        scratch_shapes=[pltpu.VMEM((n_n, tm, tn), F32)],
        compiler_params=_params(("parallel", "parallel", "arbitrary")),
        name="ffn_down",
    )(h, x, mod, w_down.astype(BF16), ln_g.reshape(1, -1), ln_b.reshape(1, -1))


def _pool_body(x_ref, halo_ref, sh_ref, sc_ref, gate_ref, pw_ref, ps_ref, wo_ref, lg_ref, lb_ref,
               o_ref, *, tm):
    i = pl.program_id(1)
    x = x_ref[0]
    hm = _modulate(x, sh_ref[0], sc_ref[0])
    halo = jnp.where(i > 0, _modulate(halo_ref[0], sh_ref[0], sc_ref[0]), 0.0)
    ext = jnp.concatenate([halo, hm], axis=0)
    ext_hi = ext.astype(BF16)
    ext_lo = (ext - ext_hi.astype(F32)).astype(BF16)
    t_loc = lax.broadcasted_iota(jnp.int32, (tm, tm + POOL_HALO), 0)
    s_loc = lax.broadcasted_iota(jnp.int32, (tm, tm + POOL_HALO), 1)
    lag = t_loc + POOL_HALO - s_loc
    t_glob = i * tm + lax.broadcasted_iota(jnp.int32, (tm, 1), 0)
    outs = []
    for gi, w in enumerate(POOL_WINDOWS):
        cs = slice(gi * POOL_GROUP_DIM, (gi + 1) * POOL_GROUP_DIM)
        band = jnp.where(lag >= 0, jnp.where(lag < w, 1.0, 0.0), 0.0).astype(BF16)
        wsum = _dot(band, ext_hi[:, cs]) + _dot(band, ext_lo[:, cs])
        count = jnp.minimum(t_glob + 1, w).astype(F32)
        d = (wsum / count - hm[:, cs]).astype(BF16)
        outs.append(_dot(d, pw_ref[gi]))
    pooled = (jnp.concatenate(outs, axis=-1) * ps_ref[...]).astype(BF16)
    y = _dot(pooled, wo_ref[...])
    o_ref[0] = _deepnorm(x, y, gate_ref[0], lg_ref[...], lb_ref[...])


def _pool_mixer(x, mod, k, pool_w, pool_scale, w_out_c, ln_g, ln_b):
    tm = 512
    per = tm // POOL_HALO
    shift, scale, gate = _mod_specs(k, 2)
    rows = pl.BlockSpec((1, tm, D_MODEL), lambda b, i: (b, i, 0))
    halo = pl.BlockSpec((1, POOL_HALO, D_MODEL), lambda b, i: (b, jnp.maximum(i * per - 1, 0), 0))
    return pl.pallas_call(
        functools.partial(_pool_body, tm=tm),
        grid=(BATCH, SEQ // tm),
        in_specs=[rows, halo, shift, scale, gate,
                  _const_spec((len(POOL_WINDOWS), POOL_GROUP_DIM, POOL_GROUP_DIM)),
                  _const_spec((1, D_MODEL)), _const_spec((D_MODEL, D_MODEL)),
                  _const_spec((1, D_MODEL)), _const_spec((1, D_MODEL))],
        out_specs=rows,
        out_shape=jax.ShapeDtypeStruct((BATCH, SEQ, D_MODEL), F32),
        compiler_params=_params(("parallel", "parallel")),
        name="pool_mixer",
    )(x, x, mod, mod, mod, pool_w.astype(BF16), pool_scale.reshape(1, -1),
      w_out_c.astype(BF16), ln_g.reshape(1, -1), ln_b.reshape(1, -1))


def _router_body(x_ref, sh_ref, sc_ref, wr_ref, g_ref, hm_ref):
    hm = _modulate(x_ref[0], sh_ref[0], sc_ref[0])
    hm_ref[0] = hm
    hm_hi = hm.astype(BF16)
    hm_lo = (hm - hm_hi.astype(F32)).astype(BF16)
    w = wr_ref[...]
    w_hi = w.astype(BF16)
    w_lo = (w - w_hi.astype(F32)).astype(BF16)
    logits = _dot(hm_hi, w_hi) + (_dot(hm_lo, w_hi) + _dot(hm_hi, w_lo))
    lane = lax.broadcasted_iota(jnp.int32, logits.shape, 1).astype(F32)
    logits = jnp.where(lane < N_EXPERTS, logits, -jnp.inf)
    v1 = jnp.max(logits, axis=-1, keepdims=True)
    i1 = jnp.min(jnp.where(logits == v1, lane, float(LANES)), axis=-1, keepdims=True)
    rest = jnp.where(lane == i1, -jnp.inf, logits)
    v2 = jnp.max(rest, axis=-1, keepdims=True)
    i2 = jnp.min(jnp.where(rest == v2, lane, float(LANES)), axis=-1, keepdims=True)
    e2 = jnp.exp(v2 - v1)
    den = 1.0 + e2
    g_ref[0] = (jnp.where(lane == 0.0, i1, 0.0) + jnp.where(lane == 1.0, i2, 0.0)
                + jnp.where(lane == 2.0, 1.0 / den, 0.0) + jnp.where(lane == 3.0, e2 / den, 0.0))


def _router(x, mod, k, w_router):
    tm = 512
    shift, scale, _ = _mod_specs(k, 2)
    wr = jnp.pad(w_router, ((0, 0), (0, LANES - N_EXPERTS)))
    rows = pl.BlockSpec((1, tm, D_MODEL), lambda b, i: (b, i, 0))
    return pl.pallas_call(
        _router_body,
        grid=(BATCH, SEQ // tm),
        in_specs=[rows, shift, scale, _const_spec((D_MODEL, LANES))],
        out_specs=(pl.BlockSpec((1, tm, LANES), lambda b, i: (b, i, 0)), rows),
        out_shape=(jax.ShapeDtypeStruct((BATCH, SEQ, LANES), F32),
                   jax.ShapeDtypeStruct((BATCH, SEQ, D_MODEL), F32)),
        compiler_params=_params(("parallel", "parallel")),
        name="router",
    )(x, mod, mod, wr)


N_TOKENS = BATCH * SEQ
N_ASSIGN = 2 * N_TOKENS
MOE_BLOCK = 1024
MOE_SUB = 256
MOE_TILE = N_ASSIGN // N_EXPERTS * 5 // 4
MOE_TILES_ANY = -(-(N_ASSIGN + N_EXPERTS * (MOE_TILE - 1)) // MOE_TILE)
MOE_TILES_FIT = N_EXPERTS
GATHER_ROWS = MOE_TILE // 4
assert MOE_TILE % GATHER_ROWS == 0 and GATHER_ROWS % 8 == 0


def _route_counts(routing):
    e_flat = routing.reshape(N_TOKENS, LANES)[:, :2].astype(jnp.int32).T.reshape(N_ASSIGN)
    onehot = (e_flat[:, None] == jnp.arange(N_EXPERTS, dtype=jnp.int32)[None, :]).astype(jnp.int32)
    csum = jnp.cumsum(onehot, axis=0)
    return onehot, jnp.sum(csum * onehot, axis=1) - 1, csum[-1]


def _route_plan(onehot, rank, counts, n_tiles):
    padded = (counts + MOE_TILE - 1) // MOE_TILE * MOE_TILE
    group_end = jnp.cumsum(padded)
    group_start = group_end - padded
    pos = (jnp.sum(onehot * group_start[None, :], axis=1) + rank).astype(jnp.int32)
    tok = jnp.tile(jnp.arange(N_TOKENS, dtype=jnp.int32), 2)
    src = jnp.zeros((n_tiles * MOE_TILE,), jnp.int32).at[pos].set(tok, unique_indices=True)
    tile_start = jnp.arange(n_tiles, dtype=jnp.int32) * MOE_TILE
    n_used = group_end[-1] // MOE_TILE
    tile_expert = jnp.sum((tile_start[:, None] >= group_end[None, :]).astype(jnp.int32), axis=1)
    last_expert = jnp.max(jnp.where(counts > 0, jnp.arange(N_EXPERTS, dtype=jnp.int32), 0))
    tile_expert = jnp.minimum(tile_expert, last_expert)
    tile_rows = jnp.clip(counts[tile_expert] - (tile_start - group_start[tile_expert]), 0, MOE_TILE)
    tile_rows = jnp.where(jnp.arange(n_tiles) < n_used, tile_rows, 0).astype(jnp.int32)
    return pos, src, tile_expert.astype(jnp.int32), tile_rows, n_used.reshape(1).astype(jnp.int32)


GATHER_UNROLL = 8


def _gather_body(src_ref, rows_ref, hm_hbm, o_ref, buf, sem, *, tg, n_steps):
    i = pl.program_id(0)
    per = MOE_TILE // tg

    def valid_rows(step):
        n_valid = jnp.clip(rows_ref[step // per] - (step % per) * tg, 0, tg)
        n_groups = n_valid // GATHER_UNROLL
        return n_valid, n_groups, pl.multiple_of(n_groups * GATHER_UNROLL, GATHER_UNROLL)

    def issue(step, slot):
        base = step * tg
        n_valid, n_groups, n_grouped = valid_rows(step)

        @pl.when(n_valid < tg)
        def _():
            buf[slot] = jnp.zeros(buf.shape[1:], F32)

        def issue_row(r, queue):
            t = src_ref[base + r]
            pltpu.make_async_copy(hm_hbm.at[pl.ds(t, 1)], buf.at[slot, pl.ds(r, 1)],
                                  sem.at[slot]).start(priority=queue)

        def issue_group(g, carry):
            for u in range(GATHER_UNROLL):
                issue_row(g * GATHER_UNROLL + u, u % 2)
            return carry

        def issue_one(r, carry):
            issue_row(r, 0)
            return carry

        lax.fori_loop(0, n_groups, issue_group, 0)
        lax.fori_loop(n_grouped, n_valid, issue_one, 0)

    def wait(step, slot):
        n_valid, n_groups, n_grouped = valid_rows(step)

        @pl.when(n_groups > 0)
        def _():
            pltpu.make_async_copy(hm_hbm.at[pl.ds(0, n_grouped)], buf.at[slot, pl.ds(0, n_grouped)],
                                  sem.at[slot]).wait()

        def wait_one(r, carry):
            pltpu.make_async_copy(hm_hbm.at[pl.ds(0, 1)], buf.at[slot, pl.ds(r, 1)], sem.at[slot]).wait()
            return carry

        lax.fori_loop(n_grouped, n_valid, wait_one, 0)

    slot = i % 2

    @pl.when(i == 0)
    def _():
        issue(i, slot)

    @pl.when(i + 1 < n_steps)
    def _():
        issue(i + 1, 1 - slot)

    wait(i, slot)
    o_ref[...] = buf[slot].astype(BF16)


def _gather_rows(src, tile_rows, hm):
    n_rows = src.shape[0]
    tg = GATHER_ROWS
    grid_spec = pltpu.PrefetchScalarGridSpec(
        num_scalar_prefetch=2,
        grid=(n_rows // tg,),
        in_specs=[pl.BlockSpec(memory_space=pl.ANY)],
        out_specs=pl.BlockSpec((tg, D_MODEL), lambda i, src, tr: (i, 0)),
        scratch_shapes=[pltpu.VMEM((2, tg, D_MODEL), F32), pltpu.SemaphoreType.DMA((2,))],
    )
    return pl.pallas_call(
        functools.partial(_gather_body, tg=tg, n_steps=n_rows // tg),
        grid_spec=grid_spec,
        out_shape=jax.ShapeDtypeStruct((n_rows, D_MODEL), BF16),
        compiler_params=_params(("arbitrary",)),
        name="moe_gather",
    )(src, tile_rows, hm.reshape(N_TOKENS, D_MODEL))


def _for_valid_blocks(rows, tm, block_fn, empty_fn):
    for b0 in range(0, tm, MOE_BLOCK):
        b1 = min(b0 + MOE_BLOCK, tm)
        whole = b1 if b1 - b0 == MOE_BLOCK else tm + 1
        if whole <= tm:
            pl.when(rows >= whole)(functools.partial(block_fn, slice(b0, b1)))

        for r0 in range(b0, b1, MOE_SUB):
            rs = slice(r0, r0 + MOE_SUB)

            @pl.when(jnp.logical_and(rows < whole, r0 < rows))
            def _():
                block_fn(rs)

            @pl.when(r0 >= rows)
            def _():
                empty_fn(rs)


def _gup_body(te_ref, rows_ref, nu_ref, xs_ref, wg_ref, wu_ref, h_ref, wgb, wub, *, tm):
    rows = rows_ref[pl.program_id(0)]

    @pl.when(rows > 0)
    def _():
        wgb[...] = wg_ref[0].astype(BF16)
        wub[...] = wu_ref[0].astype(BF16)

    def block(rs):
        xsb = xs_ref[rs, :]
        h_ref[rs, :] = (jax.nn.silu(_dot(xsb, wgb[...])) * _dot(xsb, wub[...])).astype(BF16)

    def empty(rs):
        h_ref[rs, :] = jnp.zeros((rs.stop - rs.start, h_ref.shape[1]), BF16)

    _for_valid_blocks(rows, tm, block, empty)


def _gdown_body(te_ref, rows_ref, nu_ref, h_ref, wd_ref, o_ref, wdb, *, tm):
    rows = rows_ref[pl.program_id(0)]

    @pl.when(rows > 0)
    def _():
        wdb[...] = wd_ref[0].astype(BF16)

    def block(rs):
        o_ref[rs, :] = _dot(h_ref[rs, :], wdb[...])

    def empty(rs):
        o_ref[rs, :] = jnp.zeros((rs.stop - rs.start, o_ref.shape[1]), F32)

    _for_valid_blocks(rows, tm, block, empty)


def _grouped_ffn(xs, tile_expert, tile_rows, n_used, layer, w_gate, w_up, w_down):
    tm = MOE_TILE
    n_tiles = tile_expert.shape[0]
    first = layer * N_EXPERTS
    n_all = w_gate.shape[0] * N_EXPERTS

    def tile(i, nu):
        return jnp.minimum(i, nu[0] - 1)

    def chunk(i, c, nu, n_chunks):
        return jnp.where(i < nu[0], c, n_chunks - 1)

    tf = 256
    n_f = D_FF_EXPERT // tf
    w_up_spec = pl.BlockSpec((1, D_MODEL, tf),
                             lambda i, f, te, tr, nu: (first + te[i], 0, chunk(i, f, nu, n_f)))
    up_spec = pltpu.PrefetchScalarGridSpec(
        num_scalar_prefetch=3,
        grid=(n_tiles, n_f),
        in_specs=[pl.BlockSpec((tm, D_MODEL), lambda i, f, te, tr, nu: (tile(i, nu), 0)),
                  w_up_spec, w_up_spec],
        out_specs=pl.BlockSpec((tm, tf), lambda i, f, te, tr, nu: (i, f)),
        scratch_shapes=[pltpu.VMEM((D_MODEL, tf), BF16), pltpu.VMEM((D_MODEL, tf), BF16)],
    )
    h = pl.pallas_call(
        functools.partial(_gup_body, tm=tm),
        grid_spec=up_spec,
        out_shape=jax.ShapeDtypeStruct((n_tiles * tm, D_FF_EXPERT), BF16),
        compiler_params=_params(("arbitrary", "arbitrary")),
        name="moe_up",
    )(tile_expert, tile_rows, n_used, xs,
      w_gate.reshape(n_all, D_MODEL, D_FF_EXPERT), w_up.reshape(n_all, D_MODEL, D_FF_EXPERT))

    tn = 512
    n_n = D_MODEL // tn
    down_spec = pltpu.PrefetchScalarGridSpec(
        num_scalar_prefetch=3,
        grid=(n_tiles, n_n),
        in_specs=[pl.BlockSpec((tm, D_FF_EXPERT), lambda i, n, te, tr, nu: (tile(i, nu), 0)),
                  pl.BlockSpec((1, D_FF_EXPERT, tn),
                               lambda i, n, te, tr, nu: (first + te[i], 0, chunk(i, n, nu, n_n)))],
        out_specs=pl.BlockSpec((tm, tn), lambda i, n, te, tr, nu: (i, n)),
        scratch_shapes=[pltpu.VMEM((D_FF_EXPERT, tn), BF16)],
    )
    return pl.pallas_call(
        functools.partial(_gdown_body, tm=tm),
        grid_spec=down_spec,
        out_shape=jax.ShapeDtypeStruct((n_tiles * tm, D_MODEL), F32),
        compiler_params=_params(("arbitrary", "arbitrary")),
        name="moe_down",
    )(tile_expert, tile_rows, n_used, h, w_down.reshape(n_all, D_FF_EXPERT, D_MODEL))


def _combine_body(pos_ref, ys_hbm, r_ref, x_ref, gate_ref, lg_ref, lb_ref, o_ref, buf, sem, *, tc):
    n_steps = pl.num_programs(0) * pl.num_programs(1)
    step = pl.program_id(0) * pl.num_programs(1) + pl.program_id(1)

    def issue(s, slot):
        base = s * tc

        def issue_token(r, carry):
            for k in range(2):
                p = pos_ref[k * N_TOKENS + base + r]
                pltpu.make_async_copy(ys_hbm.at[pl.ds(p, 1)], buf.at[slot, k, pl.ds(r, 1)],
                                      sem.at[slot, k]).start(priority=k)
            return carry

        lax.fori_loop(0, tc, issue_token, 0, unroll=4)

    slot = step % 2

    @pl.when(step == 0)
    def _():
        issue(step, slot)

    @pl.when(step + 1 < n_steps)
    def _():
        issue(step + 1, 1 - slot)

    for k in range(2):
        pltpu.make_async_copy(ys_hbm.at[pl.ds(0, tc)], buf.at[slot, k], sem.at[slot, k]).wait()
    routing = r_ref[0]
    y = routing[:, 2:3] * buf[slot, 0] + routing[:, 3:4] * buf[slot, 1]
    o_ref[0] = _deepnorm(x_ref[0], y, gate_ref[0], lg_ref[...], lb_ref[...])


def _combine(pos, ys, routing, x, mod, k, ln_g, ln_b):
    tc = 512
    rows = pl.BlockSpec((1, tc, D_MODEL), lambda b, i, pos: (b, i, 0))
    vec = pl.BlockSpec((1, D_MODEL), lambda b, i, pos: (0, 0))
    grid_spec = pltpu.PrefetchScalarGridSpec(
        num_scalar_prefetch=1,
        grid=(BATCH, SEQ // tc),
        in_specs=[pl.BlockSpec(memory_space=pl.ANY),
                  pl.BlockSpec((1, tc, LANES), lambda b, i, pos: (b, i, 0)),
                  rows,
                  pl.BlockSpec((1, 1, D_MODEL), lambda b, i, pos: (k * 8 + b, 0, 2)),
                  vec, vec],
        out_specs=rows,
        scratch_shapes=[pltpu.VMEM((2, 2, tc, D_MODEL), F32), pltpu.SemaphoreType.DMA((2, 2))],
    )
    return pl.pallas_call(
        functools.partial(_combine_body, tc=tc),
        grid_spec=grid_spec,
        out_shape=jax.ShapeDtypeStruct((BATCH, SEQ, D_MODEL), F32),
        compiler_params=_params(("arbitrary", "arbitrary")),
        name="moe_combine",
    )(pos, ys, routing, x, mod, ln_g.reshape(1, -1), ln_b.reshape(1, -1))


def _moe(x, mod, k, w_router, layer, w_gate, w_up, w_down, ln_g, ln_b):
    routing, hm = _router(x, mod, k, w_router)
    onehot, rank, counts = _route_counts(routing)

    def experts_over(n_tiles):
        def run(_):
            pos, src, tile_expert, tile_rows, n_used = _route_plan(onehot, rank, counts, n_tiles)
            xs = _gather_rows(src, tile_rows, hm)
            ys = _grouped_ffn(xs, tile_expert, tile_rows, n_used, layer, w_gate, w_up, w_down)
            return _combine(pos, ys, routing, x, mod, k, ln_g, ln_b)
        return run

    return lax.cond(jnp.all(counts <= MOE_TILE),
                    experts_over(MOE_TILES_FIT), experts_over(MOE_TILES_ANY), None)


def kernel(x, c, positions, ada_w, ada_b, ln_g, ln_b, w_in_ab, q_norm_g, w_q_up, kv_norm_g, w_kv_up, sgu_norm_g, sgu_norm_b, sgu_w, sgu_b, w_out_ab, ffn_w_gate, ffn_w_up, ffn_w_down, pool_w, pool_scale, w_out_c, router_w, moe_w_gate, moe_w_up, moe_w_down):
    mod = _ada_modulation(c, ada_w, ada_b)
    tables = _rope_tables(positions)
    for l in range(DEPTH):
        j = l // 2
        k_tok, k_ch = 2 * l, 2 * l + 1
        if l % 2 == 0:
            q, k, vt, sgu = _front(x, mod, k_tok, tables, w_in_ab[j], q_norm_g[j], w_q_up[j],
                                   kv_norm_g[j], w_kv_up[j],
                                   sgu_norm_g[j], sgu_norm_b[j], sgu_w[j], sgu_b[j])
            attn = _attention(q, k, vt)
            x = _outproj(attn, sgu, x, mod, k_tok, w_out_ab[j], ln_g[l, 0], ln_b[l, 0])
            x = _ffn(x, mod, k_ch, j, ffn_w_gate, ffn_w_up, ffn_w_down[j], ln_g[l, 1], ln_b[l, 1])
        else:
            x = _pool_mixer(x, mod, k_tok, pool_w[j], pool_scale[j], w_out_c[j], ln_g[l, 0], ln_b[l, 0])
            x = _moe(x, mod, k_ch, router_w[j], j, moe_w_gate, moe_w_up, moe_w_down,
                     ln_g[l, 1], ln_b[l, 1])
    return x
```

```python
import functools

import jax
import jax.numpy as jnp
import numpy as np
from jax import lax
from jax.experimental import pallas as pl
from jax.experimental.pallas import tpu as pltpu

F32 = jnp.float32
BF16 = jnp.bfloat16

D_MODEL = 2048
BATCH = 2
SEQ = 4096
DEPTH = 4
MLA_HEADS = 8
QK_NOPE_DIM = 128
QK_ROPE_DIM = 64
V_HEAD_DIM = 128
Q_LORA_RANK = 768
KV_LORA_RANK = 512
ROPE_THETA = 10000.0
SGU_GROUPS = 8
SGU_GROUP_DIM = 128
SGU_WIDTH = SGU_GROUPS * SGU_GROUP_DIM
CHUNK = 128
ATTN_WIDTH = MLA_HEADS * V_HEAD_DIM
POOL_WINDOWS = (2, 4, 8, 16)
POOL_GROUP_DIM = D_MODEL // 4
D_FF = 5632
N_EXPERTS = 8
D_FF_EXPERT = 2816
DEEPNORM_ALPHA = (2 * DEPTH) ** 0.25
LN_EPS = 1e-5
RMS_EPS = 1e-6
SM_SCALE = (QK_NOPE_DIM + QK_ROPE_DIM) ** -0.5
LOG2_E = 1.4426950408889634

LANES = 128
HEAD_LANES = 2 * LANES
QK_WIDTH = MLA_HEADS * HEAD_LANES
VMEM_LIMIT = 60 * 1024 * 1024
POOL_HALO = 128


def _params(semantics):
    return pltpu.CompilerParams(dimension_semantics=semantics, vmem_limit_bytes=VMEM_LIMIT)


def _const_spec(shape):
    nd = len(shape)
    return pl.BlockSpec(shape, lambda *_: (0,) * nd, pipeline_mode=pl.Buffered(1))


def _dot(a, b):
    return jnp.dot(a, b, preferred_element_type=F32)


def _modulate(x, shift, scale):
    return x * (1.0 + scale) + shift


def _deepnorm(x, y, gate, g, b):
    r = DEEPNORM_ALPHA * x + (1.0 + gate) * y
    mu = jnp.mean(r, axis=-1, keepdims=True)
    rc = r - mu
    var = jnp.mean(rc * rc, axis=-1, keepdims=True)
    return rc * lax.rsqrt(var + LN_EPS) * g + b


def _rms(x, g):
    return x * lax.rsqrt(jnp.mean(x * x, axis=-1, keepdims=True) + RMS_EPS) * g


def _rope(r, cos_t, sin_hi, sin_lo):
    return (r * cos_t + pltpu.roll(r, QK_ROPE_DIM // 2, 1) * sin_hi
            + pltpu.roll(r, LANES - QK_ROPE_DIM // 2, 1) * sin_lo)


def _mod_specs(k, n_grid):
    def spec(part):
        if n_grid == 2:
            return pl.BlockSpec((1, 1, D_MODEL), lambda b, i: (k * 8 + b, 0, part))
        if n_grid == 3:
            return pl.BlockSpec((1, 1, D_MODEL), lambda b, i, f: (k * 8 + b, 0, part))
        return pl.BlockSpec((1, 1, D_MODEL), lambda b, i, e, f: (k * 8 + b, 0, part))
    return spec(0), spec(1), spec(2)


def _ada_body(c_ref, w_ref, b_ref, o_ref):
    s = jax.nn.silu(c_ref[...]).astype(BF16)
    o_ref[0] = _dot(s, w_ref[0].astype(BF16)) + b_ref[0]


def _ada_modulation(c, ada_w, ada_b):
    n_mod = 2 * DEPTH
    tn = 1024
    c_pad = jnp.pad(c, ((0, 8 - BATCH), (0, 0)))
    out = pl.pallas_call(
        _ada_body,
        grid=(n_mod, 3 * D_MODEL // tn),
        in_specs=[
            pl.BlockSpec((8, D_MODEL), lambda k, n: (0, 0)),
            pl.BlockSpec((1, D_MODEL, tn), lambda k, n: (k, 0, n)),
            pl.BlockSpec((1, 1, tn), lambda k, n: (k, 0, n)),
        ],
        out_specs=pl.BlockSpec((1, 8, tn), lambda k, n: (k, 0, n)),
        out_shape=jax.ShapeDtypeStruct((n_mod, 8, 3 * D_MODEL), F32),
        compiler_params=_params(("parallel", "parallel")),
        name="ada_mod",
    )(c_pad, ada_w.reshape(n_mod, D_MODEL, 3 * D_MODEL), ada_b.reshape(n_mod, 1, 3 * D_MODEL))
    return out.reshape(n_mod * 8, 1, 3 * D_MODEL)


def _rope_table_body(pos_ref, inv_ref, cos_ref, hi_ref, lo_ref):
    ang = pos_ref[0].astype(F32) * inv_ref[...]
    lane = lax.broadcasted_iota(jnp.int32, ang.shape, 1)
    cos = jnp.cos(ang)
    sin = jnp.sin(ang)
    half = QK_ROPE_DIM // 2
    cos_ref[0] = jnp.where(lane < QK_ROPE_DIM, cos, 0.0)
    hi_ref[0] = jnp.where(lane < half, 0.0, jnp.where(lane < QK_ROPE_DIM, sin, 0.0))
    lo_ref[0] = jnp.where(lane < half, -sin, 0.0)


def _rope_tables(positions):
    tm = 512
    half = QK_ROPE_DIM // 2
    inv = ROPE_THETA ** (-jnp.arange(0, QK_ROPE_DIM, 2, dtype=F32) / QK_ROPE_DIM)
    inv_lanes = jnp.concatenate([inv, inv, jnp.zeros((LANES - 2 * half,), F32)]).reshape(1, LANES)
    tab = jax.ShapeDtypeStruct((BATCH, SEQ, LANES), F32)
    spec = pl.BlockSpec((1, tm, LANES), lambda b, i: (b, i, 0))
    return pl.pallas_call(
        _rope_table_body,
        grid=(BATCH, SEQ // tm),
        in_specs=[pl.BlockSpec((1, tm, 1), lambda b, i: (b, i, 0)),
                  pl.BlockSpec((1, LANES), lambda b, i: (0, 0))],
        out_specs=(spec, spec, spec),
        out_shape=(tab, tab, tab),
        compiler_params=_params(("parallel", "parallel")),
        name="rope_tables",
    )(positions.reshape(BATCH, SEQ, 1), inv_lanes)


def _front_body(x_ref, sh_ref, sc_ref, wq_ref, wkv_ref, wkr_ref, wu_ref, wv_ref,
                cos_ref, hi_ref, lo_ref, qg_ref, kvg_ref, wqn_ref, wqr_ref, wkn_ref, wvt_ref,
                sg_ref, sb_ref, sw_ref, sbias_ref,
                q_ref, k_ref, vt_ref, sgu_ref, *, tm):
    hm = _modulate(x_ref[0], sh_ref[0], sc_ref[0]).astype(BF16)
    _qkv_compute(_dot(hm, wq_ref[...]), _dot(hm, wkv_ref[...]), _dot(hm, wkr_ref[...]),
                 cos_ref[0], hi_ref[0], lo_ref[0], qg_ref, kvg_ref,
                 wqn_ref, wqr_ref, wkn_ref, wvt_ref, q_ref, k_ref, vt_ref)
    _sgu_compute(_dot(hm, wu_ref[...]), _dot(hm, wv_ref[...]),
                 sg_ref, sb_ref, sw_ref, sbias_ref, sgu_ref, tm)


def _front(x, mod, k, tables, w_in, q_norm_g, w_q_up, kv_norm_g, w_kv_up,
           sgu_norm_g, sgu_norm_b, sgu_w, sgu_b):
    tm = 512
    a, b, c_, d = (Q_LORA_RANK, Q_LORA_RANK + KV_LORA_RANK,
                   Q_LORA_RANK + KV_LORA_RANK + QK_ROPE_DIM,
                   Q_LORA_RANK + KV_LORA_RANK + QK_ROPE_DIM + SGU_WIDTH)
    wq = w_in[:, :a].astype(BF16)
    wkv = w_in[:, a:b].astype(BF16)
    wkr = jnp.pad(w_in[:, b:c_], ((0, 0), (0, LANES - QK_ROPE_DIM))).astype(BF16)
    wu = w_in[:, c_:d].astype(BF16)
    wv = w_in[:, d:].astype(BF16)
    in_widths = (Q_LORA_RANK, KV_LORA_RANK, LANES, SGU_WIDTH, SGU_WIDTH)

    wq_up = w_q_up.reshape(Q_LORA_RANK, MLA_HEADS, QK_NOPE_DIM + QK_ROPE_DIM)
    wqn = wq_up[:, :, :QK_NOPE_DIM].reshape(Q_LORA_RANK, MLA_HEADS * LANES).astype(BF16)
    wqr = jnp.pad(wq_up[:, :, QK_NOPE_DIM:], ((0, 0), (0, 0), (0, LANES - QK_ROPE_DIM)))
    wqr = wqr.reshape(Q_LORA_RANK, MLA_HEADS * LANES).astype(BF16)
    wkv_up = w_kv_up.reshape(KV_LORA_RANK, MLA_HEADS, QK_NOPE_DIM + V_HEAD_DIM)
    wkn = wkv_up[:, :, :QK_NOPE_DIM].reshape(KV_LORA_RANK, MLA_HEADS * LANES).astype(BF16)
    wvt = wkv_up[:, :, QK_NOPE_DIM:].reshape(KV_LORA_RANK, ATTN_WIDTH).T.astype(BF16)
    sgu_bias = jnp.repeat(sgu_b.T, SGU_GROUP_DIM, axis=1)

    shift, scale, _ = _mod_specs(k, 2)

    def rows(w):
        return pl.BlockSpec((1, tm, w), lambda b, i: (b, i, 0))

    return pl.pallas_call(
        functools.partial(_front_body, tm=tm),
        grid=(BATCH, SEQ // tm),
        in_specs=[rows(D_MODEL), shift, scale]
                 + [_const_spec((D_MODEL, w)) for w in in_widths]
                 + [rows(LANES), rows(LANES), rows(LANES),
                    _const_spec((1, Q_LORA_RANK)), _const_spec((1, KV_LORA_RANK)),
                    _const_spec((Q_LORA_RANK, MLA_HEADS * LANES)),
                    _const_spec((Q_LORA_RANK, MLA_HEADS * LANES)),
                    _const_spec((KV_LORA_RANK, MLA_HEADS * LANES)),
                    _const_spec((ATTN_WIDTH, KV_LORA_RANK)),
                    _const_spec((1, SGU_WIDTH)), _const_spec((1, SGU_WIDTH)),
                    _const_spec((SGU_GROUPS, CHUNK, CHUNK)), _const_spec((CHUNK, SGU_WIDTH))],
        out_specs=(rows(QK_WIDTH), rows(QK_WIDTH),
                   pl.BlockSpec((1, ATTN_WIDTH, tm), lambda b, i: (b, 0, i)),
                   rows(SGU_WIDTH)),
        out_shape=(jax.ShapeDtypeStruct((BATCH, SEQ, QK_WIDTH), BF16),
                   jax.ShapeDtypeStruct((BATCH, SEQ, QK_WIDTH), BF16),
                   jax.ShapeDtypeStruct((BATCH, ATTN_WIDTH, SEQ), BF16),
                   jax.ShapeDtypeStruct((BATCH, SEQ, SGU_WIDTH), BF16)),
        compiler_params=_params(("parallel", "parallel")),
        name="front_ab",
    )(x, mod, mod, wq, wkv, wkr, wu, wv, *tables,
      q_norm_g.reshape(1, -1), kv_norm_g.reshape(1, -1), wqn, wqr, wkn, wvt,
      sgu_norm_g.reshape(1, -1), sgu_norm_b.reshape(1, -1), sgu_w, sgu_bias)


def _qkv_compute(cq, ckv, kr, cos_t, sin_hi, sin_lo, qg_ref, kvg_ref,
                 wqn_ref, wqr_ref, wkn_ref, wvt_ref, q_ref, k_ref, vt_ref):
    cqn = _rms(cq, qg_ref[...]).astype(BF16)
    qn = _dot(cqn, wqn_ref[...])
    qr = _dot(cqn, wqr_ref[...])
    ckvn = _rms(ckv, kvg_ref[...]).astype(BF16)
    kn = _dot(ckvn, wkn_ref[...])
    vt_ref[0] = lax.dot_general(wvt_ref[...], ckvn, (((1,), (1,)), ((), ())),
                                preferred_element_type=F32).astype(BF16)
    k_rope = _rope(kr, cos_t, sin_hi, sin_lo).astype(BF16)
    for h in range(MLA_HEADS):
        nope = slice(h * LANES, (h + 1) * LANES)
        lo = h * HEAD_LANES
        q_ref[0, :, lo:lo + LANES] = qn[:, nope].astype(BF16)
        q_ref[0, :, lo + LANES:lo + HEAD_LANES] = _rope(qr[:, nope], cos_t, sin_hi, sin_lo).astype(BF16)
        k_ref[0, :, lo:lo + LANES] = kn[:, nope].astype(BF16)
        k_ref[0, :, lo + LANES:lo + HEAD_LANES] = k_rope


def _attn_body(qi_ref, kj_ref, q_ref, k_ref, vt_ref, o_ref, m_sc, l_sc, acc_sc, *, tq):
    p = pl.program_id(1)
    i = qi_ref[p]
    j = kj_ref[p]

    @pl.when(j == 0)
    def _():
        m_sc[...] = jnp.full(m_sc.shape, -jnp.inf, F32)
        l_sc[...] = jnp.zeros(l_sc.shape, F32)
        acc_sc[...] = jnp.zeros(acc_sc.shape, F32)

    def accumulate(diagonal):
        if diagonal:
            key = lax.broadcasted_iota(jnp.int32, (tq, tq), 0)
            qry = lax.broadcasted_iota(jnp.int32, (tq, tq), 1)
            visible = key <= qry
        for h in range(MLA_HEADS):
            qk = slice(h * HEAD_LANES, (h + 1) * HEAD_LANES)
            hv = slice(h * V_HEAD_DIM, (h + 1) * V_HEAD_DIM)
            s = lax.dot_general(k_ref[0, :, qk], q_ref[0, :, qk], (((1,), (1,)), ((), ())),
                                preferred_element_type=F32) * (SM_SCALE * LOG2_E)
            if diagonal:
                s = jnp.where(visible, s, -jnp.inf)
            m_prev = m_sc[h]
            m_new = jnp.maximum(m_prev, jnp.max(s, axis=0, keepdims=True))
            alpha = jnp.exp2(m_prev - m_new)
            pr = jnp.exp2(s - m_new)
            l_sc[h] = alpha * l_sc[h] + jnp.sum(pr, axis=0, keepdims=True)
            acc_sc[hv, :] = alpha * acc_sc[hv, :] + _dot(vt_ref[0, hv, :], pr.astype(BF16))
            m_sc[h] = m_new

    @pl.when(j < i)
    def _():
        accumulate(False)

    @pl.when(j == i)
    def _():
        accumulate(True)
        for h in range(MLA_HEADS):
            hv = slice(h * V_HEAD_DIM, (h + 1) * V_HEAD_DIM)
            o_ref[0, :, hv] = (acc_sc[hv, :] / l_sc[h]).T.astype(BF16)


def _attention(q, k, vt):
    tq = 512
    nb = SEQ // tq
    pairs = [(i, j) for i in range(nb) for j in range(i + 1)]
    qi = jnp.asarray(np.array([p[0] for p in pairs], np.int32))
    kj = jnp.asarray(np.array([p[1] for p in pairs], np.int32))
    grid_spec = pltpu.PrefetchScalarGridSpec(
        num_scalar_prefetch=2,
        grid=(BATCH, len(pairs)),
        in_specs=[pl.BlockSpec((1, tq, QK_WIDTH), lambda b, p, qi, kj: (b, qi[p], 0)),
                  pl.BlockSpec((1, tq, QK_WIDTH), lambda b, p, qi, kj: (b, kj[p], 0)),
                  pl.BlockSpec((1, ATTN_WIDTH, tq), lambda b, p, qi, kj: (b, 0, kj[p]))],
        out_specs=pl.BlockSpec((1, tq, ATTN_WIDTH), lambda b, p, qi, kj: (b, qi[p], 0)),
        scratch_shapes=[pltpu.VMEM((MLA_HEADS, 1, tq), F32),
                        pltpu.VMEM((MLA_HEADS, 1, tq), F32),
                        pltpu.VMEM((ATTN_WIDTH, tq), F32)],
    )
    return pl.pallas_call(
        functools.partial(_attn_body, tq=tq),
        grid_spec=grid_spec,
        out_shape=jax.ShapeDtypeStruct((BATCH, SEQ, ATTN_WIDTH), BF16),
        compiler_params=_params(("parallel", "arbitrary")),
        name="mla_attention",
    )(qi, kj, q, k, vt)


def _sgu_compute(zu, zv, g_ref, b_ref, w_ref, bias_ref, o_ref, tm):
    row = lax.broadcasted_iota(jnp.int32, (CHUNK, CHUNK), 0)
    col = lax.broadcasted_iota(jnp.int32, (CHUNK, CHUNK), 1)
    for g in range(SGU_GROUPS):
        gs = slice(g * SGU_GROUP_DIM, (g + 1) * SGU_GROUP_DIM)
        v = jax.nn.gelu(zv[:, gs])
        mu = jnp.mean(v, axis=-1, keepdims=True)
        vc = v - mu
        var = jnp.mean(vc * vc, axis=-1, keepdims=True)
        vn = (vc * lax.rsqrt(var + LN_EPS) * g_ref[:, gs] + b_ref[:, gs]).astype(BF16)
        w = jnp.where(col <= row, w_ref[g], 0.0).astype(BF16)
        for n in range(tm // CHUNK):
            rs = slice(n * CHUNK, (n + 1) * CHUNK)
            s = _dot(w, vn[rs]) + bias_ref[:, gs]
            o_ref[0, rs, gs] = (jax.nn.gelu(zu[rs, gs]) * s).astype(BF16)


def _outproj_body(a_ref, s_ref, x_ref, gate_ref, wa_ref, ws_ref, lg_ref, lb_ref, o_ref):
    half = a_ref.shape[1] // 2
    for r0 in (0, half):
        rs = slice(r0, r0 + half)
        y = _dot(a_ref[0, rs], wa_ref[...]) + _dot(s_ref[0, rs], ws_ref[...])
        o_ref[0, rs] = _deepnorm(x_ref[0, rs], y, gate_ref[0], lg_ref[...], lb_ref[...])


def _outproj(attn, sgu, x, mod, k, w_out, ln_g, ln_b):
    tm = 512
    wa = w_out[:ATTN_WIDTH].astype(BF16)
    ws = w_out[ATTN_WIDTH:].astype(BF16)
    _, _, gate = _mod_specs(k, 2)

    def rows(w):
        return pl.BlockSpec((1, tm, w), lambda b, i: (b, i, 0))

    return pl.pallas_call(
        _outproj_body,
        grid=(BATCH, SEQ // tm),
        in_specs=[rows(ATTN_WIDTH), rows(SGU_WIDTH), rows(D_MODEL), gate,
                  _const_spec((ATTN_WIDTH, D_MODEL)), _const_spec((SGU_WIDTH, D_MODEL)),
                  _const_spec((1, D_MODEL)), _const_spec((1, D_MODEL))],
        out_specs=rows(D_MODEL),
        out_shape=jax.ShapeDtypeStruct((BATCH, SEQ, D_MODEL), F32),
        compiler_params=_params(("parallel", "parallel")),
        name="outproj_ab",
    )(attn, sgu, x, mod, wa, ws, ln_g.reshape(1, -1), ln_b.reshape(1, -1))


def _ffn_up_body(x_ref, sh_ref, sc_ref, wg_ref, wu_ref, h_ref, hm_sc):
    @pl.when(pl.program_id(2) == 0)
    def _():
        hm_sc[...] = _modulate(x_ref[0], sh_ref[0], sc_ref[0]).astype(BF16)

    hm = hm_sc[...]
    g = _dot(hm, wg_ref[0].astype(BF16))
    u = _dot(hm, wu_ref[0].astype(BF16))
    h_ref[0] = (jax.nn.silu(g) * u).astype(BF16)


def _ffn_down_body(h_ref, x_ref, gate_ref, wd_ref, lg_ref, lb_ref, o_ref, y_sc, *, n_n):
    n = pl.program_id(2)
    y_sc[n] = _dot(h_ref[0], wd_ref[...])

    @pl.when(n == n_n - 1)
    def _():
        y = jnp.concatenate([y_sc[c] for c in range(n_n)], axis=-1)
        o_ref[0] = _deepnorm(x_ref[0], y, gate_ref[0], lg_ref[...], lb_ref[...])


def _ffn(x, mod, k, layer, w_gate, w_up, w_down, ln_g, ln_b):
    tm, tf = 2048, 256
    shift, scale, gate = _mod_specs(k, 3)
    h = pl.pallas_call(
        _ffn_up_body,
        grid=(BATCH, SEQ // tm, D_FF // tf),
        in_specs=[pl.BlockSpec((1, tm, D_MODEL), lambda b, i, f: (b, i, 0), pipeline_mode=pl.Buffered(1)),
                  shift, scale,
                  pl.BlockSpec((1, D_MODEL, tf), lambda b, i, f: (layer, 0, f)),
                  pl.BlockSpec((1, D_MODEL, tf), lambda b, i, f: (layer, 0, f))],
        out_specs=pl.BlockSpec((1, tm, tf), lambda b, i, f: (b, i, f)),
        out_shape=jax.ShapeDtypeStruct((BATCH, SEQ, D_FF), BF16),
        scratch_shapes=[pltpu.VMEM((tm, D_MODEL), BF16)],
        compiler_params=_params(("parallel", "parallel", "arbitrary")),
        name="ffn_up",
    )(x, mod, mod, w_gate, w_up)

    tm, tn = 512, 512
    n_n = D_MODEL // tn
    rows = pl.BlockSpec((1, tm, D_MODEL), lambda b, i, n: (b, i, 0))
    vec = pl.BlockSpec((1, D_MODEL), lambda b, i, n: (0, 0))
    return pl.pallas_call(
        functools.partial(_ffn_down_body, n_n=n_n),
        grid=(BATCH, SEQ // tm, n_n),
        in_specs=[pl.BlockSpec((1, tm, D_FF), lambda b, i, n: (b, i, 0)),
                  rows,
                  gate,
                  pl.BlockSpec((D_FF, tn), lambda b, i, n: (0, n)),
                  vec, vec],
        out_specs=rows,
        out_shape=jax.ShapeDtypeStruct((BATCH, SEQ, D_MODEL), F32),
        scratch_shapes=[pltpu.VMEM((n_n, tm, tn), F32)],
        compiler_params=_params(("parallel", "parallel", "arbitrary")),
        name="ffn_down",
    )(h, x, mod, w_down.astype(BF16), ln_g.reshape(1, -1), ln_b.reshape(1, -1))


def _pool_body(x_ref, halo_ref, sh_ref, sc_ref, gate_ref, pw_ref, ps_ref, wo_ref, lg_ref, lb_ref,
               o_ref, *, tm):
    i = pl.program_id(1)
    x = x_ref[0]
    hm = _modulate(x, sh_ref[0], sc_ref[0])
    halo = jnp.where(i > 0, _modulate(halo_ref[0], sh_ref[0], sc_ref[0]), 0.0)
    ext = jnp.concatenate([halo, hm], axis=0)
    ext_hi = ext.astype(BF16)
    ext_lo = (ext - ext_hi.astype(F32)).astype(BF16)
    t_loc = lax.broadcasted_iota(jnp.int32, (POOL_HALO, 2 * POOL_HALO), 0)
    s_loc = lax.broadcasted_iota(jnp.int32, (POOL_HALO, 2 * POOL_HALO), 1)
    lag = t_loc + POOL_HALO - s_loc
    t_glob = i * tm + lax.broadcasted_iota(jnp.int32, (tm, 1), 0)
    outs = []
    for gi, w in enumerate(POOL_WINDOWS):
        cs = slice(gi * POOL_GROUP_DIM, (gi + 1) * POOL_GROUP_DIM)
        band = jnp.where(lag >= 0, jnp.where(lag < w, 1.0, 0.0), 0.0).astype(BF16)
        wsum = jnp.concatenate(
            [_dot(band, ext_hi[c:c + 2 * POOL_HALO, cs]) + _dot(band, ext_lo[c:c + 2 * POOL_HALO, cs])
             for c in range(0, tm, POOL_HALO)], axis=0)
        count = jnp.minimum(t_glob + 1, w).astype(F32)
        d = (wsum / count - hm[:, cs]).astype(BF16)
        outs.append(_dot(d, pw_ref[gi]))
    pooled = (jnp.concatenate(outs, axis=-1) * ps_ref[...]).astype(BF16)
    y = _dot(pooled, wo_ref[...])
    o_ref[0] = _deepnorm(x, y, gate_ref[0], lg_ref[...], lb_ref[...])


def _pool_mixer(x, mod, k, pool_w, pool_scale, w_out_c, ln_g, ln_b):
    tm = 512
    per = tm // POOL_HALO
    shift, scale, gate = _mod_specs(k, 2)
    rows = pl.BlockSpec((1, tm, D_MODEL), lambda b, i: (b, i, 0))
    halo = pl.BlockSpec((1, POOL_HALO, D_MODEL), lambda b, i: (b, jnp.maximum(i * per - 1, 0), 0))
    return pl.pallas_call(
        functools.partial(_pool_body, tm=tm),
        grid=(BATCH, SEQ // tm),
        in_specs=[rows, halo, shift, scale, gate,
                  _const_spec((len(POOL_WINDOWS), POOL_GROUP_DIM, POOL_GROUP_DIM)),
                  _const_spec((1, D_MODEL)), _const_spec((D_MODEL, D_MODEL)),
                  _const_spec((1, D_MODEL)), _const_spec((1, D_MODEL))],
        out_specs=rows,
        out_shape=jax.ShapeDtypeStruct((BATCH, SEQ, D_MODEL), F32),
        compiler_params=_params(("parallel", "parallel")),
        name="pool_mixer",
    )(x, x, mod, mod, mod, pool_w.astype(BF16), pool_scale.reshape(1, -1),
      w_out_c.astype(BF16), ln_g.reshape(1, -1), ln_b.reshape(1, -1))


def _router_body(x_ref, sh_ref, sc_ref, wr_ref, g_ref, hm_ref):
    hm = _modulate(x_ref[0], sh_ref[0], sc_ref[0])
    hm_ref[0] = hm
    hm_hi = hm.astype(BF16)
    hm_lo = (hm - hm_hi.astype(F32)).astype(BF16)
    w = wr_ref[...]
    w_hi = w.astype(BF16)
    w_lo = (w - w_hi.astype(F32)).astype(BF16)
    logits = _dot(hm_hi, w_hi) + (_dot(hm_lo, w_hi) + _dot(hm_hi, w_lo))
    lane = lax.broadcasted_iota(jnp.int32, logits.shape, 1).astype(F32)
    logits = jnp.where(lane < N_EXPERTS, logits, -jnp.inf)
    v1 = jnp.max(logits, axis=-1, keepdims=True)
    i1 = jnp.min(jnp.where(logits == v1, lane, float(LANES)), axis=-1, keepdims=True)
    rest = jnp.where(lane == i1, -jnp.inf, logits)
    v2 = jnp.max(rest, axis=-1, keepdims=True)
    i2 = jnp.min(jnp.where(rest == v2, lane, float(LANES)), axis=-1, keepdims=True)
    e2 = jnp.exp(v2 - v1)
    den = 1.0 + e2
    g_ref[0] = (jnp.where(lane == 0.0, i1, 0.0) + jnp.where(lane == 1.0, i2, 0.0)
                + jnp.where(lane == 2.0, 1.0 / den, 0.0) + jnp.where(lane == 3.0, e2 / den, 0.0))


def _router(x, mod, k, w_router):
    tm = 512
    shift, scale, _ = _mod_specs(k, 2)
    wr = jnp.pad(w_router, ((0, 0), (0, LANES - N_EXPERTS)))
    rows = pl.BlockSpec((1, tm, D_MODEL), lambda b, i: (b, i, 0))
    return pl.pallas_call(
        _router_body,
        grid=(BATCH, SEQ // tm),
        in_specs=[rows, shift, scale, _const_spec((D_MODEL, LANES))],
        out_specs=(pl.BlockSpec((1, tm, LANES), lambda b, i: (b, i, 0)), rows),
        out_shape=(jax.ShapeDtypeStruct((BATCH, SEQ, LANES), F32),
                   jax.ShapeDtypeStruct((BATCH, SEQ, D_MODEL), F32)),
        compiler_params=_params(("parallel", "parallel")),
        name="router",
    )(x, mod, mod, wr)


N_TOKENS = BATCH * SEQ
N_ASSIGN = 2 * N_TOKENS
MOE_BLOCK = 1024
MOE_SUB = 256
MOE_TILE = N_ASSIGN // N_EXPERTS * 5 // 4
MOE_TILES_ANY = -(-(N_ASSIGN + N_EXPERTS * (MOE_TILE - 1)) // MOE_TILE)
MOE_TILES_FIT = N_EXPERTS
GATHER_ROWS = MOE_TILE // 4
assert MOE_TILE % GATHER_ROWS == 0 and GATHER_ROWS % 8 == 0


def _route_counts(routing):
    e_flat = routing.reshape(N_TOKENS, LANES)[:, :2].astype(jnp.int32).T.reshape(N_ASSIGN)
    onehot = (e_flat[:, None] == jnp.arange(N_EXPERTS, dtype=jnp.int32)[None, :]).astype(jnp.int32)
    csum = jnp.cumsum(onehot, axis=0)
    return onehot, jnp.sum(csum * onehot, axis=1) - 1, csum[-1]


def _route_plan(onehot, rank, counts, n_tiles):
    padded = (counts + MOE_TILE - 1) // MOE_TILE * MOE_TILE
    group_end = jnp.cumsum(padded)
    group_start = group_end - padded
    pos = (jnp.sum(onehot * group_start[None, :], axis=1) + rank).astype(jnp.int32)
    tok = jnp.tile(jnp.arange(N_TOKENS, dtype=jnp.int32), 2)
    src = jnp.zeros((n_tiles * MOE_TILE,), jnp.int32).at[pos].set(tok, unique_indices=True)
    tile_start = jnp.arange(n_tiles, dtype=jnp.int32) * MOE_TILE
    n_used = group_end[-1] // MOE_TILE
    tile_expert = jnp.sum((tile_start[:, None] >= group_end[None, :]).astype(jnp.int32), axis=1)
    last_expert = jnp.max(jnp.where(counts > 0, jnp.arange(N_EXPERTS, dtype=jnp.int32), 0))
    tile_expert = jnp.minimum(tile_expert, last_expert)
    tile_rows = jnp.clip(counts[tile_expert] - (tile_start - group_start[tile_expert]), 0, MOE_TILE)
    tile_rows = jnp.where(jnp.arange(n_tiles) < n_used, tile_rows, 0).astype(jnp.int32)
    return pos, src, tile_expert.astype(jnp.int32), tile_rows, n_used.reshape(1).astype(jnp.int32)


GATHER_UNROLL = 8


def _gather_body(src_ref, rows_ref, hm_hbm, o_ref, buf, sem, *, tg, n_steps):
    i = pl.program_id(0)
    per = MOE_TILE // tg

    def valid_rows(step):
        n_valid = jnp.clip(rows_ref[step // per] - (step % per) * tg, 0, tg)
        n_groups = n_valid // GATHER_UNROLL
        return n_valid, n_groups, pl.multiple_of(n_groups * GATHER_UNROLL, GATHER_UNROLL)

    def issue(step, slot):
        base = step * tg
        n_valid, n_groups, n_grouped = valid_rows(step)

        @pl.when(n_valid < tg)
        def _():
            buf[slot] = jnp.zeros(buf.shape[1:], F32)

        def issue_row(r, queue):
            t = src_ref[base + r]
            pltpu.make_async_copy(hm_hbm.at[pl.ds(t, 1)], buf.at[slot, pl.ds(r, 1)],
                                  sem.at[slot]).start(priority=queue)

        def issue_group(g, carry):
            for u in range(GATHER_UNROLL):
                issue_row(g * GATHER_UNROLL + u, u % 2)
            return carry

        def issue_one(r, carry):
            issue_row(r, 0)
            return carry

        lax.fori_loop(0, n_groups, issue_group, 0)
        lax.fori_loop(n_grouped, n_valid, issue_one, 0)

    def wait(step, slot):
        n_valid, n_groups, n_grouped = valid_rows(step)

        @pl.when(n_groups > 0)
        def _():
            pltpu.make_async_copy(hm_hbm.at[pl.ds(0, n_grouped)], buf.at[slot, pl.ds(0, n_grouped)],
                                  sem.at[slot]).wait()

        def wait_one(r, carry):
            pltpu.make_async_copy(hm_hbm.at[pl.ds(0, 1)], buf.at[slot, pl.ds(r, 1)], sem.at[slot]).wait()
            return carry

        lax.fori_loop(n_grouped, n_valid, wait_one, 0)

    slot = i % 2

    @pl.when(i == 0)
    def _():
        issue(i, slot)

    @pl.when(i + 1 < n_steps)
    def _():
        issue(i + 1, 1 - slot)

    wait(i, slot)
    o_ref[...] = buf[slot].astype(BF16)


def _gather_rows(src, tile_rows, hm):
    n_rows = src.shape[0]
    tg = GATHER_ROWS
    grid_spec = pltpu.PrefetchScalarGridSpec(
        num_scalar_prefetch=2,
        grid=(n_rows // tg,),
        in_specs=[pl.BlockSpec(memory_space=pl.ANY)],
        out_specs=pl.BlockSpec((tg, D_MODEL), lambda i, src, tr: (i, 0)),
        scratch_shapes=[pltpu.VMEM((2, tg, D_MODEL), F32), pltpu.SemaphoreType.DMA((2,))],
    )
    return pl.pallas_call(
        functools.partial(_gather_body, tg=tg, n_steps=n_rows // tg),
        grid_spec=grid_spec,
        out_shape=jax.ShapeDtypeStruct((n_rows, D_MODEL), BF16),
        compiler_params=_params(("arbitrary",)),
        name="moe_gather",
    )(src, tile_rows, hm.reshape(N_TOKENS, D_MODEL))


def _for_valid_blocks(rows, tm, block_fn, empty_fn):
    for b0 in range(0, tm, MOE_BLOCK):
        b1 = min(b0 + MOE_BLOCK, tm)
        whole = b1 if b1 - b0 == MOE_BLOCK else tm + 1
        if whole <= tm:
            pl.when(rows >= whole)(functools.partial(block_fn, slice(b0, b1)))

        for r0 in range(b0, b1, MOE_SUB):
            rs = slice(r0, r0 + MOE_SUB)

            @pl.when(jnp.logical_and(rows < whole, r0 < rows))
            def _():
                block_fn(rs)

            @pl.when(r0 >= rows)
            def _():
                empty_fn(rs)


def _gup_body(te_ref, rows_ref, nu_ref, xs_ref, wg_ref, wu_ref, h_ref, wgb, wub, *, tm):
    rows = rows_ref[pl.program_id(0)]

    @pl.when(rows > 0)
    def _():
        wgb[...] = wg_ref[0].astype(BF16)
        wub[...] = wu_ref[0].astype(BF16)

    def block(rs):
        xsb = xs_ref[rs, :]
        h_ref[rs, :] = (jax.nn.silu(_dot(xsb, wgb[...])) * _dot(xsb, wub[...])).astype(BF16)

    def empty(rs):
        h_ref[rs, :] = jnp.zeros((rs.stop - rs.start, h_ref.shape[1]), BF16)

    _for_valid_blocks(rows, tm, block, empty)


def _gdown_body(te_ref, rows_ref, nu_ref, h_ref, wd_ref, o_ref, wdb, *, tm):
    rows = rows_ref[pl.program_id(0)]

    @pl.when(rows > 0)
    def _():
        wdb[...] = wd_ref[0].astype(BF16)

    def block(rs):
        o_ref[rs, :] = _dot(h_ref[rs, :], wdb[...])

    def empty(rs):
        o_ref[rs, :] = jnp.zeros((rs.stop - rs.start, o_ref.shape[1]), F32)

    _for_valid_blocks(rows, tm, block, empty)


def _grouped_ffn(xs, tile_expert, tile_rows, n_used, layer, w_gate, w_up, w_down):
    tm = MOE_TILE
    n_tiles = tile_expert.shape[0]
    first = layer * N_EXPERTS
    n_all = w_gate.shape[0] * N_EXPERTS

    def tile(i, nu):
        return jnp.minimum(i, nu[0] - 1)

    def chunk(i, c, nu, n_chunks):
        return jnp.where(i < nu[0], c, n_chunks - 1)

    tf = 256
    n_f = D_FF_EXPERT // tf
    w_up_spec = pl.BlockSpec((1, D_MODEL, tf),
                             lambda i, f, te, tr, nu: (first + te[i], 0, chunk(i, f, nu, n_f)))
    up_spec = pltpu.PrefetchScalarGridSpec(
        num_scalar_prefetch=3,
        grid=(n_tiles, n_f),
        in_specs=[pl.BlockSpec((tm, D_MODEL), lambda i, f, te, tr, nu: (tile(i, nu), 0)),
                  w_up_spec, w_up_spec],
        out_specs=pl.BlockSpec((tm, tf), lambda i, f, te, tr, nu: (i, f)),
        scratch_shapes=[pltpu.VMEM((D_MODEL, tf), BF16), pltpu.VMEM((D_MODEL, tf), BF16)],
    )
    h = pl.pallas_call(
        functools.partial(_gup_body, tm=tm),
        grid_spec=up_spec,
        out_shape=jax.ShapeDtypeStruct((n_tiles * tm, D_FF_EXPERT), BF16),
        compiler_params=_params(("arbitrary", "arbitrary")),
        name="moe_up",
    )(tile_expert, tile_rows, n_used, xs,
      w_gate.reshape(n_all, D_MODEL, D_FF_EXPERT), w_up.reshape(n_all, D_MODEL, D_FF_EXPERT))

    tn = 512
    n_n = D_MODEL // tn
    down_spec = pltpu.PrefetchScalarGridSpec(
        num_scalar_prefetch=3,
        grid=(n_tiles, n_n),
        in_specs=[pl.BlockSpec((tm, D_FF_EXPERT), lambda i, n, te, tr, nu: (tile(i, nu), 0)),
                  pl.BlockSpec((1, D_FF_EXPERT, tn),
                               lambda i, n, te, tr, nu: (first + te[i], 0, chunk(i, n, nu, n_n)))],
        out_specs=pl.BlockSpec((tm, tn), lambda i, n, te, tr, nu: (i, n)),
        scratch_shapes=[pltpu.VMEM((D_FF_EXPERT, tn), BF16)],
    )
    return pl.pallas_call(
        functools.partial(_gdown_body, tm=tm),
        grid_spec=down_spec,
        out_shape=jax.ShapeDtypeStruct((n_tiles * tm, D_MODEL), F32),
        compiler_params=_params(("arbitrary", "arbitrary")),
        name="moe_down",
    )(tile_expert, tile_rows, n_used, h, w_down.reshape(n_all, D_FF_EXPERT, D_MODEL))


def _combine_body(pos_ref, ys_hbm, r_ref, x_ref, gate_ref, lg_ref, lb_ref, o_ref, buf, sem, *, tc):
    n_steps = pl.num_programs(0) * pl.num_programs(1)
    step = pl.program_id(0) * pl.num_programs(1) + pl.program_id(1)

    def issue(s, slot):
        base = s * tc

        def issue_token(r, carry):
            for k in range(2):
                p = pos_ref[k * N_TOKENS + base + r]
                pltpu.make_async_copy(ys_hbm.at[pl.ds(p, 1)], buf.at[slot, k, pl.ds(r, 1)],
                                      sem.at[slot, k]).start(priority=k)
            return carry

        lax.fori_loop(0, tc, issue_token, 0, unroll=4)

    slot = step % 2

    @pl.when(step == 0)
    def _():
        issue(step, slot)

    @pl.when(step + 1 < n_steps)
    def _():
        issue(step + 1, 1 - slot)

    for k in range(2):
        pltpu.make_async_copy(ys_hbm.at[pl.ds(0, tc)], buf.at[slot, k], sem.at[slot, k]).wait()
    routing = r_ref[0]
    y = routing[:, 2:3] * buf[slot, 0] + routing[:, 3:4] * buf[slot, 1]
    o_ref[0] = _deepnorm(x_ref[0], y, gate_ref[0], lg_ref[...], lb_ref[...])


def _combine(pos, ys, routing, x, mod, k, ln_g, ln_b):
    tc = 512
    rows = pl.BlockSpec((1, tc, D_MODEL), lambda b, i, pos: (b, i, 0))
    vec = pl.BlockSpec((1, D_MODEL), lambda b, i, pos: (0, 0))
    grid_spec = pltpu.PrefetchScalarGridSpec(
        num_scalar_prefetch=1,
        grid=(BATCH, SEQ // tc),
        in_specs=[pl.BlockSpec(memory_space=pl.ANY),
                  pl.BlockSpec((1, tc, LANES), lambda b, i, pos: (b, i, 0)),
                  rows,
                  pl.BlockSpec((1, 1, D_MODEL), lambda b, i, pos: (k * 8 + b, 0, 2)),
                  vec, vec],
        out_specs=rows,
        scratch_shapes=[pltpu.VMEM((2, 2, tc, D_MODEL), F32), pltpu.SemaphoreType.DMA((2, 2))],
    )
    return pl.pallas_call(
        functools.partial(_combine_body, tc=tc),
        grid_spec=grid_spec,
        out_shape=jax.ShapeDtypeStruct((BATCH, SEQ, D_MODEL), F32),
        compiler_params=_params(("arbitrary", "arbitrary")),
        name="moe_combine",
    )(pos, ys, routing, x, mod, ln_g.reshape(1, -1), ln_b.reshape(1, -1))


def _moe(x, mod, k, w_router, layer, w_gate, w_up, w_down, ln_g, ln_b):
    routing, hm = _router(x, mod, k, w_router)
    onehot, rank, counts = _route_counts(routing)

    def experts_over(n_tiles):
        def run(_):
            pos, src, tile_expert, tile_rows, n_used = _route_plan(onehot, rank, counts, n_tiles)
            xs = _gather_rows(src, tile_rows, hm)
            ys = _grouped_ffn(xs, tile_expert, tile_rows, n_used, layer, w_gate, w_up, w_down)
            return _combine(pos, ys, routing, x, mod, k, ln_g, ln_b)
        return run

    return lax.cond(jnp.all(counts <= MOE_TILE),
                    experts_over(MOE_TILES_FIT), experts_over(MOE_TILES_ANY), None)


def kernel(x, c, positions, ada_w, ada_b, ln_g, ln_b, w_in_ab, q_norm_g, w_q_up, kv_norm_g, w_kv_up, sgu_norm_g, sgu_norm_b, sgu_w, sgu_b, w_out_ab, ffn_w_gate, ffn_w_up, ffn_w_down, pool_w, pool_scale, w_out_c, router_w, moe_w_gate, moe_w_up, moe_w_down):
    mod = _ada_modulation(c, ada_w, ada_b)
    tables = _rope_tables(positions)
    for l in range(DEPTH):
        j = l // 2
        k_tok, k_ch = 2 * l, 2 * l + 1
        if l % 2 == 0:
            q, k, vt, sgu = _front(x, mod, k_tok, tables, w_in_ab[j], q_norm_g[j], w_q_up[j],
                                   kv_norm_g[j], w_kv_up[j],
                                   sgu_norm_g[j], sgu_norm_b[j], sgu_w[j], sgu_b[j])
            attn = _attention(q, k, vt)
            x = _outproj(attn, sgu, x, mod, k_tok, w_out_ab[j], ln_g[l, 0], ln_b[l, 0])
            x = _ffn(x, mod, k_ch, j, ffn_w_gate, ffn_w_up, ffn_w_down[j], ln_g[l, 1], ln_b[l, 1])
        else:
            x = _pool_mixer(x, mod, k_tok, pool_w[j], pool_scale[j], w_out_c[j], ln_g[l, 0], ln_b[l, 0])
            x = _moe(x, mod, k_ch, router_w[j], j, moe_w_gate, moe_w_up, moe_w_down,
                     ln_g[l, 1], ln_b[l, 1])
    return x
```

```python
import functools

import jax
import jax.numpy as jnp
import numpy as np
from jax import lax
from jax.experimental import pallas as pl
from jax.experimental.pallas import tpu as pltpu

F32 = jnp.float32
BF16 = jnp.bfloat16

D_MODEL = 2048
BATCH = 2
SEQ = 4096
DEPTH = 4
MLA_HEADS = 8
QK_NOPE_DIM = 128
QK_ROPE_DIM = 64
V_HEAD_DIM = 128
Q_LORA_RANK = 768
KV_LORA_RANK = 512
ROPE_THETA = 10000.0
SGU_GROUPS = 8
SGU_GROUP_DIM = 128
SGU_WIDTH = SGU_GROUPS * SGU_GROUP_DIM
CHUNK = 128
ATTN_WIDTH = MLA_HEADS * V_HEAD_DIM
POOL_WINDOWS = (2, 4, 8, 16)
POOL_GROUP_DIM = D_MODEL // 4
D_FF = 5632
N_EXPERTS = 8
D_FF_EXPERT = 2816
DEEPNORM_ALPHA = (2 * DEPTH) ** 0.25
LN_EPS = 1e-5
RMS_EPS = 1e-6
SM_SCALE = (QK_NOPE_DIM + QK_ROPE_DIM) ** -0.5
LOG2_E = 1.4426950408889634

LANES = 128
HEAD_LANES = 2 * LANES
QK_WIDTH = MLA_HEADS * HEAD_LANES
VMEM_LIMIT = 60 * 1024 * 1024
POOL_HALO = 128


def _params(semantics):
    return pltpu.CompilerParams(dimension_semantics=semantics, vmem_limit_bytes=VMEM_LIMIT)


def _const_spec(shape):
    nd = len(shape)
    return pl.BlockSpec(shape, lambda *_: (0,) * nd, pipeline_mode=pl.Buffered(1))


def _dot(a, b):
    return jnp.dot(a, b, preferred_element_type=F32)


def _modulate(x, shift, scale):
    return x * (1.0 + scale) + shift


def _deepnorm(x, y, gate, g, b):
    r = DEEPNORM_ALPHA * x + (1.0 + gate) * y
    mu = jnp.mean(r, axis=-1, keepdims=True)
    rc = r - mu
    var = jnp.mean(rc * rc, axis=-1, keepdims=True)
    return rc * lax.rsqrt(var + LN_EPS) * g + b


def _rms(x, g):
    return x * lax.rsqrt(jnp.mean(x * x, axis=-1, keepdims=True) + RMS_EPS) * g


def _rope(r, cos_t, sin_hi, sin_lo):
    return (r * cos_t + pltpu.roll(r, QK_ROPE_DIM // 2, 1) * sin_hi
            + pltpu.roll(r, LANES - QK_ROPE_DIM // 2, 1) * sin_lo)


def _mod_specs(k, n_grid):
    def spec(part):
        if n_grid == 2:
            return pl.BlockSpec((1, 1, D_MODEL), lambda b, i: (k * 8 + b, 0, part))
        if n_grid == 3:
            return pl.BlockSpec((1, 1, D_MODEL), lambda b, i, f: (k * 8 + b, 0, part))
        return pl.BlockSpec((1, 1, D_MODEL), lambda b, i, e, f: (k * 8 + b, 0, part))
    return spec(0), spec(1), spec(2)


def _ada_body(c_ref, w_ref, b_ref, o_ref):
    s = jax.nn.silu(c_ref[...]).astype(BF16)
    o_ref[0] = _dot(s, w_ref[0].astype(BF16)) + b_ref[0]


def _ada_modulation(c, ada_w, ada_b):
    n_mod = 2 * DEPTH
    tn = 1024
    c_pad = jnp.pad(c, ((0, 8 - BATCH), (0, 0)))
    out = pl.pallas_call(
        _ada_body,
        grid=(n_mod, 3 * D_MODEL // tn),
        in_specs=[
            pl.BlockSpec((8, D_MODEL), lambda k, n: (0, 0)),
            pl.BlockSpec((1, D_MODEL, tn), lambda k, n: (k, 0, n)),
            pl.BlockSpec((1, 1, tn), lambda k, n: (k, 0, n)),
        ],
        out_specs=pl.BlockSpec((1, 8, tn), lambda k, n: (k, 0, n)),
        out_shape=jax.ShapeDtypeStruct((n_mod, 8, 3 * D_MODEL), F32),
        compiler_params=_params(("parallel", "parallel")),
        name="ada_mod",
    )(c_pad, ada_w.reshape(n_mod, D_MODEL, 3 * D_MODEL), ada_b.reshape(n_mod, 1, 3 * D_MODEL))
    return out.reshape(n_mod * 8, 1, 3 * D_MODEL)


def _rope_table_body(pos_ref, inv_ref, cos_ref, hi_ref, lo_ref):
    ang = pos_ref[0].astype(F32) * inv_ref[...]
    lane = lax.broadcasted_iota(jnp.int32, ang.shape, 1)
    cos = jnp.cos(ang)
    sin = jnp.sin(ang)
    half = QK_ROPE_DIM // 2
    cos_ref[0] = jnp.where(lane < QK_ROPE_DIM, cos, 0.0)
    hi_ref[0] = jnp.where(lane < half, 0.0, jnp.where(lane < QK_ROPE_DIM, sin, 0.0))
    lo_ref[0] = jnp.where(lane < half, -sin, 0.0)


def _rope_tables(positions):
    tm = 512
    half = QK_ROPE_DIM // 2
    inv = ROPE_THETA ** (-jnp.arange(0, QK_ROPE_DIM, 2, dtype=F32) / QK_ROPE_DIM)
    inv_lanes = jnp.concatenate([inv, inv, jnp.zeros((LANES - 2 * half,), F32)]).reshape(1, LANES)
    tab = jax.ShapeDtypeStruct((BATCH, SEQ, LANES), F32)
    spec = pl.BlockSpec((1, tm, LANES), lambda b, i: (b, i, 0))
    return pl.pallas_call(
        _rope_table_body,
        grid=(BATCH, SEQ // tm),
        in_specs=[pl.BlockSpec((1, tm, 1), lambda b, i: (b, i, 0)),
                  pl.BlockSpec((1, LANES), lambda b, i: (0, 0))],
        out_specs=(spec, spec, spec),
        out_shape=(tab, tab, tab),
        compiler_params=_params(("parallel", "parallel")),
        name="rope_tables",
    )(positions.reshape(BATCH, SEQ, 1), inv_lanes)


def _front_body(x_ref, sh_ref, sc_ref, wq_ref, wkv_ref, wkr_ref, wu_ref, wv_ref,
                cos_ref, hi_ref, lo_ref, qg_ref, kvg_ref, wqn_ref, wqr_ref, wkn_ref, wvt_ref,
                sg_ref, sb_ref, sw_ref, sbias_ref,
                q_ref, k_ref, vt_ref, sgu_ref, *, tm):
    hm = _modulate(x_ref[0], sh_ref[0], sc_ref[0]).astype(BF16)
    _qkv_compute(_dot(hm, wq_ref[...]), _dot(hm, wkv_ref[...]), _dot(hm, wkr_ref[...]),
                 cos_ref[0], hi_ref[0], lo_ref[0], qg_ref, kvg_ref,
                 wqn_ref, wqr_ref, wkn_ref, wvt_ref, q_ref, k_ref, vt_ref)
    _sgu_compute(_dot(hm, wu_ref[...]), _dot(hm, wv_ref[...]),
                 sg_ref, sb_ref, sw_ref, sbias_ref, sgu_ref, tm)


def _front(x, mod, k, tables, w_in, q_norm_g, w_q_up, kv_norm_g, w_kv_up,
           sgu_norm_g, sgu_norm_b, sgu_w, sgu_b):
    tm = 512
    a, b, c_, d = (Q_LORA_RANK, Q_LORA_RANK + KV_LORA_RANK,
                   Q_LORA_RANK + KV_LORA_RANK + QK_ROPE_DIM,
                   Q_LORA_RANK + KV_LORA_RANK + QK_ROPE_DIM + SGU_WIDTH)
    wq = w_in[:, :a].astype(BF16)
    wkv = w_in[:, a:b].astype(BF16)
    wkr = jnp.pad(w_in[:, b:c_], ((0, 0), (0, LANES - QK_ROPE_DIM))).astype(BF16)
    wu = w_in[:, c_:d].astype(BF16)
    wv = w_in[:, d:].astype(BF16)
    in_widths = (Q_LORA_RANK, KV_LORA_RANK, LANES, SGU_WIDTH, SGU_WIDTH)

    wq_up = w_q_up.reshape(Q_LORA_RANK, MLA_HEADS, QK_NOPE_DIM + QK_ROPE_DIM)
    wqn = wq_up[:, :, :QK_NOPE_DIM].reshape(Q_LORA_RANK, MLA_HEADS * LANES).astype(BF16)
    wqr = jnp.pad(wq_up[:, :, QK_NOPE_DIM:], ((0, 0), (0, 0), (0, LANES - QK_ROPE_DIM)))
    wqr = wqr.reshape(Q_LORA_RANK, MLA_HEADS * LANES).astype(BF16)
    wkv_up = w_kv_up.reshape(KV_LORA_RANK, MLA_HEADS, QK_NOPE_DIM + V_HEAD_DIM)
    wkn = wkv_up[:, :, :QK_NOPE_DIM].reshape(KV_LORA_RANK, MLA_HEADS * LANES).astype(BF16)
    wvt = wkv_up[:, :, QK_NOPE_DIM:].reshape(KV_LORA_RANK, ATTN_WIDTH).T.astype(BF16)
    sgu_bias = jnp.repeat(sgu_b.T, SGU_GROUP_DIM, axis=1)

    shift, scale, _ = _mod_specs(k, 2)

    def rows(w):
        return pl.BlockSpec((1, tm, w), lambda b, i: (b, i, 0))

    return pl.pallas_call(
        functools.partial(_front_body, tm=tm),
        grid=(BATCH, SEQ // tm),
        in_specs=[rows(D_MODEL), shift, scale]
                 + [_const_spec((D_MODEL, w)) for w in in_widths]
                 + [rows(LANES), rows(LANES), rows(LANES),
                    _const_spec((1, Q_LORA_RANK)), _const_spec((1, KV_LORA_RANK)),
                    _const_spec((Q_LORA_RANK, MLA_HEADS * LANES)),
                    _const_spec((Q_LORA_RANK, MLA_HEADS * LANES)),
                    _const_spec((KV_LORA_RANK, MLA_HEADS * LANES)),
                    _const_spec((ATTN_WIDTH, KV_LORA_RANK)),
                    _const_spec((1, SGU_WIDTH)), _const_spec((1, SGU_WIDTH)),
                    _const_spec((SGU_GROUPS, CHUNK, CHUNK)), _const_spec((CHUNK, SGU_WIDTH))],
        out_specs=(rows(QK_WIDTH), rows(QK_WIDTH),
                   pl.BlockSpec((1, ATTN_WIDTH, tm), lambda b, i: (b, 0, i)),
                   rows(SGU_WIDTH)),
        out_shape=(jax.ShapeDtypeStruct((BATCH, SEQ, QK_WIDTH), BF16),
                   jax.ShapeDtypeStruct((BATCH, SEQ, QK_WIDTH), BF16),
                   jax.ShapeDtypeStruct((BATCH, ATTN_WIDTH, SEQ), BF16),
                   jax.ShapeDtypeStruct((BATCH, SEQ, SGU_WIDTH), BF16)),
        compiler_params=_params(("parallel", "parallel")),
        name="front_ab",
    )(x, mod, mod, wq, wkv, wkr, wu, wv, *tables,
      q_norm_g.reshape(1, -1), kv_norm_g.reshape(1, -1), wqn, wqr, wkn, wvt,
      sgu_norm_g.reshape(1, -1), sgu_norm_b.reshape(1, -1), sgu_w, sgu_bias)


def _qkv_compute(cq, ckv, kr, cos_t, sin_hi, sin_lo, qg_ref, kvg_ref,
                 wqn_ref, wqr_ref, wkn_ref, wvt_ref, q_ref, k_ref, vt_ref):
    cqn = _rms(cq, qg_ref[...]).astype(BF16)
    qn = _dot(cqn, wqn_ref[...])
    qr = _dot(cqn, wqr_ref[...])
    ckvn = _rms(ckv, kvg_ref[...]).astype(BF16)
    kn = _dot(ckvn, wkn_ref[...])
    vt_ref[0] = lax.dot_general(wvt_ref[...], ckvn, (((1,), (1,)), ((), ())),
                                preferred_element_type=F32).astype(BF16)
    k_rope = _rope(kr, cos_t, sin_hi, sin_lo).astype(BF16)
    for h in range(MLA_HEADS):
        nope = slice(h * LANES, (h + 1) * LANES)
        lo = h * HEAD_LANES
        q_ref[0, :, lo:lo + LANES] = qn[:, nope].astype(BF16)
        q_ref[0, :, lo + LANES:lo + HEAD_LANES] = _rope(qr[:, nope], cos_t, sin_hi, sin_lo).astype(BF16)
        k_ref[0, :, lo:lo + LANES] = kn[:, nope].astype(BF16)
        k_ref[0, :, lo + LANES:lo + HEAD_LANES] = k_rope


def _attn_body(qi_ref, kj_ref, q_ref, k_ref, vt_ref, o_ref, m_sc, l_sc, acc_sc, *, tq):
    p = pl.program_id(1)
    i = qi_ref[p]
    j = kj_ref[p]

    @pl.when(j == 0)
    def _():
        m_sc[...] = jnp.full(m_sc.shape, -jnp.inf, F32)
        l_sc[...] = jnp.zeros(l_sc.shape, F32)
        acc_sc[...] = jnp.zeros(acc_sc.shape, F32)

    def accumulate(diagonal):
        if diagonal:
            key = lax.broadcasted_iota(jnp.int32, (tq, tq), 0)
            qry = lax.broadcasted_iota(jnp.int32, (tq, tq), 1)
            visible = key <= qry
        for h in range(MLA_HEADS):
            qk = slice(h * HEAD_LANES, (h + 1) * HEAD_LANES)
            hv = slice(h * V_HEAD_DIM, (h + 1) * V_HEAD_DIM)
            s = lax.dot_general(k_ref[0, :, qk], q_ref[0, :, qk], (((1,), (1,)), ((), ())),
                                preferred_element_type=F32) * (SM_SCALE * LOG2_E)
            if diagonal:
                s = jnp.where(visible, s, -jnp.inf)
            m_prev = m_sc[h]
            m_new = jnp.maximum(m_prev, jnp.max(s, axis=0, keepdims=True))
            alpha = jnp.exp2(m_prev - m_new)
            pr = jnp.exp2(s - m_new)
            l_sc[h] = alpha * l_sc[h] + jnp.sum(pr, axis=0, keepdims=True)
            acc_sc[hv, :] = alpha * acc_sc[hv, :] + _dot(vt_ref[0, hv, :], pr.astype(BF16))
            m_sc[h] = m_new

    @pl.when(j < i)
    def _():
        accumulate(False)

    @pl.when(j == i)
    def _():
        accumulate(True)
        for h in range(MLA_HEADS):
            hv = slice(h * V_HEAD_DIM, (h + 1) * V_HEAD_DIM)
            o_ref[0, :, hv] = (acc_sc[hv, :] / l_sc[h]).T.astype(BF16)


def _attention(q, k, vt):
    tq = 512
    nb = SEQ // tq
    pairs = [(i, j) for i in range(nb) for j in range(i + 1)]
    qi = jnp.asarray(np.array([p[0] for p in pairs], np.int32))
    kj = jnp.asarray(np.array([p[1] for p in pairs], np.int32))
    grid_spec = pltpu.PrefetchScalarGridSpec(
        num_scalar_prefetch=2,
        grid=(BATCH, len(pairs)),
        in_specs=[pl.BlockSpec((1, tq, QK_WIDTH), lambda b, p, qi, kj: (b, qi[p], 0)),
                  pl.BlockSpec((1, tq, QK_WIDTH), lambda b, p, qi, kj: (b, kj[p], 0)),
                  pl.BlockSpec((1, ATTN_WIDTH, tq), lambda b, p, qi, kj: (b, 0, kj[p]))],
        out_specs=pl.BlockSpec((1, tq, ATTN_WIDTH), lambda b, p, qi, kj: (b, qi[p], 0)),
        scratch_shapes=[pltpu.VMEM((MLA_HEADS, 1, tq), F32),
                        pltpu.VMEM((MLA_HEADS, 1, tq), F32),
                        pltpu.VMEM((ATTN_WIDTH, tq), F32)],
    )
    return pl.pallas_call(
        functools.partial(_attn_body, tq=tq),
        grid_spec=grid_spec,
        out_shape=jax.ShapeDtypeStruct((BATCH, SEQ, ATTN_WIDTH), BF16),
        compiler_params=_params(("parallel", "arbitrary")),
        name="mla_attention",
    )(qi, kj, q, k, vt)


def _sgu_compute(zu, zv, g_ref, b_ref, w_ref, bias_ref, o_ref, tm):
    row = lax.broadcasted_iota(jnp.int32, (CHUNK, CHUNK), 0)
    col = lax.broadcasted_iota(jnp.int32, (CHUNK, CHUNK), 1)
    for g in range(SGU_GROUPS):
        gs = slice(g * SGU_GROUP_DIM, (g + 1) * SGU_GROUP_DIM)
        v = jax.nn.gelu(zv[:, gs])
        mu = jnp.mean(v, axis=-1, keepdims=True)
        vc = v - mu
        var = jnp.mean(vc * vc, axis=-1, keepdims=True)
        vn = (vc * lax.rsqrt(var + LN_EPS) * g_ref[:, gs] + b_ref[:, gs]).astype(BF16)
        w = jnp.where(col <= row, w_ref[g], 0.0).astype(BF16)
        for n in range(tm // CHUNK):
            rs = slice(n * CHUNK, (n + 1) * CHUNK)
            s = _dot(w, vn[rs]) + bias_ref[:, gs]
            o_ref[0, rs, gs] = (jax.nn.gelu(zu[rs, gs]) * s).astype(BF16)


def _outproj_body(a_ref, s_ref, x_ref, gate_ref, wa_ref, ws_ref, lg_ref, lb_ref, o_ref):
    half = a_ref.shape[1] // 2
    for r0 in (0, half):
        rs = slice(r0, r0 + half)
        y = _dot(a_ref[0, rs], wa_ref[...]) + _dot(s_ref[0, rs], ws_ref[...])
        o_ref[0, rs] = _deepnorm(x_ref[0, rs], y, gate_ref[0], lg_ref[...], lb_ref[...])


def _outproj(attn, sgu, x, mod, k, w_out, ln_g, ln_b):
    tm = 512
    wa = w_out[:ATTN_WIDTH].astype(BF16)
    ws = w_out[ATTN_WIDTH:].astype(BF16)
    _, _, gate = _mod_specs(k, 2)

    def rows(w):
        return pl.BlockSpec((1, tm, w), lambda b, i: (b, i, 0))

    return pl.pallas_call(
        _outproj_body,
        grid=(BATCH, SEQ // tm),
        in_specs=[rows(ATTN_WIDTH), rows(SGU_WIDTH), rows(D_MODEL), gate,
                  _const_spec((ATTN_WIDTH, D_MODEL)), _const_spec((SGU_WIDTH, D_MODEL)),
                  _const_spec((1, D_MODEL)), _const_spec((1, D_MODEL))],
        out_specs=rows(D_MODEL),
        out_shape=jax.ShapeDtypeStruct((BATCH, SEQ, D_MODEL), F32),
        compiler_params=_params(("parallel", "parallel")),
        name="outproj_ab",
    )(attn, sgu, x, mod, wa, ws, ln_g.reshape(1, -1), ln_b.reshape(1, -1))


def _ffn_up_body(x_ref, sh_ref, sc_ref, wg_ref, wu_ref, h_ref, hm_sc):
    @pl.when(pl.program_id(2) == 0)
    def _():
        hm_sc[...] = _modulate(x_ref[0], sh_ref[0], sc_ref[0]).astype(BF16)

    hm = hm_sc[...]
    g = _dot(hm, wg_ref[0].astype(BF16))
    u = _dot(hm, wu_ref[0].astype(BF16))
    h_ref[0] = (jax.nn.silu(g) * u).astype(BF16)


def _ffn_down_body(h_ref, x_ref, gate_ref, wd_ref, lg_ref, lb_ref, o_ref, y_sc, *, n_n):
    n = pl.program_id(2)
    y_sc[n] = _dot(h_ref[0], wd_ref[...])

    @pl.when(n == n_n - 1)
    def _():
        y = jnp.concatenate([y_sc[c] for c in range(n_n)], axis=-1)
        o_ref[0] = _deepnorm(x_ref[0], y, gate_ref[0], lg_ref[...], lb_ref[...])


def _ffn(x, mod, k, layer, w_gate, w_up, w_down, ln_g, ln_b):
    tm, tf = 2048, 256
    shift, scale, gate = _mod_specs(k, 3)
    h = pl.pallas_call(
        _ffn_up_body,
        grid=(BATCH, SEQ // tm, D_FF // tf),
        in_specs=[pl.BlockSpec((1, tm, D_MODEL), lambda b, i, f: (b, i, 0), pipeline_mode=pl.Buffered(1)),
                  shift, scale,
                  pl.BlockSpec((1, D_MODEL, tf), lambda b, i, f: (layer, 0, f)),
                  pl.BlockSpec((1, D_MODEL, tf), lambda b, i, f: (layer, 0, f))],
        out_specs=pl.BlockSpec((1, tm, tf), lambda b, i, f: (b, i, f)),
        out_shape=jax.ShapeDtypeStruct((BATCH, SEQ, D_FF), BF16),
        scratch_shapes=[pltpu.VMEM((tm, D_MODEL), BF16)],
        compiler_params=_params(("parallel", "parallel", "arbitrary")),
        name="ffn_up",
    )(x, mod, mod, w_gate, w_up)

    tm, tn = 512, 512
    n_n = D_MODEL // tn
    rows = pl.BlockSpec((1, tm, D_MODEL), lambda b, i, n: (b, i, 0))
    vec = pl.BlockSpec((1, D_MODEL), lambda b, i, n: (0, 0))
    return pl.pallas_call(
        functools.partial(_ffn_down_body, n_n=n_n),
        grid=(BATCH, SEQ // tm, n_n),
        in_specs=[pl.BlockSpec((1, tm, D_FF), lambda b, i, n: (b, i, 0)),
                  rows,
                  gate,
                  pl.BlockSpec((D_FF, tn), lambda b, i, n: (0, n)),
                  vec, vec],
        out_specs=rows,
        out_shape=jax.ShapeDtypeStruct((BATCH, SEQ, D_MODEL), F32),
        scratch_shapes=[pltpu.VMEM((n_n, tm, tn), F32)],
        compiler_params=_params(("parallel", "parallel", "arbitrary")),
        name="ffn_down",
    )(h, x, mod, w_down.astype(BF16), ln_g.reshape(1, -1), ln_b.reshape(1, -1))


def _pool_body(x_ref, halo_ref, sh_ref, sc_ref, gate_ref, pw_ref, ps_ref, wo_ref, lg_ref, lb_ref,
               o_ref, *, tm):
    i = pl.program_id(1)
    x = x_ref[0]
    hm = _modulate(x, sh_ref[0], sc_ref[0])
    halo = jnp.where(i > 0, _modulate(halo_ref[0], sh_ref[0], sc_ref[0]), 0.0)
    ext = jnp.concatenate([halo, hm], axis=0)
    ext_hi = ext.astype(BF16)
    ext_lo = (ext - ext_hi.astype(F32)).astype(BF16)
    t_loc = lax.broadcasted_iota(jnp.int32, (POOL_HALO, 2 * POOL_HALO), 0)
    s_loc = lax.broadcasted_iota(jnp.int32, (POOL_HALO, 2 * POOL_HALO), 1)
    lag = t_loc + POOL_HALO - s_loc
    t_glob = i * tm + lax.broadcasted_iota(jnp.int32, (tm, 1), 0)
    outs = []
    for gi, w in enumerate(POOL_WINDOWS):
        cs = slice(gi * POOL_GROUP_DIM, (gi + 1) * POOL_GROUP_DIM)
        band = jnp.where(lag >= 0, jnp.where(lag < w, 1.0, 0.0), 0.0).astype(BF16)
        wsum = jnp.concatenate(
            [_dot(band, ext_hi[c:c + 2 * POOL_HALO, cs]) + _dot(band, ext_lo[c:c + 2 * POOL_HALO, cs])
             for c in range(0, tm, POOL_HALO)], axis=0)
        count = jnp.minimum(t_glob + 1, w).astype(F32)
        d = (wsum / count - hm[:, cs]).astype(BF16)
        outs.append(_dot(d, pw_ref[gi]))
    pooled = (jnp.concatenate(outs, axis=-1) * ps_ref[...]).astype(BF16)
    y = _dot(pooled, wo_ref[...])
    o_ref[0] = _deepnorm(x, y, gate_ref[0], lg_ref[...], lb_ref[...])


def _pool_mixer(x, mod, k, pool_w, pool_scale, w_out_c, ln_g, ln_b):
    tm = 512
    per = tm // POOL_HALO
    shift, scale, gate = _mod_specs(k, 2)
    rows = pl.BlockSpec((1, tm, D_MODEL), lambda b, i: (b, i, 0))
    halo = pl.BlockSpec((1, POOL_HALO, D_MODEL), lambda b, i: (b, jnp.maximum(i * per - 1, 0), 0))
    return pl.pallas_call(
        functools.partial(_pool_body, tm=tm),
        grid=(BATCH, SEQ // tm),
        in_specs=[rows, halo, shift, scale, gate,
                  _const_spec((len(POOL_WINDOWS), POOL_GROUP_DIM, POOL_GROUP_DIM)),
                  _const_spec((1, D_MODEL)), _const_spec((D_MODEL, D_MODEL)),
                  _const_spec((1, D_MODEL)), _const_spec((1, D_MODEL))],
        out_specs=rows,
        out_shape=jax.ShapeDtypeStruct((BATCH, SEQ, D_MODEL), F32),
        compiler_params=_params(("parallel", "parallel")),
        name="pool_mixer",
    )(x, x, mod, mod, mod, pool_w.astype(BF16), pool_scale.reshape(1, -1),
      w_out_c.astype(BF16), ln_g.reshape(1, -1), ln_b.reshape(1, -1))


def _router_body(x_ref, sh_ref, sc_ref, wr_ref, g_ref, hm_ref):
    hm = _modulate(x_ref[0], sh_ref[0], sc_ref[0])
    hm_ref[0] = hm
    hm_hi = hm.astype(BF16)
    hm_lo = (hm - hm_hi.astype(F32)).astype(BF16)
    w = wr_ref[...]
    w_hi = w.astype(BF16)
    w_lo = (w - w_hi.astype(F32)).astype(BF16)
    logits = _dot(hm_hi, w_hi) + (_dot(hm_lo, w_hi) + _dot(hm_hi, w_lo))
    lane = lax.broadcasted_iota(jnp.int32, logits.shape, 1).astype(F32)
    logits = jnp.where(lane < N_EXPERTS, logits, -jnp.inf)
    v1 = jnp.max(logits, axis=-1, keepdims=True)
    i1 = jnp.min(jnp.where(logits == v1, lane, float(LANES)), axis=-1, keepdims=True)
    rest = jnp.where(lane == i1, -jnp.inf, logits)
    v2 = jnp.max(rest, axis=-1, keepdims=True)
    i2 = jnp.min(jnp.where(rest == v2, lane, float(LANES)), axis=-1, keepdims=True)
    e2 = jnp.exp(v2 - v1)
    den = 1.0 + e2
    g_ref[0] = (jnp.where(lane == 0.0, i1, 0.0) + jnp.where(lane == 1.0, i2, 0.0)
                + jnp.where(lane == 2.0, 1.0 / den, 0.0) + jnp.where(lane == 3.0, e2 / den, 0.0))


def _router(x, mod, k, w_router):
    tm = 512
    shift, scale, _ = _mod_specs(k, 2)
    wr = jnp.pad(w_router, ((0, 0), (0, LANES - N_EXPERTS)))
    rows = pl.BlockSpec((1, tm, D_MODEL), lambda b, i: (b, i, 0))
    return pl.pallas_call(
        _router_body,
        grid=(BATCH, SEQ // tm),
        in_specs=[rows, shift, scale, _const_spec((D_MODEL, LANES))],
        out_specs=(pl.BlockSpec((1, tm, LANES), lambda b, i: (b, i, 0)), rows),
        out_shape=(jax.ShapeDtypeStruct((BATCH, SEQ, LANES), F32),
                   jax.ShapeDtypeStruct((BATCH, SEQ, D_MODEL), F32)),
        compiler_params=_params(("parallel", "parallel")),
        name="router",
    )(x, mod, mod, wr)


N_TOKENS = BATCH * SEQ
N_ASSIGN = 2 * N_TOKENS
MOE_BLOCK = 1024
MOE_SUB = 256
MOE_TILE = N_ASSIGN // N_EXPERTS * 5 // 4
MOE_TILES_ANY = -(-(N_ASSIGN + N_EXPERTS * (MOE_TILE - 1)) // MOE_TILE)
MOE_TILES_FIT = N_EXPERTS
GATHER_ROWS = MOE_TILE // 4
assert MOE_TILE % GATHER_ROWS == 0 and GATHER_ROWS % 8 == 0


def _route_counts(routing):
    e_flat = routing.reshape(N_TOKENS, LANES)[:, :2].astype(jnp.int32).T.reshape(N_ASSIGN)
    onehot = (e_flat[:, None] == jnp.arange(N_EXPERTS, dtype=jnp.int32)[None, :]).astype(jnp.int32)
    csum = jnp.cumsum(onehot, axis=0)
    return onehot, jnp.sum(csum * onehot, axis=1) - 1, csum[-1]


def _route_plan(onehot, rank, counts, n_tiles):
    padded = (counts + MOE_TILE - 1) // MOE_TILE * MOE_TILE
    group_end = jnp.cumsum(padded)
    group_start = group_end - padded
    pos = (jnp.sum(onehot * group_start[None, :], axis=1) + rank).astype(jnp.int32)
    tok = jnp.tile(jnp.arange(N_TOKENS, dtype=jnp.int32), 2)
    src = jnp.zeros((n_tiles * MOE_TILE,), jnp.int32).at[pos].set(tok, unique_indices=True)
    tile_start = jnp.arange(n_tiles, dtype=jnp.int32) * MOE_TILE
    n_used = group_end[-1] // MOE_TILE
    tile_expert = jnp.sum((tile_start[:, None] >= group_end[None, :]).astype(jnp.int32), axis=1)
    last_expert = jnp.max(jnp.where(counts > 0, jnp.arange(N_EXPERTS, dtype=jnp.int32), 0))
    tile_expert = jnp.minimum(tile_expert, last_expert)
    tile_rows = jnp.clip(counts[tile_expert] - (tile_start - group_start[tile_expert]), 0, MOE_TILE)
    tile_rows = jnp.where(jnp.arange(n_tiles) < n_used, tile_rows, 0).astype(jnp.int32)
    return pos, src, tile_expert.astype(jnp.int32), tile_rows, n_used.reshape(1).astype(jnp.int32)


GATHER_UNROLL = 8


def _gather_body(src_ref, rows_ref, hm_hbm, o_ref, buf, sem, *, tg, n_steps):
    i = pl.program_id(0)
    per = MOE_TILE // tg

    def valid_rows(step):
        n_valid = jnp.clip(rows_ref[step // per] - (step % per) * tg, 0, tg)
        return n_valid, n_valid // GATHER_UNROLL

    def issue(step, slot):
        base = step * tg
        n_valid, n_groups = valid_rows(step)

        @pl.when(n_valid < tg)
        def _():
            buf[slot] = jnp.zeros(buf.shape[1:], F32)

        def issue_group(g, carry):
            for u in range(GATHER_UNROLL):
                t = src_ref[base + g * GATHER_UNROLL + u]
                pltpu.make_async_copy(hm_hbm.at[pl.ds(t, 1)], buf.at[slot, g, pl.ds(u, 1)],
                                      sem.at[slot]).start(priority=u % 2)
            return carry

        def issue_one(r, carry):
            t = src_ref[base + r]
            pltpu.make_async_copy(hm_hbm.at[pl.ds(t, 1)],
                                  buf.at[slot, r // GATHER_UNROLL, pl.ds(r % GATHER_UNROLL, 1)],
                                  sem.at[slot]).start()
            return carry

        lax.fori_loop(0, n_groups, issue_group, 0)
        lax.fori_loop(n_groups * GATHER_UNROLL, n_valid, issue_one, 0)

    def wait(step, slot):
        n_valid, n_groups = valid_rows(step)

        @pl.when(n_groups > 0)
        def _():
            whole = buf.at[slot, pl.ds(0, n_groups)]
            pltpu.make_async_copy(whole, whole, sem.at[slot]).wait()

        def wait_one(r, carry):
            row = buf.at[slot, 0, pl.ds(0, 1)]
            pltpu.make_async_copy(row, row, sem.at[slot]).wait()
            return carry

        lax.fori_loop(n_groups * GATHER_UNROLL, n_valid, wait_one, 0)

    for slot in range(2):
        @pl.when(i % 2 == slot)
        def _():
            if slot == 0:
                pl.when(i == 0)(functools.partial(issue, i, slot))
            pl.when(i + 1 < n_steps)(functools.partial(issue, i + 1, 1 - slot))
            wait(i, slot)
            o_ref[...] = buf[slot].reshape(tg, D_MODEL).astype(BF16)


def _gather_rows(src, tile_rows, hm):
    n_rows = src.shape[0]
    tg = GATHER_ROWS
    grid_spec = pltpu.PrefetchScalarGridSpec(
        num_scalar_prefetch=2,
        grid=(n_rows // tg,),
        in_specs=[pl.BlockSpec(memory_space=pl.ANY)],
        out_specs=pl.BlockSpec((tg, D_MODEL), lambda i, src, tr: (i, 0)),
        scratch_shapes=[pltpu.VMEM((2, tg // GATHER_UNROLL, GATHER_UNROLL, D_MODEL), F32),
                        pltpu.SemaphoreType.DMA((2,))],
    )
    return pl.pallas_call(
        functools.partial(_gather_body, tg=tg, n_steps=n_rows // tg),
        grid_spec=grid_spec,
        out_shape=jax.ShapeDtypeStruct((n_rows, D_MODEL), BF16),
        compiler_params=_params(("arbitrary",)),
        name="moe_gather",
    )(src, tile_rows, hm.reshape(N_TOKENS, D_MODEL))


def _for_valid_blocks(rows, tm, block_fn, empty_fn):
    for b0 in range(0, tm, MOE_BLOCK):
        b1 = min(b0 + MOE_BLOCK, tm)
        whole = b1 if b1 - b0 == MOE_BLOCK else tm + 1
        if whole <= tm:
            pl.when(rows >= whole)(functools.partial(block_fn, slice(b0, b1)))

        for r0 in range(b0, b1, MOE_SUB):
            rs = slice(r0, r0 + MOE_SUB)

            @pl.when(jnp.logical_and(rows < whole, r0 < rows))
            def _():
                block_fn(rs)

            @pl.when(r0 >= rows)
            def _():
                empty_fn(rs)


def _gup_body(te_ref, rows_ref, nu_ref, xs_ref, wg_ref, wu_ref, h_ref, wgb, wub, *, tm):
    rows = rows_ref[pl.program_id(0)]

    @pl.when(rows > 0)
    def _():
        wgb[...] = wg_ref[0].astype(BF16)
        wub[...] = wu_ref[0].astype(BF16)

    def block(rs):
        xsb = xs_ref[rs, :]
        h_ref[rs, :] = (jax.nn.silu(_dot(xsb, wgb[...])) * _dot(xsb, wub[...])).astype(BF16)

    def empty(rs):
        h_ref[rs, :] = jnp.zeros((rs.stop - rs.start, h_ref.shape[1]), BF16)

    _for_valid_blocks(rows, tm, block, empty)


def _gdown_body(te_ref, rows_ref, nu_ref, h_ref, wd_ref, o_ref, wdb, *, tm):
    rows = rows_ref[pl.program_id(0)]

    @pl.when(rows > 0)
    def _():
        wdb[...] = wd_ref[0].astype(BF16)

    def block(rs):
        o_ref[rs, :] = _dot(h_ref[rs, :], wdb[...])

    def empty(rs):
        o_ref[rs, :] = jnp.zeros((rs.stop - rs.start, o_ref.shape[1]), F32)

    _for_valid_blocks(rows, tm, block, empty)


def _grouped_ffn(xs, tile_expert, tile_rows, n_used, layer, w_gate, w_up, w_down):
    tm = MOE_TILE
    n_tiles = tile_expert.shape[0]
    first = layer * N_EXPERTS
    n_all = w_gate.shape[0] * N_EXPERTS

    def tile(i, nu):
        return jnp.minimum(i, nu[0] - 1)

    def chunk(i, c, nu, n_chunks):
        return jnp.where(i < nu[0], c, n_chunks - 1)

    tf = 256
    n_f = D_FF_EXPERT // tf
    w_up_spec = pl.BlockSpec((1, D_MODEL, tf),
                             lambda i, f, te, tr, nu: (first + te[i], 0, chunk(i, f, nu, n_f)))
    up_spec = pltpu.PrefetchScalarGridSpec(
        num_scalar_prefetch=3,
        grid=(n_tiles, n_f),
        in_specs=[pl.BlockSpec((tm, D_MODEL), lambda i, f, te, tr, nu: (tile(i, nu), 0)),
                  w_up_spec, w_up_spec],
        out_specs=pl.BlockSpec((tm, tf), lambda i, f, te, tr, nu: (i, f)),
        scratch_shapes=[pltpu.VMEM((D_MODEL, tf), BF16), pltpu.VMEM((D_MODEL, tf), BF16)],
    )
    h = pl.pallas_call(
        functools.partial(_gup_body, tm=tm),
        grid_spec=up_spec,
        out_shape=jax.ShapeDtypeStruct((n_tiles * tm, D_FF_EXPERT), BF16),
        compiler_params=_params(("arbitrary", "arbitrary")),
        name="moe_up",
    )(tile_expert, tile_rows, n_used, xs,
      w_gate.reshape(n_all, D_MODEL, D_FF_EXPERT), w_up.reshape(n_all, D_MODEL, D_FF_EXPERT))

    tn = 512
    n_n = D_MODEL // tn
    down_spec = pltpu.PrefetchScalarGridSpec(
        num_scalar_prefetch=3,
        grid=(n_tiles, n_n),
        in_specs=[pl.BlockSpec((tm, D_FF_EXPERT), lambda i, n, te, tr, nu: (tile(i, nu), 0)),
                  pl.BlockSpec((1, D_FF_EXPERT, tn),
                               lambda i, n, te, tr, nu: (first + te[i], 0, chunk(i, n, nu, n_n)))],
        out_specs=pl.BlockSpec((tm, tn), lambda i, n, te, tr, nu: (i, n)),
        scratch_shapes=[pltpu.VMEM((D_FF_EXPERT, tn), BF16)],
    )
    return pl.pallas_call(
        functools.partial(_gdown_body, tm=tm),
        grid_spec=down_spec,
        out_shape=jax.ShapeDtypeStruct((n_tiles * tm, D_MODEL), F32),
        compiler_params=_params(("arbitrary", "arbitrary")),
        name="moe_down",
    )(tile_expert, tile_rows, n_used, h, w_down.reshape(n_all, D_FF_EXPERT, D_MODEL))


def _combine_body(pos_ref, ys_hbm, r_ref, x_ref, gate_ref, lg_ref, lb_ref, o_ref, buf, sem, *, tc):
    n_steps = pl.num_programs(0) * pl.num_programs(1)
    step = pl.program_id(0) * pl.num_programs(1) + pl.program_id(1)

    def issue(s, slot):
        base = s * tc

        def issue_group(g, carry):
            for u in range(GATHER_UNROLL):
                for k in range(2):
                    p = pos_ref[k * N_TOKENS + base + g * GATHER_UNROLL + u]
                    pltpu.make_async_copy(ys_hbm.at[pl.ds(p, 1)], buf.at[slot, k, g, pl.ds(u, 1)],
                                          sem.at[slot, k]).start(priority=k)
            return carry

        lax.fori_loop(0, tc // GATHER_UNROLL, issue_group, 0)

    for slot in range(2):
        @pl.when(step % 2 == slot)
        def _():
            if slot == 0:
                pl.when(step == 0)(functools.partial(issue, step, slot))
            pl.when(step + 1 < n_steps)(functools.partial(issue, step + 1, 1 - slot))
            for k in range(2):
                pltpu.make_async_copy(buf.at[slot, k], buf.at[slot, k], sem.at[slot, k]).wait()
            routing = r_ref[0]
            y = (routing[:, 2:3] * buf[slot, 0].reshape(tc, D_MODEL)
                 + routing[:, 3:4] * buf[slot, 1].reshape(tc, D_MODEL))
            o_ref[0] = _deepnorm(x_ref[0], y, gate_ref[0], lg_ref[...], lb_ref[...])


def _combine(pos, ys, routing, x, mod, k, ln_g, ln_b):
    tc = 512
    rows = pl.BlockSpec((1, tc, D_MODEL), lambda b, i, pos: (b, i, 0))
    vec = pl.BlockSpec((1, D_MODEL), lambda b, i, pos: (0, 0))
    grid_spec = pltpu.PrefetchScalarGridSpec(
        num_scalar_prefetch=1,
        grid=(BATCH, SEQ // tc),
        in_specs=[pl.BlockSpec(memory_space=pl.ANY),
                  pl.BlockSpec((1, tc, LANES), lambda b, i, pos: (b, i, 0)),
                  rows,
                  pl.BlockSpec((1, 1, D_MODEL), lambda b, i, pos: (k * 8 + b, 0, 2)),
                  vec, vec],
        out_specs=rows,
        scratch_shapes=[pltpu.VMEM((2, 2, tc // GATHER_UNROLL, GATHER_UNROLL, D_MODEL), F32),
                        pltpu.SemaphoreType.DMA((2, 2))],
    )
    return pl.pallas_call(
        functools.partial(_combine_body, tc=tc),
        grid_spec=grid_spec,
        out_shape=jax.ShapeDtypeStruct((BATCH, SEQ, D_MODEL), F32),
        compiler_params=_params(("arbitrary", "arbitrary")),
        name="moe_combine",
    )(pos, ys, routing, x, mod, ln_g.reshape(1, -1), ln_b.reshape(1, -1))


def _moe(x, mod, k, w_router, layer, w_gate, w_up, w_down, ln_g, ln_b):
    routing, hm = _router(x, mod, k, w_router)
    onehot, rank, counts = _route_counts(routing)

    def experts_over(n_tiles):
        def run(_):
            pos, src, tile_expert, tile_rows, n_used = _route_plan(onehot, rank, counts, n_tiles)
            xs = _gather_rows(src, tile_rows, hm)
            ys = _grouped_ffn(xs, tile_expert, tile_rows, n_used, layer, w_gate, w_up, w_down)
            return _combine(pos, ys, routing, x, mod, k, ln_g, ln_b)
        return run

    return lax.cond(jnp.all(counts <= MOE_TILE),
                    experts_over(MOE_TILES_FIT), experts_over(MOE_TILES_ANY), None)


def kernel(x, c, positions, ada_w, ada_b, ln_g, ln_b, w_in_ab, q_norm_g, w_q_up, kv_norm_g, w_kv_up, sgu_norm_g, sgu_norm_b, sgu_w, sgu_b, w_out_ab, ffn_w_gate, ffn_w_up, ffn_w_down, pool_w, pool_scale, w_out_c, router_w, moe_w_gate, moe_w_up, moe_w_down):
    mod = _ada_modulation(c, ada_w, ada_b)
    tables = _rope_tables(positions)
    for l in range(DEPTH):
        j = l // 2
        k_tok, k_ch = 2 * l, 2 * l + 1
        if l % 2 == 0:
            q, k, vt, sgu = _front(x, mod, k_tok, tables, w_in_ab[j], q_norm_g[j], w_q_up[j],
                                   kv_norm_g[j], w_kv_up[j],
                                   sgu_norm_g[j], sgu_norm_b[j], sgu_w[j], sgu_b[j])
            attn = _attention(q, k, vt)
            x = _outproj(attn, sgu, x, mod, k_tok, w_out_ab[j], ln_g[l, 0], ln_b[l, 0])
            x = _ffn(x, mod, k_ch, j, ffn_w_gate, ffn_w_up, ffn_w_down[j], ln_g[l, 1], ln_b[l, 1])
        else:
            x = _pool_mixer(x, mod, k_tok, pool_w[j], pool_scale[j], w_out_c[j], ln_g[l, 0], ln_b[l, 0])
            x = _moe(x, mod, k_ch, router_w[j], j, moe_w_gate, moe_w_up, moe_w_down,
                     ln_g[l, 1], ln_b[l, 1])
    return x
```

```python
import functools

import jax
import jax.numpy as jnp
import numpy as np
from jax import lax
from jax.experimental import pallas as pl
from jax.experimental.pallas import tpu as pltpu

F32 = jnp.float32
BF16 = jnp.bfloat16

D_MODEL = 2048
BATCH = 2
SEQ = 4096
DEPTH = 4
MLA_HEADS = 8
QK_NOPE_DIM = 128
QK_ROPE_DIM = 64
V_HEAD_DIM = 128
Q_LORA_RANK = 768
KV_LORA_RANK = 512
ROPE_THETA = 10000.0
SGU_GROUPS = 8
SGU_GROUP_DIM = 128
SGU_WIDTH = SGU_GROUPS * SGU_GROUP_DIM
CHUNK = 128
ATTN_WIDTH = MLA_HEADS * V_HEAD_DIM
POOL_WINDOWS = (2, 4, 8, 16)
POOL_GROUP_DIM = D_MODEL // 4
D_FF = 5632
N_EXPERTS = 8
D_FF_EXPERT = 2816
DEEPNORM_ALPHA = (2 * DEPTH) ** 0.25
LN_EPS = 1e-5
RMS_EPS = 1e-6
SM_SCALE = (QK_NOPE_DIM + QK_ROPE_DIM) ** -0.5
LOG2_E = 1.4426950408889634

LANES = 128
HEAD_LANES = 2 * LANES
QK_WIDTH = MLA_HEADS * HEAD_LANES
VMEM_LIMIT = 60 * 1024 * 1024
POOL_HALO = 128


def _params(semantics):
    return pltpu.CompilerParams(dimension_semantics=semantics, vmem_limit_bytes=VMEM_LIMIT)


def _const_spec(shape):
    nd = len(shape)
    return pl.BlockSpec(shape, lambda *_: (0,) * nd, pipeline_mode=pl.Buffered(1))


def _dot(a, b):
    return jnp.dot(a, b, preferred_element_type=F32)


def _modulate(x, shift, scale):
    return x * (1.0 + scale) + shift


def _deepnorm(x, y, gate, g, b):
    r = DEEPNORM_ALPHA * x + (1.0 + gate) * y
    mu = jnp.mean(r, axis=-1, keepdims=True)
    rc = r - mu
    var = jnp.mean(rc * rc, axis=-1, keepdims=True)
    return rc * lax.rsqrt(var + LN_EPS) * g + b


def _rms(x, g):
    return x * lax.rsqrt(jnp.mean(x * x, axis=-1, keepdims=True) + RMS_EPS) * g


def _rope(r, cos_t, sin_hi, sin_lo):
    return (r * cos_t + pltpu.roll(r, QK_ROPE_DIM // 2, 1) * sin_hi
            + pltpu.roll(r, LANES - QK_ROPE_DIM // 2, 1) * sin_lo)


def _mod_specs(k, n_grid):
    def spec(part):
        if n_grid == 2:
            return pl.BlockSpec((1, 1, D_MODEL), lambda b, i: (k * 8 + b, 0, part))
        if n_grid == 3:
            return pl.BlockSpec((1, 1, D_MODEL), lambda b, i, f: (k * 8 + b, 0, part))
        return pl.BlockSpec((1, 1, D_MODEL), lambda b, i, e, f: (k * 8 + b, 0, part))
    return spec(0), spec(1), spec(2)


def _ada_body(c_ref, w_ref, b_ref, o_ref):
    s = jax.nn.silu(c_ref[...]).astype(BF16)
    o_ref[0] = _dot(s, w_ref[0].astype(BF16)) + b_ref[0]


def _ada_modulation(c, ada_w, ada_b):
    n_mod = 2 * DEPTH
    tn = 1024
    c_pad = jnp.pad(c, ((0, 8 - BATCH), (0, 0)))
    out = pl.pallas_call(
        _ada_body,
        grid=(n_mod, 3 * D_MODEL // tn),
        in_specs=[
            pl.BlockSpec((8, D_MODEL), lambda k, n: (0, 0)),
            pl.BlockSpec((1, D_MODEL, tn), lambda k, n: (k, 0, n)),
            pl.BlockSpec((1, 1, tn), lambda k, n: (k, 0, n)),
        ],
        out_specs=pl.BlockSpec((1, 8, tn), lambda k, n: (k, 0, n)),
        out_shape=jax.ShapeDtypeStruct((n_mod, 8, 3 * D_MODEL), F32),
        compiler_params=_params(("parallel", "parallel")),
        name="ada_mod",
    )(c_pad, ada_w.reshape(n_mod, D_MODEL, 3 * D_MODEL), ada_b.reshape(n_mod, 1, 3 * D_MODEL))
    return out.reshape(n_mod * 8, 1, 3 * D_MODEL)


def _rope_table_body(pos_ref, inv_ref, cos_ref, hi_ref, lo_ref):
    ang = pos_ref[0].astype(F32) * inv_ref[...]
    lane = lax.broadcasted_iota(jnp.int32, ang.shape, 1)
    cos = jnp.cos(ang)
    sin = jnp.sin(ang)
    half = QK_ROPE_DIM // 2
    cos_ref[0] = jnp.where(lane < QK_ROPE_DIM, cos, 0.0)
    hi_ref[0] = jnp.where(lane < half, 0.0, jnp.where(lane < QK_ROPE_DIM, sin, 0.0))
    lo_ref[0] = jnp.where(lane < half, -sin, 0.0)


def _rope_tables(positions):
    tm = 512
    half = QK_ROPE_DIM // 2
    inv = ROPE_THETA ** (-jnp.arange(0, QK_ROPE_DIM, 2, dtype=F32) / QK_ROPE_DIM)
    inv_lanes = jnp.concatenate([inv, inv, jnp.zeros((LANES - 2 * half,), F32)]).reshape(1, LANES)
    tab = jax.ShapeDtypeStruct((BATCH, SEQ, LANES), F32)
    spec = pl.BlockSpec((1, tm, LANES), lambda b, i: (b, i, 0))
    return pl.pallas_call(
        _rope_table_body,
        grid=(BATCH, SEQ // tm),
        in_specs=[pl.BlockSpec((1, tm, 1), lambda b, i: (b, i, 0)),
                  pl.BlockSpec((1, LANES), lambda b, i: (0, 0))],
        out_specs=(spec, spec, spec),
        out_shape=(tab, tab, tab),
        compiler_params=_params(("parallel", "parallel")),
        name="rope_tables",
    )(positions.reshape(BATCH, SEQ, 1), inv_lanes)


def _front_body(x_ref, sh_ref, sc_ref, wq_ref, wkv_ref, wkr_ref, wu_ref, wv_ref,
                cos_ref, hi_ref, lo_ref, qg_ref, kvg_ref, wqn_ref, wqr_ref, wkn_ref, wvt_ref,
                sg_ref, sb_ref, sw_ref, sbias_ref,
                q_ref, k_ref, vt_ref, sgu_ref, *, tm):
    hm = _modulate(x_ref[0], sh_ref[0], sc_ref[0]).astype(BF16)
    _qkv_compute(_dot(hm, wq_ref[...]), _dot(hm, wkv_ref[...]), _dot(hm, wkr_ref[...]),
                 cos_ref[0], hi_ref[0], lo_ref[0], qg_ref, kvg_ref,
                 wqn_ref, wqr_ref, wkn_ref, wvt_ref, q_ref, k_ref, vt_ref)
    _sgu_compute(_dot(hm, wu_ref[...]), _dot(hm, wv_ref[...]),
                 sg_ref, sb_ref, sw_ref, sbias_ref, sgu_ref, tm)


def _front(x, mod, k, tables, w_in, q_norm_g, w_q_up, kv_norm_g, w_kv_up,
           sgu_norm_g, sgu_norm_b, sgu_w, sgu_b):
    tm = 512
    a, b, c_, d = (Q_LORA_RANK, Q_LORA_RANK + KV_LORA_RANK,
                   Q_LORA_RANK + KV_LORA_RANK + QK_ROPE_DIM,
                   Q_LORA_RANK + KV_LORA_RANK + QK_ROPE_DIM + SGU_WIDTH)
    wq = w_in[:, :a].astype(BF16)
    wkv = w_in[:, a:b].astype(BF16)
    wkr = jnp.pad(w_in[:, b:c_], ((0, 0), (0, LANES - QK_ROPE_DIM))).astype(BF16)
    wu = w_in[:, c_:d].astype(BF16)
    wv = w_in[:, d:].astype(BF16)
    in_widths = (Q_LORA_RANK, KV_LORA_RANK, LANES, SGU_WIDTH, SGU_WIDTH)

    wq_up = w_q_up.reshape(Q_LORA_RANK, MLA_HEADS, QK_NOPE_DIM + QK_ROPE_DIM)
    wqn = wq_up[:, :, :QK_NOPE_DIM].reshape(Q_LORA_RANK, MLA_HEADS * LANES).astype(BF16)
    wqr = jnp.pad(wq_up[:, :, QK_NOPE_DIM:], ((0, 0), (0, 0), (0, LANES - QK_ROPE_DIM)))
    wqr = wqr.reshape(Q_LORA_RANK, MLA_HEADS * LANES).astype(BF16)
    wkv_up = w_kv_up.reshape(KV_LORA_RANK, MLA_HEADS, QK_NOPE_DIM + V_HEAD_DIM)
    wkn = wkv_up[:, :, :QK_NOPE_DIM].reshape(KV_LORA_RANK, MLA_HEADS * LANES).astype(BF16)
    wvt = wkv_up[:, :, QK_NOPE_DIM:].reshape(KV_LORA_RANK, ATTN_WIDTH).T.astype(BF16)
    sgu_bias = jnp.repeat(sgu_b.T, SGU_GROUP_DIM, axis=1)

    shift, scale, _ = _mod_specs(k, 2)

    def rows(w):
        return pl.BlockSpec((1, tm, w), lambda b, i: (b, i, 0))

    return pl.pallas_call(
        functools.partial(_front_body, tm=tm),
        grid=(BATCH, SEQ // tm),
        in_specs=[rows(D_MODEL), shift, scale]
                 + [_const_spec((D_MODEL, w)) for w in in_widths]
                 + [rows(LANES), rows(LANES), rows(LANES),
                    _const_spec((1, Q_LORA_RANK)), _const_spec((1, KV_LORA_RANK)),
                    _const_spec((Q_LORA_RANK, MLA_HEADS * LANES)),
                    _const_spec((Q_LORA_RANK, MLA_HEADS * LANES)),
                    _const_spec((KV_LORA_RANK, MLA_HEADS * LANES)),
                    _const_spec((ATTN_WIDTH, KV_LORA_RANK)),
                    _const_spec((1, SGU_WIDTH)), _const_spec((1, SGU_WIDTH)),
                    _const_spec((SGU_GROUPS, CHUNK, CHUNK)), _const_spec((CHUNK, SGU_WIDTH))],
        out_specs=(rows(QK_WIDTH), rows(QK_WIDTH),
                   pl.BlockSpec((1, ATTN_WIDTH, tm), lambda b, i: (b, 0, i)),
                   rows(SGU_WIDTH)),
        out_shape=(jax.ShapeDtypeStruct((BATCH, SEQ, QK_WIDTH), BF16),
                   jax.ShapeDtypeStruct((BATCH, SEQ, QK_WIDTH), BF16),
                   jax.ShapeDtypeStruct((BATCH, ATTN_WIDTH, SEQ), BF16),
                   jax.ShapeDtypeStruct((BATCH, SEQ, SGU_WIDTH), BF16)),
        compiler_params=_params(("parallel", "parallel")),
        name="front_ab",
    )(x, mod, mod, wq, wkv, wkr, wu, wv, *tables,
      q_norm_g.reshape(1, -1), kv_norm_g.reshape(1, -1), wqn, wqr, wkn, wvt,
      sgu_norm_g.reshape(1, -1), sgu_norm_b.reshape(1, -1), sgu_w, sgu_bias)


def _qkv_compute(cq, ckv, kr, cos_t, sin_hi, sin_lo, qg_ref, kvg_ref,
                 wqn_ref, wqr_ref, wkn_ref, wvt_ref, q_ref, k_ref, vt_ref):
    cqn = _rms(cq, qg_ref[...]).astype(BF16)
    qn = _dot(cqn, wqn_ref[...])
    qr = _dot(cqn, wqr_ref[...])
    ckvn = _rms(ckv, kvg_ref[...]).astype(BF16)
    kn = _dot(ckvn, wkn_ref[...])
    vt_ref[0] = lax.dot_general(wvt_ref[...], ckvn, (((1,), (1,)), ((), ())),
                                preferred_element_type=F32).astype(BF16)
    k_rope = _rope(kr, cos_t, sin_hi, sin_lo).astype(BF16)
    for h in range(MLA_HEADS):
        nope = slice(h * LANES, (h + 1) * LANES)
        lo = h * HEAD_LANES
        q_ref[0, :, lo:lo + LANES] = qn[:, nope].astype(BF16)
        q_ref[0, :, lo + LANES:lo + HEAD_LANES] = _rope(qr[:, nope], cos_t, sin_hi, sin_lo).astype(BF16)
        k_ref[0, :, lo:lo + LANES] = kn[:, nope].astype(BF16)
        k_ref[0, :, lo + LANES:lo + HEAD_LANES] = k_rope


def _attn_body(qi_ref, kj_ref, q_ref, k_ref, vt_ref, o_ref, m_sc, l_sc, acc_sc, *, tq):
    p = pl.program_id(1)
    i = qi_ref[p]
    j = kj_ref[p]

    @pl.when(j == 0)
    def _():
        m_sc[...] = jnp.full(m_sc.shape, -jnp.inf, F32)
        l_sc[...] = jnp.zeros(l_sc.shape, F32)
        acc_sc[...] = jnp.zeros(acc_sc.shape, F32)

    def accumulate(diagonal):
        if diagonal:
            key = lax.broadcasted_iota(jnp.int32, (tq, tq), 0)
            qry = lax.broadcasted_iota(jnp.int32, (tq, tq), 1)
            visible = key <= qry
        for h in range(MLA_HEADS):
            qk = slice(h * HEAD_LANES, (h + 1) * HEAD_LANES)
            hv = slice(h * V_HEAD_DIM, (h + 1) * V_HEAD_DIM)
            s = lax.dot_general(k_ref[0, :, qk], q_ref[0, :, qk], (((1,), (1,)), ((), ())),
                                preferred_element_type=F32) * (SM_SCALE * LOG2_E)
            if diagonal:
                s = jnp.where(visible, s, -jnp.inf)
            m_prev = m_sc[h]
            m_new = jnp.maximum(m_prev, jnp.max(s, axis=0, keepdims=True))
            alpha = jnp.exp2(m_prev - m_new)
            pr = jnp.exp2(s - m_new)
            l_sc[h] = alpha * l_sc[h] + jnp.sum(pr, axis=0, keepdims=True)
            acc_sc[hv, :] = alpha * acc_sc[hv, :] + _dot(vt_ref[0, hv, :], pr.astype(BF16))
            m_sc[h] = m_new

    @pl.when(j < i)
    def _():
        accumulate(False)

    @pl.when(j == i)
    def _():
        accumulate(True)
        for h in range(MLA_HEADS):
            hv = slice(h * V_HEAD_DIM, (h + 1) * V_HEAD_DIM)
            o_ref[0, :, hv] = (acc_sc[hv, :] / l_sc[h]).T.astype(BF16)


def _attention(q, k, vt):
    tq = 512
    nb = SEQ // tq
    pairs = [(i, j) for i in range(nb) for j in range(i + 1)]
    qi = jnp.asarray(np.array([p[0] for p in pairs], np.int32))
    kj = jnp.asarray(np.array([p[1] for p in pairs], np.int32))
    grid_spec = pltpu.PrefetchScalarGridSpec(
        num_scalar_prefetch=2,
        grid=(BATCH, len(pairs)),
        in_specs=[pl.BlockSpec((1, tq, QK_WIDTH), lambda b, p, qi, kj: (b, qi[p], 0)),
                  pl.BlockSpec((1, tq, QK_WIDTH), lambda b, p, qi, kj: (b, kj[p], 0)),
                  pl.BlockSpec((1, ATTN_WIDTH, tq), lambda b, p, qi, kj: (b, 0, kj[p]))],
        out_specs=pl.BlockSpec((1, tq, ATTN_WIDTH), lambda b, p, qi, kj: (b, qi[p], 0)),
        scratch_shapes=[pltpu.VMEM((MLA_HEADS, 1, tq), F32),
                        pltpu.VMEM((MLA_HEADS, 1, tq), F32),
                        pltpu.VMEM((ATTN_WIDTH, tq), F32)],
    )
    return pl.pallas_call(
        functools.partial(_attn_body, tq=tq),
        grid_spec=grid_spec,
        out_shape=jax.ShapeDtypeStruct((BATCH, SEQ, ATTN_WIDTH), BF16),
        compiler_params=_params(("parallel", "arbitrary")),
        name="mla_attention",
    )(qi, kj, q, k, vt)


def _sgu_compute(zu, zv, g_ref, b_ref, w_ref, bias_ref, o_ref, tm):
    row = lax.broadcasted_iota(jnp.int32, (CHUNK, CHUNK), 0)
    col = lax.broadcasted_iota(jnp.int32, (CHUNK, CHUNK), 1)
    for g in range(SGU_GROUPS):
        gs = slice(g * SGU_GROUP_DIM, (g + 1) * SGU_GROUP_DIM)
        v = jax.nn.gelu(zv[:, gs])
        mu = jnp.mean(v, axis=-1, keepdims=True)
        vc = v - mu
        var = jnp.mean(vc * vc, axis=-1, keepdims=True)
        vn = (vc * lax.rsqrt(var + LN_EPS) * g_ref[:, gs] + b_ref[:, gs]).astype(BF16)
        w = jnp.where(col <= row, w_ref[g], 0.0).astype(BF16)
        for n in range(tm // CHUNK):
            rs = slice(n * CHUNK, (n + 1) * CHUNK)
            s = _dot(w, vn[rs]) + bias_ref[:, gs]
            o_ref[0, rs, gs] = (jax.nn.gelu(zu[rs, gs]) * s).astype(BF16)


def _outproj_body(a_ref, s_ref, x_ref, gate_ref, wa_ref, ws_ref, lg_ref, lb_ref, o_ref):
    half = a_ref.shape[1] // 2
    for r0 in (0, half):
        rs = slice(r0, r0 + half)
        y = _dot(a_ref[0, rs], wa_ref[...]) + _dot(s_ref[0, rs], ws_ref[...])
        o_ref[0, rs] = _deepnorm(x_ref[0, rs], y, gate_ref[0], lg_ref[...], lb_ref[...])


def _outproj(attn, sgu, x, mod, k, w_out, ln_g, ln_b):
    tm = 512
    wa = w_out[:ATTN_WIDTH].astype(BF16)
    ws = w_out[ATTN_WIDTH:].astype(BF16)
    _, _, gate = _mod_specs(k, 2)

    def rows(w):
        return pl.BlockSpec((1, tm, w), lambda b, i: (b, i, 0))

    return pl.pallas_call(
        _outproj_body,
        grid=(BATCH, SEQ // tm),
        in_specs=[rows(ATTN_WIDTH), rows(SGU_WIDTH), rows(D_MODEL), gate,
                  _const_spec((ATTN_WIDTH, D_MODEL)), _const_spec((SGU_WIDTH, D_MODEL)),
                  _const_spec((1, D_MODEL)), _const_spec((1, D_MODEL))],
        out_specs=rows(D_MODEL),
        out_shape=jax.ShapeDtypeStruct((BATCH, SEQ, D_MODEL), F32),
        compiler_params=_params(("parallel", "parallel")),
        name="outproj_ab",
    )(attn, sgu, x, mod, wa, ws, ln_g.reshape(1, -1), ln_b.reshape(1, -1))


def _ffn_up_body(x_ref, sh_ref, sc_ref, wg_ref, wu_ref, h_ref, hm_sc):
    @pl.when(pl.program_id(2) == 0)
    def _():
        hm_sc[...] = _modulate(x_ref[0], sh_ref[0], sc_ref[0]).astype(BF16)

    hm = hm_sc[...]
    g = _dot(hm, wg_ref[0].astype(BF16))
    u = _dot(hm, wu_ref[0].astype(BF16))
    h_ref[0] = (jax.nn.silu(g) * u).astype(BF16)


def _ffn_down_body(h_ref, x_ref, gate_ref, wd_ref, lg_ref, lb_ref, o_ref, y_sc, *, n_n):
    n = pl.program_id(2)
    y_sc[n] = _dot(h_ref[0], wd_ref[...])

    @pl.when(n == n_n - 1)
    def _():
        y = jnp.concatenate([y_sc[c] for c in range(n_n)], axis=-1)
        o_ref[0] = _deepnorm(x_ref[0], y, gate_ref[0], lg_ref[...], lb_ref[...])


def _ffn(x, mod, k, layer, w_gate, w_up, w_down, ln_g, ln_b):
    tm, tf = 2048, 256
    shift, scale, gate = _mod_specs(k, 3)
    h = pl.pallas_call(
        _ffn_up_body,
        grid=(BATCH, SEQ // tm, D_FF // tf),
        in_specs=[pl.BlockSpec((1, tm, D_MODEL), lambda b, i, f: (b, i, 0), pipeline_mode=pl.Buffered(1)),
                  shift, scale,
                  pl.BlockSpec((1, D_MODEL, tf), lambda b, i, f: (layer, 0, f)),
                  pl.BlockSpec((1, D_MODEL, tf), lambda b, i, f: (layer, 0, f))],
        out_specs=pl.BlockSpec((1, tm, tf), lambda b, i, f: (b, i, f)),
        out_shape=jax.ShapeDtypeStruct((BATCH, SEQ, D_FF), BF16),
        scratch_shapes=[pltpu.VMEM((tm, D_MODEL), BF16)],
        compiler_params=_params(("parallel", "parallel", "arbitrary")),
        name="ffn_up",
    )(x, mod, mod, w_gate, w_up)

    tm, tn = 512, 512
    n_n = D_MODEL // tn
    rows = pl.BlockSpec((1, tm, D_MODEL), lambda b, i, n: (b, i, 0))
    vec = pl.BlockSpec((1, D_MODEL), lambda b, i, n: (0, 0))
    return pl.pallas_call(
        functools.partial(_ffn_down_body, n_n=n_n),
        grid=(BATCH, SEQ // tm, n_n),
        in_specs=[pl.BlockSpec((1, tm, D_FF), lambda b, i, n: (b, i, 0)),
                  rows,
                  gate,
                  pl.BlockSpec((D_FF, tn), lambda b, i, n: (0, n)),
                  vec, vec],
        out_specs=rows,
        out_shape=jax.ShapeDtypeStruct((BATCH, SEQ, D_MODEL), F32),
        scratch_shapes=[pltpu.VMEM((n_n, tm, tn), F32)],
        compiler_params=_params(("parallel", "parallel", "arbitrary")),
        name="ffn_down",
    )(h, x, mod, w_down.astype(BF16), ln_g.reshape(1, -1), ln_b.reshape(1, -1))


def _pool_body(x_ref, halo_ref, sh_ref, sc_ref, gate_ref, pw_ref, ps_ref, wo_ref, lg_ref, lb_ref,
               o_ref, *, tm):
    i = pl.program_id(1)
    x = x_ref[0]
    hm = _modulate(x, sh_ref[0], sc_ref[0])
    halo = jnp.where(i > 0, _modulate(halo_ref[0], sh_ref[0], sc_ref[0]), 0.0)
    ext = jnp.concatenate([halo, hm], axis=0)
    ext_hi = ext.astype(BF16)
    ext_lo = (ext - ext_hi.astype(F32)).astype(BF16)
    t_loc = lax.broadcasted_iota(jnp.int32, (POOL_HALO, 2 * POOL_HALO), 0)
    s_loc = lax.broadcasted_iota(jnp.int32, (POOL_HALO, 2 * POOL_HALO), 1)
    lag = t_loc + POOL_HALO - s_loc
    t_glob = i * tm + lax.broadcasted_iota(jnp.int32, (tm, 1), 0)
    outs = []
    for gi, w in enumerate(POOL_WINDOWS):
        cs = slice(gi * POOL_GROUP_DIM, (gi + 1) * POOL_GROUP_DIM)
        band = jnp.where(lag >= 0, jnp.where(lag < w, 1.0, 0.0), 0.0).astype(BF16)
        wsum = jnp.concatenate(
            [_dot(band, ext_hi[c:c + 2 * POOL_HALO, cs]) + _dot(band, ext_lo[c:c + 2 * POOL_HALO, cs])
             for c in range(0, tm, POOL_HALO)], axis=0)
        count = jnp.minimum(t_glob + 1, w).astype(F32)
        d = (wsum / count - hm[:, cs]).astype(BF16)
        outs.append(_dot(d, pw_ref[gi]))
    pooled = (jnp.concatenate(outs, axis=-1) * ps_ref[...]).astype(BF16)
    y = _dot(pooled, wo_ref[...])
    o_ref[0] = _deepnorm(x, y, gate_ref[0], lg_ref[...], lb_ref[...])


def _pool_mixer(x, mod, k, pool_w, pool_scale, w_out_c, ln_g, ln_b):
    tm = 512
    per = tm // POOL_HALO
    shift, scale, gate = _mod_specs(k, 2)
    rows = pl.BlockSpec((1, tm, D_MODEL), lambda b, i: (b, i, 0))
    halo = pl.BlockSpec((1, POOL_HALO, D_MODEL), lambda b, i: (b, jnp.maximum(i * per - 1, 0), 0))
    return pl.pallas_call(
        functools.partial(_pool_body, tm=tm),
        grid=(BATCH, SEQ // tm),
        in_specs=[rows, halo, shift, scale, gate,
                  _const_spec((len(POOL_WINDOWS), POOL_GROUP_DIM, POOL_GROUP_DIM)),
                  _const_spec((1, D_MODEL)), _const_spec((D_MODEL, D_MODEL)),
                  _const_spec((1, D_MODEL)), _const_spec((1, D_MODEL))],
        out_specs=rows,
        out_shape=jax.ShapeDtypeStruct((BATCH, SEQ, D_MODEL), F32),
        compiler_params=_params(("parallel", "parallel")),
        name="pool_mixer",
    )(x, x, mod, mod, mod, pool_w.astype(BF16), pool_scale.reshape(1, -1),
      w_out_c.astype(BF16), ln_g.reshape(1, -1), ln_b.reshape(1, -1))


def _router_body(x_ref, sh_ref, sc_ref, wr_ref, g_ref, hm_ref):
    hm = _modulate(x_ref[0], sh_ref[0], sc_ref[0])
    hm_ref[0] = hm
    hm_hi = hm.astype(BF16)
    hm_lo = (hm - hm_hi.astype(F32)).astype(BF16)
    w = wr_ref[...]
    w_hi = w.astype(BF16)
    w_lo = (w - w_hi.astype(F32)).astype(BF16)
    logits = _dot(hm_hi, w_hi) + (_dot(hm_lo, w_hi) + _dot(hm_hi, w_lo))
    lane = lax.broadcasted_iota(jnp.int32, logits.shape, 1).astype(F32)
    logits = jnp.where(lane < N_EXPERTS, logits, -jnp.inf)
    v1 = jnp.max(logits, axis=-1, keepdims=True)
    i1 = jnp.min(jnp.where(logits == v1, lane, float(LANES)), axis=-1, keepdims=True)
    rest = jnp.where(lane == i1, -jnp.inf, logits)
    v2 = jnp.max(rest, axis=-1, keepdims=True)
    i2 = jnp.min(jnp.where(rest == v2, lane, float(LANES)), axis=-1, keepdims=True)
    e2 = jnp.exp(v2 - v1)
    den = 1.0 + e2
    g_ref[0] = (jnp.where(lane == 0.0, i1, 0.0) + jnp.where(lane == 1.0, i2, 0.0)
                + jnp.where(lane == 2.0, 1.0 / den, 0.0) + jnp.where(lane == 3.0, e2 / den, 0.0))


def _router(x, mod, k, w_router):
    tm = 512
    shift, scale, _ = _mod_specs(k, 2)
    wr = jnp.pad(w_router, ((0, 0), (0, LANES - N_EXPERTS)))
    rows = pl.BlockSpec((1, tm, D_MODEL), lambda b, i: (b, i, 0))
    return pl.pallas_call(
        _router_body,
        grid=(BATCH, SEQ // tm),
        in_specs=[rows, shift, scale, _const_spec((D_MODEL, LANES))],
        out_specs=(pl.BlockSpec((1, tm, LANES), lambda b, i: (b, i, 0)), rows),
        out_shape=(jax.ShapeDtypeStruct((BATCH, SEQ, LANES), F32),
                   jax.ShapeDtypeStruct((BATCH, SEQ, D_MODEL), F32)),
        compiler_params=_params(("parallel", "parallel")),
        name="router",
    )(x, mod, mod, wr)


N_TOKENS = BATCH * SEQ
N_ASSIGN = 2 * N_TOKENS
MOE_BLOCK = 1024
MOE_SUB = 256
MOE_TILE = N_ASSIGN // N_EXPERTS * 5 // 4
MOE_TILES_ANY = -(-(N_ASSIGN + N_EXPERTS * (MOE_TILE - 1)) // MOE_TILE)
MOE_TILES_FIT = N_EXPERTS
GATHER_ROWS = MOE_TILE // 4
assert MOE_TILE % GATHER_ROWS == 0 and GATHER_ROWS % 8 == 0


def _route_counts(routing):
    e_flat = routing.reshape(N_TOKENS, LANES)[:, :2].astype(jnp.int32).T.reshape(N_ASSIGN)
    onehot = (e_flat[:, None] == jnp.arange(N_EXPERTS, dtype=jnp.int32)[None, :]).astype(jnp.int32)
    csum = jnp.cumsum(onehot, axis=0)
    return onehot, jnp.sum(csum * onehot, axis=1) - 1, csum[-1]


def _route_plan(onehot, rank, counts, n_tiles):
    padded = (counts + MOE_TILE - 1) // MOE_TILE * MOE_TILE
    group_end = jnp.cumsum(padded)
    group_start = group_end - padded
    pos = (jnp.sum(onehot * group_start[None, :], axis=1) + rank).astype(jnp.int32)
    tok = jnp.tile(jnp.arange(N_TOKENS, dtype=jnp.int32), 2)
    src = jnp.zeros((n_tiles * MOE_TILE,), jnp.int32).at[pos].set(tok, unique_indices=True)
    tile_start = jnp.arange(n_tiles, dtype=jnp.int32) * MOE_TILE
    n_used = group_end[-1] // MOE_TILE
    tile_expert = jnp.sum((tile_start[:, None] >= group_end[None, :]).astype(jnp.int32), axis=1)
    last_expert = jnp.max(jnp.where(counts > 0, jnp.arange(N_EXPERTS, dtype=jnp.int32), 0))
    tile_expert = jnp.minimum(tile_expert, last_expert)
    tile_rows = jnp.clip(counts[tile_expert] - (tile_start - group_start[tile_expert]), 0, MOE_TILE)
    tile_rows = jnp.where(jnp.arange(n_tiles) < n_used, tile_rows, 0).astype(jnp.int32)
    return pos, src, tile_expert.astype(jnp.int32), tile_rows, n_used.reshape(1).astype(jnp.int32)


GATHER_UNROLL = 8


def _gather_body(src_ref, rows_ref, hm_hbm, o_ref, buf, sem, *, tg, n_steps):
    i = pl.program_id(0)
    per = MOE_TILE // tg

    def valid_rows(step):
        n_valid = jnp.clip(rows_ref[step // per] - (step % per) * tg, 0, tg)
        return n_valid, n_valid // GATHER_UNROLL

    def issue(step, slot):
        base = step * tg
        n_valid, n_groups = valid_rows(step)

        @pl.when(n_valid < tg)
        def _():
            buf[slot] = jnp.zeros(buf.shape[1:], F32)

        def issue_group(g, carry):
            for u in range(GATHER_UNROLL):
                t = src_ref[base + g * GATHER_UNROLL + u]
                pltpu.make_async_copy(hm_hbm.at[pl.ds(t, 1)], buf.at[slot, g, pl.ds(u, 1)],
                                      sem.at[slot]).start(priority=u % 2)
            return carry

        def issue_one(r, carry):
            t = src_ref[base + r]
            pltpu.make_async_copy(hm_hbm.at[pl.ds(t, 1)],
                                  buf.at[slot, r // GATHER_UNROLL, pl.ds(r % GATHER_UNROLL, 1)],
                                  sem.at[slot]).start()
            return carry

        lax.fori_loop(0, n_groups, issue_group, 0)
        lax.fori_loop(n_groups * GATHER_UNROLL, n_valid, issue_one, 0)

    def wait(step, slot):
        n_valid, n_groups = valid_rows(step)

        @pl.when(n_groups > 0)
        def _():
            whole = buf.at[slot, pl.ds(0, n_groups)]
            pltpu.make_async_copy(whole, whole, sem.at[slot]).wait()

        def wait_one(r, carry):
            row = buf.at[slot, 0, pl.ds(0, 1)]
            pltpu.make_async_copy(row, row, sem.at[slot]).wait()
            return carry

        lax.fori_loop(n_groups * GATHER_UNROLL, n_valid, wait_one, 0)

    for slot in range(2):
        @pl.when(i % 2 == slot)
        def _():
            if slot == 0:
                pl.when(i == 0)(functools.partial(issue, i, slot))
            pl.when(i + 1 < n_steps)(functools.partial(issue, i + 1, 1 - slot))
            wait(i, slot)
            o_ref[...] = buf[slot].reshape(tg, D_MODEL).astype(BF16)


def _gather_rows(src, tile_rows, hm):
    n_rows = src.shape[0]
    tg = GATHER_ROWS
    grid_spec = pltpu.PrefetchScalarGridSpec(
        num_scalar_prefetch=2,
        grid=(n_rows // tg,),
        in_specs=[pl.BlockSpec(memory_space=pl.ANY)],
        out_specs=pl.BlockSpec((tg, D_MODEL), lambda i, src, tr: (i, 0)),
        scratch_shapes=[pltpu.VMEM((2, tg // GATHER_UNROLL, GATHER_UNROLL, D_MODEL), F32),
                        pltpu.SemaphoreType.DMA((2,))],
    )
    return pl.pallas_call(
        functools.partial(_gather_body, tg=tg, n_steps=n_rows // tg),
        grid_spec=grid_spec,
        out_shape=jax.ShapeDtypeStruct((n_rows, D_MODEL), BF16),
        compiler_params=_params(("arbitrary",)),
        name="moe_gather",
    )(src, tile_rows, hm.reshape(N_TOKENS, D_MODEL))


def _for_valid_blocks(rows, tm, block_fn, empty_fn):
    for b0 in range(0, tm, MOE_BLOCK):
        b1 = min(b0 + MOE_BLOCK, tm)
        whole = b1 if b1 - b0 == MOE_BLOCK else tm + 1
        if whole <= tm:
            pl.when(rows >= whole)(functools.partial(block_fn, slice(b0, b1)))

        for r0 in range(b0, b1, MOE_SUB):
            rs = slice(r0, r0 + MOE_SUB)

            @pl.when(jnp.logical_and(rows < whole, r0 < rows))
            def _():
                block_fn(rs)

            @pl.when(r0 >= rows)
            def _():
                empty_fn(rs)


def _gup_body(te_ref, rows_ref, nu_ref, xs_ref, wg_ref, wu_ref, h_ref, wgb, wub, *, tm):
    rows = rows_ref[pl.program_id(0)]

    @pl.when(rows > 0)
    def _():
        wgb[...] = wg_ref[0].astype(BF16)
        wub[...] = wu_ref[0].astype(BF16)

    def block(rs):
        xsb = xs_ref[rs, :]
        h_ref[rs, :] = (jax.nn.silu(_dot(xsb, wgb[...])) * _dot(xsb, wub[...])).astype(BF16)

    def empty(rs):
        h_ref[rs, :] = jnp.zeros((rs.stop - rs.start, h_ref.shape[1]), BF16)

    _for_valid_blocks(rows, tm, block, empty)


def _gdown_body(te_ref, rows_ref, nu_ref, h_ref, wd_ref, o_ref, wdb, *, tm):
    rows = rows_ref[pl.program_id(0)]

    @pl.when(rows > 0)
    def _():
        wdb[...] = wd_ref[0].astype(BF16)

    def block(rs):
        o_ref[rs, :] = _dot(h_ref[rs, :], wdb[...])

    def empty(rs):
        o_ref[rs, :] = jnp.zeros((rs.stop - rs.start, o_ref.shape[1]), F32)

    _for_valid_blocks(rows, tm, block, empty)


def _grouped_ffn(xs, tile_expert, tile_rows, n_used, layer, w_gate, w_up, w_down):
    tm = MOE_TILE
    n_tiles = tile_expert.shape[0]
    first = layer * N_EXPERTS
    n_all = w_gate.shape[0] * N_EXPERTS

    def tile(i, nu):
        return jnp.minimum(i, nu[0] - 1)

    def chunk(i, c, nu, n_chunks):
        return jnp.where(i < nu[0], c, n_chunks - 1)

    tf = 256
    n_f = D_FF_EXPERT // tf
    w_up_spec = pl.BlockSpec((1, D_MODEL, tf),
                             lambda i, f, te, tr, nu: (first + te[i], 0, chunk(i, f, nu, n_f)))
    up_spec = pltpu.PrefetchScalarGridSpec(
        num_scalar_prefetch=3,
        grid=(n_tiles, n_f),
        in_specs=[pl.BlockSpec((tm, D_MODEL), lambda i, f, te, tr, nu: (tile(i, nu), 0)),
                  w_up_spec, w_up_spec],
        out_specs=pl.BlockSpec((tm, tf), lambda i, f, te, tr, nu: (i, f)),
        scratch_shapes=[pltpu.VMEM((D_MODEL, tf), BF16), pltpu.VMEM((D_MODEL, tf), BF16)],
    )
    h = pl.pallas_call(
        functools.partial(_gup_body, tm=tm),
        grid_spec=up_spec,
        out_shape=jax.ShapeDtypeStruct((n_tiles * tm, D_FF_EXPERT), BF16),
        compiler_params=_params(("arbitrary", "arbitrary")),
        name="moe_up",
    )(tile_expert, tile_rows, n_used, xs,
      w_gate.reshape(n_all, D_MODEL, D_FF_EXPERT), w_up.reshape(n_all, D_MODEL, D_FF_EXPERT))

    tn = 512
    n_n = D_MODEL // tn
    down_spec = pltpu.PrefetchScalarGridSpec(
        num_scalar_prefetch=3,
        grid=(n_tiles, n_n),
        in_specs=[pl.BlockSpec((tm, D_FF_EXPERT), lambda i, n, te, tr, nu: (tile(i, nu), 0)),
                  pl.BlockSpec((1, D_FF_EXPERT, tn),
                               lambda i, n, te, tr, nu: (first + te[i], 0, chunk(i, n, nu, n_n)))],
        out_specs=pl.BlockSpec((tm, tn), lambda i, n, te, tr, nu: (i, n)),
        scratch_shapes=[pltpu.VMEM((D_FF_EXPERT, tn), BF16)],
    )
    return pl.pallas_call(
        functools.partial(_gdown_body, tm=tm),
        grid_spec=down_spec,
        out_shape=jax.ShapeDtypeStruct((n_tiles * tm, D_MODEL), F32),
        compiler_params=_params(("arbitrary", "arbitrary")),
        name="moe_down",
    )(tile_expert, tile_rows, n_used, h, w_down.reshape(n_all, D_FF_EXPERT, D_MODEL))


COMBINE_SLAB = 128


def _combine_body(pos_ref, ys_hbm, r_ref, x_ref, gate_ref, lg_ref, lb_ref, o_ref, buf, sem, *, tc):
    n_steps = pl.num_programs(0) * pl.num_programs(1)
    step = pl.program_id(0) * pl.num_programs(1) + pl.program_id(1)

    def issue(s, slot):
        base = s * tc

        def issue_group(g, carry):
            for u in range(GATHER_UNROLL):
                for k in range(2):
                    p = pos_ref[k * N_TOKENS + base + g * GATHER_UNROLL + u]
                    pltpu.make_async_copy(ys_hbm.at[pl.ds(p, 1)], buf.at[slot, k, g, pl.ds(u, 1)],
                                          sem.at[slot, k]).start(priority=k)
            return carry

        lax.fori_loop(0, tc // GATHER_UNROLL, issue_group, 0)

    def issue_rows(s, slot, r_lo, r_hi):
        base = s * tc
        for r in range(r_lo, r_hi):
            for k in range(2):
                p = pos_ref[k * N_TOKENS + base + r]
                pltpu.make_async_copy(ys_hbm.at[pl.ds(p, 1)],
                                      buf.at[slot, k, r // GATHER_UNROLL, pl.ds(r % GATHER_UNROLL, 1)],
                                      sem.at[slot, k]).start(priority=k)

    def wait(slot):
        for k in range(2):
            pltpu.make_async_copy(buf.at[slot, k], buf.at[slot, k], sem.at[slot, k]).wait()

    for slot in range(2):
        @pl.when(step % 2 == slot)
        def _():
            if slot == 0:
                pl.when(step == 0)(functools.partial(issue, step, slot))
            wait(slot)
            nxt = jnp.minimum(step + 1, n_steps - 1)
            routing = r_ref[0]
            for r0 in range(0, tc, COMBINE_SLAB):
                rs = slice(r0, r0 + COMBINE_SLAB)
                gs = slice(r0 // GATHER_UNROLL, (r0 + COMBINE_SLAB) // GATHER_UNROLL)
                issue_rows(nxt, 1 - slot, r0, r0 + COMBINE_SLAB)
                y = (routing[rs, 2:3] * buf[slot, 0, gs].reshape(COMBINE_SLAB, D_MODEL)
                     + routing[rs, 3:4] * buf[slot, 1, gs].reshape(COMBINE_SLAB, D_MODEL))
                o_ref[0, rs] = _deepnorm(x_ref[0, rs], y, gate_ref[0], lg_ref[...], lb_ref[...])
            pl.when(step == n_steps - 1)(functools.partial(wait, 1 - slot))


def _combine(pos, ys, routing, x, mod, k, ln_g, ln_b):
    tc = 512
    rows = pl.BlockSpec((1, tc, D_MODEL), lambda b, i, pos: (b, i, 0))
    vec = pl.BlockSpec((1, D_MODEL), lambda b, i, pos: (0, 0))
    grid_spec = pltpu.PrefetchScalarGridSpec(
        num_scalar_prefetch=1,
        grid=(BATCH, SEQ // tc),
        in_specs=[pl.BlockSpec(memory_space=pl.ANY),
                  pl.BlockSpec((1, tc, LANES), lambda b, i, pos: (b, i, 0)),
                  rows,
                  pl.BlockSpec((1, 1, D_MODEL), lambda b, i, pos: (k * 8 + b, 0, 2)),
                  vec, vec],
        out_specs=rows,
        scratch_shapes=[pltpu.VMEM((2, 2, tc // GATHER_UNROLL, GATHER_UNROLL, D_MODEL), F32),
                        pltpu.SemaphoreType.DMA((2, 2))],
    )
    return pl.pallas_call(
        functools.partial(_combine_body, tc=tc),
        grid_spec=grid_spec,
        out_shape=jax.ShapeDtypeStruct((BATCH, SEQ, D_MODEL), F32),
        compiler_params=_params(("arbitrary", "arbitrary")),
        name="moe_combine",
    )(pos, ys, routing, x, mod, ln_g.reshape(1, -1), ln_b.reshape(1, -1))


def _moe(x, mod, k, w_router, layer, w_gate, w_up, w_down, ln_g, ln_b):
    routing, hm = _router(x, mod, k, w_router)
    onehot, rank, counts = _route_counts(routing)

    def experts_over(n_tiles):
        def run(_):
            pos, src, tile_expert, tile_rows, n_used = _route_plan(onehot, rank, counts, n_tiles)
            xs = _gather_rows(src, tile_rows, hm)
            ys = _grouped_ffn(xs, tile_expert, tile_rows, n_used, layer, w_gate, w_up, w_down)
            return _combine(pos, ys, routing, x, mod, k, ln_g, ln_b)
        return run

    return lax.cond(jnp.all(counts <= MOE_TILE),
                    experts_over(MOE_TILES_FIT), experts_over(MOE_TILES_ANY), None)


def kernel(x, c, positions, ada_w, ada_b, ln_g, ln_b, w_in_ab, q_norm_g, w_q_up, kv_norm_g, w_kv_up, sgu_norm_g, sgu_norm_b, sgu_w, sgu_b, w_out_ab, ffn_w_gate, ffn_w_up, ffn_w_down, pool_w, pool_scale, w_out_c, router_w, moe_w_gate, moe_w_up, moe_w_down):
    mod = _ada_modulation(c, ada_w, ada_b)
    tables = _rope_tables(positions)
    for l in range(DEPTH):
        j = l // 2
        k_tok, k_ch = 2 * l, 2 * l + 1
        if l % 2 == 0:
            q, k, vt, sgu = _front(x, mod, k_tok, tables, w_in_ab[j], q_norm_g[j], w_q_up[j],
                                   kv_norm_g[j], w_kv_up[j],
                                   sgu_norm_g[j], sgu_norm_b[j], sgu_w[j], sgu_b[j])
            attn = _attention(q, k, vt)
            x = _outproj(attn, sgu, x, mod, k_tok, w_out_ab[j], ln_g[l, 0], ln_b[l, 0])
            x = _ffn(x, mod, k_ch, j, ffn_w_gate, ffn_w_up, ffn_w_down[j], ln_g[l, 1], ln_b[l, 1])
        else:
            x = _pool_mixer(x, mod, k_tok, pool_w[j], pool_scale[j], w_out_c[j], ln_g[l, 0], ln_b[l, 0])
            x = _moe(x, mod, k_ch, router_w[j], j, moe_w_gate, moe_w_up, moe_w_down,
                     ln_g[l, 1], ln_b[l, 1])
    return x
```

```python
import functools

import jax
import jax.numpy as jnp
import numpy as np
from jax import lax
from jax.experimental import pallas as pl
from jax.experimental.pallas import tpu as pltpu

F32 = jnp.float32
BF16 = jnp.bfloat16

D_MODEL = 2048
BATCH = 2
SEQ = 4096
DEPTH = 4
MLA_HEADS = 8
QK_NOPE_DIM = 128
QK_ROPE_DIM = 64
V_HEAD_DIM = 128
Q_LORA_RANK = 768
KV_LORA_RANK = 512
ROPE_THETA = 10000.0
SGU_GROUPS = 8
SGU_GROUP_DIM = 128
SGU_WIDTH = SGU_GROUPS * SGU_GROUP_DIM
CHUNK = 128
ATTN_WIDTH = MLA_HEADS * V_HEAD_DIM
POOL_WINDOWS = (2, 4, 8, 16)
POOL_GROUP_DIM = D_MODEL // 4
D_FF = 5632
N_EXPERTS = 8
D_FF_EXPERT = 2816
DEEPNORM_ALPHA = (2 * DEPTH) ** 0.25
LN_EPS = 1e-5
RMS_EPS = 1e-6
SM_SCALE = (QK_NOPE_DIM + QK_ROPE_DIM) ** -0.5
LOG2_E = 1.4426950408889634

LANES = 128
HEAD_LANES = 2 * LANES
QK_WIDTH = MLA_HEADS * HEAD_LANES
VMEM_LIMIT = 60 * 1024 * 1024
POOL_HALO = 128


def _params(semantics):
    return pltpu.CompilerParams(dimension_semantics=semantics, vmem_limit_bytes=VMEM_LIMIT)


def _const_spec(shape):
    nd = len(shape)
    return pl.BlockSpec(shape, lambda *_: (0,) * nd, pipeline_mode=pl.Buffered(1))


def _dot(a, b):
    return jnp.dot(a, b, preferred_element_type=F32)


def _modulate(x, shift, scale):
    return x * (1.0 + scale) + shift


def _deepnorm(x, y, gate, g, b):
    r = DEEPNORM_ALPHA * x + (1.0 + gate) * y
    mu = jnp.mean(r, axis=-1, keepdims=True)
    rc = r - mu
    var = jnp.mean(rc * rc, axis=-1, keepdims=True)
    return rc * lax.rsqrt(var + LN_EPS) * g + b


def _rms(x, g):
    return x * lax.rsqrt(jnp.mean(x * x, axis=-1, keepdims=True) + RMS_EPS) * g


def _rope(r, cos_t, sin_hi, sin_lo):
    return (r * cos_t + pltpu.roll(r, QK_ROPE_DIM // 2, 1) * sin_hi
            + pltpu.roll(r, LANES - QK_ROPE_DIM // 2, 1) * sin_lo)


def _mod_specs(k, n_grid):
    def spec(part):
        if n_grid == 2:
            return pl.BlockSpec((1, 1, D_MODEL), lambda b, i: (k * 8 + b, 0, part))
        if n_grid == 3:
            return pl.BlockSpec((1, 1, D_MODEL), lambda b, i, f: (k * 8 + b, 0, part))
        return pl.BlockSpec((1, 1, D_MODEL), lambda b, i, e, f: (k * 8 + b, 0, part))
    return spec(0), spec(1), spec(2)


def _ada_body(c_ref, w_ref, b_ref, o_ref):
    s = jax.nn.silu(c_ref[...]).astype(BF16)
    o_ref[0] = _dot(s, w_ref[0].astype(BF16)) + b_ref[0]


def _ada_modulation(c, ada_w, ada_b):
    n_mod = 2 * DEPTH
    tn = 1024
    c_pad = jnp.pad(c, ((0, 8 - BATCH), (0, 0)))
    out = pl.pallas_call(
        _ada_body,
        grid=(n_mod, 3 * D_MODEL // tn),
        in_specs=[
            pl.BlockSpec((8, D_MODEL), lambda k, n: (0, 0)),
            pl.BlockSpec((1, D_MODEL, tn), lambda k, n: (k, 0, n)),
            pl.BlockSpec((1, 1, tn), lambda k, n: (k, 0, n)),
        ],
        out_specs=pl.BlockSpec((1, 8, tn), lambda k, n: (k, 0, n)),
        out_shape=jax.ShapeDtypeStruct((n_mod, 8, 3 * D_MODEL), F32),
        compiler_params=_params(("parallel", "parallel")),
        name="ada_mod",
    )(c_pad, ada_w.reshape(n_mod, D_MODEL, 3 * D_MODEL), ada_b.reshape(n_mod, 1, 3 * D_MODEL))
    return out.reshape(n_mod * 8, 1, 3 * D_MODEL)


def _rope_table_body(pos_ref, inv_ref, cos_ref, hi_ref, lo_ref):
    ang = pos_ref[0].astype(F32) * inv_ref[...]
    lane = lax.broadcasted_iota(jnp.int32, ang.shape, 1)
    cos = jnp.cos(ang)
    sin = jnp.sin(ang)
    half = QK_ROPE_DIM // 2
    cos_ref[0] = jnp.where(lane < QK_ROPE_DIM, cos, 0.0)
    hi_ref[0] = jnp.where(lane < half, 0.0, jnp.where(lane < QK_ROPE_DIM, sin, 0.0))
    lo_ref[0] = jnp.where(lane < half, -sin, 0.0)


def _rope_tables(positions):
    tm = 512
    half = QK_ROPE_DIM // 2
    inv = ROPE_THETA ** (-jnp.arange(0, QK_ROPE_DIM, 2, dtype=F32) / QK_ROPE_DIM)
    inv_lanes = jnp.concatenate([inv, inv, jnp.zeros((LANES - 2 * half,), F32)]).reshape(1, LANES)
    tab = jax.ShapeDtypeStruct((BATCH, SEQ, LANES), F32)
    spec = pl.BlockSpec((1, tm, LANES), lambda b, i: (b, i, 0))
    return pl.pallas_call(
        _rope_table_body,
        grid=(BATCH, SEQ // tm),
        in_specs=[pl.BlockSpec((1, tm, 1), lambda b, i: (b, i, 0)),
                  pl.BlockSpec((1, LANES), lambda b, i: (0, 0))],
        out_specs=(spec, spec, spec),
        out_shape=(tab, tab, tab),
        compiler_params=_params(("parallel", "parallel")),
        name="rope_tables",
    )(positions.reshape(BATCH, SEQ, 1), inv_lanes)


def _front_body(x_ref, sh_ref, sc_ref, wq_ref, wkv_ref, wkr_ref, wu_ref, wv_ref,
                cos_ref, hi_ref, lo_ref, qg_ref, kvg_ref, wqn_ref, wqr_ref, wkn_ref, wvt_ref,
                sg_ref, sb_ref, sw_ref, sbias_ref,
                q_ref, k_ref, vt_ref, sgu_ref, *, tm):
    hm = _modulate(x_ref[0], sh_ref[0], sc_ref[0]).astype(BF16)
    _qkv_compute(_dot(hm, wq_ref[...]), _dot(hm, wkv_ref[...]), _dot(hm, wkr_ref[...]),
                 cos_ref[0], hi_ref[0], lo_ref[0], qg_ref, kvg_ref,
                 wqn_ref, wqr_ref, wkn_ref, wvt_ref, q_ref, k_ref, vt_ref)
    _sgu_compute(_dot(hm, wu_ref[...]), _dot(hm, wv_ref[...]),
                 sg_ref, sb_ref, sw_ref, sbias_ref, sgu_ref, tm)


def _front(x, mod, k, tables, w_in, q_norm_g, w_q_up, kv_norm_g, w_kv_up,
           sgu_norm_g, sgu_norm_b, sgu_w, sgu_b):
    tm = 512
    a, b, c_, d = (Q_LORA_RANK, Q_LORA_RANK + KV_LORA_RANK,
                   Q_LORA_RANK + KV_LORA_RANK + QK_ROPE_DIM,
                   Q_LORA_RANK + KV_LORA_RANK + QK_ROPE_DIM + SGU_WIDTH)
    wq = w_in[:, :a].astype(BF16)
    wkv = w_in[:, a:b].astype(BF16)
    wkr = jnp.pad(w_in[:, b:c_], ((0, 0), (0, LANES - QK_ROPE_DIM))).astype(BF16)
    wu = w_in[:, c_:d].astype(BF16)
    wv = w_in[:, d:].astype(BF16)
    in_widths = (Q_LORA_RANK, KV_LORA_RANK, LANES, SGU_WIDTH, SGU_WIDTH)

    wq_up = w_q_up.reshape(Q_LORA_RANK, MLA_HEADS, QK_NOPE_DIM + QK_ROPE_DIM)
    wqn = wq_up[:, :, :QK_NOPE_DIM].reshape(Q_LORA_RANK, MLA_HEADS * LANES).astype(BF16)
    wqr = jnp.pad(wq_up[:, :, QK_NOPE_DIM:], ((0, 0), (0, 0), (0, LANES - QK_ROPE_DIM)))
    wqr = wqr.reshape(Q_LORA_RANK, MLA_HEADS * LANES).astype(BF16)
    wkv_up = w_kv_up.reshape(KV_LORA_RANK, MLA_HEADS, QK_NOPE_DIM + V_HEAD_DIM)
    wkn = wkv_up[:, :, :QK_NOPE_DIM].reshape(KV_LORA_RANK, MLA_HEADS * LANES).astype(BF16)
    wvt = wkv_up[:, :, QK_NOPE_DIM:].reshape(KV_LORA_RANK, ATTN_WIDTH).T.astype(BF16)
    sgu_bias = jnp.repeat(sgu_b.T, SGU_GROUP_DIM, axis=1)

    shift, scale, _ = _mod_specs(k, 2)

    def rows(w):
        return pl.BlockSpec((1, tm, w), lambda b, i: (b, i, 0))

    return pl.pallas_call(
        functools.partial(_front_body, tm=tm),
        grid=(BATCH, SEQ // tm),
        in_specs=[rows(D_MODEL), shift, scale]
                 + [_const_spec((D_MODEL, w)) for w in in_widths]
                 + [rows(LANES), rows(LANES), rows(LANES),
                    _const_spec((1, Q_LORA_RANK)), _const_spec((1, KV_LORA_RANK)),
                    _const_spec((Q_LORA_RANK, MLA_HEADS * LANES)),
                    _const_spec((Q_LORA_RANK, MLA_HEADS * LANES)),
                    _const_spec((KV_LORA_RANK, MLA_HEADS * LANES)),
                    _const_spec((ATTN_WIDTH, KV_LORA_RANK)),
                    _const_spec((1, SGU_WIDTH)), _const_spec((1, SGU_WIDTH)),
                    _const_spec((SGU_GROUPS, CHUNK, CHUNK)), _const_spec((CHUNK, SGU_WIDTH))],
        out_specs=(rows(QK_WIDTH), rows(QK_WIDTH),
                   pl.BlockSpec((1, ATTN_WIDTH, tm), lambda b, i: (b, 0, i)),
                   rows(SGU_WIDTH)),
        out_shape=(jax.ShapeDtypeStruct((BATCH, SEQ, QK_WIDTH), BF16),
                   jax.ShapeDtypeStruct((BATCH, SEQ, QK_WIDTH), BF16),
                   jax.ShapeDtypeStruct((BATCH, ATTN_WIDTH, SEQ), BF16),
                   jax.ShapeDtypeStruct((BATCH, SEQ, SGU_WIDTH), BF16)),
        compiler_params=_params(("parallel", "parallel")),
        name="front_ab",
    )(x, mod, mod, wq, wkv, wkr, wu, wv, *tables,
      q_norm_g.reshape(1, -1), kv_norm_g.reshape(1, -1), wqn, wqr, wkn, wvt,
      sgu_norm_g.reshape(1, -1), sgu_norm_b.reshape(1, -1), sgu_w, sgu_bias)


def _qkv_compute(cq, ckv, kr, cos_t, sin_hi, sin_lo, qg_ref, kvg_ref,
                 wqn_ref, wqr_ref, wkn_ref, wvt_ref, q_ref, k_ref, vt_ref):
    cqn = _rms(cq, qg_ref[...]).astype(BF16)
    qn = _dot(cqn, wqn_ref[...])
    qr = _dot(cqn, wqr_ref[...])
    ckvn = _rms(ckv, kvg_ref[...]).astype(BF16)
    kn = _dot(ckvn, wkn_ref[...])
    vt_ref[0] = lax.dot_general(wvt_ref[...], ckvn, (((1,), (1,)), ((), ())),
                                preferred_element_type=F32).astype(BF16)
    k_rope = _rope(kr, cos_t, sin_hi, sin_lo).astype(BF16)
    for h in range(MLA_HEADS):
        nope = slice(h * LANES, (h + 1) * LANES)
        lo = h * HEAD_LANES
        q_ref[0, :, lo:lo + LANES] = qn[:, nope].astype(BF16)
        q_ref[0, :, lo + LANES:lo + HEAD_LANES] = _rope(qr[:, nope], cos_t, sin_hi, sin_lo).astype(BF16)
        k_ref[0, :, lo:lo + LANES] = kn[:, nope].astype(BF16)
        k_ref[0, :, lo + LANES:lo + HEAD_LANES] = k_rope


def _attn_body(qi_ref, kj_ref, q_ref, k_ref, vt_ref, o_ref, m_sc, l_sc, acc_sc, *, tq):
    p = pl.program_id(1)
    i = qi_ref[p]
    j = kj_ref[p]

    @pl.when(j == 0)
    def _():
        m_sc[...] = jnp.full(m_sc.shape, -jnp.inf, F32)
        l_sc[...] = jnp.zeros(l_sc.shape, F32)
        acc_sc[...] = jnp.zeros(acc_sc.shape, F32)

    def accumulate(diagonal):
        if diagonal:
            key = lax.broadcasted_iota(jnp.int32, (tq, tq), 0)
            qry = lax.broadcasted_iota(jnp.int32, (tq, tq), 1)
            visible = key <= qry
        for h in range(MLA_HEADS):
            qk = slice(h * HEAD_LANES, (h + 1) * HEAD_LANES)
            hv = slice(h * V_HEAD_DIM, (h + 1) * V_HEAD_DIM)
            s = lax.dot_general(k_ref[0, :, qk], q_ref[0, :, qk], (((1,), (1,)), ((), ())),
                                preferred_element_type=F32) * (SM_SCALE * LOG2_E)
            if diagonal:
                s = jnp.where(visible, s, -jnp.inf)
            m_prev = m_sc[h]
            m_new = jnp.maximum(m_prev, jnp.max(s, axis=0, keepdims=True))
            alpha = jnp.exp2(m_prev - m_new)
            pr = jnp.exp2(s - m_new)
            l_sc[h] = alpha * l_sc[h] + jnp.sum(pr, axis=0, keepdims=True)
            acc_sc[hv, :] = alpha * acc_sc[hv, :] + _dot(vt_ref[0, hv, :], pr.astype(BF16))
            m_sc[h] = m_new

    @pl.when(j < i)
    def _():
        accumulate(False)

    @pl.when(j == i)
    def _():
        accumulate(True)
        for h in range(MLA_HEADS):
            hv = slice(h * V_HEAD_DIM, (h + 1) * V_HEAD_DIM)
            o_ref[0, :, hv] = (acc_sc[hv, :] / l_sc[h]).T.astype(BF16)


def _attention(q, k, vt):
    tq = 512
    nb = SEQ // tq
    pairs = [(i, j) for i in range(nb) for j in range(i + 1)]
    qi = jnp.asarray(np.array([p[0] for p in pairs], np.int32))
    kj = jnp.asarray(np.array([p[1] for p in pairs], np.int32))
    grid_spec = pltpu.PrefetchScalarGridSpec(
        num_scalar_prefetch=2,
        grid=(BATCH, len(pairs)),
        in_specs=[pl.BlockSpec((1, tq, QK_WIDTH), lambda b, p, qi, kj: (b, qi[p], 0)),
                  pl.BlockSpec((1, tq, QK_WIDTH), lambda b, p, qi, kj: (b, kj[p], 0)),
                  pl.BlockSpec((1, ATTN_WIDTH, tq), lambda b, p, qi, kj: (b, 0, kj[p]))],
        out_specs=pl.BlockSpec((1, tq, ATTN_WIDTH), lambda b, p, qi, kj: (b, qi[p], 0)),
        scratch_shapes=[pltpu.VMEM((MLA_HEADS, 1, tq), F32),
                        pltpu.VMEM((MLA_HEADS, 1, tq), F32),
                        pltpu.VMEM((ATTN_WIDTH, tq), F32)],
    )
    return pl.pallas_call(
        functools.partial(_attn_body, tq=tq),
        grid_spec=grid_spec,
        out_shape=jax.ShapeDtypeStruct((BATCH, SEQ, ATTN_WIDTH), BF16),
        compiler_params=_params(("parallel", "arbitrary")),
        name="mla_attention",
    )(qi, kj, q, k, vt)


def _sgu_compute(zu, zv, g_ref, b_ref, w_ref, bias_ref, o_ref, tm):
    row = lax.broadcasted_iota(jnp.int32, (CHUNK, CHUNK), 0)
    col = lax.broadcasted_iota(jnp.int32, (CHUNK, CHUNK), 1)
    for g in range(SGU_GROUPS):
        gs = slice(g * SGU_GROUP_DIM, (g + 1) * SGU_GROUP_DIM)
        v = jax.nn.gelu(zv[:, gs])
        mu = jnp.mean(v, axis=-1, keepdims=True)
        vc = v - mu
        var = jnp.mean(vc * vc, axis=-1, keepdims=True)
        vn = (vc * lax.rsqrt(var + LN_EPS) * g_ref[:, gs] + b_ref[:, gs]).astype(BF16)
        w = jnp.where(col <= row, w_ref[g], 0.0).astype(BF16)
        for n in range(tm // CHUNK):
            rs = slice(n * CHUNK, (n + 1) * CHUNK)
            s = _dot(w, vn[rs]) + bias_ref[:, gs]
            o_ref[0, rs, gs] = (jax.nn.gelu(zu[rs, gs]) * s).astype(BF16)


def _outproj_body(a_ref, s_ref, x_ref, gate_ref, wa_ref, ws_ref, lg_ref, lb_ref, o_ref):
    half = a_ref.shape[1] // 2
    for r0 in (0, half):
        rs = slice(r0, r0 + half)
        y = _dot(a_ref[0, rs], wa_ref[...]) + _dot(s_ref[0, rs], ws_ref[...])
        o_ref[0, rs] = _deepnorm(x_ref[0, rs], y, gate_ref[0], lg_ref[...], lb_ref[...])


def _outproj(attn, sgu, x, mod, k, w_out, ln_g, ln_b):
    tm = 512
    wa = w_out[:ATTN_WIDTH].astype(BF16)
    ws = w_out[ATTN_WIDTH:].astype(BF16)
    _, _, gate = _mod_specs(k, 2)

    def rows(w):
        return pl.BlockSpec((1, tm, w), lambda b, i: (b, i, 0))

    return pl.pallas_call(
        _outproj_body,
        grid=(BATCH, SEQ // tm),
        in_specs=[rows(ATTN_WIDTH), rows(SGU_WIDTH), rows(D_MODEL), gate,
                  _const_spec((ATTN_WIDTH, D_MODEL)), _const_spec((SGU_WIDTH, D_MODEL)),
                  _const_spec((1, D_MODEL)), _const_spec((1, D_MODEL))],
        out_specs=rows(D_MODEL),
        out_shape=jax.ShapeDtypeStruct((BATCH, SEQ, D_MODEL), F32),
        compiler_params=_params(("parallel", "parallel")),
        name="outproj_ab",
    )(attn, sgu, x, mod, wa, ws, ln_g.reshape(1, -1), ln_b.reshape(1, -1))


def _ffn_up_body(x_ref, sh_ref, sc_ref, wg_ref, wu_ref, h_ref, hm_sc):
    @pl.when(pl.program_id(2) == 0)
    def _():
        hm_sc[...] = _modulate(x_ref[0], sh_ref[0], sc_ref[0]).astype(BF16)

    hm = hm_sc[...]
    g = _dot(hm, wg_ref[0].astype(BF16))
    u = _dot(hm, wu_ref[0].astype(BF16))
    h_ref[0] = (jax.nn.silu(g) * u).astype(BF16)


def _ffn_down_body(h_ref, x_ref, gate_ref, wd_ref, lg_ref, lb_ref, o_ref, y_sc, *, n_n):
    n = pl.program_id(2)
    y_sc[n] = _dot(h_ref[0], wd_ref[...])

    @pl.when(n == n_n - 1)
    def _():
        y = jnp.concatenate([y_sc[c] for c in range(n_n)], axis=-1)
        o_ref[0] = _deepnorm(x_ref[0], y, gate_ref[0], lg_ref[...], lb_ref[...])


def _ffn(x, mod, k, layer, w_gate, w_up, w_down, ln_g, ln_b):
    tm, tf = 2048, 256
    shift, scale, gate = _mod_specs(k, 3)
    h = pl.pallas_call(
        _ffn_up_body,
        grid=(BATCH, SEQ // tm, D_FF // tf),
        in_specs=[pl.BlockSpec((1, tm, D_MODEL), lambda b, i, f: (b, i, 0), pipeline_mode=pl.Buffered(1)),
                  shift, scale,
                  pl.BlockSpec((1, D_MODEL, tf), lambda b, i, f: (layer, 0, f)),
                  pl.BlockSpec((1, D_MODEL, tf), lambda b, i, f: (layer, 0, f))],
        out_specs=pl.BlockSpec((1, tm, tf), lambda b, i, f: (b, i, f)),
        out_shape=jax.ShapeDtypeStruct((BATCH, SEQ, D_FF), BF16),
        scratch_shapes=[pltpu.VMEM((tm, D_MODEL), BF16)],
        compiler_params=_params(("parallel", "parallel", "arbitrary")),
        name="ffn_up",
    )(x, mod, mod, w_gate, w_up)

    tm, tn = 512, 512
    n_n = D_MODEL // tn
    rows = pl.BlockSpec((1, tm, D_MODEL), lambda b, i, n: (b, i, 0))
    vec = pl.BlockSpec((1, D_MODEL), lambda b, i, n: (0, 0))
    return pl.pallas_call(
        functools.partial(_ffn_down_body, n_n=n_n),
        grid=(BATCH, SEQ // tm, n_n),
        in_specs=[pl.BlockSpec((1, tm, D_FF), lambda b, i, n: (b, i, 0)),
                  rows,
                  gate,
                  pl.BlockSpec((D_FF, tn), lambda b, i, n: (0, n)),
                  vec, vec],
        out_specs=rows,
        out_shape=jax.ShapeDtypeStruct((BATCH, SEQ, D_MODEL), F32),
        scratch_shapes=[pltpu.VMEM((n_n, tm, tn), F32)],
        compiler_params=_params(("parallel", "parallel", "arbitrary")),
        name="ffn_down",
    )(h, x, mod, w_down.astype(BF16), ln_g.reshape(1, -1), ln_b.reshape(1, -1))


def _pool_body(x_ref, halo_ref, sh_ref, sc_ref, gate_ref, pw_ref, ps_ref, wo_ref, lg_ref, lb_ref,
               o_ref, *, tm):
    i = pl.program_id(1)
    x = x_ref[0]
    hm = _modulate(x, sh_ref[0], sc_ref[0])
    halo = jnp.where(i > 0, _modulate(halo_ref[0], sh_ref[0], sc_ref[0]), 0.0)
    ext = jnp.concatenate([halo, hm], axis=0)
    ext_hi = ext.astype(BF16)
    ext_lo = (ext - ext_hi.astype(F32)).astype(BF16)
    t_loc = lax.broadcasted_iota(jnp.int32, (POOL_HALO, 2 * POOL_HALO), 0)
    s_loc = lax.broadcasted_iota(jnp.int32, (POOL_HALO, 2 * POOL_HALO), 1)
    lag = t_loc + POOL_HALO - s_loc
    t_glob = i * tm + lax.broadcasted_iota(jnp.int32, (tm, 1), 0)
    outs = []
    for gi, w in enumerate(POOL_WINDOWS):
        cs = slice(gi * POOL_GROUP_DIM, (gi + 1) * POOL_GROUP_DIM)
        band = jnp.where(lag >= 0, jnp.where(lag < w, 1.0, 0.0), 0.0).astype(BF16)
        wsum = jnp.concatenate(
            [_dot(band, ext_hi[c:c + 2 * POOL_HALO, cs]) + _dot(band, ext_lo[c:c + 2 * POOL_HALO, cs])
             for c in range(0, tm, POOL_HALO)], axis=0)
        count = jnp.minimum(t_glob + 1, w).astype(F32)
        d = (wsum / count - hm[:, cs]).astype(BF16)
        outs.append(_dot(d, pw_ref[gi]))
    pooled = (jnp.concatenate(outs, axis=-1) * ps_ref[...]).astype(BF16)
    y = _dot(pooled, wo_ref[...])
    o_ref[0] = _deepnorm(x, y, gate_ref[0], lg_ref[...], lb_ref[...])


def _pool_mixer(x, mod, k, pool_w, pool_scale, w_out_c, ln_g, ln_b):
    tm = 512
    per = tm // POOL_HALO
    shift, scale, gate = _mod_specs(k, 2)
    rows = pl.BlockSpec((1, tm, D_MODEL), lambda b, i: (b, i, 0))
    halo = pl.BlockSpec((1, POOL_HALO, D_MODEL), lambda b, i: (b, jnp.maximum(i * per - 1, 0), 0))
    return pl.pallas_call(
        functools.partial(_pool_body, tm=tm),
        grid=(BATCH, SEQ // tm),
        in_specs=[rows, halo, shift, scale, gate,
                  _const_spec((len(POOL_WINDOWS), POOL_GROUP_DIM, POOL_GROUP_DIM)),
                  _const_spec((1, D_MODEL)), _const_spec((D_MODEL, D_MODEL)),
                  _const_spec((1, D_MODEL)), _const_spec((1, D_MODEL))],
        out_specs=rows,
        out_shape=jax.ShapeDtypeStruct((BATCH, SEQ, D_MODEL), F32),
        compiler_params=_params(("parallel", "parallel")),
        name="pool_mixer",
    )(x, x, mod, mod, mod, pool_w.astype(BF16), pool_scale.reshape(1, -1),
      w_out_c.astype(BF16), ln_g.reshape(1, -1), ln_b.reshape(1, -1))


def _router_body(x_ref, sh_ref, sc_ref, wr_ref, g_ref, hm_ref):
    hm = _modulate(x_ref[0], sh_ref[0], sc_ref[0])
    hm_ref[0] = hm
    hm_hi = hm.astype(BF16)
    hm_lo = (hm - hm_hi.astype(F32)).astype(BF16)
    w = wr_ref[...]
    w_hi = w.astype(BF16)
    w_lo = (w - w_hi.astype(F32)).astype(BF16)
    logits = _dot(hm_hi, w_hi) + (_dot(hm_lo, w_hi) + _dot(hm_hi, w_lo))
    lane = lax.broadcasted_iota(jnp.int32, logits.shape, 1).astype(F32)
    logits = jnp.where(lane < N_EXPERTS, logits, -jnp.inf)
    v1 = jnp.max(logits, axis=-1, keepdims=True)
    i1 = jnp.min(jnp.where(logits == v1, lane, float(LANES)), axis=-1, keepdims=True)
    rest = jnp.where(lane == i1, -jnp.inf, logits)
    v2 = jnp.max(rest, axis=-1, keepdims=True)
    i2 = jnp.min(jnp.where(rest == v2, lane, float(LANES)), axis=-1, keepdims=True)
    e2 = jnp.exp(v2 - v1)
    den = 1.0 + e2
    g_ref[0] = (jnp.where(lane == 0.0, i1, 0.0) + jnp.where(lane == 1.0, i2, 0.0)
                + jnp.where(lane == 2.0, 1.0 / den, 0.0) + jnp.where(lane == 3.0, e2 / den, 0.0))


def _router(x, mod, k, w_router):
    tm = 512
    shift, scale, _ = _mod_specs(k, 2)
    wr = jnp.pad(w_router, ((0, 0), (0, LANES - N_EXPERTS)))
    rows = pl.BlockSpec((1, tm, D_MODEL), lambda b, i: (b, i, 0))
    return pl.pallas_call(
        _router_body,
        grid=(BATCH, SEQ // tm),
        in_specs=[rows, shift, scale, _const_spec((D_MODEL, LANES))],
        out_specs=(pl.BlockSpec((1, tm, LANES), lambda b, i: (b, i, 0)), rows),
        out_shape=(jax.ShapeDtypeStruct((BATCH, SEQ, LANES), F32),
                   jax.ShapeDtypeStruct((BATCH, SEQ, D_MODEL), F32)),
        compiler_params=_params(("parallel", "parallel")),
        name="router",
    )(x, mod, mod, wr)


N_TOKENS = BATCH * SEQ
N_ASSIGN = 2 * N_TOKENS
MOE_BLOCK = 1024
MOE_SUB = 256
MOE_TILE = N_ASSIGN // N_EXPERTS * 5 // 4
MOE_TILES_ANY = -(-(N_ASSIGN + N_EXPERTS * (MOE_TILE - 1)) // MOE_TILE)
MOE_TILES_FIT = N_EXPERTS
GATHER_ROWS = MOE_TILE // 4
assert MOE_TILE % GATHER_ROWS == 0 and GATHER_ROWS % 8 == 0


def _route_counts(routing):
    e_flat = routing.reshape(N_TOKENS, LANES)[:, :2].astype(jnp.int32).T.reshape(N_ASSIGN)
    onehot = (e_flat[:, None] == jnp.arange(N_EXPERTS, dtype=jnp.int32)[None, :]).astype(jnp.int32)
    csum = jnp.cumsum(onehot, axis=0)
    return onehot, jnp.sum(csum * onehot, axis=1) - 1, csum[-1]


def _route_plan(onehot, rank, counts, n_tiles):
    padded = (counts + MOE_TILE - 1) // MOE_TILE * MOE_TILE
    group_end = jnp.cumsum(padded)
    group_start = group_end - padded
    pos = (jnp.sum(onehot * group_start[None, :], axis=1) + rank).astype(jnp.int32)
    tile_start = jnp.arange(n_tiles, dtype=jnp.int32) * MOE_TILE
    n_used = group_end[-1] // MOE_TILE
    tile_expert = jnp.sum((tile_start[:, None] >= group_end[None, :]).astype(jnp.int32), axis=1)
    last_expert = jnp.max(jnp.where(counts > 0, jnp.arange(N_EXPERTS, dtype=jnp.int32), 0))
    tile_expert = jnp.minimum(tile_expert, last_expert)
    tile_rows = jnp.clip(counts[tile_expert] - (tile_start - group_start[tile_expert]), 0, MOE_TILE)
    tile_rows = jnp.where(jnp.arange(n_tiles) < n_used, tile_rows, 0).astype(jnp.int32)
    return pos, tile_expert.astype(jnp.int32), tile_rows, n_used.reshape(1).astype(jnp.int32)


GATHER_UNROLL = 8


def _gather_body(pos_ref, rows_ref, hm_hbm, o_ref, src_ref, buf, sem, *, tg, n_steps):
    i = pl.program_id(0)
    per = MOE_TILE // tg

    @pl.when(i == 0)
    def _():
        def invert(g, carry):
            for u in range(GATHER_UNROLL):
                a = g * GATHER_UNROLL + u
                src_ref[pos_ref[a]] = jnp.where(a < N_TOKENS, a, a - N_TOKENS)
            return carry

        lax.fori_loop(0, N_ASSIGN // GATHER_UNROLL, invert, 0)

    def valid_rows(step):
        n_valid = jnp.clip(rows_ref[step // per] - (step % per) * tg, 0, tg)
        return n_valid, n_valid // GATHER_UNROLL

    def issue(step, slot):
        base = step * tg
        n_valid, n_groups = valid_rows(step)

        @pl.when(n_valid < tg)
        def _():
            buf[slot] = jnp.zeros(buf.shape[1:], F32)

        def issue_group(g, carry):
            for u in range(GATHER_UNROLL):
                t = src_ref[base + g * GATHER_UNROLL + u]
                pltpu.make_async_copy(hm_hbm.at[pl.ds(t, 1)], buf.at[slot, g, pl.ds(u, 1)],
                                      sem.at[slot]).start(priority=u % 2)
            return carry

        def issue_one(r, carry):
            t = src_ref[base + r]
            pltpu.make_async_copy(hm_hbm.at[pl.ds(t, 1)],
                                  buf.at[slot, r // GATHER_UNROLL, pl.ds(r % GATHER_UNROLL, 1)],
                                  sem.at[slot]).start()
            return carry

        lax.fori_loop(0, n_groups, issue_group, 0)
        lax.fori_loop(n_groups * GATHER_UNROLL, n_valid, issue_one, 0)

    def wait(step, slot):
        n_valid, n_groups = valid_rows(step)

        @pl.when(n_groups > 0)
        def _():
            whole = buf.at[slot, pl.ds(0, n_groups)]
            pltpu.make_async_copy(whole, whole, sem.at[slot]).wait()

        def wait_one(r, carry):
            row = buf.at[slot, 0, pl.ds(0, 1)]
            pltpu.make_async_copy(row, row, sem.at[slot]).wait()
            return carry

        lax.fori_loop(n_groups * GATHER_UNROLL, n_valid, wait_one, 0)

    for slot in range(2):
        @pl.when(i % 2 == slot)
        def _():
            if slot == 0:
                pl.when(i == 0)(functools.partial(issue, i, slot))
            pl.when(i + 1 < n_steps)(functools.partial(issue, i + 1, 1 - slot))
            wait(i, slot)
            o_ref[...] = buf[slot].reshape(tg, D_MODEL).astype(BF16)


def _gather_rows(pos, tile_rows, hm):
    n_rows = tile_rows.shape[0] * MOE_TILE
    tg = GATHER_ROWS
    grid_spec = pltpu.PrefetchScalarGridSpec(
        num_scalar_prefetch=2,
        grid=(n_rows // tg,),
        in_specs=[pl.BlockSpec(memory_space=pl.ANY)],
        out_specs=pl.BlockSpec((tg, D_MODEL), lambda i, pos, tr: (i, 0)),
        scratch_shapes=[pltpu.SMEM((n_rows,), jnp.int32),
                        pltpu.VMEM((2, tg // GATHER_UNROLL, GATHER_UNROLL, D_MODEL), F32),
                        pltpu.SemaphoreType.DMA((2,))],
    )
    return pl.pallas_call(
        functools.partial(_gather_body, tg=tg, n_steps=n_rows // tg),
        grid_spec=grid_spec,
        out_shape=jax.ShapeDtypeStruct((n_rows, D_MODEL), BF16),
        compiler_params=_params(("arbitrary",)),
        name="moe_gather",
    )(pos, tile_rows, hm.reshape(N_TOKENS, D_MODEL))


def _for_valid_blocks(rows, tm, block_fn, empty_fn):
    for b0 in range(0, tm, MOE_BLOCK):
        b1 = min(b0 + MOE_BLOCK, tm)
        whole = b1 if b1 - b0 == MOE_BLOCK else tm + 1
        if whole <= tm:
            pl.when(rows >= whole)(functools.partial(block_fn, slice(b0, b1)))

        for r0 in range(b0, b1, MOE_SUB):
            rs = slice(r0, r0 + MOE_SUB)

            @pl.when(jnp.logical_and(rows < whole, r0 < rows))
            def _():
                block_fn(rs)

            @pl.when(r0 >= rows)
            def _():
                empty_fn(rs)


def _gup_body(te_ref, rows_ref, nu_ref, xs_ref, wg_ref, wu_ref, h_ref, wgb, wub, *, tm):
    rows = rows_ref[pl.program_id(0)]

    @pl.when(rows > 0)
    def _():
        wgb[...] = wg_ref[0].astype(BF16)
        wub[...] = wu_ref[0].astype(BF16)

    def block(rs):
        xsb = xs_ref[rs, :]
        h_ref[rs, :] = (jax.nn.silu(_dot(xsb, wgb[...])) * _dot(xsb, wub[...])).astype(BF16)

    def empty(rs):
        h_ref[rs, :] = jnp.zeros((rs.stop - rs.start, h_ref.shape[1]), BF16)

    _for_valid_blocks(rows, tm, block, empty)


def _gdown_body(te_ref, rows_ref, nu_ref, h_ref, wd_ref, o_ref, wdb, *, tm):
    rows = rows_ref[pl.program_id(0)]

    @pl.when(rows > 0)
    def _():
        wdb[...] = wd_ref[0].astype(BF16)

    def block(rs):
        o_ref[rs, :] = _dot(h_ref[rs, :], wdb[...])

    def empty(rs):
        o_ref[rs, :] = jnp.zeros((rs.stop - rs.start, o_ref.shape[1]), F32)

    _for_valid_blocks(rows, tm, block, empty)


def _grouped_ffn(xs, tile_expert, tile_rows, n_used, layer, w_gate, w_up, w_down):
    tm = MOE_TILE
    n_tiles = tile_expert.shape[0]
    first = layer * N_EXPERTS
    n_all = w_gate.shape[0] * N_EXPERTS

    def tile(i, nu):
        return jnp.minimum(i, nu[0] - 1)

    def chunk(i, c, nu, n_chunks):
        return jnp.where(i < nu[0], c, n_chunks - 1)

    tf = 256
    n_f = D_FF_EXPERT // tf
    w_up_spec = pl.BlockSpec((1, D_MODEL, tf),
                             lambda i, f, te, tr, nu: (first + te[i], 0, chunk(i, f, nu, n_f)))
    up_spec = pltpu.PrefetchScalarGridSpec(
        num_scalar_prefetch=3,
        grid=(n_tiles, n_f),
        in_specs=[pl.BlockSpec((tm, D_MODEL), lambda i, f, te, tr, nu: (tile(i, nu), 0)),
                  w_up_spec, w_up_spec],
        out_specs=pl.BlockSpec((tm, tf), lambda i, f, te, tr, nu: (i, f)),
        scratch_shapes=[pltpu.VMEM((D_MODEL, tf), BF16), pltpu.VMEM((D_MODEL, tf), BF16)],
    )
    h = pl.pallas_call(
        functools.partial(_gup_body, tm=tm),
        grid_spec=up_spec,
        out_shape=jax.ShapeDtypeStruct((n_tiles * tm, D_FF_EXPERT), BF16),
        compiler_params=_params(("arbitrary", "arbitrary")),
        name="moe_up",
    )(tile_expert, tile_rows, n_used, xs,
      w_gate.reshape(n_all, D_MODEL, D_FF_EXPERT), w_up.reshape(n_all, D_MODEL, D_FF_EXPERT))

    tn = 512
    n_n = D_MODEL // tn
    down_spec = pltpu.PrefetchScalarGridSpec(
        num_scalar_prefetch=3,
        grid=(n_tiles, n_n),
        in_specs=[pl.BlockSpec((tm, D_FF_EXPERT), lambda i, n, te, tr, nu: (tile(i, nu), 0)),
                  pl.BlockSpec((1, D_FF_EXPERT, tn),
                               lambda i, n, te, tr, nu: (first + te[i], 0, chunk(i, n, nu, n_n)))],
        out_specs=pl.BlockSpec((tm, tn), lambda i, n, te, tr, nu: (i, n)),
        scratch_shapes=[pltpu.VMEM((D_FF_EXPERT, tn), BF16)],
    )
    return pl.pallas_call(
        functools.partial(_gdown_body, tm=tm),
        grid_spec=down_spec,
        out_shape=jax.ShapeDtypeStruct((n_tiles * tm, D_MODEL), F32),
        compiler_params=_params(("arbitrary", "arbitrary")),
        name="moe_down",
    )(tile_expert, tile_rows, n_used, h, w_down.reshape(n_all, D_FF_EXPERT, D_MODEL))


COMBINE_SLAB = 128


def _combine_body(pos_ref, ys_hbm, r_ref, x_ref, gate_ref, lg_ref, lb_ref, o_ref, buf, sem, *, tc):
    n_steps = pl.num_programs(0) * pl.num_programs(1)
    step = pl.program_id(0) * pl.num_programs(1) + pl.program_id(1)

    def issue(s, slot):
        base = s * tc

        def issue_group(g, carry):
            for u in range(GATHER_UNROLL):
                for k in range(2):
                    p = pos_ref[k * N_TOKENS + base + g * GATHER_UNROLL + u]
                    pltpu.make_async_copy(ys_hbm.at[pl.ds(p, 1)], buf.at[slot, k, g, pl.ds(u, 1)],
                                          sem.at[slot, k]).start(priority=k)
            return carry

        lax.fori_loop(0, tc // GATHER_UNROLL, issue_group, 0)

    def issue_rows(s, slot, r_lo, r_hi):
        base = s * tc
        for r in range(r_lo, r_hi):
            for k in range(2):
                p = pos_ref[k * N_TOKENS + base + r]
                pltpu.make_async_copy(ys_hbm.at[pl.ds(p, 1)],
                                      buf.at[slot, k, r // GATHER_UNROLL, pl.ds(r % GATHER_UNROLL, 1)],
                                      sem.at[slot, k]).start(priority=k)

    def wait(slot):
        for k in range(2):
            pltpu.make_async_copy(buf.at[slot, k], buf.at[slot, k], sem.at[slot, k]).wait()

    for slot in range(2):
        @pl.when(step % 2 == slot)
        def _():
            if slot == 0:
                pl.when(step == 0)(functools.partial(issue, step, slot))
            wait(slot)
            nxt = jnp.minimum(step + 1, n_steps - 1)
            routing = r_ref[0]
            for r0 in range(0, tc, COMBINE_SLAB):
                rs = slice(r0, r0 + COMBINE_SLAB)
                gs = slice(r0 // GATHER_UNROLL, (r0 + COMBINE_SLAB) // GATHER_UNROLL)
                issue_rows(nxt, 1 - slot, r0, r0 + COMBINE_SLAB)
                y = (routing[rs, 2:3] * buf[slot, 0, gs].reshape(COMBINE_SLAB, D_MODEL)
                     + routing[rs, 3:4] * buf[slot, 1, gs].reshape(COMBINE_SLAB, D_MODEL))
                o_ref[0, rs] = _deepnorm(x_ref[0, rs], y, gate_ref[0], lg_ref[...], lb_ref[...])
            pl.when(step == n_steps - 1)(functools.partial(wait, 1 - slot))


def _combine(pos, ys, routing, x, mod, k, ln_g, ln_b):
    tc = 512
    rows = pl.BlockSpec((1, tc, D_MODEL), lambda b, i, pos: (b, i, 0))
    vec = pl.BlockSpec((1, D_MODEL), lambda b, i, pos: (0, 0))
    grid_spec = pltpu.PrefetchScalarGridSpec(
        num_scalar_prefetch=1,
        grid=(BATCH, SEQ // tc),
        in_specs=[pl.BlockSpec(memory_space=pl.ANY),
                  pl.BlockSpec((1, tc, LANES), lambda b, i, pos: (b, i, 0)),
                  rows,
                  pl.BlockSpec((1, 1, D_MODEL), lambda b, i, pos: (k * 8 + b, 0, 2)),
                  vec, vec],
        out_specs=rows,
        scratch_shapes=[pltpu.VMEM((2, 2, tc // GATHER_UNROLL, GATHER_UNROLL, D_MODEL), F32),
                        pltpu.SemaphoreType.DMA((2, 2))],
    )
    return pl.pallas_call(
        functools.partial(_combine_body, tc=tc),
        grid_spec=grid_spec,
        out_shape=jax.ShapeDtypeStruct((BATCH, SEQ, D_MODEL), F32),
        compiler_params=_params(("arbitrary", "arbitrary")),
        name="moe_combine",
    )(pos, ys, routing, x, mod, ln_g.reshape(1, -1), ln_b.reshape(1, -1))


def _moe(x, mod, k, w_router, layer, w_gate, w_up, w_down, ln_g, ln_b):
    routing, hm = _router(x, mod, k, w_router)
    onehot, rank, counts = _route_counts(routing)

    def experts_over(n_tiles):
        def run(_):
            pos, tile_expert, tile_rows, n_used = _route_plan(onehot, rank, counts, n_tiles)
            xs = _gather_rows(pos, tile_rows, hm)
            ys = _grouped_ffn(xs, tile_expert, tile_rows, n_used, layer, w_gate, w_up, w_down)
            return _combine(pos, ys, routing, x, mod, k, ln_g, ln_b)
        return run

    return lax.cond(jnp.all(counts <= MOE_TILE),
                    experts_over(MOE_TILES_FIT), experts_over(MOE_TILES_ANY), None)


def kernel(x, c, positions, ada_w, ada_b, ln_g, ln_b, w_in_ab, q_norm_g, w_q_up, kv_norm_g, w_kv_up, sgu_norm_g, sgu_norm_b, sgu_w, sgu_b, w_out_ab, ffn_w_gate, ffn_w_up, ffn_w_down, pool_w, pool_scale, w_out_c, router_w, moe_w_gate, moe_w_up, moe_w_down):
    mod = _ada_modulation(c, ada_w, ada_b)
    tables = _rope_tables(positions)
    for l in range(DEPTH):
        j = l // 2
        k_tok, k_ch = 2 * l, 2 * l + 1
        if l % 2 == 0:
            q, k, vt, sgu = _front(x, mod, k_tok, tables, w_in_ab[j], q_norm_g[j], w_q_up[j],
                                   kv_norm_g[j], w_kv_up[j],
                                   sgu_norm_g[j], sgu_norm_b[j], sgu_w[j], sgu_b[j])
            attn = _attention(q, k, vt)
            x = _outproj(attn, sgu, x, mod, k_tok, w_out_ab[j], ln_g[l, 0], ln_b[l, 0])
            x = _ffn(x, mod, k_ch, j, ffn_w_gate, ffn_w_up, ffn_w_down[j], ln_g[l, 1], ln_b[l, 1])
        else:
            x = _pool_mixer(x, mod, k_tok, pool_w[j], pool_scale[j], w_out_c[j], ln_g[l, 0], ln_b[l, 0])
            x = _moe(x, mod, k_ch, router_w[j], j, moe_w_gate, moe_w_up, moe_w_down,
                     ln_g[l, 1], ln_b[l, 1])
    return x
```

```python
import functools

import jax
import jax.numpy as jnp
import numpy as np
from jax import lax
from jax.experimental import pallas as pl
from jax.experimental.pallas import tpu as pltpu

F32 = jnp.float32
BF16 = jnp.bfloat16

D_MODEL = 2048
BATCH = 2
SEQ = 4096
DEPTH = 4
MLA_HEADS = 8
QK_NOPE_DIM = 128
QK_ROPE_DIM = 64
V_HEAD_DIM = 128
Q_LORA_RANK = 768
KV_LORA_RANK = 512
ROPE_THETA = 10000.0
SGU_GROUPS = 8
SGU_GROUP_DIM = 128
SGU_WIDTH = SGU_GROUPS * SGU_GROUP_DIM
CHUNK = 128
ATTN_WIDTH = MLA_HEADS * V_HEAD_DIM
POOL_WINDOWS = (2, 4, 8, 16)
POOL_GROUP_DIM = D_MODEL // 4
D_FF = 5632
N_EXPERTS = 8
D_FF_EXPERT = 2816
DEEPNORM_ALPHA = (2 * DEPTH) ** 0.25
LN_EPS = 1e-5
RMS_EPS = 1e-6
SM_SCALE = (QK_NOPE_DIM + QK_ROPE_DIM) ** -0.5
LOG2_E = 1.4426950408889634

LANES = 128
HEAD_LANES = 2 * LANES
QK_WIDTH = MLA_HEADS * HEAD_LANES
VMEM_LIMIT = 60 * 1024 * 1024
POOL_HALO = 128


def _params(semantics):
    return pltpu.CompilerParams(dimension_semantics=semantics, vmem_limit_bytes=VMEM_LIMIT)


def _const_spec(shape):
    nd = len(shape)
    return pl.BlockSpec(shape, lambda *_: (0,) * nd, pipeline_mode=pl.Buffered(1))


def _dot(a, b):
    return jnp.dot(a, b, preferred_element_type=F32)


def _modulate(x, shift, scale):
    return x * (1.0 + scale) + shift


def _deepnorm(x, y, gate, g, b):
    r = DEEPNORM_ALPHA * x + (1.0 + gate) * y
    mu = jnp.mean(r, axis=-1, keepdims=True)
    rc = r - mu
    var = jnp.mean(rc * rc, axis=-1, keepdims=True)
    return rc * lax.rsqrt(var + LN_EPS) * g + b


def _rms(x, g):
    return x * lax.rsqrt(jnp.mean(x * x, axis=-1, keepdims=True) + RMS_EPS) * g


def _rope(r, cos_t, sin_hi, sin_lo):
    return (r * cos_t + pltpu.roll(r, QK_ROPE_DIM // 2, 1) * sin_hi
            + pltpu.roll(r, LANES - QK_ROPE_DIM // 2, 1) * sin_lo)


def _mod_specs(k, n_grid):
    def spec(part):
        if n_grid == 2:
            return pl.BlockSpec((1, 1, D_MODEL), lambda b, i: (k * 8 + b, 0, part))
        if n_grid == 3:
            return pl.BlockSpec((1, 1, D_MODEL), lambda b, i, f: (k * 8 + b, 0, part))
        return pl.BlockSpec((1, 1, D_MODEL), lambda b, i, e, f: (k * 8 + b, 0, part))
    return spec(0), spec(1), spec(2)


def _ada_body(c_ref, w_ref, b_ref, o_ref):
    s = jax.nn.silu(c_ref[...]).astype(BF16)
    o_ref[0] = _dot(s, w_ref[0].astype(BF16)) + b_ref[0]


def _ada_modulation(c, ada_w, ada_b):
    n_mod = 2 * DEPTH
    tn = 1024
    c_pad = jnp.pad(c, ((0, 8 - BATCH), (0, 0)))
    out = pl.pallas_call(
        _ada_body,
        grid=(n_mod, 3 * D_MODEL // tn),
        in_specs=[
            pl.BlockSpec((8, D_MODEL), lambda k, n: (0, 0)),
            pl.BlockSpec((1, D_MODEL, tn), lambda k, n: (k, 0, n)),
            pl.BlockSpec((1, 1, tn), lambda k, n: (k, 0, n)),
        ],
        out_specs=pl.BlockSpec((1, 8, tn), lambda k, n: (k, 0, n)),
        out_shape=jax.ShapeDtypeStruct((n_mod, 8, 3 * D_MODEL), F32),
        compiler_params=_params(("parallel", "parallel")),
        name="ada_mod",
    )(c_pad, ada_w.reshape(n_mod, D_MODEL, 3 * D_MODEL), ada_b.reshape(n_mod, 1, 3 * D_MODEL))
    return out.reshape(n_mod * 8, 1, 3 * D_MODEL)


def _rope_table_body(pos_ref, inv_ref, cos_ref, hi_ref, lo_ref):
    ang = pos_ref[0].astype(F32) * inv_ref[...]
    lane = lax.broadcasted_iota(jnp.int32, ang.shape, 1)
    cos = jnp.cos(ang)
    sin = jnp.sin(ang)
    half = QK_ROPE_DIM // 2
    cos_ref[0] = jnp.where(lane < QK_ROPE_DIM, cos, 0.0)
    hi_ref[0] = jnp.where(lane < half, 0.0, jnp.where(lane < QK_ROPE_DIM, sin, 0.0))
    lo_ref[0] = jnp.where(lane < half, -sin, 0.0)


def _rope_tables(positions):
    tm = 512
    half = QK_ROPE_DIM // 2
    inv = ROPE_THETA ** (-jnp.arange(0, QK_ROPE_DIM, 2, dtype=F32) / QK_ROPE_DIM)
    inv_lanes = jnp.concatenate([inv, inv, jnp.zeros((LANES - 2 * half,), F32)]).reshape(1, LANES)
    tab = jax.ShapeDtypeStruct((BATCH, SEQ, LANES), F32)
    spec = pl.BlockSpec((1, tm, LANES), lambda b, i: (b, i, 0))
    return pl.pallas_call(
        _rope_table_body,
        grid=(BATCH, SEQ // tm),
        in_specs=[pl.BlockSpec((1, tm, 1), lambda b, i: (b, i, 0)),
                  pl.BlockSpec((1, LANES), lambda b, i: (0, 0))],
        out_specs=(spec, spec, spec),
        out_shape=(tab, tab, tab),
        compiler_params=_params(("parallel", "parallel")),
        name="rope_tables",
    )(positions.reshape(BATCH, SEQ, 1), inv_lanes)


def _front_body(x_ref, sh_ref, sc_ref, wq_ref, wkv_ref, wkr_ref, wu_ref, wv_ref,
                cos_ref, hi_ref, lo_ref, qg_ref, kvg_ref, wqn_ref, wqr_ref, wkn_ref, wvt_ref,
                sg_ref, sb_ref, sw_ref, sbias_ref,
                q_ref, k_ref, vt_ref, sgu_ref, *, tm):
    hm = _modulate(x_ref[0], sh_ref[0], sc_ref[0]).astype(BF16)
    _qkv_compute(_dot(hm, wq_ref[...]), _dot(hm, wkv_ref[...]), _dot(hm, wkr_ref[...]),
                 cos_ref[0], hi_ref[0], lo_ref[0], qg_ref, kvg_ref,
                 wqn_ref, wqr_ref, wkn_ref, wvt_ref, q_ref, k_ref, vt_ref)
    _sgu_compute(_dot(hm, wu_ref[...]), _dot(hm, wv_ref[...]),
                 sg_ref, sb_ref, sw_ref, sbias_ref, sgu_ref, tm)


def _front(x, mod, k, tables, w_in, q_norm_g, w_q_up, kv_norm_g, w_kv_up,
           sgu_norm_g, sgu_norm_b, sgu_w, sgu_b):
    tm = 512
    a, b, c_, d = (Q_LORA_RANK, Q_LORA_RANK + KV_LORA_RANK,
                   Q_LORA_RANK + KV_LORA_RANK + QK_ROPE_DIM,
                   Q_LORA_RANK + KV_LORA_RANK + QK_ROPE_DIM + SGU_WIDTH)
    wq = w_in[:, :a].astype(BF16)
    wkv = w_in[:, a:b].astype(BF16)
    wkr = jnp.pad(w_in[:, b:c_], ((0, 0), (0, LANES - QK_ROPE_DIM))).astype(BF16)
    wu = w_in[:, c_:d].astype(BF16)
    wv = w_in[:, d:].astype(BF16)
    in_widths = (Q_LORA_RANK, KV_LORA_RANK, LANES, SGU_WIDTH, SGU_WIDTH)

    wq_up = w_q_up.reshape(Q_LORA_RANK, MLA_HEADS, QK_NOPE_DIM + QK_ROPE_DIM)
    wqn = wq_up[:, :, :QK_NOPE_DIM].reshape(Q_LORA_RANK, MLA_HEADS * LANES).astype(BF16)
    wqr = jnp.pad(wq_up[:, :, QK_NOPE_DIM:], ((0, 0), (0, 0), (0, LANES - QK_ROPE_DIM)))
    wqr = wqr.reshape(Q_LORA_RANK, MLA_HEADS * LANES).astype(BF16)
    wkv_up = w_kv_up.reshape(KV_LORA_RANK, MLA_HEADS, QK_NOPE_DIM + V_HEAD_DIM)
    wkn = wkv_up[:, :, :QK_NOPE_DIM].reshape(KV_LORA_RANK, MLA_HEADS * LANES).astype(BF16)
    wvt = wkv_up[:, :, QK_NOPE_DIM:].reshape(KV_LORA_RANK, ATTN_WIDTH).T.astype(BF16)
    sgu_bias = jnp.repeat(sgu_b.T, SGU_GROUP_DIM, axis=1)

    shift, scale, _ = _mod_specs(k, 2)

    def rows(w):
        return pl.BlockSpec((1, tm, w), lambda b, i: (b, i, 0))

    return pl.pallas_call(
        functools.partial(_front_body, tm=tm),
        grid=(BATCH, SEQ // tm),
        in_specs=[rows(D_MODEL), shift, scale]
                 + [_const_spec((D_MODEL, w)) for w in in_widths]
                 + [rows(LANES), rows(LANES), rows(LANES),
                    _const_spec((1, Q_LORA_RANK)), _const_spec((1, KV_LORA_RANK)),
                    _const_spec((Q_LORA_RANK, MLA_HEADS * LANES)),
                    _const_spec((Q_LORA_RANK, MLA_HEADS * LANES)),
                    _const_spec((KV_LORA_RANK, MLA_HEADS * LANES)),
                    _const_spec((ATTN_WIDTH, KV_LORA_RANK)),
                    _const_spec((1, SGU_WIDTH)), _const_spec((1, SGU_WIDTH)),
                    _const_spec((SGU_GROUPS, CHUNK, CHUNK)), _const_spec((CHUNK, SGU_WIDTH))],
        out_specs=(rows(QK_WIDTH), rows(QK_WIDTH),
                   pl.BlockSpec((1, ATTN_WIDTH, tm), lambda b, i: (b, 0, i)),
                   rows(SGU_WIDTH)),
        out_shape=(jax.ShapeDtypeStruct((BATCH, SEQ, QK_WIDTH), BF16),
                   jax.ShapeDtypeStruct((BATCH, SEQ, QK_WIDTH), BF16),
                   jax.ShapeDtypeStruct((BATCH, ATTN_WIDTH, SEQ), BF16),
                   jax.ShapeDtypeStruct((BATCH, SEQ, SGU_WIDTH), BF16)),
        compiler_params=_params(("parallel", "parallel")),
        name="front_ab",
    )(x, mod, mod, wq, wkv, wkr, wu, wv, *tables,
      q_norm_g.reshape(1, -1), kv_norm_g.reshape(1, -1), wqn, wqr, wkn, wvt,
      sgu_norm_g.reshape(1, -1), sgu_norm_b.reshape(1, -1), sgu_w, sgu_bias)


def _qkv_compute(cq, ckv, kr, cos_t, sin_hi, sin_lo, qg_ref, kvg_ref,
                 wqn_ref, wqr_ref, wkn_ref, wvt_ref, q_ref, k_ref, vt_ref):
    cqn = _rms(cq, qg_ref[...]).astype(BF16)
    qn = _dot(cqn, wqn_ref[...])
    qr = _dot(cqn, wqr_ref[...])
    ckvn = _rms(ckv, kvg_ref[...]).astype(BF16)
    kn = _dot(ckvn, wkn_ref[...])
    vt_ref[0] = lax.dot_general(wvt_ref[...], ckvn, (((1,), (1,)), ((), ())),
                                preferred_element_type=F32).astype(BF16)
    k_rope = _rope(kr, cos_t, sin_hi, sin_lo).astype(BF16)
    for h in range(MLA_HEADS):
        nope = slice(h * LANES, (h + 1) * LANES)
        lo = h * HEAD_LANES
        q_ref[0, :, lo:lo + LANES] = qn[:, nope].astype(BF16)
        q_ref[0, :, lo + LANES:lo + HEAD_LANES] = _rope(qr[:, nope], cos_t, sin_hi, sin_lo).astype(BF16)
        k_ref[0, :, lo:lo + LANES] = kn[:, nope].astype(BF16)
        k_ref[0, :, lo + LANES:lo + HEAD_LANES] = k_rope


def _attn_body(qi_ref, kj_ref, q_ref, k_ref, vt_ref, o_ref, m_sc, l_sc, acc_sc, *, tq):
    p = pl.program_id(1)
    i = qi_ref[p]
    j = kj_ref[p]

    @pl.when(j == 0)
    def _():
        m_sc[...] = jnp.full(m_sc.shape, -jnp.inf, F32)
        l_sc[...] = jnp.zeros(l_sc.shape, F32)
        acc_sc[...] = jnp.zeros(acc_sc.shape, F32)

    def accumulate(diagonal):
        if diagonal:
            key = lax.broadcasted_iota(jnp.int32, (tq, tq), 0)
            qry = lax.broadcasted_iota(jnp.int32, (tq, tq), 1)
            visible = key <= qry
        for h in range(MLA_HEADS):
            qk = slice(h * HEAD_LANES, (h + 1) * HEAD_LANES)
            hv = slice(h * V_HEAD_DIM, (h + 1) * V_HEAD_DIM)
            s = lax.dot_general(k_ref[0, :, qk], q_ref[0, :, qk], (((1,), (1,)), ((), ())),
                                preferred_element_type=F32) * (SM_SCALE * LOG2_E)
            if diagonal:
                s = jnp.where(visible, s, -jnp.inf)
            m_prev = m_sc[h]
            m_new = jnp.maximum(m_prev, jnp.max(s, axis=0, keepdims=True))
            alpha = jnp.exp2(m_prev - m_new)
            pr = jnp.exp2(s - m_new)
            l_sc[h] = alpha * l_sc[h] + jnp.sum(pr, axis=0, keepdims=True)
            acc_sc[hv, :] = alpha * acc_sc[hv, :] + _dot(vt_ref[0, hv, :], pr.astype(BF16))
            m_sc[h] = m_new

    @pl.when(j < i)
    def _():
        accumulate(False)

    @pl.when(j == i)
    def _():
        accumulate(True)
        for h in range(MLA_HEADS):
            hv = slice(h * V_HEAD_DIM, (h + 1) * V_HEAD_DIM)
            o_ref[0, :, hv] = (acc_sc[hv, :] / l_sc[h]).T.astype(BF16)


def _attention(q, k, vt):
    tq = 512
    nb = SEQ // tq
    pairs = [(i, j) for i in range(nb) for j in range(i + 1)]
    qi = jnp.asarray(np.array([p[0] for p in pairs], np.int32))
    kj = jnp.asarray(np.array([p[1] for p in pairs], np.int32))
    grid_spec = pltpu.PrefetchScalarGridSpec(
        num_scalar_prefetch=2,
        grid=(BATCH, len(pairs)),
        in_specs=[pl.BlockSpec((1, tq, QK_WIDTH), lambda b, p, qi, kj: (b, qi[p], 0)),
                  pl.BlockSpec((1, tq, QK_WIDTH), lambda b, p, qi, kj: (b, kj[p], 0)),
                  pl.BlockSpec((1, ATTN_WIDTH, tq), lambda b, p, qi, kj: (b, 0, kj[p]))],
        out_specs=pl.BlockSpec((1, tq, ATTN_WIDTH), lambda b, p, qi, kj: (b, qi[p], 0)),
        scratch_shapes=[pltpu.VMEM((MLA_HEADS, 1, tq), F32),
                        pltpu.VMEM((MLA_HEADS, 1, tq), F32),
                        pltpu.VMEM((ATTN_WIDTH, tq), F32)],
    )
    return pl.pallas_call(
        functools.partial(_attn_body, tq=tq),
        grid_spec=grid_spec,
        out_shape=jax.ShapeDtypeStruct((BATCH, SEQ, ATTN_WIDTH), BF16),
        compiler_params=_params(("parallel", "arbitrary")),
        name="mla_attention",
    )(qi, kj, q, k, vt)


def _sgu_compute(zu, zv, g_ref, b_ref, w_ref, bias_ref, o_ref, tm):
    row = lax.broadcasted_iota(jnp.int32, (CHUNK, CHUNK), 0)
    col = lax.broadcasted_iota(jnp.int32, (CHUNK, CHUNK), 1)
    for g in range(SGU_GROUPS):
        gs = slice(g * SGU_GROUP_DIM, (g + 1) * SGU_GROUP_DIM)
        v = jax.nn.gelu(zv[:, gs])
        mu = jnp.mean(v, axis=-1, keepdims=True)
        vc = v - mu
        var = jnp.mean(vc * vc, axis=-1, keepdims=True)
        vn = (vc * lax.rsqrt(var + LN_EPS) * g_ref[:, gs] + b_ref[:, gs]).astype(BF16)
        w = jnp.where(col <= row, w_ref[g], 0.0).astype(BF16)
        for n in range(tm // CHUNK):
            rs = slice(n * CHUNK, (n + 1) * CHUNK)
            s = _dot(w, vn[rs]) + bias_ref[:, gs]
            o_ref[0, rs, gs] = (jax.nn.gelu(zu[rs, gs]) * s).astype(BF16)


def _outproj_body(a_ref, s_ref, x_ref, gate_ref, wa_ref, ws_ref, lg_ref, lb_ref, o_ref):
    half = a_ref.shape[1] // 2
    for r0 in (0, half):
        rs = slice(r0, r0 + half)
        y = _dot(a_ref[0, rs], wa_ref[...]) + _dot(s_ref[0, rs], ws_ref[...])
        o_ref[0, rs] = _deepnorm(x_ref[0, rs], y, gate_ref[0], lg_ref[...], lb_ref[...])


def _outproj(attn, sgu, x, mod, k, w_out, ln_g, ln_b):
    tm = 512
    wa = w_out[:ATTN_WIDTH].astype(BF16)
    ws = w_out[ATTN_WIDTH:].astype(BF16)
    _, _, gate = _mod_specs(k, 2)

    def rows(w):
        return pl.BlockSpec((1, tm, w), lambda b, i: (b, i, 0))

    return pl.pallas_call(
        _outproj_body,
        grid=(BATCH, SEQ // tm),
        in_specs=[rows(ATTN_WIDTH), rows(SGU_WIDTH), rows(D_MODEL), gate,
                  _const_spec((ATTN_WIDTH, D_MODEL)), _const_spec((SGU_WIDTH, D_MODEL)),
                  _const_spec((1, D_MODEL)), _const_spec((1, D_MODEL))],
        out_specs=rows(D_MODEL),
        out_shape=jax.ShapeDtypeStruct((BATCH, SEQ, D_MODEL), F32),
        compiler_params=_params(("parallel", "parallel")),
        name="outproj_ab",
    )(attn, sgu, x, mod, wa, ws, ln_g.reshape(1, -1), ln_b.reshape(1, -1))


def _ffn_up_body(x_ref, sh_ref, sc_ref, wg_ref, wu_ref, h_ref, hm_sc):
    @pl.when(pl.program_id(2) == 0)
    def _():
        hm_sc[...] = _modulate(x_ref[0], sh_ref[0], sc_ref[0]).astype(BF16)

    hm = hm_sc[...]
    g = _dot(hm, wg_ref[0].astype(BF16))
    u = _dot(hm, wu_ref[0].astype(BF16))
    h_ref[0] = (jax.nn.silu(g) * u).astype(BF16)


def _ffn_down_body(h_ref, x_ref, gate_ref, wd_ref, lg_ref, lb_ref, o_ref, y_sc, *, n_n):
    n = pl.program_id(2)
    y_sc[n] = _dot(h_ref[0], wd_ref[...])

    @pl.when(n == n_n - 1)
    def _():
        y = jnp.concatenate([y_sc[c] for c in range(n_n)], axis=-1)
        o_ref[0] = _deepnorm(x_ref[0], y, gate_ref[0], lg_ref[...], lb_ref[...])


def _ffn(x, mod, k, layer, w_gate, w_up, w_down, ln_g, ln_b):
    tm, tf = 2048, 256
    shift, scale, gate = _mod_specs(k, 3)
    h = pl.pallas_call(
        _ffn_up_body,
        grid=(BATCH, SEQ // tm, D_FF // tf),
        in_specs=[pl.BlockSpec((1, tm, D_MODEL), lambda b, i, f: (b, i, 0), pipeline_mode=pl.Buffered(1)),
                  shift, scale,
                  pl.BlockSpec((1, D_MODEL, tf), lambda b, i, f: (layer, 0, f)),
                  pl.BlockSpec((1, D_MODEL, tf), lambda b, i, f: (layer, 0, f))],
        out_specs=pl.BlockSpec((1, tm, tf), lambda b, i, f: (b, i, f)),
        out_shape=jax.ShapeDtypeStruct((BATCH, SEQ, D_FF), BF16),
        scratch_shapes=[pltpu.VMEM((tm, D_MODEL), BF16)],
        compiler_params=_params(("parallel", "parallel", "arbitrary")),
        name="ffn_up",
    )(x, mod, mod, w_gate, w_up)

    tm, tn = 512, 512
    n_n = D_MODEL // tn
    rows = pl.BlockSpec((1, tm, D_MODEL), lambda b, i, n: (b, i, 0))
    vec = pl.BlockSpec((1, D_MODEL), lambda b, i, n: (0, 0))
    return pl.pallas_call(
        functools.partial(_ffn_down_body, n_n=n_n),
        grid=(BATCH, SEQ // tm, n_n),
        in_specs=[pl.BlockSpec((1, tm, D_FF), lambda b, i, n: (b, i, 0)),
                  rows,
                  gate,
                  pl.BlockSpec((D_FF, tn), lambda b, i, n: (0, n)),
                  vec, vec],
        out_specs=rows,
        out_shape=jax.ShapeDtypeStruct((BATCH, SEQ, D_MODEL), F32),
        scratch_shapes=[pltpu.VMEM((n_n, tm, tn), F32)],
        compiler_params=_params(("parallel", "parallel", "arbitrary")),
        name="ffn_down",
    )(h, x, mod, w_down.astype(BF16), ln_g.reshape(1, -1), ln_b.reshape(1, -1))


def _pool_body(x_ref, halo_ref, sh_ref, sc_ref, gate_ref, pw_ref, ps_ref, wo_ref, lg_ref, lb_ref,
               o_ref, *, tm):
    i = pl.program_id(1)
    x = x_ref[0]
    hm = _modulate(x, sh_ref[0], sc_ref[0])
    halo = jnp.where(i > 0, _modulate(halo_ref[0], sh_ref[0], sc_ref[0]), 0.0)
    ext = jnp.concatenate([halo, hm], axis=0)
    ext_hi = ext.astype(BF16)
    ext_lo = (ext - ext_hi.astype(F32)).astype(BF16)
    t_loc = lax.broadcasted_iota(jnp.int32, (POOL_HALO, 2 * POOL_HALO), 0)
    s_loc = lax.broadcasted_iota(jnp.int32, (POOL_HALO, 2 * POOL_HALO), 1)
    lag = t_loc + POOL_HALO - s_loc
    t_glob = i * tm + lax.broadcasted_iota(jnp.int32, (tm, 1), 0)
    outs = []
    for gi, w in enumerate(POOL_WINDOWS):
        cs = slice(gi * POOL_GROUP_DIM, (gi + 1) * POOL_GROUP_DIM)
        band = jnp.where(lag >= 0, jnp.where(lag < w, 1.0, 0.0), 0.0).astype(BF16)
        wsum = jnp.concatenate(
            [_dot(band, ext_hi[c:c + 2 * POOL_HALO, cs]) + _dot(band, ext_lo[c:c + 2 * POOL_HALO, cs])
             for c in range(0, tm, POOL_HALO)], axis=0)
        count = jnp.minimum(t_glob + 1, w).astype(F32)
        d = (wsum / count - hm[:, cs]).astype(BF16)
        outs.append(_dot(d, pw_ref[gi]))
    pooled = (jnp.concatenate(outs, axis=-1) * ps_ref[...]).astype(BF16)
    y = _dot(pooled, wo_ref[...])
    o_ref[0] = _deepnorm(x, y, gate_ref[0], lg_ref[...], lb_ref[...])


def _pool_mixer(x, mod, k, pool_w, pool_scale, w_out_c, ln_g, ln_b):
    tm = 512
    per = tm // POOL_HALO
    shift, scale, gate = _mod_specs(k, 2)
    rows = pl.BlockSpec((1, tm, D_MODEL), lambda b, i: (b, i, 0))
    halo = pl.BlockSpec((1, POOL_HALO, D_MODEL), lambda b, i: (b, jnp.maximum(i * per - 1, 0), 0))
    return pl.pallas_call(
        functools.partial(_pool_body, tm=tm),
        grid=(BATCH, SEQ // tm),
        in_specs=[rows, halo, shift, scale, gate,
                  _const_spec((len(POOL_WINDOWS), POOL_GROUP_DIM, POOL_GROUP_DIM)),
                  _const_spec((1, D_MODEL)), _const_spec((D_MODEL, D_MODEL)),
                  _const_spec((1, D_MODEL)), _const_spec((1, D_MODEL))],
        out_specs=rows,
        out_shape=jax.ShapeDtypeStruct((BATCH, SEQ, D_MODEL), F32),
        compiler_params=_params(("parallel", "parallel")),
        name="pool_mixer",
    )(x, x, mod, mod, mod, pool_w.astype(BF16), pool_scale.reshape(1, -1),
      w_out_c.astype(BF16), ln_g.reshape(1, -1), ln_b.reshape(1, -1))


def _router_body(x_ref, sh_ref, sc_ref, wr_ref, g_ref, hm_ref):
    hm = _modulate(x_ref[0], sh_ref[0], sc_ref[0])
    hm_ref[0] = hm
    hm_hi = hm.astype(BF16)
    hm_lo = (hm - hm_hi.astype(F32)).astype(BF16)
    w = wr_ref[...]
    w_hi = w.astype(BF16)
    w_lo = (w - w_hi.astype(F32)).astype(BF16)
    logits = _dot(hm_hi, w_hi) + (_dot(hm_lo, w_hi) + _dot(hm_hi, w_lo))
    lane = lax.broadcasted_iota(jnp.int32, logits.shape, 1).astype(F32)
    logits = jnp.where(lane < N_EXPERTS, logits, -jnp.inf)
    v1 = jnp.max(logits, axis=-1, keepdims=True)
    i1 = jnp.min(jnp.where(logits == v1, lane, float(LANES)), axis=-1, keepdims=True)
    rest = jnp.where(lane == i1, -jnp.inf, logits)
    v2 = jnp.max(rest, axis=-1, keepdims=True)
    i2 = jnp.min(jnp.where(rest == v2, lane, float(LANES)), axis=-1, keepdims=True)
    e2 = jnp.exp(v2 - v1)
    den = 1.0 + e2
    g_ref[0] = (jnp.where(lane == 0.0, i1, 0.0) + jnp.where(lane == 1.0, i2, 0.0)
                + jnp.where(lane == 2.0, 1.0 / den, 0.0) + jnp.where(lane == 3.0, e2 / den, 0.0))


def _router(x, mod, k, w_router):
    tm = 512
    shift, scale, _ = _mod_specs(k, 2)
    wr = jnp.pad(w_router, ((0, 0), (0, LANES - N_EXPERTS)))
    rows = pl.BlockSpec((1, tm, D_MODEL), lambda b, i: (b, i, 0))
    return pl.pallas_call(
        _router_body,
        grid=(BATCH, SEQ // tm),
        in_specs=[rows, shift, scale, _const_spec((D_MODEL, LANES))],
        out_specs=(pl.BlockSpec((1, tm, LANES), lambda b, i: (b, i, 0)), rows),
        out_shape=(jax.ShapeDtypeStruct((BATCH, SEQ, LANES), F32),
                   jax.ShapeDtypeStruct((BATCH, SEQ, D_MODEL), F32)),
        compiler_params=_params(("parallel", "parallel")),
        name="router",
    )(x, mod, mod, wr)


N_TOKENS = BATCH * SEQ
N_ASSIGN = 2 * N_TOKENS
MOE_BLOCK = 1024
MOE_SUB = 256
MOE_TILE = N_ASSIGN // N_EXPERTS * 5 // 4
MOE_TILES_ANY = -(-(N_ASSIGN + N_EXPERTS * (MOE_TILE - 1)) // MOE_TILE)
MOE_TILES_FIT = N_EXPERTS
GATHER_ROWS = MOE_TILE // 4
assert MOE_TILE % GATHER_ROWS == 0 and GATHER_ROWS % 8 == 0


def _route_counts(routing):
    e_flat = routing.reshape(N_TOKENS, LANES)[:, :2].astype(jnp.int32).T.reshape(N_ASSIGN)
    onehot = (e_flat[:, None] == jnp.arange(N_EXPERTS, dtype=jnp.int32)[None, :]).astype(jnp.int32)
    csum = jnp.cumsum(onehot, axis=0)
    return onehot, jnp.sum(csum * onehot, axis=1) - 1, csum[-1]


def _route_plan(onehot, rank, counts, n_tiles):
    padded = (counts + MOE_TILE - 1) // MOE_TILE * MOE_TILE
    group_end = jnp.cumsum(padded)
    group_start = group_end - padded
    pos = (jnp.sum(onehot * group_start[None, :], axis=1) + rank).astype(jnp.int32)
    tile_start = jnp.arange(n_tiles, dtype=jnp.int32) * MOE_TILE
    n_used = group_end[-1] // MOE_TILE
    tile_expert = jnp.sum((tile_start[:, None] >= group_end[None, :]).astype(jnp.int32), axis=1)
    last_expert = jnp.max(jnp.where(counts > 0, jnp.arange(N_EXPERTS, dtype=jnp.int32), 0))
    tile_expert = jnp.minimum(tile_expert, last_expert)
    tile_rows = jnp.clip(counts[tile_expert] - (tile_start - group_start[tile_expert]), 0, MOE_TILE)
    tile_rows = jnp.where(jnp.arange(n_tiles) < n_used, tile_rows, 0).astype(jnp.int32)
    return pos, tile_expert.astype(jnp.int32), tile_rows, n_used.reshape(1).astype(jnp.int32)


GATHER_UNROLL = 8


def _gather_body(pos_ref, rows_ref, hm_hbm, o_ref, src_ref, buf, sem, *, tg, n_steps):
    i = pl.program_id(0)
    per = MOE_TILE // tg

    @pl.when(i == 0)
    def _():
        def invert(g, carry):
            for u in range(GATHER_UNROLL):
                t = g * GATHER_UNROLL + u
                src_ref[pos_ref[t]] = t
                src_ref[pos_ref[N_TOKENS + t]] = t
            return carry

        lax.fori_loop(0, N_TOKENS // GATHER_UNROLL, invert, 0)

    def valid_rows(step):
        n_valid = jnp.clip(rows_ref[step // per] - (step % per) * tg, 0, tg)
        return n_valid, n_valid // GATHER_UNROLL

    def issue(step, slot):
        base = step * tg
        n_valid, n_groups = valid_rows(step)

        @pl.when(n_valid < tg)
        def _():
            buf[slot] = jnp.zeros(buf.shape[1:], F32)

        def issue_group(g, carry):
            for u in range(GATHER_UNROLL):
                t = src_ref[base + g * GATHER_UNROLL + u]
                pltpu.make_async_copy(hm_hbm.at[pl.ds(t, 1)], buf.at[slot, g, pl.ds(u, 1)],
                                      sem.at[slot]).start(priority=u % 2)
            return carry

        def issue_one(r, carry):
            t = src_ref[base + r]
            pltpu.make_async_copy(hm_hbm.at[pl.ds(t, 1)],
                                  buf.at[slot, r // GATHER_UNROLL, pl.ds(r % GATHER_UNROLL, 1)],
                                  sem.at[slot]).start()
            return carry

        lax.fori_loop(0, n_groups, issue_group, 0)
        lax.fori_loop(n_groups * GATHER_UNROLL, n_valid, issue_one, 0)

    def wait(step, slot):
        n_valid, n_groups = valid_rows(step)

        @pl.when(n_groups > 0)
        def _():
            whole = buf.at[slot, pl.ds(0, n_groups)]
            pltpu.make_async_copy(whole, whole, sem.at[slot]).wait()

        def wait_one(r, carry):
            row = buf.at[slot, 0, pl.ds(0, 1)]
            pltpu.make_async_copy(row, row, sem.at[slot]).wait()
            return carry

        lax.fori_loop(n_groups * GATHER_UNROLL, n_valid, wait_one, 0)

    for slot in range(2):
        @pl.when(i % 2 == slot)
        def _():
            if slot == 0:
                pl.when(i == 0)(functools.partial(issue, i, slot))
            pl.when(i + 1 < n_steps)(functools.partial(issue, i + 1, 1 - slot))
            wait(i, slot)
            o_ref[...] = buf[slot].reshape(tg, D_MODEL).astype(BF16)


def _gather_rows(pos, tile_rows, hm):
    n_rows = tile_rows.shape[0] * MOE_TILE
    tg = GATHER_ROWS
    grid_spec = pltpu.PrefetchScalarGridSpec(
        num_scalar_prefetch=2,
        grid=(n_rows // tg,),
        in_specs=[pl.BlockSpec(memory_space=pl.ANY)],
        out_specs=pl.BlockSpec((tg, D_MODEL), lambda i, pos, tr: (i, 0)),
        scratch_shapes=[pltpu.SMEM((n_rows,), jnp.int32),
                        pltpu.VMEM((2, tg // GATHER_UNROLL, GATHER_UNROLL, D_MODEL), F32),
                        pltpu.SemaphoreType.DMA((2,))],
    )
    return pl.pallas_call(
        functools.partial(_gather_body, tg=tg, n_steps=n_rows // tg),
        grid_spec=grid_spec,
        out_shape=jax.ShapeDtypeStruct((n_rows, D_MODEL), BF16),
        compiler_params=_params(("arbitrary",)),
        name="moe_gather",
    )(pos, tile_rows, hm.reshape(N_TOKENS, D_MODEL))


def _for_valid_blocks(rows, tm, block_fn, empty_fn):
    for b0 in range(0, tm, MOE_BLOCK):
        b1 = min(b0 + MOE_BLOCK, tm)
        whole = b1 if b1 - b0 == MOE_BLOCK else tm + 1
        if whole <= tm:
            pl.when(rows >= whole)(functools.partial(block_fn, slice(b0, b1)))

        for r0 in range(b0, b1, MOE_SUB):
            rs = slice(r0, r0 + MOE_SUB)

            @pl.when(jnp.logical_and(rows < whole, r0 < rows))
            def _():
                block_fn(rs)

            @pl.when(r0 >= rows)
            def _():
                empty_fn(rs)


def _gup_body(te_ref, rows_ref, nu_ref, xs_ref, wg_ref, wu_ref, h_ref, wgb, wub, *, tm):
    rows = rows_ref[pl.program_id(0)]

    @pl.when(rows > 0)
    def _():
        wgb[...] = wg_ref[0].astype(BF16)
        wub[...] = wu_ref[0].astype(BF16)

    def block(rs):
        xsb = xs_ref[rs, :]
        h_ref[rs, :] = (jax.nn.silu(_dot(xsb, wgb[...])) * _dot(xsb, wub[...])).astype(BF16)

    def empty(rs):
        h_ref[rs, :] = jnp.zeros((rs.stop - rs.start, h_ref.shape[1]), BF16)

    _for_valid_blocks(rows, tm, block, empty)


def _gdown_body(te_ref, rows_ref, nu_ref, h_ref, wd_ref, o_ref, wdb, *, tm):
    rows = rows_ref[pl.program_id(0)]

    @pl.when(rows > 0)
    def _():
        wdb[...] = wd_ref[0].astype(BF16)

    def block(rs):
        o_ref[rs, :] = _dot(h_ref[rs, :], wdb[...])

    def empty(rs):
        o_ref[rs, :] = jnp.zeros((rs.stop - rs.start, o_ref.shape[1]), F32)

    _for_valid_blocks(rows, tm, block, empty)


def _grouped_ffn(xs, tile_expert, tile_rows, n_used, layer, w_gate, w_up, w_down):
    tm = MOE_TILE
    n_tiles = tile_expert.shape[0]
    first = layer * N_EXPERTS
    n_all = w_gate.shape[0] * N_EXPERTS

    def tile(i, nu):
        return jnp.minimum(i, nu[0] - 1)

    def chunk(i, c, nu, n_chunks):
        return jnp.where(i < nu[0], c, n_chunks - 1)

    tf = 256
    n_f = D_FF_EXPERT // tf
    w_up_spec = pl.BlockSpec((1, D_MODEL, tf),
                             lambda i, f, te, tr, nu: (first + te[i], 0, chunk(i, f, nu, n_f)))
    up_spec = pltpu.PrefetchScalarGridSpec(
        num_scalar_prefetch=3,
        grid=(n_tiles, n_f),
        in_specs=[pl.BlockSpec((tm, D_MODEL), lambda i, f, te, tr, nu: (tile(i, nu), 0)),
                  w_up_spec, w_up_spec],
        out_specs=pl.BlockSpec((tm, tf), lambda i, f, te, tr, nu: (i, f)),
        scratch_shapes=[pltpu.VMEM((D_MODEL, tf), BF16), pltpu.VMEM((D_MODEL, tf), BF16)],
    )
    h = pl.pallas_call(
        functools.partial(_gup_body, tm=tm),
        grid_spec=up_spec,
        out_shape=jax.ShapeDtypeStruct((n_tiles * tm, D_FF_EXPERT), BF16),
        compiler_params=_params(("arbitrary", "arbitrary")),
        name="moe_up",
    )(tile_expert, tile_rows, n_used, xs,
      w_gate.reshape(n_all, D_MODEL, D_FF_EXPERT), w_up.reshape(n_all, D_MODEL, D_FF_EXPERT))

    tn = 512
    n_n = D_MODEL // tn
    down_spec = pltpu.PrefetchScalarGridSpec(
        num_scalar_prefetch=3,
        grid=(n_tiles, n_n),
        in_specs=[pl.BlockSpec((tm, D_FF_EXPERT), lambda i, n, te, tr, nu: (tile(i, nu), 0)),
                  pl.BlockSpec((1, D_FF_EXPERT, tn),
                               lambda i, n, te, tr, nu: (first + te[i], 0, chunk(i, n, nu, n_n)))],
        out_specs=pl.BlockSpec((tm, tn), lambda i, n, te, tr, nu: (i, n)),
        scratch_shapes=[pltpu.VMEM((D_FF_EXPERT, tn), BF16)],
    )
    return pl.pallas_call(
        functools.partial(_gdown_body, tm=tm),
        grid_spec=down_spec,
        out_shape=jax.ShapeDtypeStruct((n_tiles * tm, D_MODEL), F32),
        compiler_params=_params(("arbitrary", "arbitrary")),
        name="moe_down",
    )(tile_expert, tile_rows, n_used, h, w_down.reshape(n_all, D_FF_EXPERT, D_MODEL))


COMBINE_SLAB = 128


def _combine_body(pos_ref, ys_hbm, r_ref, x_ref, gate_ref, lg_ref, lb_ref, o_ref, buf, sem, *, tc):
    n_steps = pl.num_programs(0) * pl.num_programs(1)
    step = pl.program_id(0) * pl.num_programs(1) + pl.program_id(1)

    def issue(s, slot):
        base = s * tc

        def issue_group(g, carry):
            for u in range(GATHER_UNROLL):
                for k in range(2):
                    p = pos_ref[k * N_TOKENS + base + g * GATHER_UNROLL + u]
                    pltpu.make_async_copy(ys_hbm.at[pl.ds(p, 1)], buf.at[slot, k, g, pl.ds(u, 1)],
                                          sem.at[slot, k]).start(priority=k)
            return carry

        lax.fori_loop(0, tc // GATHER_UNROLL, issue_group, 0)

    def issue_rows(s, slot, r_lo, r_hi):
        base = s * tc
        for r in range(r_lo, r_hi):
            for k in range(2):
                p = pos_ref[k * N_TOKENS + base + r]
                pltpu.make_async_copy(ys_hbm.at[pl.ds(p, 1)],
                                      buf.at[slot, k, r // GATHER_UNROLL, pl.ds(r % GATHER_UNROLL, 1)],
                                      sem.at[slot, k]).start(priority=k)

    def wait(slot):
        for k in range(2):
            pltpu.make_async_copy(buf.at[slot, k], buf.at[slot, k], sem.at[slot, k]).wait()

    for slot in range(2):
        @pl.when(step % 2 == slot)
        def _():
            if slot == 0:
                pl.when(step == 0)(functools.partial(issue, step, slot))
            wait(slot)
            nxt = jnp.minimum(step + 1, n_steps - 1)
            routing = r_ref[0]
            for r0 in range(0, tc, COMBINE_SLAB):
                rs = slice(r0, r0 + COMBINE_SLAB)
                gs = slice(r0 // GATHER_UNROLL, (r0 + COMBINE_SLAB) // GATHER_UNROLL)
                issue_rows(nxt, 1 - slot, r0, r0 + COMBINE_SLAB)
                y = (routing[rs, 2:3] * buf[slot, 0, gs].reshape(COMBINE_SLAB, D_MODEL)
                     + routing[rs, 3:4] * buf[slot, 1, gs].reshape(COMBINE_SLAB, D_MODEL))
                o_ref[0, rs] = _deepnorm(x_ref[0, rs], y, gate_ref[0], lg_ref[...], lb_ref[...])
            pl.when(step == n_steps - 1)(functools.partial(wait, 1 - slot))


def _combine(pos, ys, routing, x, mod, k, ln_g, ln_b):
    tc = 512
    rows = pl.BlockSpec((1, tc, D_MODEL), lambda b, i, pos: (b, i, 0))
    vec = pl.BlockSpec((1, D_MODEL), lambda b, i, pos: (0, 0))
    grid_spec = pltpu.PrefetchScalarGridSpec(
        num_scalar_prefetch=1,
        grid=(BATCH, SEQ // tc),
        in_specs=[pl.BlockSpec(memory_space=pl.ANY),
                  pl.BlockSpec((1, tc, LANES), lambda b, i, pos: (b, i, 0)),
                  rows,
                  pl.BlockSpec((1, 1, D_MODEL), lambda b, i, pos: (k * 8 + b, 0, 2)),
                  vec, vec],
        out_specs=rows,
        scratch_shapes=[pltpu.VMEM((2, 2, tc // GATHER_UNROLL, GATHER_UNROLL, D_MODEL), F32),
                        pltpu.SemaphoreType.DMA((2, 2))],
    )
    return pl.pallas_call(
        functools.partial(_combine_body, tc=tc),
        grid_spec=grid_spec,
        out_shape=jax.ShapeDtypeStruct((BATCH, SEQ, D_MODEL), F32),
        compiler_params=_params(("arbitrary", "arbitrary")),
        name="moe_combine",
    )(pos, ys, routing, x, mod, ln_g.reshape(1, -1), ln_b.reshape(1, -1))


def _moe(x, mod, k, w_router, layer, w_gate, w_up, w_down, ln_g, ln_b):
    routing, hm = _router(x, mod, k, w_router)
    onehot, rank, counts = _route_counts(routing)

    def experts_over(n_tiles):
        def run(_):
            pos, tile_expert, tile_rows, n_used = _route_plan(onehot, rank, counts, n_tiles)
            xs = _gather_rows(pos, tile_rows, hm)
            ys = _grouped_ffn(xs, tile_expert, tile_rows, n_used, layer, w_gate, w_up, w_down)
            return _combine(pos, ys, routing, x, mod, k, ln_g, ln_b)
        return run

    return lax.cond(jnp.all(counts <= MOE_TILE),
                    experts_over(MOE_TILES_FIT), experts_over(MOE_TILES_ANY), None)


def kernel(x, c, positions, ada_w, ada_b, ln_g, ln_b, w_in_ab, q_norm_g, w_q_up, kv_norm_g, w_kv_up, sgu_norm_g, sgu_norm_b, sgu_w, sgu_b, w_out_ab, ffn_w_gate, ffn_w_up, ffn_w_down, pool_w, pool_scale, w_out_c, router_w, moe_w_gate, moe_w_up, moe_w_down):
    mod = _ada_modulation(c, ada_w, ada_b)
    tables = _rope_tables(positions)
    for l in range(DEPTH):
        j = l // 2
        k_tok, k_ch = 2 * l, 2 * l + 1
        if l % 2 == 0:
            q, k, vt, sgu = _front(x, mod, k_tok, tables, w_in_ab[j], q_norm_g[j], w_q_up[j],
                                   kv_norm_g[j], w_kv_up[j],
                                   sgu_norm_g[j], sgu_norm_b[j], sgu_w[j], sgu_b[j])
            attn = _attention(q, k, vt)
            x = _outproj(attn, sgu, x, mod, k_tok, w_out_ab[j], ln_g[l, 0], ln_b[l, 0])
            x = _ffn(x, mod, k_ch, j, ffn_w_gate, ffn_w_up, ffn_w_down[j], ln_g[l, 1], ln_b[l, 1])
        else:
            x = _pool_mixer(x, mod, k_tok, pool_w[j], pool_scale[j], w_out_c[j], ln_g[l, 0], ln_b[l, 0])
            x = _moe(x, mod, k_ch, router_w[j], j, moe_w_gate, moe_w_up, moe_w_down,
                     ln_g[l, 1], ln_b[l, 1])
    return x
```
